```python
import math
import jax, jax.numpy as jnp
from jax import lax
import numpy as np

D_MODEL = 1024
BATCH = 8
SEQ = 4096
DEPTH = 4

CHUNK = 64
Q_BLOCK = 128
TOKEN_BLOCK = 128
EPS = 1e-6
H_A = 4
DH_A = 64
DV_A = 2 * DH_A
H_B = 4
DK_B = 64
DV_B = 128
GATE_RANK = 16
GATE_TAU = 16.0
H_C = 16
DH_C = 64
N_BUCKETS = 32
MAX_DISTANCE = 128
N_GROUPS = 4
EXPERTS_PER_GROUP = 8
N_EXPERTS = N_GROUPS * EXPERTS_PER_GROUP
TOP_K = 2
D_EXPERT = 512

N_EVEN = (DEPTH + 1) // 2
N_ODD = DEPTH // 2
EVEN_SIZES = (H_A * 2 * DH_A, H_A * 2 * DH_A, H_A * DV_A,
              H_B * DK_B, H_B * DK_B, H_B * DV_B, H_B * DV_B, GATE_RANK)
EVEN_COLS = sum(EVEN_SIZES)
MIX_EVEN = H_A * DV_A + H_B * DV_B
MIX_ODD = H_C * DH_C

kernel_name = "hybrid_diff_gla_stickbreak_hmoe"


def rmsnorm(x, g):
    xf = x.astype(jnp.float32)
    y = xf * lax.rsqrt(jnp.mean(xf * xf, axis=-1, keepdims=True) + EPS)
    return (y * g.astype(jnp.float32)).astype(x.dtype)


def t5_bucket(rel):
    nb = N_BUCKETS // 2
    max_exact = nb // 2
    ret = jnp.where(rel > 0, nb, 0)
    n = jnp.abs(rel)
    nf = jnp.maximum(n, 1).astype(jnp.float32)
    large = max_exact + (jnp.log(nf / max_exact) / math.log(MAX_DISTANCE / max_exact)
                         * (nb - max_exact)).astype(jnp.int32)
    large = jnp.minimum(large, nb - 1)
    return ret + jnp.where(n < max_exact, n, large)


def diff_attention(q, k, v, lam, rel_bias):
    B, S = q.shape[0], q.shape[1]
    nq = S // Q_BLOCK
    kpos = jnp.arange(S)
    q_blocks = jnp.moveaxis(q.reshape(B, nq, Q_BLOCK, H_A, 2, DH_A), 1, 0)
    starts = jnp.arange(nq) * Q_BLOCK

    def one(args):
        qb, start = args
        qpos = start + jnp.arange(Q_BLOCK)
        logits = jnp.einsum('bqhmd,bkhmd->bhmqk', qb, k).astype(jnp.float32) * (DH_A ** -0.5)
        bias = jnp.transpose(rel_bias[t5_bucket(kpos[None, :] - qpos[:, None])], (2, 0, 1))
        logits = logits + bias[None, :, None].astype(jnp.float32)
        mask = (kpos[None, :] // CHUNK) <= (qpos[:, None] // CHUNK)
        p = jax.nn.softmax(jnp.where(mask, logits, -jnp.inf), axis=-1)
        w = p[:, :, 0] - lam * p[:, :, 1]
        return jnp.einsum('bhqk,bkhe->bqhe', w.astype(v.dtype), v)

    o = lax.map(one, (q_blocks, starts))
    return jnp.moveaxis(o, 0, 1).reshape(B, S, H_A, DV_A)


def gla(q, k, v, log_alpha):
    B, S = q.shape[0], q.shape[1]
    nc = S // CHUNK
    f32 = jnp.float32
    q = q.astype(f32).reshape(B, nc, CHUNK, H_B, DK_B) * (DK_B ** -0.5)
    k = k.astype(f32).reshape(B, nc, CHUNK, H_B, DK_B)
    v = v.astype(f32).reshape(B, nc, CHUNK, H_B, DV_B)
    b = jnp.cumsum(log_alpha.reshape(B, nc, CHUNK, H_B, DK_B), axis=2)
    b_last = b[:, :, -1]
    q_dec = q * jnp.exp(b)
    k_inv = k * jnp.exp(-b)
    k_dec = k * jnp.exp(b_last[:, :, None] - b)
    causal = jnp.tril(jnp.ones((CHUNK, CHUNK), dtype=bool))
    scores = jnp.einsum('bnthd,bnshd->bnhts', q_dec, k_inv)
    intra = jnp.einsum('bnhts,bnshe->bnthe', jnp.where(causal, scores, 0.0), v)

    def step(state, xs):
        q_c, k_c, v_c, bl = xs
        o = jnp.einsum('bthd,bhde->bthe', q_c, state)
        state = jnp.exp(bl)[..., None] * state + jnp.einsum('bshd,bshe->bhde', k_c, v_c)
        return state, o

    xs = (jnp.moveaxis(q_dec, 1, 0), jnp.moveaxis(k_dec, 1, 0), jnp.moveaxis(v, 1, 0),
          jnp.moveaxis(b_last, 1, 0))
    _, inter = lax.scan(step, jnp.zeros((B, H_B, DK_B, DV_B), f32), xs)
    return (intra + jnp.moveaxis(inter, 0, 1)).reshape(B, S, H_B, DV_B)


def stick_breaking(q, k, v):
    B, S = q.shape[0], q.shape[1]
    nq = S // Q_BLOCK
    kpos = jnp.arange(S)
    q_blocks = jnp.moveaxis(q.reshape(B, nq, Q_BLOCK, H_C, DH_C), 1, 0)
    starts = jnp.arange(nq) * Q_BLOCK

    def one(args):
        qb, start = args
        qpos = start + jnp.arange(Q_BLOCK)
        z = jnp.einsum('bqhd,bkhd->bhqk', qb, k).astype(jnp.float32) * (DH_C ** -0.5)
        mask = kpos[None, :] < qpos[:, None]
        log_1m = jnp.where(mask, jax.nn.log_sigmoid(-z), 0.0)
        suffix = lax.cumsum(log_1m, axis=3, reverse=True) - log_1m
        a = jnp.where(mask, jnp.exp(jax.nn.log_sigmoid(z) + suffix), 0.0)
        return jnp.einsum('bhqk,bkhd->bqhd', a.astype(v.dtype), v)

    o = lax.map(one, (q_blocks, starts))
    return jnp.moveaxis(o, 0, 1).reshape(B, S, H_C, DH_C)


def even_mixer(h, w_in, lam_p, subln_g, w_gk2, b_gk, gla_g, w_out, rel_bias, layer_idx):
    B, S, _ = h.shape
    idx = np.cumsum(EVEN_SIZES)[:-1].tolist()
    aq, ak, av, bq, bk, bv, br, bg = jnp.split(h @ w_in, idx, axis=-1)
    lambda_init = 0.8 - 0.6 * math.exp(-0.3 * layer_idx)
    lp = lam_p.astype(jnp.float32)
    lam = jnp.exp(jnp.sum(lp[0] * lp[1])) - jnp.exp(jnp.sum(lp[2] * lp[3])) + lambda_init
    oa = diff_attention(aq.reshape(B, S, H_A, 2, DH_A), ak.reshape(B, S, H_A, 2, DH_A),
                        av.reshape(B, S, H_A, DV_A), lam, rel_bias)
    oa = rmsnorm(oa, subln_g) * (1.0 - lambda_init)
    log_alpha = jax.nn.log_sigmoid((bg @ w_gk2 + b_gk).astype(jnp.float32)) / GATE_TAU
    ob = gla(bq.reshape(B, S, H_B, DK_B), bk.reshape(B, S, H_B, DK_B),
             bv.reshape(B, S, H_B, DV_B), log_alpha.reshape(B, S, H_B, DK_B)).astype(h.dtype)
    ob = rmsnorm(ob, gla_g) * jax.nn.silu(br.reshape(B, S, H_B, DV_B))
    o = jnp.concatenate([oa.reshape(B, S, H_A * DV_A), ob.reshape(B, S, H_B * DV_B)], axis=-1)
    return o @ w_out


def odd_mixer(h, w_in, w_out):
    B, S, _ = h.shape
    q, k, v = jnp.split(h @ w_in, 3, axis=-1)
    o = stick_breaking(q.reshape(B, S, H_C, DH_C), k.reshape(B, S, H_C, DH_C),
                       v.reshape(B, S, H_C, DH_C))
    return o.reshape(B, S, MIX_ODD) @ w_out


def hier_moe(h, w_rg, b_rg, w_re, b_re, w_gate, w_up, w_down):
    B, S, D = h.shape
    hf = h.reshape(B * S, D)
    n_tok = B * S
    p_g = jax.nn.softmax((hf @ w_rg + b_rg).astype(jnp.float32), axis=-1)
    g_val, g_idx = lax.top_k(p_g, 1)
    g_sel = g_idx[:, 0]
    e_logits = (hf @ w_re + b_re).astype(jnp.float32).reshape(n_tok, N_GROUPS, EXPERTS_PER_GROUP)
    e_in_group = jnp.take_along_axis(e_logits, g_sel[:, None, None], axis=1)[:, 0]
    top_v, top_i = lax.top_k(jax.nn.softmax(e_in_group, axis=-1), TOP_K)
    top_v = top_v / jnp.sum(top_v, axis=-1, keepdims=True)
    weights = g_val * top_v
    expert_idx = g_sel[:, None] * EXPERTS_PER_GROUP + top_i
    combine = jnp.sum(jax.nn.one_hot(expert_idx, N_EXPERTS, dtype=jnp.float32)
                      * weights[..., None], axis=1).astype(h.dtype)
    nb = n_tok // TOKEN_BLOCK

    def run(args):
        xb, cb = args
        g = jnp.einsum('td,edf->tef', xb, w_gate)
        u = jnp.einsum('td,edf->tef', xb, w_up)
        act = jax.nn.silu(g) * u * cb[:, :, None]
        return jnp.einsum('tef,efd->td', act, w_down)

    y = lax.map(run, (hf.reshape(nb, TOKEN_BLOCK, D), combine.reshape(nb, TOKEN_BLOCK, N_EXPERTS)))
    return y.reshape(B, S, D)


def setup_inputs(seed: int = 0) -> dict:
    key = jax.random.key(seed)
    ks = jax.random.split(key, 26)
    f32 = jnp.float32

    def nrm(k, shape, scale):
        return jax.random.normal(k, shape, f32) * scale

    D = D_MODEL
    return {
        "x": nrm(ks[0], (BATCH, SEQ, D), 1.0),
        "c": nrm(ks[1], (BATCH, D), 1.0),
        "w_ada": nrm(ks[2], (DEPTH, D, 6 * D), 0.5 * D ** -0.5),
        "b_ada": nrm(ks[3], (DEPTH, 6 * D), 0.02),
        "norm_mix": 1.0 + nrm(ks[4], (DEPTH, D), 0.05),
        "norm_ffn": 1.0 + nrm(ks[5], (DEPTH, D), 0.05),
        "norm_final": 1.0 + nrm(ks[6], (D,), 0.05),
        "rel_bias": nrm(ks[7], (N_BUCKETS, H_A), 0.5),
        "even_w_in": nrm(ks[8], (N_EVEN, D, EVEN_COLS), D ** -0.5),
        "even_lambda": nrm(ks[9], (N_EVEN, 4, DH_A), 0.1),
        "even_subln": 1.0 + nrm(ks[10], (N_EVEN, DV_A), 0.05),
        "even_w_gk2": nrm(ks[11], (N_EVEN, GATE_RANK, H_B * DK_B), GATE_RANK ** -0.5),
        "even_b_gk": nrm(ks[12], (N_EVEN, H_B * DK_B), 0.1),
        "even_gla_norm": 1.0 + nrm(ks[13], (N_EVEN, DV_B), 0.05),
        "even_w_out": nrm(ks[14], (N_EVEN, MIX_EVEN, D), MIX_EVEN ** -0.5),
        "odd_w_in": nrm(ks[15], (N_ODD, D, 3 * MIX_ODD), D ** -0.5),
        "odd_w_out": nrm(ks[16], (N_ODD, MIX_ODD, D), MIX_ODD ** -0.5),
        "router_group_w": nrm(ks[17], (DEPTH, D, N_GROUPS), D ** -0.5),
        "router_group_b": nrm(ks[18], (DEPTH, N_GROUPS), 0.01),
        "router_expert_w": nrm(ks[19], (DEPTH, D, N_EXPERTS), D ** -0.5),
        "router_expert_b": nrm(ks[20], (DEPTH, N_EXPERTS), 0.01),
        "expert_w_gate": nrm(ks[21], (DEPTH, N_EXPERTS, D, D_EXPERT), D ** -0.5),
        "expert_w_up": nrm(ks[22], (DEPTH, N_EXPERTS, D, D_EXPERT), D ** -0.5),
        "expert_w_down": nrm(ks[23], (DEPTH, N_EXPERTS, D_EXPERT, D), D_EXPERT ** -0.5),
    }


def reference(x, c, w_ada, b_ada, norm_mix, norm_ffn, norm_final, rel_bias,
              even_w_in, even_lambda, even_subln, even_w_gk2, even_b_gk, even_gla_norm,
              even_w_out, odd_w_in, odd_w_out, router_group_w, router_group_b,
              router_expert_w, router_expert_b, expert_w_gate, expert_w_up, expert_w_down):
    mod_all = jnp.einsum('bd,lde->lbe', jax.nn.silu(c), w_ada) + b_ada[:, None, :]
    for l in range(DEPTH):
        shift1, scale1, gate1, shift2, scale2, gate2 = jnp.split(mod_all[l][:, None, :], 6, axis=-1)
        h = rmsnorm(x, norm_mix[l]) * (1.0 + scale1) + shift1
        i = l // 2
        if l % 2 == 0:
            m = even_mixer(h, even_w_in[i], even_lambda[i], even_subln[i], even_w_gk2[i],
                           even_b_gk[i], even_gla_norm[i], even_w_out[i], rel_bias, l)
        else:
            m = odd_mixer(h, odd_w_in[i], odd_w_out[i])
        x = x + gate1 * m
        h = rmsnorm(x, norm_ffn[l]) * (1.0 + scale2) + shift2
        x = x + gate2 * hier_moe(h, router_group_w[l], router_group_b[l], router_expert_w[l],
                                 router_expert_b[l], expert_w_gate[l], expert_w_up[l],
                                 expert_w_down[l])
    return rmsnorm(x, norm_final)
```

```python
import functools
import math

import numpy as np
import jax
import jax.numpy as jnp
from jax import lax
from jax.experimental import pallas as pl
from jax.experimental.pallas import tpu as pltpu

F32 = jnp.float32
BF16 = jnp.bfloat16

D_MODEL = 1024
DEPTH = 4
CHUNK = 64
EPS = 1e-6
H_A, DH_A, DV_A = 4, 64, 128
H_B, DK_B, DV_B = 4, 64, 128
GATE_RANK = 16
GATE_TAU = 16.0
H_C, DH_C = 16, 64
N_BUCKETS = 32
MAX_DISTANCE = 128
N_GROUPS = 4
EXPERTS_PER_GROUP = 8
N_EXPERTS = N_GROUPS * EXPERTS_PER_GROUP
D_EXPERT = 512
MIX = 1024

LANES = 128
V7X_VMEM_BYTES = 64 * 1024 * 1024
NEG_BIG = -1e30
EXP_ZERO_BELOW = -104.0

TOKEN_TILE = 512
ATTN_TILE = 128
GLA_STEP = 512
EXPERT_TILE = 256


def _vmem_limit(block_bytes, scratch_bytes=0):
    est = 2 * block_bytes + scratch_bytes + 16 * 1024 * 1024
    return int(min(est, V7X_VMEM_BYTES - 8 * 1024 * 1024))


def _nt(a, b):
    return lax.dot_general(a, b, (((1,), (1,)), ((), ())), preferred_element_type=F32)


def _tn(a, b):
    return lax.dot_general(a, b, (((0,), (0,)), ((), ())), preferred_element_type=F32)


def _dot(a, b):
    return jnp.dot(a, b, preferred_element_type=F32)


def _split(x):
    hi = x.astype(BF16)
    lo = (x - hi.astype(F32)).astype(BF16)
    return hi, lo


def _dot3(a, b):
    ah, al = _split(a)
    bh, bl = _split(b)
    return _dot(ah, bh) + _dot(al, bh) + _dot(ah, bl)


def _rms(x, g):
    return x * lax.rsqrt(jnp.mean(x * x, axis=-1, keepdims=True) + EPS) * g


def _softplus(z):
    return jnp.maximum(z, 0.0) + jnp.log(1.0 + jnp.exp(-jnp.abs(z)))


def _silu(x):
    return x / (1.0 + jnp.exp(-x))


def _mod_kernel(c_ref, w_ref, b_ref, o_ref):
    c = c_ref[...]
    o_ref[0] = _dot3(_silu(c), w_ref[0]) + b_ref[0]


def _modulation(c, w_ada, b_ada):
    depth, d, n = w_ada.shape
    bsz = c.shape[0]
    tn = 1536
    return pl.pallas_call(
        _mod_kernel,
        out_shape=jax.ShapeDtypeStruct((depth, bsz, n), F32),
        grid=(depth, n // tn),
        in_specs=[
            pl.BlockSpec((bsz, d), lambda l, j: (0, 0)),
            pl.BlockSpec((1, d, tn), lambda l, j: (l, 0, j)),
            pl.BlockSpec((1, 1, tn), lambda l, j: (l, 0, j)),
        ],
        out_specs=pl.BlockSpec((1, bsz, tn), lambda l, j: (l, 0, j)),
        compiler_params=pltpu.CompilerParams(
            dimension_semantics=("arbitrary", "arbitrary"),
            vmem_limit_bytes=_vmem_limit(d * tn * 4 + bsz * (d + tn) * 4)),
        name="adaln_modulation",
    )(c, w_ada, b_ada.reshape(depth, 1, n))


def _proj_kernel(*refs, has_res, n_main, has_gate):
    it = iter(refs)
    x_ref = next(it)
    if has_res:
        y_ref = next(it)
        pmod_ref = next(it)
    g_ref = next(it)
    mod_ref = next(it)
    w_ref = next(it)
    wg_ref = next(it) if has_gate else None
    xo_ref = next(it) if has_res else None
    o_ref = next(it)
    og_ref = next(it) if has_gate else None

    x = x_ref[0]
    if has_res:
        x = x + pmod_ref[0, 5:6, :] * y_ref[0]
        xo_ref[0] = x
    h = _rms(x, g_ref[...]) * (1.0 + mod_ref[0, 1:2, :]) + mod_ref[0, 0:1, :]
    hb = h.astype(BF16)
    step = 512
    for n0 in range(0, n_main, step):
        o_ref[0, :, n0:n0 + step] = _dot(hb, w_ref[:, n0:n0 + step]).astype(BF16)
    if has_gate:
        og_ref[0] = _dot(hb, wg_ref[...])


def _in_projection(x, y, prev_mod, g, mod, w, w_gate):
    bsz, s, d = x.shape
    tm = TOKEN_TILE
    has_res = y is not None
    has_gate = w_gate is not None
    n_main = w.shape[1]
    tok = lambda b, i: (b, i, 0)
    per_b = lambda b, i: (b, 0, 0)
    const2 = lambda b, i: (0, 0)
    in_specs = [pl.BlockSpec((1, tm, d), tok)]
    args = [x]
    if has_res:
        in_specs += [pl.BlockSpec((1, tm, d), tok), pl.BlockSpec((1, 6, d), per_b)]
        args += [y, prev_mod]
    in_specs += [pl.BlockSpec((1, d), const2), pl.BlockSpec((1, 6, d), per_b),
                 pl.BlockSpec((d, n_main), const2)]
    args += [g.reshape(1, d), mod, w]
    if has_gate:
        in_specs.append(pl.BlockSpec((d, LANES), const2))
        args.append(w_gate)
    out_shape, out_specs = [], []
    if has_res:
        out_shape.append(jax.ShapeDtypeStruct((bsz, s, d), F32))
        out_specs.append(pl.BlockSpec((1, tm, d), tok))
    out_shape.append(jax.ShapeDtypeStruct((bsz, s, n_main), BF16))
    out_specs.append(pl.BlockSpec((1, tm, n_main), tok))
    if has_gate:
        out_shape.append(jax.ShapeDtypeStruct((bsz, s, LANES), F32))
        out_specs.append(pl.BlockSpec((1, tm, LANES), tok))
    block_bytes = (tm * d * 4 * (3 if has_res else 1) + d * n_main * 2 + tm * n_main * 2
                   + d * LANES * 2 + tm * LANES * 4)
    outs = pl.pallas_call(
        functools.partial(_proj_kernel, has_res=has_res, n_main=n_main, has_gate=has_gate),
        out_shape=out_shape,
        grid=(bsz, s // tm),
        in_specs=in_specs,
        out_specs=out_specs,
        compiler_params=pltpu.CompilerParams(
            dimension_semantics=("arbitrary", "arbitrary"),
            vmem_limit_bytes=_vmem_limit(block_bytes, tm * d * 8)),
        name="norm_mod_in_projection",
    )(*args)
    outs = list(outs)
    x_new = outs.pop(0) if has_res else x
    proj = outs.pop(0)
    gate = outs.pop(0) if has_gate else None
    return x_new, proj, gate


def _t5_bucket(rel):
    nb = N_BUCKETS // 2
    max_exact = nb // 2
    ret = jnp.where(rel > 0, nb, 0)
    n = jnp.abs(rel)
    nf = jnp.maximum(n, 1).astype(F32)
    large = max_exact + (jnp.log(nf / max_exact) / math.log(MAX_DISTANCE / max_exact)
                         * (nb - max_exact)).astype(jnp.int32)
    large = jnp.minimum(large, nb - 1)
    return ret + jnp.where(n < max_exact, n, large)


def _diff_bias_tiles(rel_bias):
    t = ATTN_TILE
    assert t >= MAX_DISTANCE, "tiles two or more away must lie beyond the last bucket edge"
    qp = jnp.arange(t)[:, None]
    kp = jnp.arange(t)[None, :]
    tiles = []
    for off in (0, 1, 2):
        rel = (kp - off * t) - qp
        b = jnp.transpose(rel_bias.astype(F32)[_t5_bucket(rel)], (2, 0, 1))
        if off == 0:
            b = jnp.where((kp // CHUNK) <= (qp // CHUNK), b, NEG_BIG)
        tiles.append(b)
    tab = jnp.stack(tiles, axis=1)
    return jnp.concatenate([tab, tab], axis=2)


def _diff_attn_kernel(q_ref, k_ref, v_ref, bias_ref, lam_ref, g_ref, o_ref, *, lambda_init):
    t = ATTN_TILE
    i = pl.program_id(2)
    q = q_ref[0]
    lane = lax.broadcasted_iota(jnp.int32, (t, LANES), 1)
    scale = jnp.asarray(DH_A ** -0.5, BF16)
    zero = jnp.zeros_like(q)
    qq = jnp.concatenate([jnp.where(lane < DH_A, q, zero), jnp.where(lane >= DH_A, q, zero)],
                         axis=0) * scale

    def body(j, carry):
        m, l, acc = carry
        r0 = pl.multiple_of(j * t, t)
        kb = k_ref[0, pl.ds(r0, t), :]
        vb = v_ref[0, pl.ds(r0, t), :]
        s = _nt(qq, kb) + bias_ref[0, jnp.minimum(i - j, 2)]
        m_new = jnp.maximum(m, jnp.max(s, axis=-1, keepdims=True))
        alpha = jnp.exp(m - m_new)
        p = jnp.exp(s - m_new)
        l = alpha * l + jnp.sum(p, axis=-1, keepdims=True)
        acc = alpha * acc + _dot(p.astype(BF16), vb)
        return m_new, l, acc

    init = (jnp.full((2 * t, 1), NEG_BIG, F32), jnp.zeros((2 * t, 1), F32),
            jnp.zeros((2 * t, DV_A), F32))
    _, l, acc = lax.fori_loop(0, i + 1, body, init)
    o = acc / l
    lp = lam_ref[...].astype(F32)
    lam = (jnp.exp(jnp.sum(lp[0:1] * lp[1:2], axis=-1, keepdims=True))
           - jnp.exp(jnp.sum(lp[2:3] * lp[3:4], axis=-1, keepdims=True)) + lambda_init)
    w = o[:t] - lam * o[t:]
    o_ref[0] = (_rms(w, g_ref[...]) * (1.0 - lambda_init)).astype(o_ref.dtype)


def _diff_attention(proj, bias_tiles, lam_p, subln_g, layer_idx):
    bsz, s, _ = proj.shape
    t = ATTN_TILE
    lambda_init = 0.8 - 0.6 * math.exp(-0.3 * layer_idx)
    return pl.pallas_call(
        functools.partial(_diff_attn_kernel, lambda_init=lambda_init),
        out_shape=jax.ShapeDtypeStruct((bsz, s, H_A * DV_A), BF16),
        grid=(bsz, H_A, s // t),
        in_specs=[
            pl.BlockSpec((1, t, LANES), lambda b, h, i: (b, i, h)),
            pl.BlockSpec((1, s, LANES), lambda b, h, i: (b, 0, H_A + h)),
            pl.BlockSpec((1, s, LANES), lambda b, h, i: (b, 0, 2 * H_A + h)),
            pl.BlockSpec((1, 3, 2 * t, t), lambda b, h, i: (h, 0, 0, 0)),
            pl.BlockSpec((4, DH_A), lambda b, h, i: (0, 0)),
            pl.BlockSpec((1, DV_A), lambda b, h, i: (0, 0)),
        ],
        out_specs=pl.BlockSpec((1, t, DV_A), lambda b, h, i: (b, i, h)),
        compiler_params=pltpu.CompilerParams(
            dimension_semantics=("arbitrary", "arbitrary", "arbitrary"),
            vmem_limit_bytes=_vmem_limit(2 * s * LANES * 2 + 3 * 2 * t * t * 4 + 4 * t * LANES)),
        name="diff_attention",
    )(proj, proj, proj, bias_tiles, lam_p, subln_g.reshape(1, DV_A))


def _gla_kernel(q_ref, k_ref, v_ref, r_ref, bg_ref, wgk_ref, bgk_ref, g_ref, o_ref, st_ref):
    c = CHUNK

    @pl.when(pl.program_id(1) == 0)
    def _():
        st_ref[...] = jnp.zeros_like(st_ref)

    row = lax.broadcasted_iota(jnp.int32, (c, c), 0)
    col = lax.broadcasted_iota(jnp.int32, (c, c), 1)
    causal = col <= row
    tri = jnp.where(causal, 1.0, 0.0).astype(BF16)
    lane = lax.broadcasted_iota(jnp.int32, (c, LANES), 1)
    wgk = wgk_ref[...]
    bgk = bgk_ref[...]
    g = g_ref[...]

    def chunk(ci, carry):
        r0 = pl.multiple_of(ci * c, c)
        bg = bg_ref[0, pl.ds(r0, c), :]
        for hp in range(H_B // 2):
            cols = slice(hp * LANES, (hp + 1) * LANES)
            la = -_softplus(-(_dot3(bg, wgk[:, cols]) + bgk[:, cols])) / GATE_TAU
            la_hi, la_lo = _split(la)
            b = _dot(tri, la_hi) + _dot(tri, la_lo)
            b_last = b[c - 1:c, :]
            qf = q_ref[0, pl.ds(r0, c), cols].astype(F32) * (DK_B ** -0.5)
            kf = k_ref[0, pl.ds(r0, c), cols].astype(F32)
            q_dec = qf * jnp.exp(b)
            k_inv = (kf * jnp.exp(-b)).astype(BF16)
            k_dec = kf * jnp.exp(b_last - b)
            decay = jnp.exp(b_last)
            for hh in range(2):
                h = 2 * hp + hh
                own = (lane >= hh * DK_B) & (lane < (hh + 1) * DK_B)
                qd = jnp.where(own, q_dec, 0.0).astype(BF16)
                kd = jnp.where(own, k_dec, 0.0).astype(BF16)
                vcols = slice(h * DV_B, (h + 1) * DV_B)
                vb = v_ref[0, pl.ds(r0, c), vcols]
                sc = jnp.where(causal, _nt(qd, k_inv), 0.0)
                st = st_ref[h]
                o = _dot(sc.astype(BF16), vb) + _nt(qd, st.astype(BF16))
                st_ref[h] = st * decay + _tn(vb, kd)
                rr = r_ref[0, pl.ds(r0, c), vcols].astype(F32)
                o_ref[0, pl.ds(r0, c), vcols] = (_rms(o, g) * _silu(rr)).astype(o_ref.dtype)
        return carry

    lax.fori_loop(0, q_ref.shape[1] // c, chunk, 0)


def _gla(proj, gate, w_gk2, b_gk, gla_g):
    bsz, s, _ = proj.shape
    ts = GLA_STEP
    nk = H_B * DK_B
    nv = H_B * DV_B
    w_pad = jnp.zeros((LANES, nk), F32).at[:GATE_RANK].set(w_gk2.astype(F32))
    return pl.pallas_call(
        _gla_kernel,
        out_shape=jax.ShapeDtypeStruct((bsz, s, nv), BF16),
        grid=(bsz, s // ts),
        in_specs=[
            pl.BlockSpec((1, ts, nk), lambda b, i: (b, i, 1536 // nk)),
            pl.BlockSpec((1, ts, nk), lambda b, i: (b, i, 1792 // nk)),
            pl.BlockSpec((1, ts, nv), lambda b, i: (b, i, 2048 // nv)),
            pl.BlockSpec((1, ts, nv), lambda b, i: (b, i, 2560 // nv)),
            pl.BlockSpec((1, ts, LANES), lambda b, i: (b, i, 0)),
            pl.BlockSpec((LANES, nk), lambda b, i: (0, 0)),
            pl.BlockSpec((1, nk), lambda b, i: (0, 0)),
            pl.BlockSpec((1, DV_B), lambda b, i: (0, 0)),
        ],
        out_specs=pl.BlockSpec((1, ts, nv), lambda b, i: (b, i, 0)),
        scratch_shapes=[pltpu.VMEM((H_B, DV_B, LANES), F32)],
        compiler_params=pltpu.CompilerParams(
            dimension_semantics=("arbitrary", "arbitrary"),
            vmem_limit_bytes=_vmem_limit(ts * (2 * nk + 3 * nv) * 2 + ts * LANES * 4,
                                         H_B * DV_B * LANES * 4)),
        name="gated_linear_attention",
    )(proj, proj, proj, proj, gate, w_pad, b_gk.reshape(1, nk), gla_g.reshape(1, DV_B))


def _stick_kernel(q_ref, k_ref, v_ref, o_ref):
    t = ATTN_TILE
    i = pl.program_id(2)
    q = q_ref[0]
    lane = lax.broadcasted_iota(jnp.int32, (t, LANES), 1)
    row = lax.broadcasted_iota(jnp.int32, (t, t), 0)
    col = lax.broadcasted_iota(jnp.int32, (t, t), 1)
    below = col < row
    later = jnp.where(row > col, 1.0, 0.0).astype(BF16)
    scale = jnp.asarray(DH_C ** -0.5, BF16)
    zero = jnp.zeros_like(q)
    outs = []
    for hh in range(2):
        own = (lane >= hh * DH_C) & (lane < (hh + 1) * DH_C)
        qh = jnp.where(own, q, zero) * scale

        def tile(j, run, acc, diagonal):
            r0 = pl.multiple_of(j * t, t)
            kb = k_ref[0, pl.ds(r0, t), :]
            vb = v_ref[0, pl.ds(r0, t), :]
            z = _nt(qh, kb)
            sp = _softplus(z)
            log_1m = -sp
            if diagonal:
                log_1m = jnp.where(below, log_1m, 0.0)
            hi, lo = _split(log_1m)
            suffix = _dot(hi, later) + _dot(lo, later) + run
            a = jnp.exp((z - sp) + suffix)
            if diagonal:
                a = jnp.where(below, a, 0.0)
            acc = acc + _dot(a.astype(BF16), vb)
            run = run + jnp.sum(log_1m, axis=-1, keepdims=True)
            return run, acc

        run, acc = tile(i, jnp.zeros((t, 1), F32), jnp.zeros((t, LANES), F32), True)

        def body(n, carry):
            return tile(i - 1 - n, carry[0], carry[1], False)

        run, acc = lax.fori_loop(0, i, body, (run, acc))
        outs.append(acc)
    o_ref[0] = jnp.where(lane < DH_C, outs[0], outs[1]).astype(o_ref.dtype)


def _stick_breaking(proj):
    bsz, s, _ = proj.shape
    t = ATTN_TILE
    n_tiles = H_C * DH_C // LANES
    return pl.pallas_call(
        _stick_kernel,
        out_shape=jax.ShapeDtypeStruct((bsz, s, H_C * DH_C), BF16),
        grid=(bsz, n_tiles, s // t),
        in_specs=[
            pl.BlockSpec((1, t, LANES), lambda b, h, i: (b, i, h)),
            pl.BlockSpec((1, s, LANES), lambda b, h, i: (b, 0, n_tiles + h)),
            pl.BlockSpec((1, s, LANES), lambda b, h, i: (b, 0, 2 * n_tiles + h)),
        ],
        out_specs=pl.BlockSpec((1, t, LANES), lambda b, h, i: (b, i, h)),
        compiler_params=pltpu.CompilerParams(
            dimension_semantics=("arbitrary", "arbitrary", "arbitrary"),
            vmem_limit_bytes=_vmem_limit(2 * s * LANES * 2 + 4 * t * LANES)),
        name="stick_breaking_attention",
    )(proj, proj, proj)


def _route(logits):
    lane_i = lax.broadcasted_iota(jnp.int32, logits.shape, 1)
    lane = lane_i.astype(F32)
    group_of_lane = (lane_i // EXPERTS_PER_GROUP).astype(F32)
    big = jnp.float32(1 << 20)
    neg = jnp.float32(-jnp.inf)
    is_group = (lane_i >= N_EXPERTS) & (lane_i < N_EXPERTS + N_GROUPS)
    gl = jnp.where(is_group, logits, neg)
    gmax = jnp.max(gl, axis=-1, keepdims=True)
    g_val = 1.0 / jnp.sum(jnp.exp(gl - gmax), axis=-1, keepdims=True)
    g_sel = jnp.min(jnp.where(gl == gmax, lane - N_EXPERTS, big), axis=-1, keepdims=True)
    in_group = (lane_i < N_EXPERTS) & (group_of_lane == g_sel)
    el = jnp.where(in_group, logits, neg)
    emax = jnp.max(el, axis=-1, keepdims=True)
    esum = jnp.sum(jnp.exp(el - emax), axis=-1, keepdims=True)
    i1 = jnp.min(jnp.where(el == emax, lane, big), axis=-1, keepdims=True)
    el2 = jnp.where(lane == i1, neg, el)
    emax2 = jnp.max(el2, axis=-1, keepdims=True)
    i2 = jnp.min(jnp.where(el2 == emax2, lane, big), axis=-1, keepdims=True)
    v1 = 1.0 / esum
    v2 = jnp.exp(emax2 - emax) / esum
    tot = v1 + v2
    w1 = g_val * (v1 / tot)
    w2 = g_val * (v2 / tot)
    out = jnp.where(lane_i == 0, i1, 0.0)
    out = jnp.where(lane_i == 1, i2, out)
    out = jnp.where(lane_i == 2, w1, out)
    out = jnp.where(lane_i == 3, w2, out)
    return out


def _out_kernel(*refs, n_o):
    o_refs = refs[:n_o]
    w_refs = refs[n_o:2 * n_o]
    x_ref, mod_ref, g_ref, wr_ref, br_ref, xo_ref, h_ref, r_ref = refs[2 * n_o:]
    m = _dot(o_refs[0][0], w_refs[0][...])
    for a in range(1, n_o):
        m = m + _dot(o_refs[a][0], w_refs[a][...])
    x = x_ref[0] + mod_ref[0, 2:3, :] * m
    xo_ref[0] = x
    h = _rms(x, g_ref[...]) * (1.0 + mod_ref[0, 4:5, :]) + mod_ref[0, 3:4, :]
    h_ref[0] = h.astype(BF16)
    r_ref[0] = _route(_dot3(h, wr_ref[...]) + br_ref[...])


def _out_projection(o_list, w_list, x, mod, g, w_router, b_router):
    bsz, s, d = x.shape
    tm = TOKEN_TILE
    n_o = len(o_list)
    tok = lambda b, i: (b, i, 0)
    const2 = lambda b, i: (0, 0)
    in_specs = ([pl.BlockSpec((1, tm, o.shape[2]), tok) for o in o_list]
                + [pl.BlockSpec(w.shape, const2) for w in w_list]
                + [pl.BlockSpec((1, tm, d), tok), pl.BlockSpec((1, 6, d), lambda b, i: (b, 0, 0)),
                   pl.BlockSpec((1, d), const2), pl.BlockSpec((d, LANES), const2),
                   pl.BlockSpec((1, LANES), const2)])
    block_bytes = (sum(tm * o.shape[2] * 2 for o in o_list) + sum(w.size * 2 for w in w_list)
                   + 2 * tm * d * 4 + tm * d * 2 + d * LANES * 4 + tm * LANES * 4)
    return pl.pallas_call(
        functools.partial(_out_kernel, n_o=n_o),
        out_shape=[jax.ShapeDtypeStruct((bsz, s, d), F32),
                   jax.ShapeDtypeStruct((bsz, s, d), BF16),
                   jax.ShapeDtypeStruct((bsz, s, LANES), F32)],
        grid=(bsz, s // tm),
        in_specs=in_specs,
        out_specs=[pl.BlockSpec((1, tm, d), tok), pl.BlockSpec((1, tm, d), tok),
                   pl.BlockSpec((1, tm, LANES), tok)],
        compiler_params=pltpu.CompilerParams(
            dimension_semantics=("arbitrary", "arbitrary"),
            vmem_limit_bytes=_vmem_limit(block_bytes, tm * d * 12)),
        name="out_projection_norm_router",
    )(*o_list, *w_list, x, mod, g.reshape(1, d), w_router, b_router)


def _expert_kernel(te_ref, nu_ref, x_ref, cw_ref, wg_ref, wu_ref, wd_ref, o_ref):
    t = pl.program_id(0)

    @pl.when(t < nu_ref[0])
    def _():
        x = x_ref[...]
        gt = _dot(x, wg_ref[0])
        up = _dot(x, wu_ref[0])
        act = (_silu(gt) * up * cw_ref[...]).astype(BF16)
        o_ref[...] = _dot(act, wd_ref[0])

    @pl.when(t >= nu_ref[0])
    def _():
        o_ref[...] = jnp.zeros_like(o_ref)


def _grouped_experts(xs, row_w, tile_expert, n_used, w_gate, w_up, w_down):
    p, d = xs.shape
    tm = EXPERT_TILE
    f = w_gate.shape[2]
    grid_spec = pltpu.PrefetchScalarGridSpec(
        num_scalar_prefetch=2,
        grid=(p // tm,),
        in_specs=[
            pl.BlockSpec((tm, d), lambda t, te, nu: (t, 0)),
            pl.BlockSpec((tm, 1), lambda t, te, nu: (t, 0)),
            pl.BlockSpec((1, d, f), lambda t, te, nu: (te[t], 0, 0)),
            pl.BlockSpec((1, d, f), lambda t, te, nu: (te[t], 0, 0)),
            pl.BlockSpec((1, f, d), lambda t, te, nu: (te[t], 0, 0)),
        ],
        out_specs=pl.BlockSpec((tm, d), lambda t, te, nu: (t, 0)),
    )
    block_bytes = tm * d * 2 + tm * LANES * 4 + 3 * d * f * 2 + tm * d * 4
    return pl.pallas_call(
        _expert_kernel,
        out_shape=jax.ShapeDtypeStruct((p, d), F32),
        grid_spec=grid_spec,
        compiler_params=pltpu.CompilerParams(
            dimension_semantics=("arbitrary",),
            vmem_limit_bytes=_vmem_limit(block_bytes, tm * f * 16)),
        name="grouped_swiglu_experts",
    )(tile_expert, n_used, xs, row_w, w_gate, w_up, w_down)


def _dispatch(route, n_tok):
    tm = EXPERT_TILE
    e = jnp.concatenate([route[:, 0], route[:, 1]]).astype(jnp.int32)
    w = jnp.concatenate([route[:, 2], route[:, 3]])
    tok = jnp.concatenate([jnp.arange(n_tok, dtype=jnp.int32)] * 2)
    onehot = (e[:, None] == jnp.arange(N_EXPERTS, dtype=jnp.int32)[None, :]).astype(jnp.int32)
    csum = jnp.cumsum(onehot, axis=0)
    rank = jnp.take_along_axis(csum, e[:, None], axis=1)[:, 0] - 1
    counts = csum[-1]
    padded = ((counts + tm - 1) // tm) * tm
    ends = jnp.cumsum(padded)
    starts = ends - padded
    pos = starts[e] + rank
    p = 2 * n_tok + N_EXPERTS * tm
    row_tok = jnp.zeros((p,), jnp.int32).at[pos].set(tok)
    row_w = jnp.zeros((p,), F32).at[pos].set(w)
    tile_start = jnp.arange(p // tm, dtype=jnp.int32) * tm
    tile_expert = jnp.minimum(jnp.searchsorted(ends, tile_start, side="right"),
                              N_EXPERTS - 1).astype(jnp.int32)
    n_used = (ends[-1] // tm).astype(jnp.int32).reshape(1)
    return row_tok, row_w.reshape(p, 1), tile_expert, n_used, pos


def _final_kernel(x_ref, y_ref, mod_ref, g_ref, o_ref):
    x = x_ref[0] + mod_ref[0, 5:6, :] * y_ref[0]
    o_ref[0] = _rms(x, g_ref[...])


def _final_norm(x, y, mod, g):
    bsz, s, d = x.shape
    tm = TOKEN_TILE
    tok = lambda b, i: (b, i, 0)
    return pl.pallas_call(
        _final_kernel,
        out_shape=jax.ShapeDtypeStruct((bsz, s, d), F32),
        grid=(bsz, s // tm),
        in_specs=[pl.BlockSpec((1, tm, d), tok), pl.BlockSpec((1, tm, d), tok),
                  pl.BlockSpec((1, 6, d), lambda b, i: (b, 0, 0)),
                  pl.BlockSpec((1, d), lambda b, i: (0, 0))],
        out_specs=pl.BlockSpec((1, tm, d), tok),
        compiler_params=pltpu.CompilerParams(
            dimension_semantics=("arbitrary", "arbitrary"),
            vmem_limit_bytes=_vmem_limit(3 * tm * d * 4)),
        name="residual_final_norm",
    )(x, y, mod, g.reshape(1, d))


def kernel(x, c, w_ada, b_ada, norm_mix, norm_ffn, norm_final, rel_bias, even_w_in, even_lambda, even_subln, even_w_gk2, even_b_gk, even_gla_norm, even_w_out, odd_w_in, odd_w_out, router_group_w, router_group_b, router_expert_w, router_expert_b, expert_w_gate, expert_w_up, expert_w_down):
    bsz, s, d = x.shape
    n_tok = bsz * s
    assert s % TOKEN_TILE == 0 and s % GLA_STEP == 0 and s % ATTN_TILE == 0
    mod_all = _modulation(c, w_ada, b_ada).reshape(DEPTH, bsz, 6, d)
    bias_tiles = _diff_bias_tiles(rel_bias)
    n_even_main = 3072
    y = None
    prev_mod = None
    for l in range(DEPTH):
        mod = mod_all[l]
        i = l // 2
        if l % 2 == 0:
            w_in = even_w_in[i]
            w_main = w_in[:, :n_even_main].astype(BF16)
            w_gate = jnp.zeros((d, LANES), BF16).at[:, :GATE_RANK].set(
                w_in[:, n_even_main:].astype(BF16))
            x, proj, gate = _in_projection(x, y, prev_mod, norm_mix[l], mod, w_main, w_gate)
            oa = _diff_attention(proj, bias_tiles, even_lambda[i], even_subln[i], l)
            ob = _gla(proj, gate, even_w_gk2[i], even_b_gk[i], even_gla_norm[i])
            w_out = even_w_out[i].astype(BF16)
            o_list = [oa, ob]
            w_list = [w_out[:H_A * DV_A], w_out[H_A * DV_A:]]
        else:
            x, proj, _ = _in_projection(x, y, prev_mod, norm_mix[l], mod,
                                        odd_w_in[i].astype(BF16), None)
            o_list = [_stick_breaking(proj)]
            w_list = [odd_w_out[i].astype(BF16)]
        w_router = jnp.zeros((d, LANES), F32)
        w_router = w_router.at[:, :N_EXPERTS].set(router_expert_w[l].astype(F32))
        w_router = w_router.at[:, N_EXPERTS:N_EXPERTS + N_GROUPS].set(router_group_w[l].astype(F32))
        b_router = jnp.zeros((1, LANES), F32)
        b_router = b_router.at[0, :N_EXPERTS].set(router_expert_b[l].astype(F32))
        b_router = b_router.at[0, N_EXPERTS:N_EXPERTS + N_GROUPS].set(router_group_b[l].astype(F32))
        x, h2, route = _out_projection(o_list, w_list, x, mod, norm_ffn[l], w_router, b_router)
        row_tok, row_w, tile_expert, n_used, pos = _dispatch(route.reshape(n_tok, LANES), n_tok)
        xs = jnp.take(h2.reshape(n_tok, d), row_tok, axis=0)
        out = _grouped_experts(xs, row_w, tile_expert, n_used, expert_w_gate[l].astype(BF16),
                               expert_w_up[l].astype(BF16), expert_w_down[l].astype(BF16))
        y = (jnp.take(out, pos[:n_tok], axis=0) + jnp.take(out, pos[n_tok:], axis=0)
             ).reshape(bsz, s, d)
        prev_mod = mod
    return _final_norm(x, y, prev_mod, norm_final)
```

```python
import functools
import math

import numpy as np
import jax
import jax.numpy as jnp
from jax import lax
from jax.experimental import pallas as pl
from jax.experimental.pallas import tpu as pltpu

F32 = jnp.float32
BF16 = jnp.bfloat16

D_MODEL = 1024
DEPTH = 4
CHUNK = 64
EPS = 1e-6
H_A, DH_A, DV_A = 4, 64, 128
H_B, DK_B, DV_B = 4, 64, 128
GATE_RANK = 16
GATE_TAU = 16.0
H_C, DH_C = 16, 64
N_BUCKETS = 32
MAX_DISTANCE = 128
N_GROUPS = 4
EXPERTS_PER_GROUP = 8
N_EXPERTS = N_GROUPS * EXPERTS_PER_GROUP
D_EXPERT = 512
MIX = 1024

LANES = 128
V7X_VMEM_BYTES = 64 * 1024 * 1024
NEG_BIG = -1e30
EXP_ZERO_BELOW = -104.0

TOKEN_TILE = 512
ATTN_TILE = 128
STICK_Q_TILE = 256
GLA_STEP = 512
EXPERT_TILE = 256


def _vmem_limit(block_bytes, scratch_bytes=0):
    est = 2 * block_bytes + scratch_bytes + 16 * 1024 * 1024
    return int(min(est, V7X_VMEM_BYTES - 8 * 1024 * 1024))


def _nt(a, b):
    return lax.dot_general(a, b, (((1,), (1,)), ((), ())), preferred_element_type=F32)


def _tn(a, b):
    return lax.dot_general(a, b, (((0,), (0,)), ((), ())), preferred_element_type=F32)


def _dot(a, b):
    return jnp.dot(a, b, preferred_element_type=F32)


def _split(x):
    hi = x.astype(BF16)
    lo = (x - hi.astype(F32)).astype(BF16)
    return hi, lo


def _dot3(a, b):
    ah, al = _split(a)
    bh, bl = _split(b)
    return _dot(ah, bh) + _dot(al, bh) + _dot(ah, bl)


def _rms(x, g):
    return x * lax.rsqrt(jnp.mean(x * x, axis=-1, keepdims=True) + EPS) * g


def _softplus(z):
    return jnp.maximum(z, 0.0) + jnp.log(1.0 + jnp.exp(-jnp.abs(z)))


def _silu(x):
    return x / (1.0 + jnp.exp(-x))


def _mod_kernel(c_ref, w_ref, b_ref, o_ref):
    c = c_ref[...]
    o_ref[0] = _dot3(_silu(c), w_ref[0]) + b_ref[0]


def _modulation(c, w_ada, b_ada):
    depth, d, n = w_ada.shape
    bsz = c.shape[0]
    tn = 1536
    return pl.pallas_call(
        _mod_kernel,
        out_shape=jax.ShapeDtypeStruct((depth, bsz, n), F32),
        grid=(depth, n // tn),
        in_specs=[
            pl.BlockSpec((bsz, d), lambda l, j: (0, 0)),
            pl.BlockSpec((1, d, tn), lambda l, j: (l, 0, j)),
            pl.BlockSpec((1, 1, tn), lambda l, j: (l, 0, j)),
        ],
        out_specs=pl.BlockSpec((1, bsz, tn), lambda l, j: (l, 0, j)),
        compiler_params=pltpu.CompilerParams(
            dimension_semantics=("arbitrary", "arbitrary"),
            vmem_limit_bytes=_vmem_limit(d * tn * 4 + bsz * (d + tn) * 4)),
        name="adaln_modulation",
    )(c, w_ada, b_ada.reshape(depth, 1, n))


def _proj_kernel(*refs, has_res, n_main, has_gate):
    it = iter(refs)
    x_ref = next(it)
    if has_res:
        y_ref = next(it)
        pmod_ref = next(it)
    g_ref = next(it)
    mod_ref = next(it)
    w_ref = next(it)
    wg_ref = next(it) if has_gate else None
    xo_ref = next(it) if has_res else None
    o_ref = next(it)
    og_ref = next(it) if has_gate else None

    x = x_ref[0]
    if has_res:
        x = x + pmod_ref[0, 5:6, :] * y_ref[0]
        xo_ref[0] = x
    h = _rms(x, g_ref[...]) * (1.0 + mod_ref[0, 1:2, :]) + mod_ref[0, 0:1, :]
    hb = h.astype(BF16)
    step = 512
    for n0 in range(0, n_main, step):
        o_ref[0, :, n0:n0 + step] = _dot(hb, w_ref[:, n0:n0 + step]).astype(BF16)
    if has_gate:
        og_ref[0] = _dot(hb, wg_ref[...])


def _in_projection(x, y, prev_mod, g, mod, w, w_gate):
    bsz, s, d = x.shape
    tm = TOKEN_TILE
    has_res = y is not None
    has_gate = w_gate is not None
    n_main = w.shape[1]
    tok = lambda b, i: (b, i, 0)
    per_b = lambda b, i: (b, 0, 0)
    const2 = lambda b, i: (0, 0)
    in_specs = [pl.BlockSpec((1, tm, d), tok)]
    args = [x]
    if has_res:
        in_specs += [pl.BlockSpec((1, tm, d), tok), pl.BlockSpec((1, 6, d), per_b)]
        args += [y, prev_mod]
    in_specs += [pl.BlockSpec((1, d), const2), pl.BlockSpec((1, 6, d), per_b),
                 pl.BlockSpec((d, n_main), const2)]
    args += [g.reshape(1, d), mod, w]
    if has_gate:
        in_specs.append(pl.BlockSpec((d, LANES), const2))
        args.append(w_gate)
    out_shape, out_specs = [], []
    if has_res:
        out_shape.append(jax.ShapeDtypeStruct((bsz, s, d), F32))
        out_specs.append(pl.BlockSpec((1, tm, d), tok))
    out_shape.append(jax.ShapeDtypeStruct((bsz, s, n_main), BF16))
    out_specs.append(pl.BlockSpec((1, tm, n_main), tok))
    if has_gate:
        out_shape.append(jax.ShapeDtypeStruct((bsz, s, LANES), F32))
        out_specs.append(pl.BlockSpec((1, tm, LANES), tok))
    block_bytes = (tm * d * 4 * (3 if has_res else 1) + d * n_main * 2 + tm * n_main * 2
                   + d * LANES * 2 + tm * LANES * 4)
    outs = pl.pallas_call(
        functools.partial(_proj_kernel, has_res=has_res, n_main=n_main, has_gate=has_gate),
        out_shape=out_shape,
        grid=(bsz, s // tm),
        in_specs=in_specs,
        out_specs=out_specs,
        compiler_params=pltpu.CompilerParams(
            dimension_semantics=("arbitrary", "arbitrary"),
            vmem_limit_bytes=_vmem_limit(block_bytes, tm * d * 8)),
        name="norm_mod_in_projection",
    )(*args)
    outs = list(outs)
    x_new = outs.pop(0) if has_res else x
    proj = outs.pop(0)
    gate = outs.pop(0) if has_gate else None
    return x_new, proj, gate


def _t5_bucket(rel):
    nb = N_BUCKETS // 2
    max_exact = nb // 2
    ret = jnp.where(rel > 0, nb, 0)
    n = jnp.abs(rel)
    nf = jnp.maximum(n, 1).astype(F32)
    large = max_exact + (jnp.log(nf / max_exact) / math.log(MAX_DISTANCE / max_exact)
                         * (nb - max_exact)).astype(jnp.int32)
    large = jnp.minimum(large, nb - 1)
    return ret + jnp.where(n < max_exact, n, large)


def _diff_bias_tiles(rel_bias):
    t = ATTN_TILE
    assert t >= MAX_DISTANCE, "tiles two or more away must lie beyond the last bucket edge"
    qp = jnp.arange(t)[:, None]
    kp = jnp.arange(t)[None, :]
    tiles = []
    for off in (0, 1, 2):
        rel = (kp - off * t) - qp
        b = jnp.transpose(rel_bias.astype(F32)[_t5_bucket(rel)], (2, 0, 1))
        if off == 0:
            b = jnp.where((kp // CHUNK) <= (qp // CHUNK), b, NEG_BIG)
        tiles.append(b)
    tab = jnp.stack(tiles, axis=1)
    return jnp.concatenate([tab, tab], axis=2)


def _diff_attn_kernel(q_ref, k_ref, v_ref, bias_ref, lam_ref, g_ref, o_ref, *, lambda_init):
    t = ATTN_TILE
    i = pl.program_id(2)
    q = q_ref[0]
    lane = lax.broadcasted_iota(jnp.int32, (t, LANES), 1)
    scale = jnp.asarray(DH_A ** -0.5, BF16)
    zero = jnp.zeros_like(q)
    qq = jnp.concatenate([jnp.where(lane < DH_A, q, zero), jnp.where(lane >= DH_A, q, zero)],
                         axis=0) * scale

    def body(j, carry):
        m, l, acc = carry
        r0 = pl.multiple_of(j * t, t)
        kb = k_ref[0, pl.ds(r0, t), :]
        vb = v_ref[0, pl.ds(r0, t), :]
        s = _nt(qq, kb) + bias_ref[0, jnp.minimum(i - j, 2)]
        m_new = jnp.maximum(m, jnp.max(s, axis=-1, keepdims=True))
        alpha = jnp.exp(m - m_new)
        p = jnp.exp(s - m_new)
        l = alpha * l + jnp.sum(p, axis=-1, keepdims=True)
        acc = alpha * acc + _dot(p.astype(BF16), vb)
        return m_new, l, acc

    init = (jnp.full((2 * t, 1), NEG_BIG, F32), jnp.zeros((2 * t, 1), F32),
            jnp.zeros((2 * t, DV_A), F32))
    _, l, acc = lax.fori_loop(0, i + 1, body, init)
    o = acc / l
    lp = lam_ref[...].astype(F32)
    lam = (jnp.exp(jnp.sum(lp[0:1] * lp[1:2], axis=-1, keepdims=True))
           - jnp.exp(jnp.sum(lp[2:3] * lp[3:4], axis=-1, keepdims=True)) + lambda_init)
    w = o[:t] - lam * o[t:]
    o_ref[0] = (_rms(w, g_ref[...]) * (1.0 - lambda_init)).astype(o_ref.dtype)


def _diff_attention(proj, bias_tiles, lam_p, subln_g, layer_idx):
    bsz, s, _ = proj.shape
    t = ATTN_TILE
    lambda_init = 0.8 - 0.6 * math.exp(-0.3 * layer_idx)
    return pl.pallas_call(
        functools.partial(_diff_attn_kernel, lambda_init=lambda_init),
        out_shape=jax.ShapeDtypeStruct((bsz, s, H_A * DV_A), BF16),
        grid=(bsz, H_A, s // t),
        in_specs=[
            pl.BlockSpec((1, t, LANES), lambda b, h, i: (b, i, h)),
            pl.BlockSpec((1, s, LANES), lambda b, h, i: (b, 0, H_A + h)),
            pl.BlockSpec((1, s, LANES), lambda b, h, i: (b, 0, 2 * H_A + h)),
            pl.BlockSpec((1, 3, 2 * t, t), lambda b, h, i: (h, 0, 0, 0)),
            pl.BlockSpec((4, DH_A), lambda b, h, i: (0, 0)),
            pl.BlockSpec((1, DV_A), lambda b, h, i: (0, 0)),
        ],
        out_specs=pl.BlockSpec((1, t, DV_A), lambda b, h, i: (b, i, h)),
        compiler_params=pltpu.CompilerParams(
            dimension_semantics=("arbitrary", "arbitrary", "arbitrary"),
            vmem_limit_bytes=_vmem_limit(2 * s * LANES * 2 + 3 * 2 * t * t * 4 + 4 * t * LANES)),
        name="diff_attention",
    )(proj, proj, proj, bias_tiles, lam_p, subln_g.reshape(1, DV_A))


def _gla_kernel(q_ref, k_ref, v_ref, r_ref, bg_ref, wgk_ref, bgk_ref, g_ref, o_ref, st_ref):
    c = CHUNK

    @pl.when(pl.program_id(1) == 0)
    def _():
        st_ref[...] = jnp.zeros_like(st_ref)

    row = lax.broadcasted_iota(jnp.int32, (c, c), 0)
    col = lax.broadcasted_iota(jnp.int32, (c, c), 1)
    causal = col <= row
    tri = jnp.where(causal, 1.0, 0.0).astype(BF16)
    lane = lax.broadcasted_iota(jnp.int32, (c, LANES), 1)
    wgk = wgk_ref[...]
    bgk = bgk_ref[...]
    g = g_ref[...]

    def chunk(ci, carry):
        r0 = pl.multiple_of(ci * c, c)
        bg = bg_ref[0, pl.ds(r0, c), :]
        for hp in range(H_B // 2):
            cols = slice(hp * LANES, (hp + 1) * LANES)
            la = -_softplus(-(_dot3(bg, wgk[:, cols]) + bgk[:, cols])) / GATE_TAU
            la_hi, la_lo = _split(la)
            b = _dot(tri, la_hi) + _dot(tri, la_lo)
            b_last = b[c - 1:c, :]
            qf = q_ref[0, pl.ds(r0, c), cols].astype(F32) * (DK_B ** -0.5)
            kf = k_ref[0, pl.ds(r0, c), cols].astype(F32)
            q_dec = qf * jnp.exp(b)
            k_inv = (kf * jnp.exp(-b)).astype(BF16)
            k_dec = kf * jnp.exp(b_last - b)
            decay = jnp.exp(b_last)
            for hh in range(2):
                h = 2 * hp + hh
                own = (lane >= hh * DK_B) & (lane < (hh + 1) * DK_B)
                qd = jnp.where(own, q_dec, 0.0).astype(BF16)
                kd = jnp.where(own, k_dec, 0.0).astype(BF16)
                vcols = slice(h * DV_B, (h + 1) * DV_B)
                vb = v_ref[0, pl.ds(r0, c), vcols]
                sc = jnp.where(causal, _nt(qd, k_inv), 0.0)
                st = st_ref[h]
                o = _dot(sc.astype(BF16), vb) + _nt(qd, st.astype(BF16))
                st_ref[h] = st * decay + _tn(vb, kd)
                rr = r_ref[0, pl.ds(r0, c), vcols].astype(F32)
                o_ref[0, pl.ds(r0, c), vcols] = (_rms(o, g) * _silu(rr)).astype(o_ref.dtype)
        return carry

    lax.fori_loop(0, q_ref.shape[1] // c, chunk, 0)


def _gla(proj, gate, w_gk2, b_gk, gla_g):
    bsz, s, _ = proj.shape
    ts = GLA_STEP
    nk = H_B * DK_B
    nv = H_B * DV_B
    w_pad = jnp.zeros((LANES, nk), F32).at[:GATE_RANK].set(w_gk2.astype(F32))
    return pl.pallas_call(
        _gla_kernel,
        out_shape=jax.ShapeDtypeStruct((bsz, s, nv), BF16),
        grid=(bsz, s // ts),
        in_specs=[
            pl.BlockSpec((1, ts, nk), lambda b, i: (b, i, 1536 // nk)),
            pl.BlockSpec((1, ts, nk), lambda b, i: (b, i, 1792 // nk)),
            pl.BlockSpec((1, ts, nv), lambda b, i: (b, i, 2048 // nv)),
            pl.BlockSpec((1, ts, nv), lambda b, i: (b, i, 2560 // nv)),
            pl.BlockSpec((1, ts, LANES), lambda b, i: (b, i, 0)),
            pl.BlockSpec((LANES, nk), lambda b, i: (0, 0)),
            pl.BlockSpec((1, nk), lambda b, i: (0, 0)),
            pl.BlockSpec((1, DV_B), lambda b, i: (0, 0)),
        ],
        out_specs=pl.BlockSpec((1, ts, nv), lambda b, i: (b, i, 0)),
        scratch_shapes=[pltpu.VMEM((H_B, DV_B, LANES), F32)],
        compiler_params=pltpu.CompilerParams(
            dimension_semantics=("arbitrary", "arbitrary"),
            vmem_limit_bytes=_vmem_limit(ts * (2 * nk + 3 * nv) * 2 + ts * LANES * 4,
                                         H_B * DV_B * LANES * 4)),
        name="gated_linear_attention",
    )(proj, proj, proj, proj, gate, w_pad, b_gk.reshape(1, nk), gla_g.reshape(1, DV_B))


def _stick_kernel(q_ref, k_ref, v_ref, o_ref):
    tq, tk = STICK_Q_TILE, ATTN_TILE
    per_q = tq // tk
    i = pl.program_id(2)
    q = q_ref[0]
    lane_q = lax.broadcasted_iota(jnp.int32, (tq, LANES), 1)
    scale = jnp.asarray(DH_C ** -0.5, BF16)
    zero_q = jnp.zeros_like(q)
    qq = jnp.concatenate([jnp.where(lane_q < DH_C, q, zero_q),
                          jnp.where(lane_q >= DH_C, q, zero_q)], axis=0) * scale
    row = lax.broadcasted_iota(jnp.int32, (tk, tk), 0)
    col = lax.broadcasted_iota(jnp.int32, (tk, tk), 1)
    later = jnp.where(row > col, 1.0, 0.0).astype(BF16)
    lane_k = lax.broadcasted_iota(jnp.int32, (tk, LANES), 1)
    q_in_tile = lax.broadcasted_iota(jnp.int32, (2 * tq, tk), 0) % tq
    k_in_tile = lax.broadcasted_iota(jnp.int32, (2 * tq, tk), 1)

    def tile(j, run, acc, key_offset):
        r0 = pl.multiple_of(j * tk, tk)
        kb = k_ref[0, pl.ds(r0, tk), :]
        vb = v_ref[0, pl.ds(r0, tk), :]
        z = _nt(qq, kb)
        sp = _softplus(z)
        log_1m = -sp
        if key_offset is not None:
            keep = (k_in_tile + key_offset) < q_in_tile
            log_1m = jnp.where(keep, log_1m, 0.0)
        hi, lo = _split(log_1m)
        suffix = _dot(hi, later) + _dot(lo, later) + run
        a = jnp.exp((z - sp) + suffix)
        if key_offset is not None:
            a = jnp.where(keep, a, 0.0)
        ab = a.astype(BF16)
        zero_v = jnp.zeros_like(vb)
        a2 = jnp.concatenate([ab[:tq], ab[tq:]], axis=1)
        v2 = jnp.concatenate([jnp.where(lane_k < DH_C, vb, zero_v),
                              jnp.where(lane_k >= DH_C, vb, zero_v)], axis=0)
        acc = acc + _dot(a2, v2)
        run = run + jnp.sum(log_1m, axis=-1, keepdims=True)
        return run, acc

    run = jnp.zeros((2 * tq, 1), F32)
    acc = jnp.zeros((tq, LANES), F32)
    for d in range(per_q):
        run, acc = tile(i * per_q + (per_q - 1 - d), run, acc, (per_q - 1 - d) * tk)

    def cond(carry):
        n, run, _ = carry
        return (n < i * per_q) & (jnp.max(run) > EXP_ZERO_BELOW)

    def body(carry):
        n, run, acc = carry
        run, acc = tile(i * per_q - 1 - n, run, acc, None)
        return n + 1, run, acc

    _, _, acc = lax.while_loop(cond, body, (jnp.int32(0), run, acc))
    o_ref[0] = acc.astype(o_ref.dtype)


def _stick_breaking(proj):
    bsz, s, _ = proj.shape
    t = STICK_Q_TILE
    n_tiles = H_C * DH_C // LANES
    return pl.pallas_call(
        _stick_kernel,
        out_shape=jax.ShapeDtypeStruct((bsz, s, H_C * DH_C), BF16),
        grid=(bsz, n_tiles, s // t),
        in_specs=[
            pl.BlockSpec((1, t, LANES), lambda b, h, i: (b, i, h)),
            pl.BlockSpec((1, s, LANES), lambda b, h, i: (b, 0, n_tiles + h)),
            pl.BlockSpec((1, s, LANES), lambda b, h, i: (b, 0, 2 * n_tiles + h)),
        ],
        out_specs=pl.BlockSpec((1, t, LANES), lambda b, h, i: (b, i, h)),
        compiler_params=pltpu.CompilerParams(
            dimension_semantics=("arbitrary", "arbitrary", "arbitrary"),
            vmem_limit_bytes=_vmem_limit(2 * s * LANES * 2 + 4 * t * LANES)),
        name="stick_breaking_attention",
    )(proj, proj, proj)


def _route(logits):
    lane_i = lax.broadcasted_iota(jnp.int32, logits.shape, 1)
    lane = lane_i.astype(F32)
    group_of_lane = (lane_i // EXPERTS_PER_GROUP).astype(F32)
    big = jnp.float32(1 << 20)
    neg = jnp.float32(-jnp.inf)
    is_group = (lane_i >= N_EXPERTS) & (lane_i < N_EXPERTS + N_GROUPS)
    gl = jnp.where(is_group, logits, neg)
    gmax = jnp.max(gl, axis=-1, keepdims=True)
    g_val = 1.0 / jnp.sum(jnp.exp(gl - gmax), axis=-1, keepdims=True)
    g_sel = jnp.min(jnp.where(gl == gmax, lane - N_EXPERTS, big), axis=-1, keepdims=True)
    in_group = (lane_i < N_EXPERTS) & (group_of_lane == g_sel)
    el = jnp.where(in_group, logits, neg)
    emax = jnp.max(el, axis=-1, keepdims=True)
    esum = jnp.sum(jnp.exp(el - emax), axis=-1, keepdims=True)
    i1 = jnp.min(jnp.where(el == emax, lane, big), axis=-1, keepdims=True)
    el2 = jnp.where(lane == i1, neg, el)
    emax2 = jnp.max(el2, axis=-1, keepdims=True)
    i2 = jnp.min(jnp.where(el2 == emax2, lane, big), axis=-1, keepdims=True)
    v1 = 1.0 / esum
    v2 = jnp.exp(emax2 - emax) / esum
    tot = v1 + v2
    w1 = g_val * (v1 / tot)
    w2 = g_val * (v2 / tot)
    out = jnp.where(lane_i == 0, i1, 0.0)
    out = jnp.where(lane_i == 1, i2, out)
    out = jnp.where(lane_i == 2, w1, out)
    out = jnp.where(lane_i == 3, w2, out)
    return out


def _out_kernel(*refs, n_o):
    o_refs = refs[:n_o]
    w_refs = refs[n_o:2 * n_o]
    x_ref, mod_ref, g_ref, wr_ref, br_ref, xo_ref, h_ref, r_ref = refs[2 * n_o:]
    m = _dot(o_refs[0][0], w_refs[0][...])
    for a in range(1, n_o):
        m = m + _dot(o_refs[a][0], w_refs[a][...])
    x = x_ref[0] + mod_ref[0, 2:3, :] * m
    xo_ref[0] = x
    h = _rms(x, g_ref[...]) * (1.0 + mod_ref[0, 4:5, :]) + mod_ref[0, 3:4, :]
    h_ref[0] = h.astype(BF16)
    r_ref[0] = _route(_dot3(h, wr_ref[...]) + br_ref[...])


def _out_projection(o_list, w_list, x, mod, g, w_router, b_router):
    bsz, s, d = x.shape
    tm = TOKEN_TILE
    n_o = len(o_list)
    tok = lambda b, i: (b, i, 0)
    const2 = lambda b, i: (0, 0)
    in_specs = ([pl.BlockSpec((1, tm, o.shape[2]), tok) for o in o_list]
                + [pl.BlockSpec(w.shape, const2) for w in w_list]
                + [pl.BlockSpec((1, tm, d), tok), pl.BlockSpec((1, 6, d), lambda b, i: (b, 0, 0)),
                   pl.BlockSpec((1, d), const2), pl.BlockSpec((d, LANES), const2),
                   pl.BlockSpec((1, LANES), const2)])
    block_bytes = (sum(tm * o.shape[2] * 2 for o in o_list) + sum(w.size * 2 for w in w_list)
                   + 2 * tm * d * 4 + tm * d * 2 + d * LANES * 4 + tm * LANES * 4)
    return pl.pallas_call(
        functools.partial(_out_kernel, n_o=n_o),
        out_shape=[jax.ShapeDtypeStruct((bsz, s, d), F32),
                   jax.ShapeDtypeStruct((bsz, s, d), BF16),
                   jax.ShapeDtypeStruct((bsz, s, LANES), F32)],
        grid=(bsz, s // tm),
        in_specs=in_specs,
        out_specs=[pl.BlockSpec((1, tm, d), tok), pl.BlockSpec((1, tm, d), tok),
                   pl.BlockSpec((1, tm, LANES), tok)],
        compiler_params=pltpu.CompilerParams(
            dimension_semantics=("arbitrary", "arbitrary"),
            vmem_limit_bytes=_vmem_limit(block_bytes, tm * d * 12)),
        name="out_projection_norm_router",
    )(*o_list, *w_list, x, mod, g.reshape(1, d), w_router, b_router)


def _expert_kernel(te_ref, nu_ref, x_ref, cw_ref, wg_ref, wu_ref, wd_ref, o_ref):
    t = pl.program_id(0)

    @pl.when(t < nu_ref[0])
    def _():
        x = x_ref[...]
        gt = _dot(x, wg_ref[0])
        up = _dot(x, wu_ref[0])
        act = (_silu(gt) * up * cw_ref[...]).astype(BF16)
        o_ref[...] = _dot(act, wd_ref[0])

    @pl.when(t >= nu_ref[0])
    def _():
        o_ref[...] = jnp.zeros_like(o_ref)


def _grouped_experts(xs, row_w, tile_expert, n_used, w_gate, w_up, w_down):
    p, d = xs.shape
    tm = EXPERT_TILE
    f = w_gate.shape[2]
    grid_spec = pltpu.PrefetchScalarGridSpec(
        num_scalar_prefetch=2,
        grid=(p // tm,),
        in_specs=[
            pl.BlockSpec((tm, d), lambda t, te, nu: (t, 0)),
            pl.BlockSpec((tm, 1), lambda t, te, nu: (t, 0)),
            pl.BlockSpec((1, d, f), lambda t, te, nu: (te[t], 0, 0)),
            pl.BlockSpec((1, d, f), lambda t, te, nu: (te[t], 0, 0)),
            pl.BlockSpec((1, f, d), lambda t, te, nu: (te[t], 0, 0)),
        ],
        out_specs=pl.BlockSpec((tm, d), lambda t, te, nu: (t, 0)),
    )
    block_bytes = tm * d * 2 + tm * LANES * 4 + 3 * d * f * 2 + tm * d * 4
    return pl.pallas_call(
        _expert_kernel,
        out_shape=jax.ShapeDtypeStruct((p, d), F32),
        grid_spec=grid_spec,
        compiler_params=pltpu.CompilerParams(
            dimension_semantics=("arbitrary",),
            vmem_limit_bytes=_vmem_limit(block_bytes, tm * f * 16)),
        name="grouped_swiglu_experts",
    )(tile_expert, n_used, xs, row_w, w_gate, w_up, w_down)


def _dispatch(route, n_tok):
    tm = EXPERT_TILE
    e = jnp.concatenate([route[:, 0], route[:, 1]]).astype(jnp.int32)
    w = jnp.concatenate([route[:, 2], route[:, 3]])
    tok = jnp.concatenate([jnp.arange(n_tok, dtype=jnp.int32)] * 2)
    onehot = (e[:, None] == jnp.arange(N_EXPERTS, dtype=jnp.int32)[None, :]).astype(jnp.int32)
    csum = jnp.cumsum(onehot, axis=0)
    rank = jnp.take_along_axis(csum, e[:, None], axis=1)[:, 0] - 1
    counts = csum[-1]
    padded = ((counts + tm - 1) // tm) * tm
    ends = jnp.cumsum(padded)
    starts = ends - padded
    pos = starts[e] + rank
    p = 2 * n_tok + N_EXPERTS * tm
    row_tok = jnp.zeros((p,), jnp.int32).at[pos].set(tok)
    row_w = jnp.zeros((p,), F32).at[pos].set(w)
    tile_start = jnp.arange(p // tm, dtype=jnp.int32) * tm
    tile_expert = jnp.minimum(jnp.searchsorted(ends, tile_start, side="right"),
                              N_EXPERTS - 1).astype(jnp.int32)
    n_used = (ends[-1] // tm).astype(jnp.int32).reshape(1)
    return row_tok, row_w.reshape(p, 1), tile_expert, n_used, pos


def _final_kernel(x_ref, y_ref, mod_ref, g_ref, o_ref):
    x = x_ref[0] + mod_ref[0, 5:6, :] * y_ref[0]
    o_ref[0] = _rms(x, g_ref[...])


def _final_norm(x, y, mod, g):
    bsz, s, d = x.shape
    tm = TOKEN_TILE
    tok = lambda b, i: (b, i, 0)
    return pl.pallas_call(
        _final_kernel,
        out_shape=jax.ShapeDtypeStruct((bsz, s, d), F32),
        grid=(bsz, s // tm),
        in_specs=[pl.BlockSpec((1, tm, d), tok), pl.BlockSpec((1, tm, d), tok),
                  pl.BlockSpec((1, 6, d), lambda b, i: (b, 0, 0)),
                  pl.BlockSpec((1, d), lambda b, i: (0, 0))],
        out_specs=pl.BlockSpec((1, tm, d), tok),
        compiler_params=pltpu.CompilerParams(
            dimension_semantics=("arbitrary", "arbitrary"),
            vmem_limit_bytes=_vmem_limit(3 * tm * d * 4)),
        name="residual_final_norm",
    )(x, y, mod, g.reshape(1, d))


def kernel(x, c, w_ada, b_ada, norm_mix, norm_ffn, norm_final, rel_bias, even_w_in, even_lambda, even_subln, even_w_gk2, even_b_gk, even_gla_norm, even_w_out, odd_w_in, odd_w_out, router_group_w, router_group_b, router_expert_w, router_expert_b, expert_w_gate, expert_w_up, expert_w_down):
    bsz, s, d = x.shape
    n_tok = bsz * s
    assert s % TOKEN_TILE == 0 and s % GLA_STEP == 0 and s % STICK_Q_TILE == 0
    assert STICK_Q_TILE % ATTN_TILE == 0
    mod_all = _modulation(c, w_ada, b_ada).reshape(DEPTH, bsz, 6, d)
    bias_tiles = _diff_bias_tiles(rel_bias)
    n_even_main = 3072
    y = None
    prev_mod = None
    for l in range(DEPTH):
        mod = mod_all[l]
        i = l // 2
        if l % 2 == 0:
            w_in = even_w_in[i]
            w_main = w_in[:, :n_even_main].astype(BF16)
            w_gate = jnp.zeros((d, LANES), BF16).at[:, :GATE_RANK].set(
                w_in[:, n_even_main:].astype(BF16))
            x, proj, gate = _in_projection(x, y, prev_mod, norm_mix[l], mod, w_main, w_gate)
            oa = _diff_attention(proj, bias_tiles, even_lambda[i], even_subln[i], l)
            ob = _gla(proj, gate, even_w_gk2[i], even_b_gk[i], even_gla_norm[i])
            w_out = even_w_out[i].astype(BF16)
            o_list = [oa, ob]
            w_list = [w_out[:H_A * DV_A], w_out[H_A * DV_A:]]
        else:
            x, proj, _ = _in_projection(x, y, prev_mod, norm_mix[l], mod,
                                        odd_w_in[i].astype(BF16), None)
            o_list = [_stick_breaking(proj)]
            w_list = [odd_w_out[i].astype(BF16)]
        w_router = jnp.zeros((d, LANES), F32)
        w_router = w_router.at[:, :N_EXPERTS].set(router_expert_w[l].astype(F32))
        w_router = w_router.at[:, N_EXPERTS:N_EXPERTS + N_GROUPS].set(router_group_w[l].astype(F32))
        b_router = jnp.zeros((1, LANES), F32)
        b_router = b_router.at[0, :N_EXPERTS].set(router_expert_b[l].astype(F32))
        b_router = b_router.at[0, N_EXPERTS:N_EXPERTS + N_GROUPS].set(router_group_b[l].astype(F32))
        x, h2, route = _out_projection(o_list, w_list, x, mod, norm_ffn[l], w_router, b_router)
        row_tok, row_w, tile_expert, n_used, pos = _dispatch(route.reshape(n_tok, LANES), n_tok)
        xs = jnp.take(h2.reshape(n_tok, d), row_tok, axis=0)
        out = _grouped_experts(xs, row_w, tile_expert, n_used, expert_w_gate[l].astype(BF16),
                               expert_w_up[l].astype(BF16), expert_w_down[l].astype(BF16))
        y = (jnp.take(out, pos[:n_tok], axis=0) + jnp.take(out, pos[n_tok:], axis=0)
             ).reshape(bsz, s, d)
        prev_mod = mod
    return _final_norm(x, y, prev_mod, norm_final)
```

```python
import functools
import math

import numpy as np
import jax
import jax.numpy as jnp
from jax import lax
from jax.experimental import pallas as pl
from jax.experimental.pallas import tpu as pltpu

F32 = jnp.float32
BF16 = jnp.bfloat16

D_MODEL = 1024
DEPTH = 4
CHUNK = 64
EPS = 1e-6
H_A, DH_A, DV_A = 4, 64, 128
H_B, DK_B, DV_B = 4, 64, 128
GATE_RANK = 16
GATE_TAU = 16.0
H_C, DH_C = 16, 64
N_BUCKETS = 32
MAX_DISTANCE = 128
N_GROUPS = 4
EXPERTS_PER_GROUP = 8
N_EXPERTS = N_GROUPS * EXPERTS_PER_GROUP
D_EXPERT = 512
MIX = 1024

LANES = 128
V7X_VMEM_BYTES = 64 * 1024 * 1024
NEG_BIG = -1e30
EXP_ZERO_BELOW = -104.0

TOKEN_TILE = 512
ATTN_TILE = 128
STICK_Q_TILE = 256
DIFF_KEY_TILES = 4
GLA_STEP = 512
EXPERT_TILE = 256


def _vmem_limit(block_bytes, scratch_bytes=0):
    est = 2 * block_bytes + scratch_bytes + 16 * 1024 * 1024
    return int(min(est, V7X_VMEM_BYTES - 8 * 1024 * 1024))


def _nt(a, b):
    return lax.dot_general(a, b, (((1,), (1,)), ((), ())), preferred_element_type=F32)


def _tn(a, b):
    return lax.dot_general(a, b, (((0,), (0,)), ((), ())), preferred_element_type=F32)


def _dot(a, b):
    return jnp.dot(a, b, preferred_element_type=F32)


def _split(x):
    hi = x.astype(BF16)
    lo = (x - hi.astype(F32)).astype(BF16)
    return hi, lo


def _dot3(a, b):
    ah, al = _split(a)
    bh, bl = _split(b)
    return _dot(ah, bh) + _dot(al, bh) + _dot(ah, bl)


def _rms(x, g):
    return x * lax.rsqrt(jnp.mean(x * x, axis=-1, keepdims=True) + EPS) * g


def _softplus(z):
    return jnp.maximum(z, 0.0) + jnp.log(1.0 + jnp.exp(-jnp.abs(z)))


def _silu(x):
    return x / (1.0 + jnp.exp(-x))


def _mod_kernel(c_ref, w_ref, b_ref, o_ref):
    c = c_ref[...]
    o_ref[0] = _dot3(_silu(c), w_ref[0]) + b_ref[0]


def _modulation(c, w_ada, b_ada):
    depth, d, n = w_ada.shape
    bsz = c.shape[0]
    tn = 1536
    return pl.pallas_call(
        _mod_kernel,
        out_shape=jax.ShapeDtypeStruct((depth, bsz, n), F32),
        grid=(depth, n // tn),
        in_specs=[
            pl.BlockSpec((bsz, d), lambda l, j: (0, 0)),
            pl.BlockSpec((1, d, tn), lambda l, j: (l, 0, j)),
            pl.BlockSpec((1, 1, tn), lambda l, j: (l, 0, j)),
        ],
        out_specs=pl.BlockSpec((1, bsz, tn), lambda l, j: (l, 0, j)),
        compiler_params=pltpu.CompilerParams(
            dimension_semantics=("arbitrary", "arbitrary"),
            vmem_limit_bytes=_vmem_limit(d * tn * 4 + bsz * (d + tn) * 4)),
        name="adaln_modulation",
    )(c, w_ada, b_ada.reshape(depth, 1, n))


def _proj_kernel(*refs, has_res, n_main, has_gate):
    it = iter(refs)
    x_ref = next(it)
    if has_res:
        y_ref = next(it)
        pmod_ref = next(it)
    g_ref = next(it)
    mod_ref = next(it)
    w_ref = next(it)
    wg_ref = next(it) if has_gate else None
    xo_ref = next(it) if has_res else None
    o_ref = next(it)
    og_ref = next(it) if has_gate else None

    x = x_ref[0]
    if has_res:
        x = x + pmod_ref[0, 5:6, :] * y_ref[0]
        xo_ref[0] = x
    h = _rms(x, g_ref[...]) * (1.0 + mod_ref[0, 1:2, :]) + mod_ref[0, 0:1, :]
    hb = h.astype(BF16)
    step = 512
    for n0 in range(0, n_main, step):
        o_ref[0, :, n0:n0 + step] = _dot(hb, w_ref[:, n0:n0 + step]).astype(BF16)
    if has_gate:
        og_ref[0] = _dot(hb, wg_ref[...])


def _in_projection(x, y, prev_mod, g, mod, w, w_gate):
    bsz, s, d = x.shape
    tm = TOKEN_TILE
    has_res = y is not None
    has_gate = w_gate is not None
    n_main = w.shape[1]
    tok = lambda b, i: (b, i, 0)
    per_b = lambda b, i: (b, 0, 0)
    const2 = lambda b, i: (0, 0)
    in_specs = [pl.BlockSpec((1, tm, d), tok)]
    args = [x]
    if has_res:
        in_specs += [pl.BlockSpec((1, tm, d), tok), pl.BlockSpec((1, 6, d), per_b)]
        args += [y, prev_mod]
    in_specs += [pl.BlockSpec((1, d), const2), pl.BlockSpec((1, 6, d), per_b),
                 pl.BlockSpec((d, n_main), const2)]
    args += [g.reshape(1, d), mod, w]
    if has_gate:
        in_specs.append(pl.BlockSpec((d, LANES), const2))
        args.append(w_gate)
    out_shape, out_specs = [], []
    if has_res:
        out_shape.append(jax.ShapeDtypeStruct((bsz, s, d), F32))
        out_specs.append(pl.BlockSpec((1, tm, d), tok))
    out_shape.append(jax.ShapeDtypeStruct((bsz, s, n_main), BF16))
    out_specs.append(pl.BlockSpec((1, tm, n_main), tok))
    if has_gate:
        out_shape.append(jax.ShapeDtypeStruct((bsz, s, LANES), F32))
        out_specs.append(pl.BlockSpec((1, tm, LANES), tok))
    block_bytes = (tm * d * 4 * (3 if has_res else 1) + d * n_main * 2 + tm * n_main * 2
                   + d * LANES * 2 + tm * LANES * 4)
    outs = pl.pallas_call(
        functools.partial(_proj_kernel, has_res=has_res, n_main=n_main, has_gate=has_gate),
        out_shape=out_shape,
        grid=(bsz, s // tm),
        in_specs=in_specs,
        out_specs=out_specs,
        compiler_params=pltpu.CompilerParams(
            dimension_semantics=("arbitrary", "arbitrary"),
            vmem_limit_bytes=_vmem_limit(block_bytes, tm * d * 8)),
        name="norm_mod_in_projection",
    )(*args)
    outs = list(outs)
    x_new = outs.pop(0) if has_res else x
    proj = outs.pop(0)
    gate = outs.pop(0) if has_gate else None
    return x_new, proj, gate


def _t5_bucket(rel):
    nb = N_BUCKETS // 2
    max_exact = nb // 2
    ret = jnp.where(rel > 0, nb, 0)
    n = jnp.abs(rel)
    nf = jnp.maximum(n, 1).astype(F32)
    large = max_exact + (jnp.log(nf / max_exact) / math.log(MAX_DISTANCE / max_exact)
                         * (nb - max_exact)).astype(jnp.int32)
    large = jnp.minimum(large, nb - 1)
    return ret + jnp.where(n < max_exact, n, large)


def _diff_bias_tiles(rel_bias):
    t = ATTN_TILE
    assert t >= MAX_DISTANCE, "tiles two or more away must lie beyond the last bucket edge"
    qp = jnp.arange(t)[:, None]
    kp = jnp.arange(t)[None, :]
    tiles = []
    for off in (0, 1, 2):
        rel = (kp - off * t) - qp
        b = jnp.transpose(rel_bias.astype(F32)[_t5_bucket(rel)], (2, 0, 1))
        if off == 0:
            b = jnp.where((kp // CHUNK) <= (qp // CHUNK), b, NEG_BIG)
        tiles.append(b)
    tiles.append(jnp.full_like(tiles[0], NEG_BIG))
    tab = jnp.stack(tiles, axis=1)
    return jnp.concatenate([tab, tab], axis=2)


def _diff_attn_kernel(q_ref, k_ref, v_ref, bias_ref, lam_ref, g_ref, o_ref, *, lambda_init):
    t = ATTN_TILE
    i = pl.program_id(2)
    q = q_ref[0]
    lane = lax.broadcasted_iota(jnp.int32, (t, LANES), 1)
    scale = jnp.asarray(DH_A ** -0.5, BF16)
    zero = jnp.zeros_like(q)
    qq = jnp.concatenate([jnp.where(lane < DH_A, q, zero), jnp.where(lane >= DH_A, q, zero)],
                         axis=0) * scale

    group = DIFF_KEY_TILES
    span = group * t

    def body(n, carry):
        m, l, acc = carry
        r0 = pl.multiple_of(n * span, span)
        kb = k_ref[0, pl.ds(r0, span), :]
        vb = v_ref[0, pl.ds(r0, span), :]
        s = _nt(qq, kb)
        parts = []
        for u in range(group):
            back = i - (n * group + u)
            which = jnp.where(back < 0, 3, jnp.minimum(back, 2))
            parts.append(s[:, u * t:(u + 1) * t] + bias_ref[0, which])
        s = jnp.concatenate(parts, axis=1)
        m_new = jnp.maximum(m, jnp.max(s, axis=-1, keepdims=True))
        alpha = jnp.exp(m - m_new)
        p = jnp.exp(s - m_new)
        l = alpha * l + jnp.sum(p, axis=-1, keepdims=True)
        acc = alpha * acc + _dot(p.astype(BF16), vb)
        return m_new, l, acc

    init = (jnp.full((2 * t, 1), NEG_BIG, F32), jnp.zeros((2 * t, 1), F32),
            jnp.zeros((2 * t, DV_A), F32))
    _, l, acc = lax.fori_loop(0, i // group + 1, body, init)
    o = acc / l
    lp = lam_ref[...].astype(F32)
    lam = (jnp.exp(jnp.sum(lp[0:1] * lp[1:2], axis=-1, keepdims=True))
           - jnp.exp(jnp.sum(lp[2:3] * lp[3:4], axis=-1, keepdims=True)) + lambda_init)
    w = o[:t] - lam * o[t:]
    o_ref[0] = (_rms(w, g_ref[...]) * (1.0 - lambda_init)).astype(o_ref.dtype)


def _diff_attention(proj, bias_tiles, lam_p, subln_g, layer_idx):
    bsz, s, _ = proj.shape
    t = ATTN_TILE
    lambda_init = 0.8 - 0.6 * math.exp(-0.3 * layer_idx)
    return pl.pallas_call(
        functools.partial(_diff_attn_kernel, lambda_init=lambda_init),
        out_shape=jax.ShapeDtypeStruct((bsz, s, H_A * DV_A), BF16),
        grid=(bsz, H_A, s // t),
        in_specs=[
            pl.BlockSpec((1, t, LANES), lambda b, h, i: (b, i, h)),
            pl.BlockSpec((1, s, LANES), lambda b, h, i: (b, 0, H_A + h)),
            pl.BlockSpec((1, s, LANES), lambda b, h, i: (b, 0, 2 * H_A + h)),
            pl.BlockSpec((1, 4, 2 * t, t), lambda b, h, i: (h, 0, 0, 0)),
            pl.BlockSpec((4, DH_A), lambda b, h, i: (0, 0)),
            pl.BlockSpec((1, DV_A), lambda b, h, i: (0, 0)),
        ],
        out_specs=pl.BlockSpec((1, t, DV_A), lambda b, h, i: (b, i, h)),
        compiler_params=pltpu.CompilerParams(
            dimension_semantics=("arbitrary", "arbitrary", "arbitrary"),
            vmem_limit_bytes=_vmem_limit(2 * s * LANES * 2 + 4 * 2 * t * t * 4 + 4 * t * LANES)),
        name="diff_attention",
    )(proj, proj, proj, bias_tiles, lam_p, subln_g.reshape(1, DV_A))


def _gla_kernel(q_ref, k_ref, v_ref, r_ref, bg_ref, wgk_ref, bgk_ref, g_ref, o_ref, st_ref):
    c = CHUNK

    @pl.when(pl.program_id(1) == 0)
    def _():
        st_ref[...] = jnp.zeros_like(st_ref)

    row = lax.broadcasted_iota(jnp.int32, (c, c), 0)
    col = lax.broadcasted_iota(jnp.int32, (c, c), 1)
    causal = col <= row
    tri = jnp.where(causal, 1.0, 0.0).astype(BF16)
    lane = lax.broadcasted_iota(jnp.int32, (c, LANES), 1)
    wgk = wgk_ref[...]
    bgk = bgk_ref[...]
    g = g_ref[...]

    def chunk(ci, carry):
        r0 = pl.multiple_of(ci * c, c)
        bg = bg_ref[0, pl.ds(r0, c), :]
        for hp in range(H_B // 2):
            cols = slice(hp * LANES, (hp + 1) * LANES)
            la = -_softplus(-(_dot3(bg, wgk[:, cols]) + bgk[:, cols])) / GATE_TAU
            la_hi, la_lo = _split(la)
            b = _dot(tri, la_hi) + _dot(tri, la_lo)
            b_last = b[c - 1:c, :]
            qf = q_ref[0, pl.ds(r0, c), cols].astype(F32) * (DK_B ** -0.5)
            kf = k_ref[0, pl.ds(r0, c), cols].astype(F32)
            q_dec = qf * jnp.exp(b)
            k_inv = (kf * jnp.exp(-b)).astype(BF16)
            k_dec = kf * jnp.exp(b_last - b)
            decay = jnp.exp(b_last)
            for hh in range(2):
                h = 2 * hp + hh
                own = (lane >= hh * DK_B) & (lane < (hh + 1) * DK_B)
                qd = jnp.where(own, q_dec, 0.0).astype(BF16)
                kd = jnp.where(own, k_dec, 0.0).astype(BF16)
                vcols = slice(h * DV_B, (h + 1) * DV_B)
                vb = v_ref[0, pl.ds(r0, c), vcols]
                sc = jnp.where(causal, _nt(qd, k_inv), 0.0)
                st = st_ref[h]
                o = _dot(sc.astype(BF16), vb) + _nt(qd, st.astype(BF16))
                st_ref[h] = st * decay + _tn(vb, kd)
                rr = r_ref[0, pl.ds(r0, c), vcols].astype(F32)
                o_ref[0, pl.ds(r0, c), vcols] = (_rms(o, g) * _silu(rr)).astype(o_ref.dtype)
        return carry

    lax.fori_loop(0, q_ref.shape[1] // c, chunk, 0)


def _gla(proj, gate, w_gk2, b_gk, gla_g):
    bsz, s, _ = proj.shape
    ts = GLA_STEP
    nk = H_B * DK_B
    nv = H_B * DV_B
    w_pad = jnp.zeros((LANES, nk), F32).at[:GATE_RANK].set(w_gk2.astype(F32))
    return pl.pallas_call(
        _gla_kernel,
        out_shape=jax.ShapeDtypeStruct((bsz, s, nv), BF16),
        grid=(bsz, s // ts),
        in_specs=[
            pl.BlockSpec((1, ts, nk), lambda b, i: (b, i, 1536 // nk)),
            pl.BlockSpec((1, ts, nk), lambda b, i: (b, i, 1792 // nk)),
            pl.BlockSpec((1, ts, nv), lambda b, i: (b, i, 2048 // nv)),
            pl.BlockSpec((1, ts, nv), lambda b, i: (b, i, 2560 // nv)),
            pl.BlockSpec((1, ts, LANES), lambda b, i: (b, i, 0)),
            pl.BlockSpec((LANES, nk), lambda b, i: (0, 0)),
            pl.BlockSpec((1, nk), lambda b, i: (0, 0)),
            pl.BlockSpec((1, DV_B), lambda b, i: (0, 0)),
        ],
        out_specs=pl.BlockSpec((1, ts, nv), lambda b, i: (b, i, 0)),
        scratch_shapes=[pltpu.VMEM((H_B, DV_B, LANES), F32)],
        compiler_params=pltpu.CompilerParams(
            dimension_semantics=("arbitrary", "arbitrary"),
            vmem_limit_bytes=_vmem_limit(ts * (2 * nk + 3 * nv) * 2 + ts * LANES * 4,
                                         H_B * DV_B * LANES * 4)),
        name="gated_linear_attention",
    )(proj, proj, proj, proj, gate, w_pad, b_gk.reshape(1, nk), gla_g.reshape(1, DV_B))


def _stick_kernel(q_ref, k_ref, v_ref, o_ref):
    tq, tk = STICK_Q_TILE, ATTN_TILE
    per_q = tq // tk
    i = pl.program_id(2)
    q = q_ref[0]
    lane_q = lax.broadcasted_iota(jnp.int32, (tq, LANES), 1)
    scale = jnp.asarray(DH_C ** -0.5, BF16)
    zero_q = jnp.zeros_like(q)
    qq = jnp.concatenate([jnp.where(lane_q < DH_C, q, zero_q),
                          jnp.where(lane_q >= DH_C, q, zero_q)], axis=0) * scale
    row = lax.broadcasted_iota(jnp.int32, (tk, tk), 0)
    col = lax.broadcasted_iota(jnp.int32, (tk, tk), 1)
    later = jnp.where(row > col, 1.0, 0.0).astype(BF16)
    lane_k = lax.broadcasted_iota(jnp.int32, (tk, LANES), 1)
    q_in_tile = lax.broadcasted_iota(jnp.int32, (2 * tq, tk), 0) % tq
    k_in_tile = lax.broadcasted_iota(jnp.int32, (2 * tq, tk), 1)

    def tile(j, run, acc, key_offset):
        r0 = pl.multiple_of(j * tk, tk)
        kb = k_ref[0, pl.ds(r0, tk), :]
        vb = v_ref[0, pl.ds(r0, tk), :]
        z = _nt(qq, kb)
        sp = _softplus(z)
        log_1m = -sp
        if key_offset is not None:
            keep = (k_in_tile + key_offset) < q_in_tile
            log_1m = jnp.where(keep, log_1m, 0.0)
        hi, lo = _split(log_1m)
        suffix = _dot(hi, later) + _dot(lo, later) + run
        a = jnp.exp((z - sp) + suffix)
        if key_offset is not None:
            a = jnp.where(keep, a, 0.0)
        ab = a.astype(BF16)
        zero_v = jnp.zeros_like(vb)
        a2 = jnp.concatenate([ab[:tq], ab[tq:]], axis=1)
        v2 = jnp.concatenate([jnp.where(lane_k < DH_C, vb, zero_v),
                              jnp.where(lane_k >= DH_C, vb, zero_v)], axis=0)
        acc = acc + _dot(a2, v2)
        run = run + jnp.sum(log_1m, axis=-1, keepdims=True)
        return run, acc

    run = jnp.zeros((2 * tq, 1), F32)
    acc = jnp.zeros((tq, LANES), F32)
    for d in range(per_q):
        run, acc = tile(i * per_q + (per_q - 1 - d), run, acc, (per_q - 1 - d) * tk)

    def cond(carry):
        n, run, _ = carry
        return (n < i * per_q) & (jnp.max(run) > EXP_ZERO_BELOW)

    def body(carry):
        n, run, acc = carry
        run, acc = tile(i * per_q - 1 - n, run, acc, None)
        return n + 1, run, acc

    _, _, acc = lax.while_loop(cond, body, (jnp.int32(0), run, acc))
    o_ref[0] = acc.astype(o_ref.dtype)


def _stick_breaking(proj):
    bsz, s, _ = proj.shape
    t = STICK_Q_TILE
    n_tiles = H_C * DH_C // LANES
    return pl.pallas_call(
        _stick_kernel,
        out_shape=jax.ShapeDtypeStruct((bsz, s, H_C * DH_C), BF16),
        grid=(bsz, n_tiles, s // t),
        in_specs=[
            pl.BlockSpec((1, t, LANES), lambda b, h, i: (b, i, h)),
            pl.BlockSpec((1, s, LANES), lambda b, h, i: (b, 0, n_tiles + h)),
            pl.BlockSpec((1, s, LANES), lambda b, h, i: (b, 0, 2 * n_tiles + h)),
        ],
        out_specs=pl.BlockSpec((1, t, LANES), lambda b, h, i: (b, i, h)),
        compiler_params=pltpu.CompilerParams(
            dimension_semantics=("arbitrary", "arbitrary", "arbitrary"),
            vmem_limit_bytes=_vmem_limit(2 * s * LANES * 2 + 4 * t * LANES)),
        name="stick_breaking_attention",
    )(proj, proj, proj)


def _route(logits):
    lane_i = lax.broadcasted_iota(jnp.int32, logits.shape, 1)
    lane = lane_i.astype(F32)
    group_of_lane = (lane_i // EXPERTS_PER_GROUP).astype(F32)
    big = jnp.float32(1 << 20)
    neg = jnp.float32(-jnp.inf)
    is_group = (lane_i >= N_EXPERTS) & (lane_i < N_EXPERTS + N_GROUPS)
    gl = jnp.where(is_group, logits, neg)
    gmax = jnp.max(gl, axis=-1, keepdims=True)
    g_val = 1.0 / jnp.sum(jnp.exp(gl - gmax), axis=-1, keepdims=True)
    g_sel = jnp.min(jnp.where(gl == gmax, lane - N_EXPERTS, big), axis=-1, keepdims=True)
    in_group = (lane_i < N_EXPERTS) & (group_of_lane == g_sel)
    el = jnp.where(in_group, logits, neg)
    emax = jnp.max(el, axis=-1, keepdims=True)
    esum = jnp.sum(jnp.exp(el - emax), axis=-1, keepdims=True)
    i1 = jnp.min(jnp.where(el == emax, lane, big), axis=-1, keepdims=True)
    el2 = jnp.where(lane == i1, neg, el)
    emax2 = jnp.max(el2, axis=-1, keepdims=True)
    i2 = jnp.min(jnp.where(el2 == emax2, lane, big), axis=-1, keepdims=True)
    v1 = 1.0 / esum
    v2 = jnp.exp(emax2 - emax) / esum
    tot = v1 + v2
    w1 = g_val * (v1 / tot)
    w2 = g_val * (v2 / tot)
    out = jnp.where(lane_i == 0, i1, 0.0)
    out = jnp.where(lane_i == 1, i2, out)
    out = jnp.where(lane_i == 2, w1, out)
    out = jnp.where(lane_i == 3, w2, out)
    return out


def _out_kernel(*refs, n_o):
    o_refs = refs[:n_o]
    w_refs = refs[n_o:2 * n_o]
    x_ref, mod_ref, g_ref, wr_ref, br_ref, xo_ref, h_ref, r_ref = refs[2 * n_o:]
    m = _dot(o_refs[0][0], w_refs[0][...])
    for a in range(1, n_o):
        m = m + _dot(o_refs[a][0], w_refs[a][...])
    x = x_ref[0] + mod_ref[0, 2:3, :] * m
    xo_ref[0] = x
    h = _rms(x, g_ref[...]) * (1.0 + mod_ref[0, 4:5, :]) + mod_ref[0, 3:4, :]
    h_ref[0] = h.astype(BF16)
    r_ref[0] = _route(_dot3(h, wr_ref[...]) + br_ref[...])


def _out_projection(o_list, w_list, x, mod, g, w_router, b_router):
    bsz, s, d = x.shape
    tm = TOKEN_TILE
    n_o = len(o_list)
    tok = lambda b, i: (b, i, 0)
    const2 = lambda b, i: (0, 0)
    in_specs = ([pl.BlockSpec((1, tm, o.shape[2]), tok) for o in o_list]
                + [pl.BlockSpec(w.shape, const2) for w in w_list]
                + [pl.BlockSpec((1, tm, d), tok), pl.BlockSpec((1, 6, d), lambda b, i: (b, 0, 0)),
                   pl.BlockSpec((1, d), const2), pl.BlockSpec((d, LANES), const2),
                   pl.BlockSpec((1, LANES), const2)])
    block_bytes = (sum(tm * o.shape[2] * 2 for o in o_list) + sum(w.size * 2 for w in w_list)
                   + 2 * tm * d * 4 + tm * d * 2 + d * LANES * 4 + tm * LANES * 4)
    return pl.pallas_call(
        functools.partial(_out_kernel, n_o=n_o),
        out_shape=[jax.ShapeDtypeStruct((bsz, s, d), F32),
                   jax.ShapeDtypeStruct((bsz, s, d), BF16),
                   jax.ShapeDtypeStruct((bsz, s, LANES), F32)],
        grid=(bsz, s // tm),
        in_specs=in_specs,
        out_specs=[pl.BlockSpec((1, tm, d), tok), pl.BlockSpec((1, tm, d), tok),
                   pl.BlockSpec((1, tm, LANES), tok)],
        compiler_params=pltpu.CompilerParams(
            dimension_semantics=("arbitrary", "arbitrary"),
            vmem_limit_bytes=_vmem_limit(block_bytes, tm * d * 12)),
        name="out_projection_norm_router",
    )(*o_list, *w_list, x, mod, g.reshape(1, d), w_router, b_router)


def _expert_kernel(te_ref, nu_ref, x_ref, cw_ref, wg_ref, wu_ref, wd_ref, o_ref):
    t = pl.program_id(0)

    @pl.when(t < nu_ref[0])
    def _():
        x = x_ref[...]
        gt = _dot(x, wg_ref[0])
        up = _dot(x, wu_ref[0])
        act = (_silu(gt) * up * cw_ref[...]).astype(BF16)
        o_ref[...] = _dot(act, wd_ref[0])

    @pl.when(t >= nu_ref[0])
    def _():
        o_ref[...] = jnp.zeros_like(o_ref)


def _grouped_experts(xs, row_w, tile_expert, n_used, w_gate, w_up, w_down):
    p, d = xs.shape
    tm = EXPERT_TILE
    f = w_gate.shape[2]
    grid_spec = pltpu.PrefetchScalarGridSpec(
        num_scalar_prefetch=2,
        grid=(p // tm,),
        in_specs=[
            pl.BlockSpec((tm, d), lambda t, te, nu: (t, 0)),
            pl.BlockSpec((tm, 1), lambda t, te, nu: (t, 0)),
            pl.BlockSpec((1, d, f), lambda t, te, nu: (te[t], 0, 0)),
            pl.BlockSpec((1, d, f), lambda t, te, nu: (te[t], 0, 0)),
            pl.BlockSpec((1, f, d), lambda t, te, nu: (te[t], 0, 0)),
        ],
        out_specs=pl.BlockSpec((tm, d), lambda t, te, nu: (t, 0)),
    )
    block_bytes = tm * d * 2 + tm * LANES * 4 + 3 * d * f * 2 + tm * d * 4
    return pl.pallas_call(
        _expert_kernel,
        out_shape=jax.ShapeDtypeStruct((p, d), F32),
        grid_spec=grid_spec,
        compiler_params=pltpu.CompilerParams(
            dimension_semantics=("arbitrary",),
            vmem_limit_bytes=_vmem_limit(block_bytes, tm * f * 16)),
        name="grouped_swiglu_experts",
    )(tile_expert, n_used, xs, row_w, w_gate, w_up, w_down)


def _dispatch(route, n_tok):
    tm = EXPERT_TILE
    e = jnp.concatenate([route[:, 0], route[:, 1]]).astype(jnp.int32)
    w = jnp.concatenate([route[:, 2], route[:, 3]])
    tok = jnp.concatenate([jnp.arange(n_tok, dtype=jnp.int32)] * 2)
    onehot = (e[:, None] == jnp.arange(N_EXPERTS, dtype=jnp.int32)[None, :]).astype(jnp.int32)
    csum = jnp.cumsum(onehot, axis=0)
    rank = jnp.take_along_axis(csum, e[:, None], axis=1)[:, 0] - 1
    counts = csum[-1]
    padded = ((counts + tm - 1) // tm) * tm
    ends = jnp.cumsum(padded)
    starts = ends - padded
    pos = starts[e] + rank
    p = 2 * n_tok + N_EXPERTS * tm
    row_tok = jnp.zeros((p,), jnp.int32).at[pos].set(tok)
    row_w = jnp.zeros((p,), F32).at[pos].set(w)
    tile_start = jnp.arange(p // tm, dtype=jnp.int32) * tm
    tile_expert = jnp.minimum(jnp.searchsorted(ends, tile_start, side="right"),
                              N_EXPERTS - 1).astype(jnp.int32)
    n_used = (ends[-1] // tm).astype(jnp.int32).reshape(1)
    return row_tok, row_w.reshape(p, 1), tile_expert, n_used, pos


def _final_kernel(x_ref, y_ref, mod_ref, g_ref, o_ref):
    x = x_ref[0] + mod_ref[0, 5:6, :] * y_ref[0]
    o_ref[0] = _rms(x, g_ref[...])


def _final_norm(x, y, mod, g):
    bsz, s, d = x.shape
    tm = TOKEN_TILE
    tok = lambda b, i: (b, i, 0)
    return pl.pallas_call(
        _final_kernel,
        out_shape=jax.ShapeDtypeStruct((bsz, s, d), F32),
        grid=(bsz, s // tm),
        in_specs=[pl.BlockSpec((1, tm, d), tok), pl.BlockSpec((1, tm, d), tok),
                  pl.BlockSpec((1, 6, d), lambda b, i: (b, 0, 0)),
                  pl.BlockSpec((1, d), lambda b, i: (0, 0))],
        out_specs=pl.BlockSpec((1, tm, d), tok),
        compiler_params=pltpu.CompilerParams(
            dimension_semantics=("arbitrary", "arbitrary"),
            vmem_limit_bytes=_vmem_limit(3 * tm * d * 4)),
        name="residual_final_norm",
    )(x, y, mod, g.reshape(1, d))


def kernel(x, c, w_ada, b_ada, norm_mix, norm_ffn, norm_final, rel_bias, even_w_in, even_lambda, even_subln, even_w_gk2, even_b_gk, even_gla_norm, even_w_out, odd_w_in, odd_w_out, router_group_w, router_group_b, router_expert_w, router_expert_b, expert_w_gate, expert_w_up, expert_w_down):
    bsz, s, d = x.shape
    n_tok = bsz * s
    assert s % TOKEN_TILE == 0 and s % GLA_STEP == 0 and s % STICK_Q_TILE == 0
    assert STICK_Q_TILE % ATTN_TILE == 0 and s % (DIFF_KEY_TILES * ATTN_TILE) == 0
    mod_all = _modulation(c, w_ada, b_ada).reshape(DEPTH, bsz, 6, d)
    bias_tiles = _diff_bias_tiles(rel_bias)
    n_even_main = 3072
    y = None
    prev_mod = None
    for l in range(DEPTH):
        mod = mod_all[l]
        i = l // 2
        if l % 2 == 0:
            w_in = even_w_in[i]
            w_main = w_in[:, :n_even_main].astype(BF16)
            w_gate = jnp.zeros((d, LANES), BF16).at[:, :GATE_RANK].set(
                w_in[:, n_even_main:].astype(BF16))
            x, proj, gate = _in_projection(x, y, prev_mod, norm_mix[l], mod, w_main, w_gate)
            oa = _diff_attention(proj, bias_tiles, even_lambda[i], even_subln[i], l)
            ob = _gla(proj, gate, even_w_gk2[i], even_b_gk[i], even_gla_norm[i])
            w_out = even_w_out[i].astype(BF16)
            o_list = [oa, ob]
            w_list = [w_out[:H_A * DV_A], w_out[H_A * DV_A:]]
        else:
            x, proj, _ = _in_projection(x, y, prev_mod, norm_mix[l], mod,
                                        odd_w_in[i].astype(BF16), None)
            o_list = [_stick_breaking(proj)]
            w_list = [odd_w_out[i].astype(BF16)]
        w_router = jnp.zeros((d, LANES), F32)
        w_router = w_router.at[:, :N_EXPERTS].set(router_expert_w[l].astype(F32))
        w_router = w_router.at[:, N_EXPERTS:N_EXPERTS + N_GROUPS].set(router_group_w[l].astype(F32))
        b_router = jnp.zeros((1, LANES), F32)
        b_router = b_router.at[0, :N_EXPERTS].set(router_expert_b[l].astype(F32))
        b_router = b_router.at[0, N_EXPERTS:N_EXPERTS + N_GROUPS].set(router_group_b[l].astype(F32))
        x, h2, route = _out_projection(o_list, w_list, x, mod, norm_ffn[l], w_router, b_router)
        row_tok, row_w, tile_expert, n_used, pos = _dispatch(route.reshape(n_tok, LANES), n_tok)
        xs = jnp.take(h2.reshape(n_tok, d), row_tok, axis=0)
        out = _grouped_experts(xs, row_w, tile_expert, n_used, expert_w_gate[l].astype(BF16),
                               expert_w_up[l].astype(BF16), expert_w_down[l].astype(BF16))
        y = (jnp.take(out, pos[:n_tok], axis=0) + jnp.take(out, pos[n_tok:], axis=0)
             ).reshape(bsz, s, d)
        prev_mod = mod
    return _final_norm(x, y, prev_mod, norm_final)
```

```python
import functools
import math

import numpy as np
import jax
import jax.numpy as jnp
from jax import lax
from jax.experimental import pallas as pl
from jax.experimental.pallas import tpu as pltpu

F32 = jnp.float32
BF16 = jnp.bfloat16

D_MODEL = 1024
DEPTH = 4
CHUNK = 64
EPS = 1e-6
H_A, DH_A, DV_A = 4, 64, 128
H_B, DK_B, DV_B = 4, 64, 128
GATE_RANK = 16
GATE_TAU = 16.0
H_C, DH_C = 16, 64
N_BUCKETS = 32
MAX_DISTANCE = 128
N_GROUPS = 4
EXPERTS_PER_GROUP = 8
N_EXPERTS = N_GROUPS * EXPERTS_PER_GROUP
D_EXPERT = 512
MIX = 1024

LANES = 128
V7X_VMEM_BYTES = 64 * 1024 * 1024
NEG_BIG = -1e30
EXP_ZERO_BELOW = -104.0

TOKEN_TILE = 512
ATTN_TILE = 128
STICK_Q_TILE = 256
DIFF_KEY_TILES = 4
GLA_STEP = 512
EXPERT_TILE = 256


def _vmem_limit(block_bytes, scratch_bytes=0):
    est = 2 * block_bytes + scratch_bytes + 16 * 1024 * 1024
    return int(min(est, V7X_VMEM_BYTES - 8 * 1024 * 1024))


def _nt(a, b):
    return lax.dot_general(a, b, (((1,), (1,)), ((), ())), preferred_element_type=F32)


def _tn(a, b):
    return lax.dot_general(a, b, (((0,), (0,)), ((), ())), preferred_element_type=F32)


def _dot(a, b):
    return jnp.dot(a, b, preferred_element_type=F32)


def _split(x):
    hi = x.astype(BF16)
    lo = (x - hi.astype(F32)).astype(BF16)
    return hi, lo


def _dot3(a, b):
    ah, al = _split(a)
    bh, bl = _split(b)
    return _dot(ah, bh) + _dot(al, bh) + _dot(ah, bl)


def _rms(x, g):
    return x * lax.rsqrt(jnp.mean(x * x, axis=-1, keepdims=True) + EPS) * g


def _softplus(z):
    return jnp.maximum(z, 0.0) + jnp.log(1.0 + jnp.exp(-jnp.abs(z)))


def _silu(x):
    return x / (1.0 + jnp.exp(-x))


def _mod_kernel(c_ref, w_ref, b_ref, o_ref):
    c = c_ref[...]
    o_ref[0] = _dot3(_silu(c), w_ref[0]) + b_ref[0]


def _modulation(c, w_ada, b_ada):
    depth, d, n = w_ada.shape
    bsz = c.shape[0]
    tn = 1536
    return pl.pallas_call(
        _mod_kernel,
        out_shape=jax.ShapeDtypeStruct((depth, bsz, n), F32),
        grid=(depth, n // tn),
        in_specs=[
            pl.BlockSpec((bsz, d), lambda l, j: (0, 0)),
            pl.BlockSpec((1, d, tn), lambda l, j: (l, 0, j)),
            pl.BlockSpec((1, 1, tn), lambda l, j: (l, 0, j)),
        ],
        out_specs=pl.BlockSpec((1, bsz, tn), lambda l, j: (l, 0, j)),
        compiler_params=pltpu.CompilerParams(
            dimension_semantics=("arbitrary", "arbitrary"),
            vmem_limit_bytes=_vmem_limit(d * tn * 4 + bsz * (d + tn) * 4)),
        name="adaln_modulation",
    )(c, w_ada, b_ada.reshape(depth, 1, n))


def _expert_residual(x, y_ref, route_ref, mod_ref):
    route = route_ref[0]
    y = route[:, 2:3] * y_ref[0, 0] + route[:, 3:4] * y_ref[1, 0]
    return x + mod_ref[0, 5:6, :] * y


def _proj_kernel(*refs, has_res, n_main, has_gate):
    it = iter(refs)
    x_ref = next(it)
    if has_res:
        y_ref = next(it)
        route_ref = next(it)
        pmod_ref = next(it)
    g_ref = next(it)
    mod_ref = next(it)
    w_ref = next(it)
    wg_ref = next(it) if has_gate else None
    xo_ref = next(it) if has_res else None
    o_ref = next(it)
    og_ref = next(it) if has_gate else None

    x = x_ref[0]
    if has_res:
        x = _expert_residual(x, y_ref, route_ref, pmod_ref)
        xo_ref[0] = x
    h = _rms(x, g_ref[...]) * (1.0 + mod_ref[0, 1:2, :]) + mod_ref[0, 0:1, :]
    hb = h.astype(BF16)
    step = 512
    for n0 in range(0, n_main, step):
        o_ref[0, :, n0:n0 + step] = _dot(hb, w_ref[:, n0:n0 + step]).astype(BF16)
    if has_gate:
        og_ref[0] = _dot(hb, wg_ref[...])


def _in_projection(x, y, route, prev_mod, g, mod, w, w_gate):
    bsz, s, d = x.shape
    tm = TOKEN_TILE
    has_res = y is not None
    has_gate = w_gate is not None
    n_main = w.shape[1]
    tok = lambda b, i: (b, i, 0)
    per_b = lambda b, i: (b, 0, 0)
    const2 = lambda b, i: (0, 0)
    in_specs = [pl.BlockSpec((1, tm, d), tok)]
    args = [x]
    if has_res:
        in_specs += [pl.BlockSpec((2, 1, tm, d), lambda b, i: (0, b, i, 0)),
                     pl.BlockSpec((1, tm, LANES), tok), pl.BlockSpec((1, 6, d), per_b)]
        args += [y, route, prev_mod]
    in_specs += [pl.BlockSpec((1, d), const2), pl.BlockSpec((1, 6, d), per_b),
                 pl.BlockSpec((d, n_main), const2)]
    args += [g.reshape(1, d), mod, w]
    if has_gate:
        in_specs.append(pl.BlockSpec((d, LANES), const2))
        args.append(w_gate)
    out_shape, out_specs = [], []
    if has_res:
        out_shape.append(jax.ShapeDtypeStruct((bsz, s, d), F32))
        out_specs.append(pl.BlockSpec((1, tm, d), tok))
    out_shape.append(jax.ShapeDtypeStruct((bsz, s, n_main), BF16))
    out_specs.append(pl.BlockSpec((1, tm, n_main), tok))
    if has_gate:
        out_shape.append(jax.ShapeDtypeStruct((bsz, s, LANES), F32))
        out_specs.append(pl.BlockSpec((1, tm, LANES), tok))
    block_bytes = (tm * d * 4 * (4 if has_res else 1) + d * n_main * 2 + tm * n_main * 2
                   + d * LANES * 2 + 2 * tm * LANES * 4)
    outs = pl.pallas_call(
        functools.partial(_proj_kernel, has_res=has_res, n_main=n_main, has_gate=has_gate),
        out_shape=out_shape,
        grid=(bsz, s // tm),
        in_specs=in_specs,
        out_specs=out_specs,
        compiler_params=pltpu.CompilerParams(
            dimension_semantics=("arbitrary", "arbitrary"),
            vmem_limit_bytes=_vmem_limit(block_bytes, tm * d * 8)),
        name="norm_mod_in_projection",
    )(*args)
    outs = list(outs)
    x_new = outs.pop(0) if has_res else x
    proj = outs.pop(0)
    gate = outs.pop(0) if has_gate else None
    return x_new, proj, gate


def _t5_bucket(rel):
    nb = N_BUCKETS // 2
    max_exact = nb // 2
    ret = jnp.where(rel > 0, nb, 0)
    n = jnp.abs(rel)
    nf = jnp.maximum(n, 1).astype(F32)
    large = max_exact + (jnp.log(nf / max_exact) / math.log(MAX_DISTANCE / max_exact)
                         * (nb - max_exact)).astype(jnp.int32)
    large = jnp.minimum(large, nb - 1)
    return ret + jnp.where(n < max_exact, n, large)


def _diff_bias_tiles(rel_bias):
    t = ATTN_TILE
    assert t >= MAX_DISTANCE, "tiles two or more away must lie beyond the last bucket edge"
    qp = jnp.arange(t)[:, None]
    kp = jnp.arange(t)[None, :]
    tiles = []
    for off in (0, 1, 2):
        rel = (kp - off * t) - qp
        b = jnp.transpose(rel_bias.astype(F32)[_t5_bucket(rel)], (2, 0, 1))
        if off == 0:
            b = jnp.where((kp // CHUNK) <= (qp // CHUNK), b, NEG_BIG)
        tiles.append(b)
    tiles.append(jnp.full_like(tiles[0], NEG_BIG))
    tab = jnp.stack(tiles, axis=1)
    return jnp.concatenate([tab, tab], axis=2)


def _diff_attn_kernel(q_ref, k_ref, v_ref, bias_ref, lam_ref, g_ref, o_ref, *, lambda_init):
    t = ATTN_TILE
    i = pl.program_id(2)
    q = q_ref[0]
    lane = lax.broadcasted_iota(jnp.int32, (t, LANES), 1)
    scale = jnp.asarray(DH_A ** -0.5, BF16)
    zero = jnp.zeros_like(q)
    qq = jnp.concatenate([jnp.where(lane < DH_A, q, zero), jnp.where(lane >= DH_A, q, zero)],
                         axis=0) * scale

    group = DIFF_KEY_TILES
    span = group * t

    def body(n, carry):
        m, l, acc = carry
        r0 = pl.multiple_of(n * span, span)
        kb = k_ref[0, pl.ds(r0, span), :]
        vb = v_ref[0, pl.ds(r0, span), :]
        s = _nt(qq, kb)
        parts = []
        for u in range(group):
            back = i - (n * group + u)
            which = jnp.where(back < 0, 3, jnp.minimum(back, 2))
            parts.append(s[:, u * t:(u + 1) * t] + bias_ref[0, which])
        s = jnp.concatenate(parts, axis=1)
        m_new = jnp.maximum(m, jnp.max(s, axis=-1, keepdims=True))
        alpha = jnp.exp(m - m_new)
        p = jnp.exp(s - m_new)
        l = alpha * l + jnp.sum(p, axis=-1, keepdims=True)
        acc = alpha * acc + _dot(p.astype(BF16), vb)
        return m_new, l, acc

    init = (jnp.full((2 * t, 1), NEG_BIG, F32), jnp.zeros((2 * t, 1), F32),
            jnp.zeros((2 * t, DV_A), F32))
    _, l, acc = lax.fori_loop(0, i // group + 1, body, init)
    o = acc / l
    lp = lam_ref[...].astype(F32)
    lam = (jnp.exp(jnp.sum(lp[0:1] * lp[1:2], axis=-1, keepdims=True))
           - jnp.exp(jnp.sum(lp[2:3] * lp[3:4], axis=-1, keepdims=True)) + lambda_init)
    w = o[:t] - lam * o[t:]
    o_ref[0] = (_rms(w, g_ref[...]) * (1.0 - lambda_init)).astype(o_ref.dtype)


def _diff_attention(proj, bias_tiles, lam_p, subln_g, layer_idx):
    bsz, s, _ = proj.shape
    t = ATTN_TILE
    lambda_init = 0.8 - 0.6 * math.exp(-0.3 * layer_idx)
    return pl.pallas_call(
        functools.partial(_diff_attn_kernel, lambda_init=lambda_init),
        out_shape=jax.ShapeDtypeStruct((bsz, s, H_A * DV_A), BF16),
        grid=(bsz, H_A, s // t),
        in_specs=[
            pl.BlockSpec((1, t, LANES), lambda b, h, i: (b, i, h)),
            pl.BlockSpec((1, s, LANES), lambda b, h, i: (b, 0, H_A + h)),
            pl.BlockSpec((1, s, LANES), lambda b, h, i: (b, 0, 2 * H_A + h)),
            pl.BlockSpec((1, 4, 2 * t, t), lambda b, h, i: (h, 0, 0, 0)),
            pl.BlockSpec((4, DH_A), lambda b, h, i: (0, 0)),
            pl.BlockSpec((1, DV_A), lambda b, h, i: (0, 0)),
        ],
        out_specs=pl.BlockSpec((1, t, DV_A), lambda b, h, i: (b, i, h)),
        compiler_params=pltpu.CompilerParams(
            dimension_semantics=("arbitrary", "arbitrary", "arbitrary"),
            vmem_limit_bytes=_vmem_limit(2 * s * LANES * 2 + 4 * 2 * t * t * 4 + 4 * t * LANES)),
        name="diff_attention",
    )(proj, proj, proj, bias_tiles, lam_p, subln_g.reshape(1, DV_A))


def _gla_kernel(q_ref, k_ref, v_ref, r_ref, bg_ref, wgk_ref, bgk_ref, g_ref, o_ref, st_ref):
    c = CHUNK

    @pl.when(pl.program_id(1) == 0)
    def _():
        st_ref[...] = jnp.zeros_like(st_ref)

    row = lax.broadcasted_iota(jnp.int32, (c, c), 0)
    col = lax.broadcasted_iota(jnp.int32, (c, c), 1)
    causal = col <= row
    tri = jnp.where(causal, 1.0, 0.0).astype(BF16)
    lane = lax.broadcasted_iota(jnp.int32, (c, LANES), 1)
    wgk = wgk_ref[...]
    bgk = bgk_ref[...]
    g = g_ref[...]

    def chunk(ci, carry):
        r0 = pl.multiple_of(ci * c, c)
        bg = bg_ref[0, pl.ds(r0, c), :]
        for hp in range(H_B // 2):
            cols = slice(hp * LANES, (hp + 1) * LANES)
            la = -_softplus(-(_dot3(bg, wgk[:, cols]) + bgk[:, cols])) / GATE_TAU
            la_hi, la_lo = _split(la)
            b = _dot(tri, la_hi) + _dot(tri, la_lo)
            b_last = b[c - 1:c, :]
            qf = q_ref[0, pl.ds(r0, c), cols].astype(F32) * (DK_B ** -0.5)
            kf = k_ref[0, pl.ds(r0, c), cols].astype(F32)
            q_dec = qf * jnp.exp(b)
            k_inv = (kf * jnp.exp(-b)).astype(BF16)
            k_dec = kf * jnp.exp(b_last - b)
            decay = jnp.exp(b_last)
            for hh in range(2):
                h = 2 * hp + hh
                own = (lane >= hh * DK_B) & (lane < (hh + 1) * DK_B)
                qd = jnp.where(own, q_dec, 0.0).astype(BF16)
                kd = jnp.where(own, k_dec, 0.0).astype(BF16)
                vcols = slice(h * DV_B, (h + 1) * DV_B)
                vb = v_ref[0, pl.ds(r0, c), vcols]
                sc = jnp.where(causal, _nt(qd, k_inv), 0.0)
                st = st_ref[h]
                o = _dot(sc.astype(BF16), vb) + _nt(qd, st.astype(BF16))
                st_ref[h] = st * decay + _tn(vb, kd)
                rr = r_ref[0, pl.ds(r0, c), vcols].astype(F32)
                o_ref[0, pl.ds(r0, c), vcols] = (_rms(o, g) * _silu(rr)).astype(o_ref.dtype)
        return carry

    lax.fori_loop(0, q_ref.shape[1] // c, chunk, 0)


def _gla(proj, gate, w_gk2, b_gk, gla_g):
    bsz, s, _ = proj.shape
    ts = GLA_STEP
    nk = H_B * DK_B
    nv = H_B * DV_B
    w_pad = jnp.zeros((LANES, nk), F32).at[:GATE_RANK].set(w_gk2.astype(F32))
    return pl.pallas_call(
        _gla_kernel,
        out_shape=jax.ShapeDtypeStruct((bsz, s, nv), BF16),
        grid=(bsz, s // ts),
        in_specs=[
            pl.BlockSpec((1, ts, nk), lambda b, i: (b, i, 1536 // nk)),
            pl.BlockSpec((1, ts, nk), lambda b, i: (b, i, 1792 // nk)),
            pl.BlockSpec((1, ts, nv), lambda b, i: (b, i, 2048 // nv)),
            pl.BlockSpec((1, ts, nv), lambda b, i: (b, i, 2560 // nv)),
            pl.BlockSpec((1, ts, LANES), lambda b, i: (b, i, 0)),
            pl.BlockSpec((LANES, nk), lambda b, i: (0, 0)),
            pl.BlockSpec((1, nk), lambda b, i: (0, 0)),
            pl.BlockSpec((1, DV_B), lambda b, i: (0, 0)),
        ],
        out_specs=pl.BlockSpec((1, ts, nv), lambda b, i: (b, i, 0)),
        scratch_shapes=[pltpu.VMEM((H_B, DV_B, LANES), F32)],
        compiler_params=pltpu.CompilerParams(
            dimension_semantics=("arbitrary", "arbitrary"),
            vmem_limit_bytes=_vmem_limit(ts * (2 * nk + 3 * nv) * 2 + ts * LANES * 4,
                                         H_B * DV_B * LANES * 4)),
        name="gated_linear_attention",
    )(proj, proj, proj, proj, gate, w_pad, b_gk.reshape(1, nk), gla_g.reshape(1, DV_B))


def _stick_kernel(q_ref, k_ref, v_ref, o_ref):
    tq, tk = STICK_Q_TILE, ATTN_TILE
    per_q = tq // tk
    i = pl.program_id(2)
    q = q_ref[0]
    lane_q = lax.broadcasted_iota(jnp.int32, (tq, LANES), 1)
    scale = jnp.asarray(DH_C ** -0.5, BF16)
    zero_q = jnp.zeros_like(q)
    qq = jnp.concatenate([jnp.where(lane_q < DH_C, q, zero_q),
                          jnp.where(lane_q >= DH_C, q, zero_q)], axis=0) * scale
    row = lax.broadcasted_iota(jnp.int32, (tk, tk), 0)
    col = lax.broadcasted_iota(jnp.int32, (tk, tk), 1)
    later = jnp.where(row > col, 1.0, 0.0).astype(BF16)
    lane_k = lax.broadcasted_iota(jnp.int32, (tk, LANES), 1)
    q_in_tile = lax.broadcasted_iota(jnp.int32, (2 * tq, tk), 0) % tq
    k_in_tile = lax.broadcasted_iota(jnp.int32, (2 * tq, tk), 1)

    def tile(j, run, acc, key_offset):
        r0 = pl.multiple_of(j * tk, tk)
        kb = k_ref[0, pl.ds(r0, tk), :]
        vb = v_ref[0, pl.ds(r0, tk), :]
        z = _nt(qq, kb)
        sp = _softplus(z)
        log_1m = -sp
        if key_offset is not None:
            keep = (k_in_tile + key_offset) < q_in_tile
            log_1m = jnp.where(keep, log_1m, 0.0)
        hi, lo = _split(log_1m)
        suffix = _dot(hi, later) + _dot(lo, later) + run
        a = jnp.exp((z - sp) + suffix)
        if key_offset is not None:
            a = jnp.where(keep, a, 0.0)
        ab = a.astype(BF16)
        zero_v = jnp.zeros_like(vb)
        a2 = jnp.concatenate([ab[:tq], ab[tq:]], axis=1)
        v2 = jnp.concatenate([jnp.where(lane_k < DH_C, vb, zero_v),
                              jnp.where(lane_k >= DH_C, vb, zero_v)], axis=0)
        acc = acc + _dot(a2, v2)
        run = run + jnp.sum(log_1m, axis=-1, keepdims=True)
        return run, acc

    run = jnp.zeros((2 * tq, 1), F32)
    acc = jnp.zeros((tq, LANES), F32)
    for d in range(per_q):
        run, acc = tile(i * per_q + (per_q - 1 - d), run, acc, (per_q - 1 - d) * tk)

    def cond(carry):
        n, run, _ = carry
        return (n < i * per_q) & (jnp.max(run) > EXP_ZERO_BELOW)

    def body(carry):
        n, run, acc = carry
        run, acc = tile(i * per_q - 1 - n, run, acc, None)
        return n + 1, run, acc

    _, _, acc = lax.while_loop(cond, body, (jnp.int32(0), run, acc))
    o_ref[0] = acc.astype(o_ref.dtype)


def _stick_breaking(proj):
    bsz, s, _ = proj.shape
    t = STICK_Q_TILE
    n_tiles = H_C * DH_C // LANES
    return pl.pallas_call(
        _stick_kernel,
        out_shape=jax.ShapeDtypeStruct((bsz, s, H_C * DH_C), BF16),
        grid=(bsz, n_tiles, s // t),
        in_specs=[
            pl.BlockSpec((1, t, LANES), lambda b, h, i: (b, i, h)),
            pl.BlockSpec((1, s, LANES), lambda b, h, i: (b, 0, n_tiles + h)),
            pl.BlockSpec((1, s, LANES), lambda b, h, i: (b, 0, 2 * n_tiles + h)),
        ],
        out_specs=pl.BlockSpec((1, t, LANES), lambda b, h, i: (b, i, h)),
        compiler_params=pltpu.CompilerParams(
            dimension_semantics=("arbitrary", "arbitrary", "arbitrary"),
            vmem_limit_bytes=_vmem_limit(2 * s * LANES * 2 + 4 * t * LANES)),
        name="stick_breaking_attention",
    )(proj, proj, proj)


def _route(logits):
    lane_i = lax.broadcasted_iota(jnp.int32, logits.shape, 1)
    lane = lane_i.astype(F32)
    group_of_lane = (lane_i // EXPERTS_PER_GROUP).astype(F32)
    big = jnp.float32(1 << 20)
    neg = jnp.float32(-jnp.inf)
    is_group = (lane_i >= N_EXPERTS) & (lane_i < N_EXPERTS + N_GROUPS)
    gl = jnp.where(is_group, logits, neg)
    gmax = jnp.max(gl, axis=-1, keepdims=True)
    g_val = 1.0 / jnp.sum(jnp.exp(gl - gmax), axis=-1, keepdims=True)
    g_sel = jnp.min(jnp.where(gl == gmax, lane - N_EXPERTS, big), axis=-1, keepdims=True)
    in_group = (lane_i < N_EXPERTS) & (group_of_lane == g_sel)
    el = jnp.where(in_group, logits, neg)
    emax = jnp.max(el, axis=-1, keepdims=True)
    esum = jnp.sum(jnp.exp(el - emax), axis=-1, keepdims=True)
    i1 = jnp.min(jnp.where(el == emax, lane, big), axis=-1, keepdims=True)
    el2 = jnp.where(lane == i1, neg, el)
    emax2 = jnp.max(el2, axis=-1, keepdims=True)
    i2 = jnp.min(jnp.where(el2 == emax2, lane, big), axis=-1, keepdims=True)
    v1 = 1.0 / esum
    v2 = jnp.exp(emax2 - emax) / esum
    tot = v1 + v2
    w1 = g_val * (v1 / tot)
    w2 = g_val * (v2 / tot)
    out = jnp.where(lane_i == 0, i1, 0.0)
    out = jnp.where(lane_i == 1, i2, out)
    out = jnp.where(lane_i == 2, w1, out)
    out = jnp.where(lane_i == 3, w2, out)
    return out, (lane == i1), (lane == i2)


def _out_kernel(*refs, n_o):
    o_refs = refs[:n_o]
    w_refs = refs[n_o:2 * n_o]
    x_ref, mod_ref, g_ref, wr_ref, br_ref, xo_ref, h_ref, r_ref, cnt_ref, run_ref = refs[2 * n_o:]

    @pl.when((pl.program_id(0) == 0) & (pl.program_id(1) == 0))
    def _():
        run_ref[...] = jnp.zeros_like(run_ref)

    m = _dot(o_refs[0][0], w_refs[0][...])
    for a in range(1, n_o):
        m = m + _dot(o_refs[a][0], w_refs[a][...])
    x = x_ref[0] + mod_ref[0, 2:3, :] * m
    xo_ref[0] = x
    h = _rms(x, g_ref[...]) * (1.0 + mod_ref[0, 4:5, :]) + mod_ref[0, 3:4, :]
    h_ref[0] = h.astype(BF16)
    route, hit1, hit2 = _route(_dot3(h, wr_ref[...]) + br_ref[...])
    tm = route.shape[0]
    picks = jnp.where(hit1 | hit2, 1.0, 0.0)
    row = lax.broadcasted_iota(jnp.int32, (tm, tm), 0)
    col = lax.broadcasted_iota(jnp.int32, (tm, tm), 1)
    before = jnp.where(col < row, 1.0, 0.0).astype(BF16)
    earlier = _dot(before, picks.astype(BF16)) + run_ref[...]
    rank1 = jnp.sum(jnp.where(hit1, earlier, 0.0), axis=-1, keepdims=True)
    rank2 = jnp.sum(jnp.where(hit2, earlier, 0.0), axis=-1, keepdims=True)
    lane_i = lax.broadcasted_iota(jnp.int32, route.shape, 1)
    route = jnp.where(lane_i == 4, rank1, route)
    route = jnp.where(lane_i == 5, rank2, route)
    r_ref[0] = route
    run_ref[...] = run_ref[...] + jnp.sum(picks, axis=0, keepdims=True)
    cnt_ref[...] = run_ref[...]


def _out_projection(o_list, w_list, x, mod, g, w_router, b_router):
    bsz, s, d = x.shape
    tm = TOKEN_TILE
    n_o = len(o_list)
    tok = lambda b, i: (b, i, 0)
    const2 = lambda b, i: (0, 0)
    in_specs = ([pl.BlockSpec((1, tm, o.shape[2]), tok) for o in o_list]
                + [pl.BlockSpec(w.shape, const2) for w in w_list]
                + [pl.BlockSpec((1, tm, d), tok), pl.BlockSpec((1, 6, d), lambda b, i: (b, 0, 0)),
                   pl.BlockSpec((1, d), const2), pl.BlockSpec((d, LANES), const2),
                   pl.BlockSpec((1, LANES), const2)])
    block_bytes = (sum(tm * o.shape[2] * 2 for o in o_list) + sum(w.size * 2 for w in w_list)
                   + 2 * tm * d * 4 + tm * d * 2 + d * LANES * 4 + tm * LANES * 4)
    return pl.pallas_call(
        functools.partial(_out_kernel, n_o=n_o),
        out_shape=[jax.ShapeDtypeStruct((bsz, s, d), F32),
                   jax.ShapeDtypeStruct((bsz, s, d), BF16),
                   jax.ShapeDtypeStruct((bsz, s, LANES), F32),
                   jax.ShapeDtypeStruct((1, LANES), F32)],
        grid=(bsz, s // tm),
        in_specs=in_specs,
        out_specs=[pl.BlockSpec((1, tm, d), tok), pl.BlockSpec((1, tm, d), tok),
                   pl.BlockSpec((1, tm, LANES), tok), pl.BlockSpec((1, LANES), const2)],
        scratch_shapes=[pltpu.VMEM((1, LANES), F32)],
        compiler_params=pltpu.CompilerParams(
            dimension_semantics=("arbitrary", "arbitrary"),
            vmem_limit_bytes=_vmem_limit(block_bytes, tm * d * 12 + tm * tm * 4)),
        name="out_projection_norm_router",
    )(*o_list, *w_list, x, mod, g.reshape(1, d), w_router, b_router)


def _expert_kernel(te_ref, nu_ref, x_ref, wg_ref, wu_ref, wd_ref, o_ref, wgb_ref, wub_ref, wdb_ref):
    t = pl.program_id(0)
    prev = te_ref[jnp.maximum(t - 1, 0)]

    @pl.when((t == 0) | (te_ref[t] != prev))
    def _():
        wgb_ref[...] = wg_ref[0].astype(BF16)
        wub_ref[...] = wu_ref[0].astype(BF16)
        wdb_ref[...] = wd_ref[0].astype(BF16)

    @pl.when(t < nu_ref[0])
    def _():
        x = x_ref[...]
        gt = _dot(x, wgb_ref[...])
        up = _dot(x, wub_ref[...])
        o_ref[...] = _dot((_silu(gt) * up).astype(BF16), wdb_ref[...])

    @pl.when(t >= nu_ref[0])
    def _():
        o_ref[...] = jnp.zeros_like(o_ref)


def _grouped_experts(xs, tile_expert, n_used, w_gate, w_up, w_down):
    p, d = xs.shape
    tm = EXPERT_TILE
    f = w_gate.shape[2]
    grid_spec = pltpu.PrefetchScalarGridSpec(
        num_scalar_prefetch=2,
        grid=(p // tm,),
        in_specs=[
            pl.BlockSpec((tm, d), lambda t, te, nu: (t, 0)),
            pl.BlockSpec((1, d, f), lambda t, te, nu: (te[t], 0, 0)),
            pl.BlockSpec((1, d, f), lambda t, te, nu: (te[t], 0, 0)),
            pl.BlockSpec((1, f, d), lambda t, te, nu: (te[t], 0, 0)),
        ],
        out_specs=pl.BlockSpec((tm, d), lambda t, te, nu: (t, 0)),
        scratch_shapes=[pltpu.VMEM((d, f), BF16), pltpu.VMEM((d, f), BF16),
                        pltpu.VMEM((f, d), BF16)],
    )
    block_bytes = tm * d * 2 + 3 * d * f * 4 + tm * d * 4
    return pl.pallas_call(
        _expert_kernel,
        out_shape=jax.ShapeDtypeStruct((p, d), F32),
        grid_spec=grid_spec,
        compiler_params=pltpu.CompilerParams(
            dimension_semantics=("arbitrary",),
            vmem_limit_bytes=_vmem_limit(block_bytes, 3 * d * f * 2 + tm * f * 16)),
        name="grouped_swiglu_experts",
    )(tile_expert, n_used, xs, w_gate, w_up, w_down)


def _dispatch(route, counts, n_tok):
    tm = EXPERT_TILE
    ids = jnp.arange(N_EXPERTS, dtype=jnp.int32)
    counts = counts[:N_EXPERTS].astype(jnp.int32)
    padded = ((counts + tm - 1) // tm) * tm
    ends = jnp.cumsum(padded)
    starts = ends - padded
    e = jnp.concatenate([route[:, 0], route[:, 1]]).astype(jnp.int32)
    rank = jnp.concatenate([route[:, 4], route[:, 5]]).astype(jnp.int32)
    start_of = jnp.sum(jnp.where(e[:, None] == ids[None, :], starts[None, :], 0), axis=1)
    pos = start_of + rank
    p = 2 * n_tok + N_EXPERTS * tm
    tok = jnp.concatenate([jnp.arange(n_tok, dtype=jnp.int32)] * 2)
    row_tok = jnp.zeros((p,), jnp.int32).at[pos].set(tok)
    tile_start = jnp.arange(p // tm, dtype=jnp.int32) * tm
    tile_expert = jnp.minimum(jnp.sum(ends[None, :] <= tile_start[:, None], axis=1),
                              N_EXPERTS - 1).astype(jnp.int32)
    n_used = (ends[-1] // tm).astype(jnp.int32).reshape(1)
    return row_tok, tile_expert, n_used, pos


def _final_kernel(x_ref, y_ref, route_ref, mod_ref, g_ref, o_ref):
    o_ref[0] = _rms(_expert_residual(x_ref[0], y_ref, route_ref, mod_ref), g_ref[...])


def _final_norm(x, y, route, mod, g):
    bsz, s, d = x.shape
    tm = TOKEN_TILE
    tok = lambda b, i: (b, i, 0)
    return pl.pallas_call(
        _final_kernel,
        out_shape=jax.ShapeDtypeStruct((bsz, s, d), F32),
        grid=(bsz, s // tm),
        in_specs=[pl.BlockSpec((1, tm, d), tok),
                  pl.BlockSpec((2, 1, tm, d), lambda b, i: (0, b, i, 0)),
                  pl.BlockSpec((1, tm, LANES), tok),
                  pl.BlockSpec((1, 6, d), lambda b, i: (b, 0, 0)),
                  pl.BlockSpec((1, d), lambda b, i: (0, 0))],
        out_specs=pl.BlockSpec((1, tm, d), tok),
        compiler_params=pltpu.CompilerParams(
            dimension_semantics=("arbitrary", "arbitrary"),
            vmem_limit_bytes=_vmem_limit(4 * tm * d * 4 + tm * LANES * 4)),
        name="residual_final_norm",
    )(x, y, route, mod, g.reshape(1, d))


def kernel(x, c, w_ada, b_ada, norm_mix, norm_ffn, norm_final, rel_bias, even_w_in, even_lambda, even_subln, even_w_gk2, even_b_gk, even_gla_norm, even_w_out, odd_w_in, odd_w_out, router_group_w, router_group_b, router_expert_w, router_expert_b, expert_w_gate, expert_w_up, expert_w_down):
    bsz, s, d = x.shape
    n_tok = bsz * s
    assert s % TOKEN_TILE == 0 and s % GLA_STEP == 0 and s % STICK_Q_TILE == 0
    assert STICK_Q_TILE % ATTN_TILE == 0 and s % (DIFF_KEY_TILES * ATTN_TILE) == 0
    mod_all = _modulation(c, w_ada, b_ada).reshape(DEPTH, bsz, 6, d)
    bias_tiles = _diff_bias_tiles(rel_bias)
    n_even_main = 3072
    y = route = prev_mod = None
    for l in range(DEPTH):
        mod = mod_all[l]
        i = l // 2
        if l % 2 == 0:
            w_in = even_w_in[i]
            w_main = w_in[:, :n_even_main].astype(BF16)
            w_gate = jnp.zeros((d, LANES), BF16).at[:, :GATE_RANK].set(
                w_in[:, n_even_main:].astype(BF16))
            x, proj, gate = _in_projection(x, y, route, prev_mod, norm_mix[l], mod, w_main, w_gate)
            oa = _diff_attention(proj, bias_tiles, even_lambda[i], even_subln[i], l)
            ob = _gla(proj, gate, even_w_gk2[i], even_b_gk[i], even_gla_norm[i])
            w_out = even_w_out[i].astype(BF16)
            o_list = [oa, ob]
            w_list = [w_out[:H_A * DV_A], w_out[H_A * DV_A:]]
        else:
            x, proj, _ = _in_projection(x, y, route, prev_mod, norm_mix[l], mod,
                                        odd_w_in[i].astype(BF16), None)
            o_list = [_stick_breaking(proj)]
            w_list = [odd_w_out[i].astype(BF16)]
        w_router = jnp.zeros((d, LANES), F32)
        w_router = w_router.at[:, :N_EXPERTS].set(router_expert_w[l].astype(F32))
        w_router = w_router.at[:, N_EXPERTS:N_EXPERTS + N_GROUPS].set(router_group_w[l].astype(F32))
        b_router = jnp.zeros((1, LANES), F32)
        b_router = b_router.at[0, :N_EXPERTS].set(router_expert_b[l].astype(F32))
        b_router = b_router.at[0, N_EXPERTS:N_EXPERTS + N_GROUPS].set(router_group_b[l].astype(F32))
        x, h2, route, counts = _out_projection(o_list, w_list, x, mod, norm_ffn[l], w_router,
                                               b_router)
        row_tok, tile_expert, n_used, pos = _dispatch(route.reshape(n_tok, LANES), counts[0], n_tok)
        xs = jnp.take(h2.reshape(n_tok, d), row_tok, axis=0)
        out = _grouped_experts(xs, tile_expert, n_used, expert_w_gate[l], expert_w_up[l],
                               expert_w_down[l])
        y = jnp.take(out, pos, axis=0).reshape(2, bsz, s, d)
        prev_mod = mod
    return _final_norm(x, y, route, prev_mod, norm_final)
```

```python
import functools
import math

import numpy as np
import jax
import jax.numpy as jnp
from jax import lax
from jax.experimental import pallas as pl
from jax.experimental.pallas import tpu as pltpu

F32 = jnp.float32
BF16 = jnp.bfloat16

D_MODEL = 1024
DEPTH = 4
CHUNK = 64
EPS = 1e-6
H_A, DH_A, DV_A = 4, 64, 128
H_B, DK_B, DV_B = 4, 64, 128
GATE_RANK = 16
GATE_TAU = 16.0
H_C, DH_C = 16, 64
N_BUCKETS = 32
MAX_DISTANCE = 128
N_GROUPS = 4
EXPERTS_PER_GROUP = 8
N_EXPERTS = N_GROUPS * EXPERTS_PER_GROUP
D_EXPERT = 512
MIX = 1024

LANES = 128
V7X_VMEM_BYTES = 64 * 1024 * 1024
NEG_BIG = -1e30
EXP_ZERO_BELOW = -104.0

TOKEN_TILE = 512
ATTN_TILE = 128
STICK_Q_TILE = 256
DIFF_KEY_TILES = 4
GLA_STEP = 512
EXPERT_TILE = 256


def _vmem_limit(block_bytes, scratch_bytes=0):
    est = 2 * block_bytes + scratch_bytes + 16 * 1024 * 1024
    return int(min(est, V7X_VMEM_BYTES - 8 * 1024 * 1024))


def _nt(a, b):
    return lax.dot_general(a, b, (((1,), (1,)), ((), ())), preferred_element_type=F32)


def _tn(a, b):
    return lax.dot_general(a, b, (((0,), (0,)), ((), ())), preferred_element_type=F32)


def _dot(a, b):
    return jnp.dot(a, b, preferred_element_type=F32)


def _split(x):
    hi = x.astype(BF16)
    lo = (x - hi.astype(F32)).astype(BF16)
    return hi, lo


def _dot3(a, b):
    ah, al = _split(a)
    bh, bl = _split(b)
    return _dot(ah, bh) + _dot(al, bh) + _dot(ah, bl)


def _rms(x, g):
    return x * lax.rsqrt(jnp.mean(x * x, axis=-1, keepdims=True) + EPS) * g


def _softplus(z):
    return jnp.maximum(z, 0.0) + jnp.log(1.0 + jnp.exp(-jnp.abs(z)))


def _silu(x):
    return x / (1.0 + jnp.exp(-x))


def _mod_kernel(c_ref, w_ref, b_ref, o_ref):
    c = c_ref[...]
    o_ref[0] = _dot3(_silu(c), w_ref[0]) + b_ref[0]


def _modulation(c, w_ada, b_ada):
    depth, d, n = w_ada.shape
    bsz = c.shape[0]
    tn = 1536
    return pl.pallas_call(
        _mod_kernel,
        out_shape=jax.ShapeDtypeStruct((depth, bsz, n), F32),
        grid=(depth, n // tn),
        in_specs=[
            pl.BlockSpec((bsz, d), lambda l, j: (0, 0)),
            pl.BlockSpec((1, d, tn), lambda l, j: (l, 0, j)),
            pl.BlockSpec((1, 1, tn), lambda l, j: (l, 0, j)),
        ],
        out_specs=pl.BlockSpec((1, bsz, tn), lambda l, j: (l, 0, j)),
        compiler_params=pltpu.CompilerParams(
            dimension_semantics=("arbitrary", "arbitrary"),
            vmem_limit_bytes=_vmem_limit(d * tn * 4 + bsz * (d + tn) * 4)),
        name="adaln_modulation",
    )(c, w_ada, b_ada.reshape(depth, 1, n))


def _expert_residual(x, y_ref, route_ref, mod_ref):
    route = route_ref[0]
    y = route[:, 2:3] * y_ref[0, 0].astype(F32) + route[:, 3:4] * y_ref[1, 0].astype(F32)
    return x + mod_ref[0, 5:6, :] * y


def _proj_kernel(*refs, has_res, n_main, has_gate):
    it = iter(refs)
    x_ref = next(it)
    if has_res:
        y_ref = next(it)
        route_ref = next(it)
        pmod_ref = next(it)
    g_ref = next(it)
    mod_ref = next(it)
    w_ref = next(it)
    wg_ref = next(it) if has_gate else None
    xo_ref = next(it) if has_res else None
    o_ref = next(it)
    og_ref = next(it) if has_gate else None

    x = x_ref[0]
    if has_res:
        x = _expert_residual(x, y_ref, route_ref, pmod_ref)
        xo_ref[0] = x
    h = _rms(x, g_ref[...]) * (1.0 + mod_ref[0, 1:2, :]) + mod_ref[0, 0:1, :]
    hb = h.astype(BF16)
    step = 512
    for n0 in range(0, n_main, step):
        o_ref[0, :, n0:n0 + step] = _dot(hb, w_ref[:, n0:n0 + step]).astype(BF16)
    if has_gate:
        og_ref[0] = _dot(hb, wg_ref[...])


def _in_projection(x, y, route, prev_mod, g, mod, w, w_gate):
    bsz, s, d = x.shape
    tm = TOKEN_TILE
    has_res = y is not None
    has_gate = w_gate is not None
    n_main = w.shape[1]
    tok = lambda b, i: (b, i, 0)
    per_b = lambda b, i: (b, 0, 0)
    const2 = lambda b, i: (0, 0)
    in_specs = [pl.BlockSpec((1, tm, d), tok)]
    args = [x]
    if has_res:
        in_specs += [pl.BlockSpec((2, 1, tm, d), lambda b, i: (0, b, i, 0)),
                     pl.BlockSpec((1, tm, LANES), tok), pl.BlockSpec((1, 6, d), per_b)]
        args += [y, route, prev_mod]
    in_specs += [pl.BlockSpec((1, d), const2), pl.BlockSpec((1, 6, d), per_b),
                 pl.BlockSpec((d, n_main), const2)]
    args += [g.reshape(1, d), mod, w]
    if has_gate:
        in_specs.append(pl.BlockSpec((d, LANES), const2))
        args.append(w_gate)
    out_shape, out_specs = [], []
    if has_res:
        out_shape.append(jax.ShapeDtypeStruct((bsz, s, d), F32))
        out_specs.append(pl.BlockSpec((1, tm, d), tok))
    out_shape.append(jax.ShapeDtypeStruct((bsz, s, n_main), BF16))
    out_specs.append(pl.BlockSpec((1, tm, n_main), tok))
    if has_gate:
        out_shape.append(jax.ShapeDtypeStruct((bsz, s, LANES), F32))
        out_specs.append(pl.BlockSpec((1, tm, LANES), tok))
    block_bytes = (tm * d * 4 * (4 if has_res else 1) + d * n_main * 2 + tm * n_main * 2
                   + d * LANES * 2 + 2 * tm * LANES * 4)
    outs = pl.pallas_call(
        functools.partial(_proj_kernel, has_res=has_res, n_main=n_main, has_gate=has_gate),
        out_shape=out_shape,
        grid=(bsz, s // tm),
        in_specs=in_specs,
        out_specs=out_specs,
        compiler_params=pltpu.CompilerParams(
            dimension_semantics=("arbitrary", "arbitrary"),
            vmem_limit_bytes=_vmem_limit(block_bytes, tm * d * 8)),
        name="norm_mod_in_projection",
    )(*args)
    outs = list(outs)
    x_new = outs.pop(0) if has_res else x
    proj = outs.pop(0)
    gate = outs.pop(0) if has_gate else None
    return x_new, proj, gate


def _t5_bucket(rel):
    nb = N_BUCKETS // 2
    max_exact = nb // 2
    ret = jnp.where(rel > 0, nb, 0)
    n = jnp.abs(rel)
    nf = jnp.maximum(n, 1).astype(F32)
    large = max_exact + (jnp.log(nf / max_exact) / math.log(MAX_DISTANCE / max_exact)
                         * (nb - max_exact)).astype(jnp.int32)
    large = jnp.minimum(large, nb - 1)
    return ret + jnp.where(n < max_exact, n, large)


def _diff_bias_tiles(rel_bias):
    t = ATTN_TILE
    assert t >= MAX_DISTANCE, "tiles two or more away must lie beyond the last bucket edge"
    qp = jnp.arange(t)[:, None]
    kp = jnp.arange(t)[None, :]
    tiles = []
    for off in (0, 1, 2):
        rel = (kp - off * t) - qp
        b = jnp.transpose(rel_bias.astype(F32)[_t5_bucket(rel)], (2, 0, 1))
        if off == 0:
            b = jnp.where((kp // CHUNK) <= (qp // CHUNK), b, NEG_BIG)
        tiles.append(b)
    tiles.append(jnp.full_like(tiles[0], NEG_BIG))
    tab = jnp.stack(tiles, axis=1)
    return jnp.concatenate([tab, tab], axis=2)


def _diff_attn_kernel(q_ref, k_ref, v_ref, bias_ref, lam_ref, g_ref, o_ref, *, lambda_init):
    t = ATTN_TILE
    i = pl.program_id(2)
    q = q_ref[0]
    lane = lax.broadcasted_iota(jnp.int32, (t, LANES), 1)
    scale = jnp.asarray(DH_A ** -0.5, BF16)
    zero = jnp.zeros_like(q)
    qq = jnp.concatenate([jnp.where(lane < DH_A, q, zero), jnp.where(lane >= DH_A, q, zero)],
                         axis=0) * scale

    group = DIFF_KEY_TILES
    span = group * t

    def body(n, carry):
        m, l, acc = carry
        r0 = pl.multiple_of(n * span, span)
        kb = k_ref[0, pl.ds(r0, span), :]
        vb = v_ref[0, pl.ds(r0, span), :]
        s = _nt(qq, kb)
        parts = []
        for u in range(group):
            back = i - (n * group + u)
            which = jnp.where(back < 0, 3, jnp.minimum(back, 2))
            parts.append(s[:, u * t:(u + 1) * t] + bias_ref[0, which])
        s = jnp.concatenate(parts, axis=1)
        m_new = jnp.maximum(m, jnp.max(s, axis=-1, keepdims=True))
        alpha = jnp.exp(m - m_new)
        p = jnp.exp(s - m_new)
        l = alpha * l + jnp.sum(p, axis=-1, keepdims=True)
        acc = alpha * acc + _dot(p.astype(BF16), vb)
        return m_new, l, acc

    init = (jnp.full((2 * t, 1), NEG_BIG, F32), jnp.zeros((2 * t, 1), F32),
            jnp.zeros((2 * t, DV_A), F32))
    _, l, acc = lax.fori_loop(0, i // group + 1, body, init)
    o = acc / l
    lp = lam_ref[...].astype(F32)
    lam = (jnp.exp(jnp.sum(lp[0:1] * lp[1:2], axis=-1, keepdims=True))
           - jnp.exp(jnp.sum(lp[2:3] * lp[3:4], axis=-1, keepdims=True)) + lambda_init)
    w = o[:t] - lam * o[t:]
    o_ref[0] = (_rms(w, g_ref[...]) * (1.0 - lambda_init)).astype(o_ref.dtype)


def _diff_attention(proj, bias_tiles, lam_p, subln_g, layer_idx):
    bsz, s, _ = proj.shape
    t = ATTN_TILE
    lambda_init = 0.8 - 0.6 * math.exp(-0.3 * layer_idx)
    return pl.pallas_call(
        functools.partial(_diff_attn_kernel, lambda_init=lambda_init),
        out_shape=jax.ShapeDtypeStruct((bsz, s, H_A * DV_A), BF16),
        grid=(bsz, H_A, s // t),
        in_specs=[
            pl.BlockSpec((1, t, LANES), lambda b, h, i: (b, i, h)),
            pl.BlockSpec((1, s, LANES), lambda b, h, i: (b, 0, H_A + h)),
            pl.BlockSpec((1, s, LANES), lambda b, h, i: (b, 0, 2 * H_A + h)),
            pl.BlockSpec((1, 4, 2 * t, t), lambda b, h, i: (h, 0, 0, 0)),
            pl.BlockSpec((4, DH_A), lambda b, h, i: (0, 0)),
            pl.BlockSpec((1, DV_A), lambda b, h, i: (0, 0)),
        ],
        out_specs=pl.BlockSpec((1, t, DV_A), lambda b, h, i: (b, i, h)),
        compiler_params=pltpu.CompilerParams(
            dimension_semantics=("arbitrary", "arbitrary", "arbitrary"),
            vmem_limit_bytes=_vmem_limit(2 * s * LANES * 2 + 4 * 2 * t * t * 4 + 4 * t * LANES)),
        name="diff_attention",
    )(proj, proj, proj, bias_tiles, lam_p, subln_g.reshape(1, DV_A))


def _gla_kernel(q_ref, k_ref, v_ref, r_ref, bg_ref, wgk_ref, bgk_ref, g_ref, o_ref, st_ref):
    c = CHUNK

    @pl.when(pl.program_id(1) == 0)
    def _():
        st_ref[...] = jnp.zeros_like(st_ref)

    row = lax.broadcasted_iota(jnp.int32, (c, c), 0)
    col = lax.broadcasted_iota(jnp.int32, (c, c), 1)
    causal = col <= row
    tri = jnp.where(causal, 1.0, 0.0).astype(BF16)
    lane = lax.broadcasted_iota(jnp.int32, (c, LANES), 1)
    wgk = wgk_ref[...]
    bgk = bgk_ref[...]
    g = g_ref[...]

    def chunk(ci, carry):
        r0 = pl.multiple_of(ci * c, c)
        bg = bg_ref[0, pl.ds(r0, c), :]
        for hp in range(H_B // 2):
            cols = slice(hp * LANES, (hp + 1) * LANES)
            la = -_softplus(-(_dot3(bg, wgk[:, cols]) + bgk[:, cols])) / GATE_TAU
            la_hi, la_lo = _split(la)
            b = _dot(tri, la_hi) + _dot(tri, la_lo)
            b_last = b[c - 1:c, :]
            qf = q_ref[0, pl.ds(r0, c), cols].astype(F32) * (DK_B ** -0.5)
            kf = k_ref[0, pl.ds(r0, c), cols].astype(F32)
            q_dec = qf * jnp.exp(b)
            k_inv = (kf * jnp.exp(-b)).astype(BF16)
            k_dec = kf * jnp.exp(b_last - b)
            decay = jnp.exp(b_last)
            for hh in range(2):
                h = 2 * hp + hh
                own = (lane >= hh * DK_B) & (lane < (hh + 1) * DK_B)
                qd = jnp.where(own, q_dec, 0.0).astype(BF16)
                kd = jnp.where(own, k_dec, 0.0).astype(BF16)
                vcols = slice(h * DV_B, (h + 1) * DV_B)
                vb = v_ref[0, pl.ds(r0, c), vcols]
                sc = jnp.where(causal, _nt(qd, k_inv), 0.0)
                st = st_ref[h]
                o = _dot(sc.astype(BF16), vb) + _nt(qd, st.astype(BF16))
                st_ref[h] = st * decay + _tn(vb, kd)
                rr = r_ref[0, pl.ds(r0, c), vcols].astype(F32)
                o_ref[0, pl.ds(r0, c), vcols] = (_rms(o, g) * _silu(rr)).astype(o_ref.dtype)
        return carry

    lax.fori_loop(0, q_ref.shape[1] // c, chunk, 0)


def _gla(proj, gate, w_gk2, b_gk, gla_g):
    bsz, s, _ = proj.shape
    ts = GLA_STEP
    nk = H_B * DK_B
    nv = H_B * DV_B
    w_pad = jnp.zeros((LANES, nk), F32).at[:GATE_RANK].set(w_gk2.astype(F32))
    return pl.pallas_call(
        _gla_kernel,
        out_shape=jax.ShapeDtypeStruct((bsz, s, nv), BF16),
        grid=(bsz, s // ts),
        in_specs=[
            pl.BlockSpec((1, ts, nk), lambda b, i: (b, i, 1536 // nk)),
            pl.BlockSpec((1, ts, nk), lambda b, i: (b, i, 1792 // nk)),
            pl.BlockSpec((1, ts, nv), lambda b, i: (b, i, 2048 // nv)),
            pl.BlockSpec((1, ts, nv), lambda b, i: (b, i, 2560 // nv)),
            pl.BlockSpec((1, ts, LANES), lambda b, i: (b, i, 0)),
            pl.BlockSpec((LANES, nk), lambda b, i: (0, 0)),
            pl.BlockSpec((1, nk), lambda b, i: (0, 0)),
            pl.BlockSpec((1, DV_B), lambda b, i: (0, 0)),
        ],
        out_specs=pl.BlockSpec((1, ts, nv), lambda b, i: (b, i, 0)),
        scratch_shapes=[pltpu.VMEM((H_B, DV_B, LANES), F32)],
        compiler_params=pltpu.CompilerParams(
            dimension_semantics=("arbitrary", "arbitrary"),
            vmem_limit_bytes=_vmem_limit(ts * (2 * nk + 3 * nv) * 2 + ts * LANES * 4,
                                         H_B * DV_B * LANES * 4)),
        name="gated_linear_attention",
    )(proj, proj, proj, proj, gate, w_pad, b_gk.reshape(1, nk), gla_g.reshape(1, DV_B))


def _stick_kernel(q_ref, k_ref, v_ref, o_ref):
    tq, tk = STICK_Q_TILE, ATTN_TILE
    per_q = tq // tk
    i = pl.program_id(2)
    q = q_ref[0]
    lane_q = lax.broadcasted_iota(jnp.int32, (tq, LANES), 1)
    scale = jnp.asarray(DH_C ** -0.5, BF16)
    zero_q = jnp.zeros_like(q)
    qq = jnp.concatenate([jnp.where(lane_q < DH_C, q, zero_q),
                          jnp.where(lane_q >= DH_C, q, zero_q)], axis=0) * scale
    row = lax.broadcasted_iota(jnp.int32, (tk, tk), 0)
    col = lax.broadcasted_iota(jnp.int32, (tk, tk), 1)
    later = jnp.where(row > col, 1.0, 0.0).astype(BF16)
    lane_k = lax.broadcasted_iota(jnp.int32, (tk, LANES), 1)
    q_in_tile = lax.broadcasted_iota(jnp.int32, (2 * tq, tk), 0) % tq
    k_in_tile = lax.broadcasted_iota(jnp.int32, (2 * tq, tk), 1)

    def logits(first):
        zs = []
        for u in range(per_q):
            r0 = pl.multiple_of(jnp.maximum(first - u, 0) * tk, tk)
            zs.append(_nt(qq, k_ref[0, pl.ds(r0, tk), :]))
        return zs

    def tiles(first, zs, run, acc, key_offsets):
        vbs = []
        for u in range(len(key_offsets)):
            r0 = pl.multiple_of((first - u) * tk, tk)
            vbs.append(v_ref[0, pl.ds(r0, tk), :])
        mids = []
        for z, off in zip(zs, key_offsets):
            sp = _softplus(z)
            log_1m = -sp
            keep = None
            if off is not None:
                keep = (k_in_tile + off) < q_in_tile
                log_1m = jnp.where(keep, log_1m, 0.0)
            hi, lo = _split(log_1m)
            within = _dot(hi, later) + _dot(lo, later)
            mids.append((z - sp, log_1m, within, keep))
        for (log_b, log_1m, within, keep), vb in zip(mids, vbs):
            a = jnp.exp(log_b + (within + run))
            if keep is not None:
                a = jnp.where(keep, a, 0.0)
            ab = a.astype(BF16)
            zero_v = jnp.zeros_like(vb)
            a2 = jnp.concatenate([ab[:tq], ab[tq:]], axis=1)
            v2 = jnp.concatenate([jnp.where(lane_k < DH_C, vb, zero_v),
                                  jnp.where(lane_k >= DH_C, vb, zero_v)], axis=0)
            acc = acc + _dot(a2, v2)
            run = run + jnp.sum(log_1m, axis=-1, keepdims=True)
        return run, acc

    n_left = i * per_q

    def go_on(done, run):
        return ((done < n_left) & (jnp.max(run) > EXP_ZERO_BELOW)).astype(jnp.int32)

    run = jnp.zeros((2 * tq, 1), F32)
    acc = jnp.zeros((tq, LANES), F32)
    zs = logits(n_left + per_q - 1)
    zs_next = logits(n_left - 1)
    run, acc = tiles(n_left + per_q - 1, zs, run, acc,
                     [(per_q - 1 - d) * tk for d in range(per_q)])

    def body(carry):
        done, _, run, acc, zs = carry
        first = n_left - 1 - done
        zs_next = logits(first - per_q)
        run, acc = tiles(first, zs, run, acc, [None] * per_q)
        done = done + per_q
        return done, go_on(done, run), run, acc, zs_next

    carry = (jnp.int32(0), go_on(0, run), run, acc, zs_next)
    acc = lax.while_loop(lambda carry: carry[1] > 0, body, carry)[3]
    o_ref[0] = acc.astype(o_ref.dtype)


def _stick_breaking(proj):
    bsz, s, _ = proj.shape
    t = STICK_Q_TILE
    n_tiles = H_C * DH_C // LANES
    return pl.pallas_call(
        _stick_kernel,
        out_shape=jax.ShapeDtypeStruct((bsz, s, H_C * DH_C), BF16),
        grid=(bsz, n_tiles, s // t),
        in_specs=[
            pl.BlockSpec((1, t, LANES), lambda b, h, i: (b, i, h)),
            pl.BlockSpec((1, s, LANES), lambda b, h, i: (b, 0, n_tiles + h)),
            pl.BlockSpec((1, s, LANES), lambda b, h, i: (b, 0, 2 * n_tiles + h)),
        ],
        out_specs=pl.BlockSpec((1, t, LANES), lambda b, h, i: (b, i, h)),
        compiler_params=pltpu.CompilerParams(
            dimension_semantics=("arbitrary", "arbitrary", "arbitrary"),
            vmem_limit_bytes=_vmem_limit(2 * s * LANES * 2 + 4 * t * LANES)),
        name="stick_breaking_attention",
    )(proj, proj, proj)


def _route(logits):
    lane_i = lax.broadcasted_iota(jnp.int32, logits.shape, 1)
    lane = lane_i.astype(F32)
    group_of_lane = (lane_i // EXPERTS_PER_GROUP).astype(F32)
    big = jnp.float32(1 << 20)
    neg = jnp.float32(-jnp.inf)
    is_group = (lane_i >= N_EXPERTS) & (lane_i < N_EXPERTS + N_GROUPS)
    gl = jnp.where(is_group, logits, neg)
    gmax = jnp.max(gl, axis=-1, keepdims=True)
    g_val = 1.0 / jnp.sum(jnp.exp(gl - gmax), axis=-1, keepdims=True)
    g_sel = jnp.min(jnp.where(gl == gmax, lane - N_EXPERTS, big), axis=-1, keepdims=True)
    in_group = (lane_i < N_EXPERTS) & (group_of_lane == g_sel)
    el = jnp.where(in_group, logits, neg)
    emax = jnp.max(el, axis=-1, keepdims=True)
    esum = jnp.sum(jnp.exp(el - emax), axis=-1, keepdims=True)
    i1 = jnp.min(jnp.where(el == emax, lane, big), axis=-1, keepdims=True)
    el2 = jnp.where(lane == i1, neg, el)
    emax2 = jnp.max(el2, axis=-1, keepdims=True)
    i2 = jnp.min(jnp.where(el2 == emax2, lane, big), axis=-1, keepdims=True)
    v1 = 1.0 / esum
    v2 = jnp.exp(emax2 - emax) / esum
    tot = v1 + v2
    w1 = g_val * (v1 / tot)
    w2 = g_val * (v2 / tot)
    out = jnp.where(lane_i == 0, i1, 0.0)
    out = jnp.where(lane_i == 1, i2, out)
    out = jnp.where(lane_i == 2, w1, out)
    out = jnp.where(lane_i == 3, w2, out)
    return out, (lane == i1), (lane == i2)


def _out_kernel(*refs, n_o):
    o_refs = refs[:n_o]
    w_refs = refs[n_o:2 * n_o]
    x_ref, mod_ref, g_ref, wr_ref, br_ref, xo_ref, h_ref, r_ref, cnt_ref, run_ref = refs[2 * n_o:]

    @pl.when((pl.program_id(0) == 0) & (pl.program_id(1) == 0))
    def _():
        run_ref[...] = jnp.zeros_like(run_ref)

    m = _dot(o_refs[0][0], w_refs[0][...])
    for a in range(1, n_o):
        m = m + _dot(o_refs[a][0], w_refs[a][...])
    x = x_ref[0] + mod_ref[0, 2:3, :] * m
    xo_ref[0] = x
    h = _rms(x, g_ref[...]) * (1.0 + mod_ref[0, 4:5, :]) + mod_ref[0, 3:4, :]
    h_ref[0] = h.astype(BF16)
    route, hit1, hit2 = _route(_dot3(h, wr_ref[...]) + br_ref[...])
    tm = route.shape[0]
    picks = jnp.where(hit1 | hit2, 1.0, 0.0)
    row = lax.broadcasted_iota(jnp.int32, (tm, tm), 0)
    col = lax.broadcasted_iota(jnp.int32, (tm, tm), 1)
    before = jnp.where(col < row, 1.0, 0.0).astype(BF16)
    earlier = _dot(before, picks.astype(BF16)) + run_ref[...]
    rank1 = jnp.sum(jnp.where(hit1, earlier, 0.0), axis=-1, keepdims=True)
    rank2 = jnp.sum(jnp.where(hit2, earlier, 0.0), axis=-1, keepdims=True)
    lane_i = lax.broadcasted_iota(jnp.int32, route.shape, 1)
    route = jnp.where(lane_i == 4, rank1, route)
    route = jnp.where(lane_i == 5, rank2, route)
    r_ref[0] = route
    run_ref[...] = run_ref[...] + jnp.sum(picks, axis=0, keepdims=True)
    cnt_ref[...] = run_ref[...]


def _out_projection(o_list, w_list, x, mod, g, w_router, b_router):
    bsz, s, d = x.shape
    tm = TOKEN_TILE
    n_o = len(o_list)
    tok = lambda b, i: (b, i, 0)
    const2 = lambda b, i: (0, 0)
    in_specs = ([pl.BlockSpec((1, tm, o.shape[2]), tok) for o in o_list]
                + [pl.BlockSpec(w.shape, const2) for w in w_list]
                + [pl.BlockSpec((1, tm, d), tok), pl.BlockSpec((1, 6, d), lambda b, i: (b, 0, 0)),
                   pl.BlockSpec((1, d), const2), pl.BlockSpec((d, LANES), const2),
                   pl.BlockSpec((1, LANES), const2)])
    block_bytes = (sum(tm * o.shape[2] * 2 for o in o_list) + sum(w.size * 2 for w in w_list)
                   + 2 * tm * d * 4 + tm * d * 2 + d * LANES * 4 + tm * LANES * 4)
    return pl.pallas_call(
        functools.partial(_out_kernel, n_o=n_o),
        out_shape=[jax.ShapeDtypeStruct((bsz, s, d), F32),
                   jax.ShapeDtypeStruct((bsz, s, d), BF16),
                   jax.ShapeDtypeStruct((bsz, s, LANES), F32),
                   jax.ShapeDtypeStruct((1, LANES), F32)],
        grid=(bsz, s // tm),
        in_specs=in_specs,
        out_specs=[pl.BlockSpec((1, tm, d), tok), pl.BlockSpec((1, tm, d), tok),
                   pl.BlockSpec((1, tm, LANES), tok), pl.BlockSpec((1, LANES), const2)],
        scratch_shapes=[pltpu.VMEM((1, LANES), F32)],
        compiler_params=pltpu.CompilerParams(
            dimension_semantics=("arbitrary", "arbitrary"),
            vmem_limit_bytes=_vmem_limit(block_bytes, tm * d * 12 + tm * tm * 4)),
        name="out_projection_norm_router",
    )(*o_list, *w_list, x, mod, g.reshape(1, d), w_router, b_router)


def _expert_kernel(te_ref, nu_ref, x_ref, wg_ref, wu_ref, wd_ref, o_ref, wgb_ref, wub_ref, wdb_ref):
    t = pl.program_id(0)
    prev = te_ref[jnp.maximum(t - 1, 0)]

    @pl.when((t == 0) | (te_ref[t] != prev))
    def _():
        wgb_ref[...] = wg_ref[0, 0].astype(BF16)
        wub_ref[...] = wu_ref[0, 0].astype(BF16)
        wdb_ref[...] = wd_ref[0, 0].astype(BF16)

    @pl.when(t < nu_ref[0])
    def _():
        x = x_ref[...]
        gt = _dot(x, wgb_ref[...])
        up = _dot(x, wub_ref[...])
        o_ref[...] = _dot((_silu(gt) * up).astype(BF16), wdb_ref[...]).astype(o_ref.dtype)

    @pl.when(t >= nu_ref[0])
    def _():
        o_ref[...] = jnp.zeros_like(o_ref)


def _grouped_experts(xs, tile_expert, n_used, w_gate, w_up, w_down, layer):
    p, d = xs.shape
    tm = EXPERT_TILE
    f = w_gate.shape[3]
    grid_spec = pltpu.PrefetchScalarGridSpec(
        num_scalar_prefetch=2,
        grid=(p // tm,),
        in_specs=[
            pl.BlockSpec((tm, d), lambda t, te, nu: (t, 0)),
            pl.BlockSpec((1, 1, d, f), lambda t, te, nu: (layer, te[t], 0, 0)),
            pl.BlockSpec((1, 1, d, f), lambda t, te, nu: (layer, te[t], 0, 0)),
            pl.BlockSpec((1, 1, f, d), lambda t, te, nu: (layer, te[t], 0, 0)),
        ],
        out_specs=pl.BlockSpec((tm, d), lambda t, te, nu: (t, 0)),
        scratch_shapes=[pltpu.VMEM((d, f), BF16), pltpu.VMEM((d, f), BF16),
                        pltpu.VMEM((f, d), BF16)],
    )
    block_bytes = tm * d * 2 + 3 * d * f * 4 + tm * d * 2
    return pl.pallas_call(
        _expert_kernel,
        out_shape=jax.ShapeDtypeStruct((p, d), BF16),
        grid_spec=grid_spec,
        compiler_params=pltpu.CompilerParams(
            dimension_semantics=("arbitrary",),
            vmem_limit_bytes=_vmem_limit(block_bytes, 3 * d * f * 2 + tm * f * 16)),
        name="grouped_swiglu_experts",
    )(tile_expert, n_used, xs, w_gate, w_up, w_down)


def _pos_kernel(route_ref, starts_ref, pos_ref):
    rt = jnp.transpose(route_ref[...])
    starts = starts_ref[...]
    ids = lax.broadcasted_iota(jnp.int32, (N_EXPERTS, rt.shape[1]), 0).astype(F32)
    for c in range(2):
        start_of = jnp.sum(jnp.where(ids == rt[c:c + 1, :], starts, 0.0), axis=0, keepdims=True)
        pos_ref[c:c + 1, :] = (start_of + rt[4 + c:5 + c, :]).astype(jnp.int32)


def _row_positions(route, starts):
    n_tok = route.shape[0]
    tm = TOKEN_TILE
    return pl.pallas_call(
        _pos_kernel,
        out_shape=jax.ShapeDtypeStruct((2, n_tok), jnp.int32),
        grid=(n_tok // tm,),
        in_specs=[pl.BlockSpec((tm, LANES), lambda i: (i, 0)),
                  pl.BlockSpec((N_EXPERTS, 1), lambda i: (0, 0))],
        out_specs=pl.BlockSpec((2, tm), lambda i: (0, i)),
        compiler_params=pltpu.CompilerParams(
            dimension_semantics=("arbitrary",),
            vmem_limit_bytes=_vmem_limit(tm * LANES * 4 + N_EXPERTS * LANES * 4 + 8 * tm * 4)),
        name="expert_row_positions",
    )(route, starts.astype(F32).reshape(N_EXPERTS, 1))


def _dispatch(route, counts, n_tok):
    tm = EXPERT_TILE
    counts = counts[:N_EXPERTS].astype(jnp.int32)
    padded = ((counts + tm - 1) // tm) * tm
    ends = jnp.cumsum(padded)
    starts = ends - padded
    pos = _row_positions(route, starts).reshape(2 * n_tok)
    p = 2 * n_tok + N_EXPERTS * tm
    tok = jnp.concatenate([jnp.arange(n_tok, dtype=jnp.int32)] * 2)
    row_tok = jnp.zeros((p,), jnp.int32).at[pos].set(tok)
    tile_start = jnp.arange(p // tm, dtype=jnp.int32) * tm
    tile_expert = jnp.minimum(jnp.sum(ends[None, :] <= tile_start[:, None], axis=1),
                              N_EXPERTS - 1).astype(jnp.int32)
    n_used = (ends[-1] // tm).astype(jnp.int32).reshape(1)
    return row_tok, tile_expert, n_used, pos


def _final_kernel(x_ref, y_ref, route_ref, mod_ref, g_ref, o_ref):
    o_ref[0] = _rms(_expert_residual(x_ref[0], y_ref, route_ref, mod_ref), g_ref[...])


def _final_norm(x, y, route, mod, g):
    bsz, s, d = x.shape
    tm = TOKEN_TILE
    tok = lambda b, i: (b, i, 0)
    return pl.pallas_call(
        _final_kernel,
        out_shape=jax.ShapeDtypeStruct((bsz, s, d), F32),
        grid=(bsz, s // tm),
        in_specs=[pl.BlockSpec((1, tm, d), tok),
                  pl.BlockSpec((2, 1, tm, d), lambda b, i: (0, b, i, 0)),
                  pl.BlockSpec((1, tm, LANES), tok),
                  pl.BlockSpec((1, 6, d), lambda b, i: (b, 0, 0)),
                  pl.BlockSpec((1, d), lambda b, i: (0, 0))],
        out_specs=pl.BlockSpec((1, tm, d), tok),
        compiler_params=pltpu.CompilerParams(
            dimension_semantics=("arbitrary", "arbitrary"),
            vmem_limit_bytes=_vmem_limit(4 * tm * d * 4 + tm * LANES * 4)),
        name="residual_final_norm",
    )(x, y, route, mod, g.reshape(1, d))


def kernel(x, c, w_ada, b_ada, norm_mix, norm_ffn, norm_final, rel_bias, even_w_in, even_lambda, even_subln, even_w_gk2, even_b_gk, even_gla_norm, even_w_out, odd_w_in, odd_w_out, router_group_w, router_group_b, router_expert_w, router_expert_b, expert_w_gate, expert_w_up, expert_w_down):
    bsz, s, d = x.shape
    n_tok = bsz * s
    assert s % TOKEN_TILE == 0 and s % GLA_STEP == 0 and s % STICK_Q_TILE == 0
    assert STICK_Q_TILE % ATTN_TILE == 0 and s % (DIFF_KEY_TILES * ATTN_TILE) == 0
    mod_all = _modulation(c, w_ada, b_ada).reshape(DEPTH, bsz, 6, d)
    bias_tiles = _diff_bias_tiles(rel_bias)
    n_even_main = 3072
    y = route = prev_mod = None
    for l in range(DEPTH):
        mod = mod_all[l]
        i = l // 2
        if l % 2 == 0:
            w_in = even_w_in[i]
            w_main = w_in[:, :n_even_main].astype(BF16)
            w_gate = jnp.zeros((d, LANES), BF16).at[:, :GATE_RANK].set(
                w_in[:, n_even_main:].astype(BF16))
            x, proj, gate = _in_projection(x, y, route, prev_mod, norm_mix[l], mod, w_main, w_gate)
            oa = _diff_attention(proj, bias_tiles, even_lambda[i], even_subln[i], l)
            ob = _gla(proj, gate, even_w_gk2[i], even_b_gk[i], even_gla_norm[i])
            w_out = even_w_out[i].astype(BF16)
            o_list = [oa, ob]
            w_list = [w_out[:H_A * DV_A], w_out[H_A * DV_A:]]
        else:
            x, proj, _ = _in_projection(x, y, route, prev_mod, norm_mix[l], mod,
                                        odd_w_in[i].astype(BF16), None)
            o_list = [_stick_breaking(proj)]
            w_list = [odd_w_out[i].astype(BF16)]
        w_router = jnp.zeros((d, LANES), F32)
        w_router = w_router.at[:, :N_EXPERTS].set(router_expert_w[l].astype(F32))
        w_router = w_router.at[:, N_EXPERTS:N_EXPERTS + N_GROUPS].set(router_group_w[l].astype(F32))
        b_router = jnp.zeros((1, LANES), F32)
        b_router = b_router.at[0, :N_EXPERTS].set(router_expert_b[l].astype(F32))
        b_router = b_router.at[0, N_EXPERTS:N_EXPERTS + N_GROUPS].set(router_group_b[l].astype(F32))
        x, h2, route, counts = _out_projection(o_list, w_list, x, mod, norm_ffn[l], w_router,
                                               b_router)
        row_tok, tile_expert, n_used, pos = _dispatch(route.reshape(n_tok, LANES), counts[0], n_tok)
        xs = jnp.take(h2.reshape(n_tok, d), row_tok, axis=0)
        out = _grouped_experts(xs, tile_expert, n_used, expert_w_gate, expert_w_up,
                               expert_w_down, l)
        y = jnp.take(out, pos, axis=0).reshape(2, bsz, s, d)
        prev_mod = mod
    return _final_norm(x, y, route, prev_mod, norm_final)
```

```python
import functools
import math

import numpy as np
import jax
import jax.numpy as jnp
from jax import lax
from jax.experimental import pallas as pl
from jax.experimental.pallas import tpu as pltpu

F32 = jnp.float32
BF16 = jnp.bfloat16

D_MODEL = 1024
DEPTH = 4
CHUNK = 64
EPS = 1e-6
H_A, DH_A, DV_A = 4, 64, 128
H_B, DK_B, DV_B = 4, 64, 128
GATE_RANK = 16
GATE_TAU = 16.0
H_C, DH_C = 16, 64
N_BUCKETS = 32
MAX_DISTANCE = 128
N_GROUPS = 4
EXPERTS_PER_GROUP = 8
N_EXPERTS = N_GROUPS * EXPERTS_PER_GROUP
D_EXPERT = 512
MIX = 1024

LANES = 128
V7X_VMEM_BYTES = 64 * 1024 * 1024
NEG_BIG = -1e30
EXP_ZERO_BELOW = -104.0

TOKEN_TILE = 512
ATTN_TILE = 128
STICK_Q_TILE = 256
DIFF_KEY_TILES = 4
DIFF_Q_TILES = 2
GLA_STEP = 512
EXPERT_TILE = 256


def _vmem_limit(block_bytes, scratch_bytes=0):
    est = 2 * block_bytes + scratch_bytes + 16 * 1024 * 1024
    return int(min(est, V7X_VMEM_BYTES - 8 * 1024 * 1024))


def _nt(a, b):
    return lax.dot_general(a, b, (((1,), (1,)), ((), ())), preferred_element_type=F32)


def _tn(a, b):
    return lax.dot_general(a, b, (((0,), (0,)), ((), ())), preferred_element_type=F32)


def _dot(a, b):
    return jnp.dot(a, b, preferred_element_type=F32)


def _split(x):
    hi = x.astype(BF16)
    lo = (x - hi.astype(F32)).astype(BF16)
    return hi, lo


def _dot3(a, b):
    ah, al = _split(a)
    bh, bl = _split(b)
    return _dot(ah, bh) + _dot(al, bh) + _dot(ah, bl)


def _rms(x, g):
    return x * lax.rsqrt(jnp.mean(x * x, axis=-1, keepdims=True) + EPS) * g


def _softplus(z):
    return jnp.maximum(z, 0.0) + jnp.log(1.0 + jnp.exp(-jnp.abs(z)))


def _silu(x):
    return x / (1.0 + jnp.exp(-x))


def _mod_kernel(c_ref, w_ref, b_ref, o_ref):
    c = c_ref[...]
    o_ref[0] = _dot3(_silu(c), w_ref[0]) + b_ref[0]


def _modulation(c, w_ada, b_ada):
    depth, d, n = w_ada.shape
    bsz = c.shape[0]
    tn = 1536
    return pl.pallas_call(
        _mod_kernel,
        out_shape=jax.ShapeDtypeStruct((depth, bsz, n), F32),
        grid=(depth, n // tn),
        in_specs=[
            pl.BlockSpec((bsz, d), lambda l, j: (0, 0)),
            pl.BlockSpec((1, d, tn), lambda l, j: (l, 0, j)),
            pl.BlockSpec((1, 1, tn), lambda l, j: (l, 0, j)),
        ],
        out_specs=pl.BlockSpec((1, bsz, tn), lambda l, j: (l, 0, j)),
        compiler_params=pltpu.CompilerParams(
            dimension_semantics=("arbitrary", "arbitrary"),
            vmem_limit_bytes=_vmem_limit(d * tn * 4 + bsz * (d + tn) * 4)),
        name="adaln_modulation",
    )(c, w_ada, b_ada.reshape(depth, 1, n))


def _expert_residual(x, y_ref, route_ref, mod_ref):
    route = route_ref[0]
    y = route[:, 2:3] * y_ref[0, 0].astype(F32) + route[:, 3:4] * y_ref[1, 0].astype(F32)
    return x + mod_ref[0, 5:6, :] * y


def _proj_kernel(*refs, has_res, n_main, has_gate):
    it = iter(refs)
    x_ref = next(it)
    if has_res:
        y_ref = next(it)
        route_ref = next(it)
        pmod_ref = next(it)
    g_ref = next(it)
    mod_ref = next(it)
    w_ref = next(it)
    wg_ref = next(it) if has_gate else None
    xo_ref = next(it) if has_res else None
    o_ref = next(it)
    og_ref = next(it) if has_gate else None

    x = x_ref[0]
    if has_res:
        x = _expert_residual(x, y_ref, route_ref, pmod_ref)
        xo_ref[0] = x
    h = _rms(x, g_ref[...]) * (1.0 + mod_ref[0, 1:2, :]) + mod_ref[0, 0:1, :]
    hb = h.astype(BF16)
    step = 512
    for n0 in range(0, n_main, step):
        o_ref[0, :, n0:n0 + step] = _dot(hb, w_ref[:, n0:n0 + step]).astype(BF16)
    if has_gate:
        og_ref[0] = _dot(hb, wg_ref[...])


def _in_projection(x, y, route, prev_mod, g, mod, w, w_gate):
    bsz, s, d = x.shape
    tm = TOKEN_TILE
    has_res = y is not None
    has_gate = w_gate is not None
    n_main = w.shape[1]
    tok = lambda b, i: (b, i, 0)
    per_b = lambda b, i: (b, 0, 0)
    const2 = lambda b, i: (0, 0)
    in_specs = [pl.BlockSpec((1, tm, d), tok)]
    args = [x]
    if has_res:
        in_specs += [pl.BlockSpec((2, 1, tm, d), lambda b, i: (0, b, i, 0)),
                     pl.BlockSpec((1, tm, LANES), tok), pl.BlockSpec((1, 6, d), per_b)]
        args += [y, route, prev_mod]
    in_specs += [pl.BlockSpec((1, d), const2), pl.BlockSpec((1, 6, d), per_b),
                 pl.BlockSpec((d, n_main), const2)]
    args += [g.reshape(1, d), mod, w]
    if has_gate:
        in_specs.append(pl.BlockSpec((d, LANES), const2))
        args.append(w_gate)
    out_shape, out_specs = [], []
    if has_res:
        out_shape.append(jax.ShapeDtypeStruct((bsz, s, d), F32))
        out_specs.append(pl.BlockSpec((1, tm, d), tok))
    out_shape.append(jax.ShapeDtypeStruct((bsz, s, n_main), BF16))
    out_specs.append(pl.BlockSpec((1, tm, n_main), tok))
    if has_gate:
        out_shape.append(jax.ShapeDtypeStruct((bsz, s, LANES), F32))
        out_specs.append(pl.BlockSpec((1, tm, LANES), tok))
    block_bytes = (tm * d * 4 * (4 if has_res else 1) + d * n_main * 2 + tm * n_main * 2
                   + d * LANES * 2 + 2 * tm * LANES * 4)
    outs = pl.pallas_call(
        functools.partial(_proj_kernel, has_res=has_res, n_main=n_main, has_gate=has_gate),
        out_shape=out_shape,
        grid=(bsz, s // tm),
        in_specs=in_specs,
        out_specs=out_specs,
        compiler_params=pltpu.CompilerParams(
            dimension_semantics=("arbitrary", "arbitrary"),
            vmem_limit_bytes=_vmem_limit(block_bytes, tm * d * 8)),
        name="norm_mod_in_projection",
    )(*args)
    outs = list(outs)
    x_new = outs.pop(0) if has_res else x
    proj = outs.pop(0)
    gate = outs.pop(0) if has_gate else None
    return x_new, proj, gate


def _t5_bucket(rel):
    nb = N_BUCKETS // 2
    max_exact = nb // 2
    ret = jnp.where(rel > 0, nb, 0)
    n = jnp.abs(rel)
    nf = jnp.maximum(n, 1).astype(F32)
    large = max_exact + (jnp.log(nf / max_exact) / math.log(MAX_DISTANCE / max_exact)
                         * (nb - max_exact)).astype(jnp.int32)
    large = jnp.minimum(large, nb - 1)
    return ret + jnp.where(n < max_exact, n, large)


def _diff_bias_tiles(rel_bias):
    t = ATTN_TILE
    assert t >= MAX_DISTANCE, "tiles two or more away must lie beyond the last bucket edge"
    qp = jnp.arange(t)[:, None]
    kp = jnp.arange(t)[None, :]
    tiles = []
    for off in (0, 1, 2):
        rel = (kp - off * t) - qp
        b = jnp.transpose(rel_bias.astype(F32)[_t5_bucket(rel)], (2, 0, 1))
        if off == 0:
            b = jnp.where((kp // CHUNK) <= (qp // CHUNK), b, NEG_BIG)
        tiles.append(b)
    tiles.append(jnp.full_like(tiles[0], NEG_BIG))
    tab = jnp.stack(tiles, axis=1)
    return jnp.concatenate([tab, tab], axis=2)


def _diff_attn_kernel(q_ref, k_ref, v_ref, bias_ref, lam_ref, g_ref, o_ref, *, lambda_init):
    t = ATTN_TILE
    chains = DIFF_Q_TILES
    i = pl.program_id(2)
    lane = lax.broadcasted_iota(jnp.int32, (t, LANES), 1)
    scale = jnp.asarray(DH_A ** -0.5, BF16)
    stacked = []
    for c in range(chains):
        q = q_ref[0, c * t:(c + 1) * t, :]
        zero = jnp.zeros_like(q)
        stacked += [jnp.where(lane < DH_A, q, zero), jnp.where(lane >= DH_A, q, zero)]
    qq = jnp.concatenate(stacked, axis=0) * scale

    group = DIFF_KEY_TILES
    span = group * t
    last_tile = i * chains + chains - 1

    def body(n, carry):
        r0 = pl.multiple_of(n * span, span)
        kb = k_ref[0, pl.ds(r0, span), :]
        vb = v_ref[0, pl.ds(r0, span), :]
        s_all = _nt(qq, kb)
        out = []
        for c in range(chains):
            m, l, acc = carry[c]

            def logits(u):
                back = (i * chains + c) - (n * group + u)
                which = jnp.where(back < 0, 3, jnp.minimum(back, 2))
                return s_all[c * 2 * t:(c + 1) * 2 * t, u * t:(u + 1) * t] + bias_ref[0, which]

            top = logits(0)
            for u in range(1, group):
                top = jnp.maximum(top, logits(u))
            m_new = jnp.maximum(m, jnp.max(top, axis=-1, keepdims=True))
            alpha = jnp.exp(m - m_new)
            ps = [jnp.exp(logits(u) - m_new) for u in range(group)]
            tot = ps[0]
            for u in range(1, group):
                tot = tot + ps[u]
            l = alpha * l + jnp.sum(tot, axis=-1, keepdims=True)
            p = jnp.concatenate([pu.astype(BF16) for pu in ps], axis=1)
            acc = alpha * acc + _dot(p, vb)
            out.append((m_new, l, acc))
        return tuple(out)

    init = tuple((jnp.full((2 * t, 1), NEG_BIG, F32), jnp.zeros((2 * t, 1), F32),
                  jnp.zeros((2 * t, DV_A), F32)) for _ in range(chains))
    final = lax.fori_loop(0, last_tile // group + 1, body, init)
    lp = lam_ref[...].astype(F32)
    lam = (jnp.exp(jnp.sum(lp[0:1] * lp[1:2], axis=-1, keepdims=True))
           - jnp.exp(jnp.sum(lp[2:3] * lp[3:4], axis=-1, keepdims=True)) + lambda_init)
    for c in range(chains):
        _, l, acc = final[c]
        o = acc / l
        w = o[:t] - lam * o[t:]
        o_ref[0, c * t:(c + 1) * t, :] = (_rms(w, g_ref[...]) * (1.0 - lambda_init)
                                          ).astype(o_ref.dtype)


def _diff_attention(proj, bias_tiles, lam_p, subln_g, layer_idx):
    bsz, s, _ = proj.shape
    t = ATTN_TILE
    tq = DIFF_Q_TILES * t
    lambda_init = 0.8 - 0.6 * math.exp(-0.3 * layer_idx)
    return pl.pallas_call(
        functools.partial(_diff_attn_kernel, lambda_init=lambda_init),
        out_shape=jax.ShapeDtypeStruct((bsz, s, H_A * DV_A), BF16),
        grid=(bsz, H_A, s // tq),
        in_specs=[
            pl.BlockSpec((1, tq, LANES), lambda b, h, i: (b, i, h)),
            pl.BlockSpec((1, s, LANES), lambda b, h, i: (b, 0, H_A + h)),
            pl.BlockSpec((1, s, LANES), lambda b, h, i: (b, 0, 2 * H_A + h)),
            pl.BlockSpec((1, 4, 2 * t, t), lambda b, h, i: (h, 0, 0, 0)),
            pl.BlockSpec((4, DH_A), lambda b, h, i: (0, 0)),
            pl.BlockSpec((1, DV_A), lambda b, h, i: (0, 0)),
        ],
        out_specs=pl.BlockSpec((1, tq, DV_A), lambda b, h, i: (b, i, h)),
        compiler_params=pltpu.CompilerParams(
            dimension_semantics=("arbitrary", "arbitrary", "arbitrary"),
            vmem_limit_bytes=_vmem_limit(2 * s * LANES * 2 + 4 * 2 * t * t * 4 + 4 * tq * LANES,
                                         4 * tq * DIFF_KEY_TILES * t * 4)),
        name="diff_attention",
    )(proj, proj, proj, bias_tiles, lam_p, subln_g.reshape(1, DV_A))


def _gla_kernel(q_ref, k_ref, v_ref, r_ref, bg_ref, wgk_ref, bgk_ref, g_ref, o_ref, st_ref):
    c = CHUNK

    @pl.when(pl.program_id(1) == 0)
    def _():
        st_ref[...] = jnp.zeros_like(st_ref)

    row = lax.broadcasted_iota(jnp.int32, (c, c), 0)
    col = lax.broadcasted_iota(jnp.int32, (c, c), 1)
    causal = col <= row
    tri = jnp.where(causal, 1.0, 0.0).astype(BF16)
    lane = lax.broadcasted_iota(jnp.int32, (c, LANES), 1)
    wgk = wgk_ref[...]
    bgk = bgk_ref[...]
    g = g_ref[...]

    def chunk(ci, carry):
        r0 = pl.multiple_of(ci * c, c)
        bg = bg_ref[0, pl.ds(r0, c), :]
        for hp in range(H_B // 2):
            cols = slice(hp * LANES, (hp + 1) * LANES)
            la = -_softplus(-(_dot3(bg, wgk[:, cols]) + bgk[:, cols])) / GATE_TAU
            la_hi, la_lo = _split(la)
            b = _dot(tri, la_hi) + _dot(tri, la_lo)
            b_last = b[c - 1:c, :]
            qf = q_ref[0, pl.ds(r0, c), cols].astype(F32) * (DK_B ** -0.5)
            kf = k_ref[0, pl.ds(r0, c), cols].astype(F32)
            q_dec = qf * jnp.exp(b)
            k_inv = (kf * jnp.exp(-b)).astype(BF16)
            k_dec = kf * jnp.exp(b_last - b)
            decay = jnp.exp(b_last)
            for hh in range(2):
                h = 2 * hp + hh
                own = (lane >= hh * DK_B) & (lane < (hh + 1) * DK_B)
                qd = jnp.where(own, q_dec, 0.0).astype(BF16)
                kd = jnp.where(own, k_dec, 0.0).astype(BF16)
                vcols = slice(h * DV_B, (h + 1) * DV_B)
                vb = v_ref[0, pl.ds(r0, c), vcols]
                sc = jnp.where(causal, _nt(qd, k_inv), 0.0)
                st = st_ref[h]
                o = _dot(sc.astype(BF16), vb) + _nt(qd, st.astype(BF16))
                st_ref[h] = st * decay + _tn(vb, kd)
                rr = r_ref[0, pl.ds(r0, c), vcols].astype(F32)
                o_ref[0, pl.ds(r0, c), vcols] = (_rms(o, g) * _silu(rr)).astype(o_ref.dtype)
        return carry

    lax.fori_loop(0, q_ref.shape[1] // c, chunk, 0)


def _gla(proj, gate, w_gk2, b_gk, gla_g):
    bsz, s, _ = proj.shape
    ts = GLA_STEP
    nk = H_B * DK_B
    nv = H_B * DV_B
    w_pad = jnp.zeros((LANES, nk), F32).at[:GATE_RANK].set(w_gk2.astype(F32))
    return pl.pallas_call(
        _gla_kernel,
        out_shape=jax.ShapeDtypeStruct((bsz, s, nv), BF16),
        grid=(bsz, s // ts),
        in_specs=[
            pl.BlockSpec((1, ts, nk), lambda b, i: (b, i, 1536 // nk)),
            pl.BlockSpec((1, ts, nk), lambda b, i: (b, i, 1792 // nk)),
            pl.BlockSpec((1, ts, nv), lambda b, i: (b, i, 2048 // nv)),
            pl.BlockSpec((1, ts, nv), lambda b, i: (b, i, 2560 // nv)),
            pl.BlockSpec((1, ts, LANES), lambda b, i: (b, i, 0)),
            pl.BlockSpec((LANES, nk), lambda b, i: (0, 0)),
            pl.BlockSpec((1, nk), lambda b, i: (0, 0)),
            pl.BlockSpec((1, DV_B), lambda b, i: (0, 0)),
        ],
        out_specs=pl.BlockSpec((1, ts, nv), lambda b, i: (b, i, 0)),
        scratch_shapes=[pltpu.VMEM((H_B, DV_B, LANES), F32)],
        compiler_params=pltpu.CompilerParams(
            dimension_semantics=("arbitrary", "arbitrary"),
            vmem_limit_bytes=_vmem_limit(ts * (2 * nk + 3 * nv) * 2 + ts * LANES * 4,
                                         H_B * DV_B * LANES * 4)),
        name="gated_linear_attention",
    )(proj, proj, proj, proj, gate, w_pad, b_gk.reshape(1, nk), gla_g.reshape(1, DV_B))


def _stick_kernel(q_ref, k_ref, v_ref, o_ref):
    tq, tk = STICK_Q_TILE, ATTN_TILE
    per_q = tq // tk
    i = pl.program_id(2)
    q = q_ref[0]
    lane_q = lax.broadcasted_iota(jnp.int32, (tq, LANES), 1)
    scale = jnp.asarray(DH_C ** -0.5, BF16)
    zero_q = jnp.zeros_like(q)
    qq = jnp.concatenate([jnp.where(lane_q < DH_C, q, zero_q),
                          jnp.where(lane_q >= DH_C, q, zero_q)], axis=0) * scale
    row = lax.broadcasted_iota(jnp.int32, (tk, tk), 0)
    col = lax.broadcasted_iota(jnp.int32, (tk, tk), 1)
    later = jnp.where(row > col, 1.0, 0.0).astype(BF16)
    lane_k = lax.broadcasted_iota(jnp.int32, (tk, LANES), 1)
    q_in_tile = lax.broadcasted_iota(jnp.int32, (2 * tq, tk), 0) % tq
    k_in_tile = lax.broadcasted_iota(jnp.int32, (2 * tq, tk), 1)

    def logits(first):
        zs = []
        for u in range(per_q):
            r0 = pl.multiple_of(jnp.maximum(first - u, 0) * tk, tk)
            zs.append(_nt(qq, k_ref[0, pl.ds(r0, tk), :]))
        return zs

    def tiles(first, zs, run, acc, key_offsets):
        vbs = []
        for u in range(len(key_offsets)):
            r0 = pl.multiple_of((first - u) * tk, tk)
            vbs.append(v_ref[0, pl.ds(r0, tk), :])
        mids = []
        for z, off in zip(zs, key_offsets):
            sp = _softplus(z)
            log_1m = -sp
            keep = None
            if off is not None:
                keep = (k_in_tile + off) < q_in_tile
                log_1m = jnp.where(keep, log_1m, 0.0)
            hi, lo = _split(log_1m)
            within = _dot(hi, later) + _dot(lo, later)
            mids.append((z - sp, log_1m, within, keep))
        for (log_b, log_1m, within, keep), vb in zip(mids, vbs):
            a = jnp.exp(log_b + (within + run))
            if keep is not None:
                a = jnp.where(keep, a, 0.0)
            ab = a.astype(BF16)
            zero_v = jnp.zeros_like(vb)
            a2 = jnp.concatenate([ab[:tq], ab[tq:]], axis=1)
            v2 = jnp.concatenate([jnp.where(lane_k < DH_C, vb, zero_v),
                                  jnp.where(lane_k >= DH_C, vb, zero_v)], axis=0)
            acc = acc + _dot(a2, v2)
            run = run + jnp.sum(log_1m, axis=-1, keepdims=True)
        return run, acc

    n_left = i * per_q

    def go_on(done, run):
        return ((done < n_left) & (jnp.max(run) > EXP_ZERO_BELOW)).astype(jnp.int32)

    run = jnp.zeros((2 * tq, 1), F32)
    acc = jnp.zeros((tq, LANES), F32)
    zs = logits(n_left + per_q - 1)
    zs_next = logits(n_left - 1)
    run, acc = tiles(n_left + per_q - 1, zs, run, acc,
                     [(per_q - 1 - d) * tk for d in range(per_q)])

    def body(carry):
        done, _, run, acc, zs = carry
        first = n_left - 1 - done
        zs_next = logits(first - per_q)
        run, acc = tiles(first, zs, run, acc, [None] * per_q)
        done = done + per_q
        return done, go_on(done, run), run, acc, zs_next

    carry = (jnp.int32(0), go_on(0, run), run, acc, zs_next)
    acc = lax.while_loop(lambda carry: carry[1] > 0, body, carry)[3]
    o_ref[0] = acc.astype(o_ref.dtype)


def _stick_breaking(proj):
    bsz, s, _ = proj.shape
    t = STICK_Q_TILE
    n_tiles = H_C * DH_C // LANES
    return pl.pallas_call(
        _stick_kernel,
        out_shape=jax.ShapeDtypeStruct((bsz, s, H_C * DH_C), BF16),
        grid=(bsz, n_tiles, s // t),
        in_specs=[
            pl.BlockSpec((1, t, LANES), lambda b, h, i: (b, i, h)),
            pl.BlockSpec((1, s, LANES), lambda b, h, i: (b, 0, n_tiles + h)),
            pl.BlockSpec((1, s, LANES), lambda b, h, i: (b, 0, 2 * n_tiles + h)),
        ],
        out_specs=pl.BlockSpec((1, t, LANES), lambda b, h, i: (b, i, h)),
        compiler_params=pltpu.CompilerParams(
            dimension_semantics=("arbitrary", "arbitrary", "arbitrary"),
            vmem_limit_bytes=_vmem_limit(2 * s * LANES * 2 + 4 * t * LANES)),
        name="stick_breaking_attention",
    )(proj, proj, proj)


def _route(logits):
    lane_i = lax.broadcasted_iota(jnp.int32, logits.shape, 1)
    lane = lane_i.astype(F32)
    group_of_lane = (lane_i // EXPERTS_PER_GROUP).astype(F32)
    big = jnp.float32(1 << 20)
    neg = jnp.float32(-jnp.inf)
    is_group = (lane_i >= N_EXPERTS) & (lane_i < N_EXPERTS + N_GROUPS)
    gl = jnp.where(is_group, logits, neg)
    gmax = jnp.max(gl, axis=-1, keepdims=True)
    g_val = 1.0 / jnp.sum(jnp.exp(gl - gmax), axis=-1, keepdims=True)
    g_sel = jnp.min(jnp.where(gl == gmax, lane - N_EXPERTS, big), axis=-1, keepdims=True)
    in_group = (lane_i < N_EXPERTS) & (group_of_lane == g_sel)
    el = jnp.where(in_group, logits, neg)
    emax = jnp.max(el, axis=-1, keepdims=True)
    esum = jnp.sum(jnp.exp(el - emax), axis=-1, keepdims=True)
    i1 = jnp.min(jnp.where(el == emax, lane, big), axis=-1, keepdims=True)
    el2 = jnp.where(lane == i1, neg, el)
    emax2 = jnp.max(el2, axis=-1, keepdims=True)
    i2 = jnp.min(jnp.where(el2 == emax2, lane, big), axis=-1, keepdims=True)
    v1 = 1.0 / esum
    v2 = jnp.exp(emax2 - emax) / esum
    tot = v1 + v2
    w1 = g_val * (v1 / tot)
    w2 = g_val * (v2 / tot)
    out = jnp.where(lane_i == 0, i1, 0.0)
    out = jnp.where(lane_i == 1, i2, out)
    out = jnp.where(lane_i == 2, w1, out)
    out = jnp.where(lane_i == 3, w2, out)
    return out, (lane == i1), (lane == i2)


def _out_kernel(*refs, n_o):
    o_refs = refs[:n_o]
    w_refs = refs[n_o:2 * n_o]
    x_ref, mod_ref, g_ref, wr_ref, br_ref, xo_ref, h_ref, r_ref, cnt_ref, run_ref = refs[2 * n_o:]

    @pl.when((pl.program_id(0) == 0) & (pl.program_id(1) == 0))
    def _():
        run_ref[...] = jnp.zeros_like(run_ref)

    m = _dot(o_refs[0][0], w_refs[0][...])
    for a in range(1, n_o):
        m = m + _dot(o_refs[a][0], w_refs[a][...])
    x = x_ref[0] + mod_ref[0, 2:3, :] * m
    xo_ref[0] = x
    h = _rms(x, g_ref[...]) * (1.0 + mod_ref[0, 4:5, :]) + mod_ref[0, 3:4, :]
    h_ref[0] = h.astype(BF16)
    route, hit1, hit2 = _route(_dot3(h, wr_ref[...]) + br_ref[...])
    tm = route.shape[0]
    picks = jnp.where(hit1 | hit2, 1.0, 0.0)
    row = lax.broadcasted_iota(jnp.int32, (tm, tm), 0)
    col = lax.broadcasted_iota(jnp.int32, (tm, tm), 1)
    before = jnp.where(col < row, 1.0, 0.0).astype(BF16)
    earlier = _dot(before, picks.astype(BF16)) + run_ref[...]
    rank1 = jnp.sum(jnp.where(hit1, earlier, 0.0), axis=-1, keepdims=True)
    rank2 = jnp.sum(jnp.where(hit2, earlier, 0.0), axis=-1, keepdims=True)
    lane_i = lax.broadcasted_iota(jnp.int32, route.shape, 1)
    route = jnp.where(lane_i == 4, rank1, route)
    route = jnp.where(lane_i == 5, rank2, route)
    r_ref[0] = route
    run_ref[...] = run_ref[...] + jnp.sum(picks, axis=0, keepdims=True)
    cnt_ref[...] = run_ref[...]


def _out_projection(o_list, w_list, x, mod, g, w_router, b_router):
    bsz, s, d = x.shape
    tm = TOKEN_TILE
    n_o = len(o_list)
    tok = lambda b, i: (b, i, 0)
    const2 = lambda b, i: (0, 0)
    in_specs = ([pl.BlockSpec((1, tm, o.shape[2]), tok) for o in o_list]
                + [pl.BlockSpec(w.shape, const2) for w in w_list]
                + [pl.BlockSpec((1, tm, d), tok), pl.BlockSpec((1, 6, d), lambda b, i: (b, 0, 0)),
                   pl.BlockSpec((1, d), const2), pl.BlockSpec((d, LANES), const2),
                   pl.BlockSpec((1, LANES), const2)])
    block_bytes = (sum(tm * o.shape[2] * 2 for o in o_list) + sum(w.size * 2 for w in w_list)
                   + 2 * tm * d * 4 + tm * d * 2 + d * LANES * 4 + tm * LANES * 4)
    return pl.pallas_call(
        functools.partial(_out_kernel, n_o=n_o),
        out_shape=[jax.ShapeDtypeStruct((bsz, s, d), F32),
                   jax.ShapeDtypeStruct((bsz, s, d), BF16),
                   jax.ShapeDtypeStruct((bsz, s, LANES), F32),
                   jax.ShapeDtypeStruct((1, LANES), F32)],
        grid=(bsz, s // tm),
        in_specs=in_specs,
        out_specs=[pl.BlockSpec((1, tm, d), tok), pl.BlockSpec((1, tm, d), tok),
                   pl.BlockSpec((1, tm, LANES), tok), pl.BlockSpec((1, LANES), const2)],
        scratch_shapes=[pltpu.VMEM((1, LANES), F32)],
        compiler_params=pltpu.CompilerParams(
            dimension_semantics=("arbitrary", "arbitrary"),
            vmem_limit_bytes=_vmem_limit(block_bytes, tm * d * 12 + tm * tm * 4)),
        name="out_projection_norm_router",
    )(*o_list, *w_list, x, mod, g.reshape(1, d), w_router, b_router)


def _expert_kernel(te_ref, nu_ref, x_ref, wg_ref, wu_ref, wd_ref, o_ref, wgb_ref, wub_ref, wdb_ref):
    t = pl.program_id(0)
    prev = te_ref[jnp.maximum(t - 1, 0)]

    @pl.when((t == 0) | (te_ref[t] != prev))
    def _():
        wgb_ref[...] = wg_ref[0, 0].astype(BF16)
        wub_ref[...] = wu_ref[0, 0].astype(BF16)
        wdb_ref[...] = wd_ref[0, 0].astype(BF16)

    @pl.when(t < nu_ref[0])
    def _():
        x = x_ref[...]
        gt = _dot(x, wgb_ref[...])
        up = _dot(x, wub_ref[...])
        o_ref[...] = _dot((_silu(gt) * up).astype(BF16), wdb_ref[...]).astype(o_ref.dtype)

    @pl.when(t >= nu_ref[0])
    def _():
        o_ref[...] = jnp.zeros_like(o_ref)


def _grouped_experts(xs, tile_expert, n_used, w_gate, w_up, w_down, layer):
    p, d = xs.shape
    tm = EXPERT_TILE
    f = w_gate.shape[3]
    grid_spec = pltpu.PrefetchScalarGridSpec(
        num_scalar_prefetch=2,
        grid=(p // tm,),
        in_specs=[
            pl.BlockSpec((tm, d), lambda t, te, nu: (t, 0)),
            pl.BlockSpec((1, 1, d, f), lambda t, te, nu: (layer, te[t], 0, 0)),
            pl.BlockSpec((1, 1, d, f), lambda t, te, nu: (layer, te[t], 0, 0)),
            pl.BlockSpec((1, 1, f, d), lambda t, te, nu: (layer, te[t], 0, 0)),
        ],
        out_specs=pl.BlockSpec((tm, d), lambda t, te, nu: (t, 0)),
        scratch_shapes=[pltpu.VMEM((d, f), BF16), pltpu.VMEM((d, f), BF16),
                        pltpu.VMEM((f, d), BF16)],
    )
    block_bytes = tm * d * 2 + 3 * d * f * 4 + tm * d * 2
    return pl.pallas_call(
        _expert_kernel,
        out_shape=jax.ShapeDtypeStruct((p, d), BF16),
        grid_spec=grid_spec,
        compiler_params=pltpu.CompilerParams(
            dimension_semantics=("arbitrary",),
            vmem_limit_bytes=_vmem_limit(block_bytes, 3 * d * f * 2 + tm * f * 16)),
        name="grouped_swiglu_experts",
    )(tile_expert, n_used, xs, w_gate, w_up, w_down)


def _pos_kernel(route_ref, starts_ref, pos_ref):
    rt = jnp.transpose(route_ref[...])
    starts = starts_ref[...]
    ids = lax.broadcasted_iota(jnp.int32, (N_EXPERTS, rt.shape[1]), 0).astype(F32)
    for c in range(2):
        start_of = jnp.sum(jnp.where(ids == rt[c:c + 1, :], starts, 0.0), axis=0, keepdims=True)
        pos_ref[c:c + 1, :] = (start_of + rt[4 + c:5 + c, :]).astype(jnp.int32)


def _row_positions(route, starts):
    n_tok = route.shape[0]
    tm = TOKEN_TILE
    return pl.pallas_call(
        _pos_kernel,
        out_shape=jax.ShapeDtypeStruct((2, n_tok), jnp.int32),
        grid=(n_tok // tm,),
        in_specs=[pl.BlockSpec((tm, LANES), lambda i: (i, 0)),
                  pl.BlockSpec((N_EXPERTS, 1), lambda i: (0, 0))],
        out_specs=pl.BlockSpec((2, tm), lambda i: (0, i)),
        compiler_params=pltpu.CompilerParams(
            dimension_semantics=("arbitrary",),
            vmem_limit_bytes=_vmem_limit(tm * LANES * 4 + N_EXPERTS * LANES * 4 + 8 * tm * 4)),
        name="expert_row_positions",
    )(route, starts.astype(F32).reshape(N_EXPERTS, 1))


def _dispatch(route, counts, n_tok):
    tm = EXPERT_TILE
    counts = counts[:N_EXPERTS].astype(jnp.int32)
    padded = ((counts + tm - 1) // tm) * tm
    ends = jnp.cumsum(padded)
    starts = ends - padded
    pos = _row_positions(route, starts).reshape(2 * n_tok)
    p = 2 * n_tok + N_EXPERTS * tm
    tok = jnp.concatenate([jnp.arange(n_tok, dtype=jnp.int32)] * 2)
    row_tok = (jnp.arange(p, dtype=jnp.int32) % n_tok).at[pos].set(
        tok, indices_are_sorted=False, unique_indices=True, mode="promise_in_bounds")
    tile_start = jnp.arange(p // tm, dtype=jnp.int32) * tm
    tile_expert = jnp.minimum(jnp.sum(ends[None, :] <= tile_start[:, None], axis=1),
                              N_EXPERTS - 1).astype(jnp.int32)
    n_used = (ends[-1] // tm).astype(jnp.int32).reshape(1)
    return row_tok, tile_expert, n_used, pos


def _final_kernel(x_ref, y_ref, route_ref, mod_ref, g_ref, o_ref):
    o_ref[0] = _rms(_expert_residual(x_ref[0], y_ref, route_ref, mod_ref), g_ref[...])


def _final_norm(x, y, route, mod, g):
    bsz, s, d = x.shape
    tm = TOKEN_TILE
    tok = lambda b, i: (b, i, 0)
    return pl.pallas_call(
        _final_kernel,
        out_shape=jax.ShapeDtypeStruct((bsz, s, d), F32),
        grid=(bsz, s // tm),
        in_specs=[pl.BlockSpec((1, tm, d), tok),
                  pl.BlockSpec((2, 1, tm, d), lambda b, i: (0, b, i, 0)),
                  pl.BlockSpec((1, tm, LANES), tok),
                  pl.BlockSpec((1, 6, d), lambda b, i: (b, 0, 0)),
                  pl.BlockSpec((1, d), lambda b, i: (0, 0))],
        out_specs=pl.BlockSpec((1, tm, d), tok),
        compiler_params=pltpu.CompilerParams(
            dimension_semantics=("arbitrary", "arbitrary"),
            vmem_limit_bytes=_vmem_limit(4 * tm * d * 4 + tm * LANES * 4)),
        name="residual_final_norm",
    )(x, y, route, mod, g.reshape(1, d))


def kernel(x, c, w_ada, b_ada, norm_mix, norm_ffn, norm_final, rel_bias, even_w_in, even_lambda, even_subln, even_w_gk2, even_b_gk, even_gla_norm, even_w_out, odd_w_in, odd_w_out, router_group_w, router_group_b, router_expert_w, router_expert_b, expert_w_gate, expert_w_up, expert_w_down):
    bsz, s, d = x.shape
    n_tok = bsz * s
    assert s % TOKEN_TILE == 0 and s % GLA_STEP == 0 and s % STICK_Q_TILE == 0
    assert STICK_Q_TILE % ATTN_TILE == 0 and s % (DIFF_KEY_TILES * ATTN_TILE) == 0
    mod_all = _modulation(c, w_ada, b_ada).reshape(DEPTH, bsz, 6, d)
    bias_tiles = _diff_bias_tiles(rel_bias)
    n_even_main = 3072
    y = route = prev_mod = None
    for l in range(DEPTH):
        mod = mod_all[l]
        i = l // 2
        if l % 2 == 0:
            w_in = even_w_in[i]
            w_main = w_in[:, :n_even_main].astype(BF16)
            w_gate = jnp.zeros((d, LANES), BF16).at[:, :GATE_RANK].set(
                w_in[:, n_even_main:].astype(BF16))
            x, proj, gate = _in_projection(x, y, route, prev_mod, norm_mix[l], mod, w_main, w_gate)
            oa = _diff_attention(proj, bias_tiles, even_lambda[i], even_subln[i], l)
            ob = _gla(proj, gate, even_w_gk2[i], even_b_gk[i], even_gla_norm[i])
            w_out = even_w_out[i].astype(BF16)
            o_list = [oa, ob]
            w_list = [w_out[:H_A * DV_A], w_out[H_A * DV_A:]]
        else:
            x, proj, _ = _in_projection(x, y, route, prev_mod, norm_mix[l], mod,
                                        odd_w_in[i].astype(BF16), None)
            o_list = [_stick_breaking(proj)]
            w_list = [odd_w_out[i].astype(BF16)]
        w_router = jnp.zeros((d, LANES), F32)
        w_router = w_router.at[:, :N_EXPERTS].set(router_expert_w[l].astype(F32))
        w_router = w_router.at[:, N_EXPERTS:N_EXPERTS + N_GROUPS].set(router_group_w[l].astype(F32))
        b_router = jnp.zeros((1, LANES), F32)
        b_router = b_router.at[0, :N_EXPERTS].set(router_expert_b[l].astype(F32))
        b_router = b_router.at[0, N_EXPERTS:N_EXPERTS + N_GROUPS].set(router_group_b[l].astype(F32))
        x, h2, route, counts = _out_projection(o_list, w_list, x, mod, norm_ffn[l], w_router,
                                               b_router)
        row_tok, tile_expert, n_used, pos = _dispatch(route.reshape(n_tok, LANES), counts[0], n_tok)
        xs = h2.reshape(n_tok, d).at[row_tok].get(mode="promise_in_bounds")
        out = _grouped_experts(xs, tile_expert, n_used, expert_w_gate, expert_w_up,
                               expert_w_down, l)
        y = out.at[pos].get(mode="promise_in_bounds").reshape(2, bsz, s, d)
        prev_mod = mod
    return _final_norm(x, y, route, prev_mod, norm_final)
```

```python
import functools
import math

import numpy as np
import jax
import jax.numpy as jnp
from jax import lax
from jax.experimental import pallas as pl
from jax.experimental.pallas import tpu as pltpu

F32 = jnp.float32
BF16 = jnp.bfloat16

D_MODEL = 1024
DEPTH = 4
CHUNK = 64
EPS = 1e-6
H_A, DH_A, DV_A = 4, 64, 128
H_B, DK_B, DV_B = 4, 64, 128
GATE_RANK = 16
GATE_TAU = 16.0
H_C, DH_C = 16, 64
N_BUCKETS = 32
MAX_DISTANCE = 128
N_GROUPS = 4
EXPERTS_PER_GROUP = 8
N_EXPERTS = N_GROUPS * EXPERTS_PER_GROUP
D_EXPERT = 512
MIX = 1024

LANES = 128
V7X_VMEM_BYTES = 64 * 1024 * 1024
NEG_BIG = -1e30
EXP_ZERO_BELOW = -104.0

TOKEN_TILE = 512
ATTN_TILE = 128
STICK_Q_TILE = 256
DIFF_KEY_TILES = 4
DIFF_Q_TILES = 2
GLA_STEP = 256
EXPERT_TILE = 512


def _vmem_limit(block_bytes, scratch_bytes=0):
    est = 2 * block_bytes + scratch_bytes + 16 * 1024 * 1024
    return int(min(est, V7X_VMEM_BYTES - 8 * 1024 * 1024))


def _nt(a, b):
    return lax.dot_general(a, b, (((1,), (1,)), ((), ())), preferred_element_type=F32)


def _tn(a, b):
    return lax.dot_general(a, b, (((0,), (0,)), ((), ())), preferred_element_type=F32)


def _dot(a, b):
    return jnp.dot(a, b, preferred_element_type=F32)


def _split(x):
    hi = x.astype(BF16)
    lo = (x - hi.astype(F32)).astype(BF16)
    return hi, lo


def _dot3(a, b):
    ah, al = _split(a)
    bh, bl = _split(b)
    return _dot(ah, bh) + _dot(al, bh) + _dot(ah, bl)


def _rms(x, g):
    return x * lax.rsqrt(jnp.mean(x * x, axis=-1, keepdims=True) + EPS) * g


def _softplus(z):
    return jnp.maximum(z, 0.0) + jnp.log(1.0 + jnp.exp(-jnp.abs(z)))


def _silu(x):
    return x / (1.0 + jnp.exp(-x))


def _mod_kernel(c_ref, w_ref, b_ref, o_ref):
    c = c_ref[...]
    o_ref[0] = _dot3(_silu(c), w_ref[0]) + b_ref[0]


def _modulation(c, w_ada, b_ada):
    depth, d, n = w_ada.shape
    bsz = c.shape[0]
    tn = 1536
    return pl.pallas_call(
        _mod_kernel,
        out_shape=jax.ShapeDtypeStruct((depth, bsz, n), F32),
        grid=(depth, n // tn),
        in_specs=[
            pl.BlockSpec((bsz, d), lambda l, j: (0, 0)),
            pl.BlockSpec((1, d, tn), lambda l, j: (l, 0, j)),
            pl.BlockSpec((1, 1, tn), lambda l, j: (l, 0, j)),
        ],
        out_specs=pl.BlockSpec((1, bsz, tn), lambda l, j: (l, 0, j)),
        compiler_params=pltpu.CompilerParams(
            dimension_semantics=("arbitrary", "arbitrary"),
            vmem_limit_bytes=_vmem_limit(d * tn * 4 + bsz * (d + tn) * 4)),
        name="adaln_modulation",
    )(c, w_ada, b_ada.reshape(depth, 1, n))


def _expert_residual(x, y_ref, route_ref, mod_ref):
    route = route_ref[0]
    y = route[:, 2:3] * y_ref[0, 0].astype(F32) + route[:, 3:4] * y_ref[1, 0].astype(F32)
    return x + mod_ref[0, 5:6, :] * y


def _proj_kernel(*refs, has_res, n_main, has_gate):
    it = iter(refs)
    x_ref = next(it)
    if has_res:
        y_ref = next(it)
        route_ref = next(it)
        pmod_ref = next(it)
    g_ref = next(it)
    mod_ref = next(it)
    w_ref = next(it)
    wg_ref = next(it) if has_gate else None
    xo_ref = next(it) if has_res else None
    o_ref = next(it)
    og_ref = next(it) if has_gate else None

    x = x_ref[0]
    if has_res:
        x = _expert_residual(x, y_ref, route_ref, pmod_ref)
        xo_ref[0] = x
    h = _rms(x, g_ref[...]) * (1.0 + mod_ref[0, 1:2, :]) + mod_ref[0, 0:1, :]
    hb = h.astype(BF16)
    step = 512
    for n0 in range(0, n_main, step):
        o_ref[0, :, n0:n0 + step] = _dot(hb, w_ref[:, n0:n0 + step]).astype(BF16)
    if has_gate:
        og_ref[0] = _dot(hb, wg_ref[...])


def _in_projection(x, y, route, prev_mod, g, mod, w, w_gate):
    bsz, s, d = x.shape
    tm = TOKEN_TILE
    has_res = y is not None
    has_gate = w_gate is not None
    n_main = w.shape[1]
    tok = lambda b, i: (b, i, 0)
    per_b = lambda b, i: (b, 0, 0)
    const2 = lambda b, i: (0, 0)
    in_specs = [pl.BlockSpec((1, tm, d), tok)]
    args = [x]
    if has_res:
        in_specs += [pl.BlockSpec((2, 1, tm, d), lambda b, i: (0, b, i, 0)),
                     pl.BlockSpec((1, tm, LANES), tok), pl.BlockSpec((1, 6, d), per_b)]
        args += [y, route, prev_mod]
    in_specs += [pl.BlockSpec((1, d), const2), pl.BlockSpec((1, 6, d), per_b),
                 pl.BlockSpec((d, n_main), const2)]
    args += [g.reshape(1, d), mod, w]
    if has_gate:
        in_specs.append(pl.BlockSpec((d, LANES), const2))
        args.append(w_gate)
    out_shape, out_specs = [], []
    if has_res:
        out_shape.append(jax.ShapeDtypeStruct((bsz, s, d), F32))
        out_specs.append(pl.BlockSpec((1, tm, d), tok))
    out_shape.append(jax.ShapeDtypeStruct((bsz, s, n_main), BF16))
    out_specs.append(pl.BlockSpec((1, tm, n_main), tok))
    if has_gate:
        out_shape.append(jax.ShapeDtypeStruct((bsz, s, LANES), F32))
        out_specs.append(pl.BlockSpec((1, tm, LANES), tok))
    block_bytes = (tm * d * 4 * (4 if has_res else 1) + d * n_main * 2 + tm * n_main * 2
                   + d * LANES * 2 + 2 * tm * LANES * 4)
    outs = pl.pallas_call(
        functools.partial(_proj_kernel, has_res=has_res, n_main=n_main, has_gate=has_gate),
        out_shape=out_shape,
        grid=(bsz, s // tm),
        in_specs=in_specs,
        out_specs=out_specs,
        compiler_params=pltpu.CompilerParams(
            dimension_semantics=("arbitrary", "arbitrary"),
            vmem_limit_bytes=_vmem_limit(block_bytes, tm * d * 8)),
        name="norm_mod_in_projection",
    )(*args)
    outs = list(outs)
    x_new = outs.pop(0) if has_res else x
    proj = outs.pop(0)
    gate = outs.pop(0) if has_gate else None
    return x_new, proj, gate


def _t5_bucket(rel):
    nb = N_BUCKETS // 2
    max_exact = nb // 2
    ret = jnp.where(rel > 0, nb, 0)
    n = jnp.abs(rel)
    nf = jnp.maximum(n, 1).astype(F32)
    large = max_exact + (jnp.log(nf / max_exact) / math.log(MAX_DISTANCE / max_exact)
                         * (nb - max_exact)).astype(jnp.int32)
    large = jnp.minimum(large, nb - 1)
    return ret + jnp.where(n < max_exact, n, large)


def _diff_bias_tiles(rel_bias):
    t = ATTN_TILE
    assert t >= MAX_DISTANCE, "tiles two or more away must lie beyond the last bucket edge"
    qp = jnp.arange(t)[:, None]
    kp = jnp.arange(t)[None, :]
    tiles = []
    for off in (0, 1, 2):
        rel = (kp - off * t) - qp
        b = jnp.transpose(rel_bias.astype(F32)[_t5_bucket(rel)], (2, 0, 1))
        if off == 0:
            b = jnp.where((kp // CHUNK) <= (qp // CHUNK), b, NEG_BIG)
        tiles.append(b)
    tiles.append(jnp.full_like(tiles[0], NEG_BIG))
    tab = jnp.stack(tiles, axis=1)
    return jnp.concatenate([tab, tab], axis=2)


def _diff_attn_kernel(q_ref, k_ref, v_ref, bias_ref, lam_ref, g_ref, o_ref, *, lambda_init):
    t = ATTN_TILE
    chains = DIFF_Q_TILES
    i = pl.program_id(2)
    lane = lax.broadcasted_iota(jnp.int32, (t, LANES), 1)
    scale = jnp.asarray(DH_A ** -0.5, BF16)
    stacked = []
    for c in range(chains):
        q = q_ref[0, c * t:(c + 1) * t, :]
        zero = jnp.zeros_like(q)
        stacked += [jnp.where(lane < DH_A, q, zero), jnp.where(lane >= DH_A, q, zero)]
    qq = jnp.concatenate(stacked, axis=0) * scale

    group = DIFF_KEY_TILES
    span = group * t
    last_tile = i * chains + chains - 1

    def body(n, carry):
        r0 = pl.multiple_of(n * span, span)
        kb = k_ref[0, pl.ds(r0, span), :]
        vb = v_ref[0, pl.ds(r0, span), :]
        s_all = _nt(qq, kb)
        out = []
        for c in range(chains):
            m, l, acc = carry[c]

            def logits(u):
                back = (i * chains + c) - (n * group + u)
                which = jnp.where(back < 0, 3, jnp.minimum(back, 2))
                return s_all[c * 2 * t:(c + 1) * 2 * t, u * t:(u + 1) * t] + bias_ref[0, which]

            top = logits(0)
            for u in range(1, group):
                top = jnp.maximum(top, logits(u))
            m_new = jnp.maximum(m, jnp.max(top, axis=-1, keepdims=True))
            alpha = jnp.exp(m - m_new)
            ps = [jnp.exp(logits(u) - m_new) for u in range(group)]
            tot = ps[0]
            for u in range(1, group):
                tot = tot + ps[u]
            l = alpha * l + jnp.sum(tot, axis=-1, keepdims=True)
            p = jnp.concatenate([pu.astype(BF16) for pu in ps], axis=1)
            acc = alpha * acc + _dot(p, vb)
            out.append((m_new, l, acc))
        return tuple(out)

    init = tuple((jnp.full((2 * t, 1), NEG_BIG, F32), jnp.zeros((2 * t, 1), F32),
                  jnp.zeros((2 * t, DV_A), F32)) for _ in range(chains))
    final = lax.fori_loop(0, last_tile // group + 1, body, init)
    lp = lam_ref[...].astype(F32)
    lam = (jnp.exp(jnp.sum(lp[0:1] * lp[1:2], axis=-1, keepdims=True))
           - jnp.exp(jnp.sum(lp[2:3] * lp[3:4], axis=-1, keepdims=True)) + lambda_init)
    for c in range(chains):
        _, l, acc = final[c]
        o = acc / l
        w = o[:t] - lam * o[t:]
        o_ref[0, c * t:(c + 1) * t, :] = (_rms(w, g_ref[...]) * (1.0 - lambda_init)
                                          ).astype(o_ref.dtype)


def _diff_attention(proj, bias_tiles, lam_p, subln_g, layer_idx):
    bsz, s, _ = proj.shape
    t = ATTN_TILE
    tq = DIFF_Q_TILES * t
    lambda_init = 0.8 - 0.6 * math.exp(-0.3 * layer_idx)
    return pl.pallas_call(
        functools.partial(_diff_attn_kernel, lambda_init=lambda_init),
        out_shape=jax.ShapeDtypeStruct((bsz, s, H_A * DV_A), BF16),
        grid=(bsz, H_A, s // tq),
        in_specs=[
            pl.BlockSpec((1, tq, LANES), lambda b, h, i: (b, i, h)),
            pl.BlockSpec((1, s, LANES), lambda b, h, i: (b, 0, H_A + h)),
            pl.BlockSpec((1, s, LANES), lambda b, h, i: (b, 0, 2 * H_A + h)),
            pl.BlockSpec((1, 4, 2 * t, t), lambda b, h, i: (h, 0, 0, 0)),
            pl.BlockSpec((4, DH_A), lambda b, h, i: (0, 0)),
            pl.BlockSpec((1, DV_A), lambda b, h, i: (0, 0)),
        ],
        out_specs=pl.BlockSpec((1, tq, DV_A), lambda b, h, i: (b, i, h)),
        compiler_params=pltpu.CompilerParams(
            dimension_semantics=("arbitrary", "arbitrary", "arbitrary"),
            vmem_limit_bytes=_vmem_limit(2 * s * LANES * 2 + 4 * 2 * t * t * 4 + 4 * tq * LANES,
                                         4 * tq * DIFF_KEY_TILES * t * 4)),
        name="diff_attention",
    )(proj, proj, proj, bias_tiles, lam_p, subln_g.reshape(1, DV_A))


def _gla_kernel(q_ref, k_ref, v_ref, r_ref, bg_ref, wgk_ref, bgk_ref, g_ref, o_ref, st_ref):
    c = CHUNK
    ts = q_ref.shape[1]
    n_chunks = ts // c

    @pl.when(pl.program_id(1) == 0)
    def _():
        st_ref[...] = jnp.zeros_like(st_ref)

    row = lax.broadcasted_iota(jnp.int32, (ts, ts), 0)
    col = lax.broadcasted_iota(jnp.int32, (ts, ts), 1)
    same_chunk_before = ((row // c) == (col // c)) & (col <= row)
    tri = jnp.where(same_chunk_before, 1.0, 0.0).astype(BF16)
    causal = (lax.broadcasted_iota(jnp.int32, (c, c), 1)
              <= lax.broadcasted_iota(jnp.int32, (c, c), 0))
    lane = lax.broadcasted_iota(jnp.int32, (ts, LANES), 1)
    wgk = wgk_ref[...]
    bgk = bgk_ref[...]
    g = g_ref[...]
    bg = bg_ref[0]

    las = [-_softplus(-(_dot3(bg, wgk[:, hp * LANES:(hp + 1) * LANES])
                        + bgk[:, hp * LANES:(hp + 1) * LANES])) / GATE_TAU
           for hp in range(H_B // 2)]
    bs = []
    for la in las:
        la_hi, la_lo = _split(la)
        bs.append(_dot(tri, la_hi) + _dot(tri, la_lo))

    heads = []
    for hp in range(H_B // 2):
        cols = slice(hp * LANES, (hp + 1) * LANES)
        b = bs[hp]
        b_last = jnp.concatenate(
            [jnp.broadcast_to(b[(ci + 1) * c - 1:(ci + 1) * c, :], (c, LANES))
             for ci in range(n_chunks)], axis=0)
        qf = q_ref[0, :, cols].astype(F32) * (DK_B ** -0.5)
        kf = k_ref[0, :, cols].astype(F32)
        q_dec = qf * jnp.exp(b)
        k_inv = (kf * jnp.exp(-b)).astype(BF16)
        k_dec = kf * jnp.exp(b_last - b)
        decays = [jnp.exp(b[(ci + 1) * c - 1:(ci + 1) * c, :]) for ci in range(n_chunks)]
        for hh in range(2):
            own = (lane >= hh * DK_B) & (lane < (hh + 1) * DK_B)
            heads.append((jnp.where(own, q_dec, 0.0).astype(BF16), k_inv,
                          jnp.where(own, k_dec, 0.0).astype(BF16), decays))

    def rows(x, ci):
        return x[ci * c:(ci + 1) * c]

    scores, increments = [], []
    for h, (qd, k_inv, kd, _) in enumerate(heads):
        vh = v_ref[0, :, h * DV_B:(h + 1) * DV_B]
        scores.append([jnp.where(causal, _nt(rows(qd, ci), rows(k_inv, ci)), 0.0).astype(BF16)
                       for ci in range(n_chunks)])
        increments.append([_tn(rows(vh, ci), rows(kd, ci)) for ci in range(n_chunks)])

    for h, (qd, _, _, decays) in enumerate(heads):
        vh = v_ref[0, :, h * DV_B:(h + 1) * DV_B]
        st = st_ref[h]
        states = []
        for ci in range(n_chunks):
            states.append(st.astype(BF16))
            st = st * decays[ci] + increments[h][ci]
        st_ref[h] = st
        o = jnp.concatenate(
            [_dot(scores[h][ci], rows(vh, ci)) + _nt(rows(qd, ci), states[ci])
             for ci in range(n_chunks)], axis=0)
        rr = r_ref[0, :, h * DV_B:(h + 1) * DV_B].astype(F32)
        o_ref[0, :, h * DV_B:(h + 1) * DV_B] = (_rms(o, g) * _silu(rr)).astype(o_ref.dtype)


def _gla(proj, gate, w_gk2, b_gk, gla_g):
    bsz, s, _ = proj.shape
    ts = GLA_STEP
    nk = H_B * DK_B
    nv = H_B * DV_B
    w_pad = jnp.zeros((LANES, nk), F32).at[:GATE_RANK].set(w_gk2.astype(F32))
    return pl.pallas_call(
        _gla_kernel,
        out_shape=jax.ShapeDtypeStruct((bsz, s, nv), BF16),
        grid=(bsz, s // ts),
        in_specs=[
            pl.BlockSpec((1, ts, nk), lambda b, i: (b, i, 1536 // nk)),
            pl.BlockSpec((1, ts, nk), lambda b, i: (b, i, 1792 // nk)),
            pl.BlockSpec((1, ts, nv), lambda b, i: (b, i, 2048 // nv)),
            pl.BlockSpec((1, ts, nv), lambda b, i: (b, i, 2560 // nv)),
            pl.BlockSpec((1, ts, LANES), lambda b, i: (b, i, 0)),
            pl.BlockSpec((LANES, nk), lambda b, i: (0, 0)),
            pl.BlockSpec((1, nk), lambda b, i: (0, 0)),
            pl.BlockSpec((1, DV_B), lambda b, i: (0, 0)),
        ],
        out_specs=pl.BlockSpec((1, ts, nv), lambda b, i: (b, i, 0)),
        scratch_shapes=[pltpu.VMEM((H_B, DV_B, LANES), F32)],
        compiler_params=pltpu.CompilerParams(
            dimension_semantics=("arbitrary", "arbitrary"),
            vmem_limit_bytes=_vmem_limit(ts * (2 * nk + 3 * nv) * 2 + ts * LANES * 4,
                                         H_B * DV_B * LANES * 4)),
        name="gated_linear_attention",
    )(proj, proj, proj, proj, gate, w_pad, b_gk.reshape(1, nk), gla_g.reshape(1, DV_B))


def _stick_kernel(q_ref, k_ref, v_ref, o_ref):
    tq, tk = STICK_Q_TILE, ATTN_TILE
    per_q = tq // tk
    i = pl.program_id(2)
    q = q_ref[0]
    lane_q = lax.broadcasted_iota(jnp.int32, (tq, LANES), 1)
    scale = jnp.asarray(DH_C ** -0.5, BF16)
    zero_q = jnp.zeros_like(q)
    qq = jnp.concatenate([jnp.where(lane_q < DH_C, q, zero_q),
                          jnp.where(lane_q >= DH_C, q, zero_q)], axis=0) * scale
    row = lax.broadcasted_iota(jnp.int32, (tk, tk), 0)
    col = lax.broadcasted_iota(jnp.int32, (tk, tk), 1)
    later = jnp.where(row > col, 1.0, 0.0).astype(BF16)
    lane_k = lax.broadcasted_iota(jnp.int32, (tk, LANES), 1)
    q_in_tile = lax.broadcasted_iota(jnp.int32, (2 * tq, tk), 0) % tq
    k_in_tile = lax.broadcasted_iota(jnp.int32, (2 * tq, tk), 1)

    def logits(first):
        zs = []
        for u in range(per_q):
            r0 = pl.multiple_of(jnp.maximum(first - u, 0) * tk, tk)
            zs.append(_nt(qq, k_ref[0, pl.ds(r0, tk), :]))
        return zs

    def tiles(first, zs, run, acc, key_offsets):
        vbs = []
        for u in range(len(key_offsets)):
            r0 = pl.multiple_of((first - u) * tk, tk)
            vbs.append(v_ref[0, pl.ds(r0, tk), :])
        mids = []
        for z, off in zip(zs, key_offsets):
            sp = _softplus(z)
            log_1m = -sp
            keep = None
            if off is not None:
                keep = (k_in_tile + off) < q_in_tile
                log_1m = jnp.where(keep, log_1m, 0.0)
            hi, lo = _split(log_1m)
            within = _dot(hi, later) + _dot(lo, later)
            mids.append((z - sp, log_1m, within, keep))
        for (log_b, log_1m, within, keep), vb in zip(mids, vbs):
            a = jnp.exp(log_b + (within + run))
            if keep is not None:
                a = jnp.where(keep, a, 0.0)
            ab = a.astype(BF16)
            zero_v = jnp.zeros_like(vb)
            a2 = jnp.concatenate([ab[:tq], ab[tq:]], axis=1)
            v2 = jnp.concatenate([jnp.where(lane_k < DH_C, vb, zero_v),
                                  jnp.where(lane_k >= DH_C, vb, zero_v)], axis=0)
            acc = acc + _dot(a2, v2)
            run = run + jnp.sum(log_1m, axis=-1, keepdims=True)
        return run, acc

    n_left = i * per_q

    def go_on(done, run):
        return ((done < n_left) & (jnp.max(run) > EXP_ZERO_BELOW)).astype(jnp.int32)

    run = jnp.zeros((2 * tq, 1), F32)
    acc = jnp.zeros((tq, LANES), F32)
    zs = logits(n_left + per_q - 1)
    zs_next = logits(n_left - 1)
    run, acc = tiles(n_left + per_q - 1, zs, run, acc,
                     [(per_q - 1 - d) * tk for d in range(per_q)])

    def body(carry):
        done, _, run, acc, zs = carry
        first = n_left - 1 - done
        zs_next = logits(first - per_q)
        run, acc = tiles(first, zs, run, acc, [None] * per_q)
        done = done + per_q
        return done, go_on(done, run), run, acc, zs_next

    carry = (jnp.int32(0), go_on(0, run), run, acc, zs_next)
    acc = lax.while_loop(lambda carry: carry[1] > 0, body, carry)[3]
    o_ref[0] = acc.astype(o_ref.dtype)


def _stick_breaking(proj):
    bsz, s, _ = proj.shape
    t = STICK_Q_TILE
    n_tiles = H_C * DH_C // LANES
    return pl.pallas_call(
        _stick_kernel,
        out_shape=jax.ShapeDtypeStruct((bsz, s, H_C * DH_C), BF16),
        grid=(bsz, n_tiles, s // t),
        in_specs=[
            pl.BlockSpec((1, t, LANES), lambda b, h, i: (b, i, h)),
            pl.BlockSpec((1, s, LANES), lambda b, h, i: (b, 0, n_tiles + h)),
            pl.BlockSpec((1, s, LANES), lambda b, h, i: (b, 0, 2 * n_tiles + h)),
        ],
        out_specs=pl.BlockSpec((1, t, LANES), lambda b, h, i: (b, i, h)),
        compiler_params=pltpu.CompilerParams(
            dimension_semantics=("arbitrary", "arbitrary", "arbitrary"),
            vmem_limit_bytes=_vmem_limit(2 * s * LANES * 2 + 4 * t * LANES)),
        name="stick_breaking_attention",
    )(proj, proj, proj)


def _route(logits):
    lane_i = lax.broadcasted_iota(jnp.int32, logits.shape, 1)
    lane = lane_i.astype(F32)
    group_of_lane = (lane_i // EXPERTS_PER_GROUP).astype(F32)
    big = jnp.float32(1 << 20)
    neg = jnp.float32(-jnp.inf)
    is_group = (lane_i >= N_EXPERTS) & (lane_i < N_EXPERTS + N_GROUPS)
    gl = jnp.where(is_group, logits, neg)
    gmax = jnp.max(gl, axis=-1, keepdims=True)
    g_val = 1.0 / jnp.sum(jnp.exp(gl - gmax), axis=-1, keepdims=True)
    g_sel = jnp.min(jnp.where(gl == gmax, lane - N_EXPERTS, big), axis=-1, keepdims=True)
    in_group = (lane_i < N_EXPERTS) & (group_of_lane == g_sel)
    el = jnp.where(in_group, logits, neg)
    emax = jnp.max(el, axis=-1, keepdims=True)
    esum = jnp.sum(jnp.exp(el - emax), axis=-1, keepdims=True)
    i1 = jnp.min(jnp.where(el == emax, lane, big), axis=-1, keepdims=True)
    el2 = jnp.where(lane == i1, neg, el)
    emax2 = jnp.max(el2, axis=-1, keepdims=True)
    i2 = jnp.min(jnp.where(el2 == emax2, lane, big), axis=-1, keepdims=True)
    v1 = 1.0 / esum
    v2 = jnp.exp(emax2 - emax) / esum
    tot = v1 + v2
    w1 = g_val * (v1 / tot)
    w2 = g_val * (v2 / tot)
    out = jnp.where(lane_i == 0, i1, 0.0)
    out = jnp.where(lane_i == 1, i2, out)
    out = jnp.where(lane_i == 2, w1, out)
    out = jnp.where(lane_i == 3, w2, out)
    return out, (lane == i1), (lane == i2)


def _out_kernel(*refs, n_o):
    o_refs = refs[:n_o]
    w_refs = refs[n_o:2 * n_o]
    x_ref, mod_ref, g_ref, wr_ref, br_ref, xo_ref, h_ref, r_ref, cnt_ref, run_ref = refs[2 * n_o:]

    @pl.when((pl.program_id(0) == 0) & (pl.program_id(1) == 0))
    def _():
        run_ref[...] = jnp.zeros_like(run_ref)

    m = _dot(o_refs[0][0], w_refs[0][...])
    for a in range(1, n_o):
        m = m + _dot(o_refs[a][0], w_refs[a][...])
    x = x_ref[0] + mod_ref[0, 2:3, :] * m
    xo_ref[0] = x
    h = _rms(x, g_ref[...]) * (1.0 + mod_ref[0, 4:5, :]) + mod_ref[0, 3:4, :]
    h_ref[0] = h.astype(BF16)
    route, hit1, hit2 = _route(_dot3(h, wr_ref[...]) + br_ref[...])
    tm = route.shape[0]
    picks = jnp.where(hit1 | hit2, 1.0, 0.0)
    row = lax.broadcasted_iota(jnp.int32, (tm, tm), 0)
    col = lax.broadcasted_iota(jnp.int32, (tm, tm), 1)
    before = jnp.where(col < row, 1.0, 0.0).astype(BF16)
    earlier = _dot(before, picks.astype(BF16)) + run_ref[...]
    rank1 = jnp.sum(jnp.where(hit1, earlier, 0.0), axis=-1, keepdims=True)
    rank2 = jnp.sum(jnp.where(hit2, earlier, 0.0), axis=-1, keepdims=True)
    lane_i = lax.broadcasted_iota(jnp.int32, route.shape, 1)
    route = jnp.where(lane_i == 4, rank1, route)
    route = jnp.where(lane_i == 5, rank2, route)
    r_ref[0] = route
    run_ref[...] = run_ref[...] + jnp.sum(picks, axis=0, keepdims=True)
    cnt_ref[...] = run_ref[...]


def _out_projection(o_list, w_list, x, mod, g, w_router, b_router):
    bsz, s, d = x.shape
    tm = TOKEN_TILE
    n_o = len(o_list)
    tok = lambda b, i: (b, i, 0)
    const2 = lambda b, i: (0, 0)
    in_specs = ([pl.BlockSpec((1, tm, o.shape[2]), tok) for o in o_list]
                + [pl.BlockSpec(w.shape, const2) for w in w_list]
                + [pl.BlockSpec((1, tm, d), tok), pl.BlockSpec((1, 6, d), lambda b, i: (b, 0, 0)),
                   pl.BlockSpec((1, d), const2), pl.BlockSpec((d, LANES), const2),
                   pl.BlockSpec((1, LANES), const2)])
    block_bytes = (sum(tm * o.shape[2] * 2 for o in o_list) + sum(w.size * 2 for w in w_list)
                   + 2 * tm * d * 4 + tm * d * 2 + d * LANES * 4 + tm * LANES * 4)
    return pl.pallas_call(
        functools.partial(_out_kernel, n_o=n_o),
        out_shape=[jax.ShapeDtypeStruct((bsz, s, d), F32),
                   jax.ShapeDtypeStruct((bsz, s, d), BF16),
                   jax.ShapeDtypeStruct((bsz, s, LANES), F32),
                   jax.ShapeDtypeStruct((1, LANES), F32)],
        grid=(bsz, s // tm),
        in_specs=in_specs,
        out_specs=[pl.BlockSpec((1, tm, d), tok), pl.BlockSpec((1, tm, d), tok),
                   pl.BlockSpec((1, tm, LANES), tok), pl.BlockSpec((1, LANES), const2)],
        scratch_shapes=[pltpu.VMEM((1, LANES), F32)],
        compiler_params=pltpu.CompilerParams(
            dimension_semantics=("arbitrary", "arbitrary"),
            vmem_limit_bytes=_vmem_limit(block_bytes, tm * d * 12 + tm * tm * 4)),
        name="out_projection_norm_router",
    )(*o_list, *w_list, x, mod, g.reshape(1, d), w_router, b_router)


def _expert_kernel(te_ref, nu_ref, x_ref, wg_ref, wu_ref, wd_ref, o_ref, wgb_ref, wub_ref, wdb_ref):
    t = pl.program_id(0)
    prev = te_ref[jnp.maximum(t - 1, 0)]

    @pl.when((t == 0) | (te_ref[t] != prev))
    def _():
        wgb_ref[...] = wg_ref[0, 0].astype(BF16)
        wub_ref[...] = wu_ref[0, 0].astype(BF16)
        wdb_ref[...] = wd_ref[0, 0].astype(BF16)

    @pl.when(t < nu_ref[0])
    def _():
        x = x_ref[...]
        gt = _dot(x, wgb_ref[...])
        up = _dot(x, wub_ref[...])
        o_ref[...] = _dot((_silu(gt) * up).astype(BF16), wdb_ref[...]).astype(o_ref.dtype)

    @pl.when(t >= nu_ref[0])
    def _():
        o_ref[...] = jnp.zeros_like(o_ref)


def _grouped_experts(xs, tile_expert, n_used, w_gate, w_up, w_down, layer):
    p, d = xs.shape
    tm = EXPERT_TILE
    f = w_gate.shape[3]
    grid_spec = pltpu.PrefetchScalarGridSpec(
        num_scalar_prefetch=2,
        grid=(p // tm,),
        in_specs=[
            pl.BlockSpec((tm, d), lambda t, te, nu: (t, 0)),
            pl.BlockSpec((1, 1, d, f), lambda t, te, nu: (layer, te[t], 0, 0)),
            pl.BlockSpec((1, 1, d, f), lambda t, te, nu: (layer, te[t], 0, 0)),
            pl.BlockSpec((1, 1, f, d), lambda t, te, nu: (layer, te[t], 0, 0)),
        ],
        out_specs=pl.BlockSpec((tm, d), lambda t, te, nu: (t, 0)),
        scratch_shapes=[pltpu.VMEM((d, f), BF16), pltpu.VMEM((d, f), BF16),
                        pltpu.VMEM((f, d), BF16)],
    )
    block_bytes = tm * d * 2 + 3 * d * f * 4 + tm * d * 2
    return pl.pallas_call(
        _expert_kernel,
        out_shape=jax.ShapeDtypeStruct((p, d), BF16),
        grid_spec=grid_spec,
        compiler_params=pltpu.CompilerParams(
            dimension_semantics=("arbitrary",),
            vmem_limit_bytes=_vmem_limit(block_bytes, 3 * d * f * 2 + tm * f * 16)),
        name="grouped_swiglu_experts",
    )(tile_expert, n_used, xs, w_gate, w_up, w_down)


def _pos_kernel(route_ref, starts_ref, pos_ref):
    rt = jnp.transpose(route_ref[...])
    starts = starts_ref[...]
    ids = lax.broadcasted_iota(jnp.int32, (N_EXPERTS, rt.shape[1]), 0).astype(F32)
    for c in range(2):
        start_of = jnp.sum(jnp.where(ids == rt[c:c + 1, :], starts, 0.0), axis=0, keepdims=True)
        pos_ref[c:c + 1, :] = (start_of + rt[4 + c:5 + c, :]).astype(jnp.int32)


def _row_positions(route, starts):
    n_tok = route.shape[0]
    tm = TOKEN_TILE
    return pl.pallas_call(
        _pos_kernel,
        out_shape=jax.ShapeDtypeStruct((2, n_tok), jnp.int32),
        grid=(n_tok // tm,),
        in_specs=[pl.BlockSpec((tm, LANES), lambda i: (i, 0)),
                  pl.BlockSpec((N_EXPERTS, 1), lambda i: (0, 0))],
        out_specs=pl.BlockSpec((2, tm), lambda i: (0, i)),
        compiler_params=pltpu.CompilerParams(
            dimension_semantics=("arbitrary",),
            vmem_limit_bytes=_vmem_limit(tm * LANES * 4 + N_EXPERTS * LANES * 4 + 8 * tm * 4)),
        name="expert_row_positions",
    )(route, starts.astype(F32).reshape(N_EXPERTS, 1))


def _dispatch(route, counts, n_tok):
    tm = EXPERT_TILE
    counts = counts[:N_EXPERTS].astype(jnp.int32)
    padded = ((counts + tm - 1) // tm) * tm
    ends = jnp.cumsum(padded)
    starts = ends - padded
    pos = _row_positions(route, starts).reshape(2 * n_tok)
    p = 2 * n_tok + N_EXPERTS * tm
    tok = jnp.concatenate([jnp.arange(n_tok, dtype=jnp.int32)] * 2)
    row_tok = (jnp.arange(p, dtype=jnp.int32) % n_tok).at[pos].set(
        tok, indices_are_sorted=False, unique_indices=True, mode="promise_in_bounds")
    tile_start = jnp.arange(p // tm, dtype=jnp.int32) * tm
    tile_expert = jnp.minimum(jnp.sum(ends[None, :] <= tile_start[:, None], axis=1),
                              N_EXPERTS - 1).astype(jnp.int32)
    n_used = (ends[-1] // tm).astype(jnp.int32).reshape(1)
    return row_tok, tile_expert, n_used, pos


def _final_kernel(x_ref, y_ref, route_ref, mod_ref, g_ref, o_ref):
    o_ref[0] = _rms(_expert_residual(x_ref[0], y_ref, route_ref, mod_ref), g_ref[...])


def _final_norm(x, y, route, mod, g):
    bsz, s, d = x.shape
    tm = TOKEN_TILE
    tok = lambda b, i: (b, i, 0)
    return pl.pallas_call(
        _final_kernel,
        out_shape=jax.ShapeDtypeStruct((bsz, s, d), F32),
        grid=(bsz, s // tm),
        in_specs=[pl.BlockSpec((1, tm, d), tok),
                  pl.BlockSpec((2, 1, tm, d), lambda b, i: (0, b, i, 0)),
                  pl.BlockSpec((1, tm, LANES), tok),
                  pl.BlockSpec((1, 6, d), lambda b, i: (b, 0, 0)),
                  pl.BlockSpec((1, d), lambda b, i: (0, 0))],
        out_specs=pl.BlockSpec((1, tm, d), tok),
        compiler_params=pltpu.CompilerParams(
            dimension_semantics=("arbitrary", "arbitrary"),
            vmem_limit_bytes=_vmem_limit(4 * tm * d * 4 + tm * LANES * 4)),
        name="residual_final_norm",
    )(x, y, route, mod, g.reshape(1, d))


def kernel(x, c, w_ada, b_ada, norm_mix, norm_ffn, norm_final, rel_bias, even_w_in, even_lambda, even_subln, even_w_gk2, even_b_gk, even_gla_norm, even_w_out, odd_w_in, odd_w_out, router_group_w, router_group_b, router_expert_w, router_expert_b, expert_w_gate, expert_w_up, expert_w_down):
    bsz, s, d = x.shape
    n_tok = bsz * s
    assert s % TOKEN_TILE == 0 and s % GLA_STEP == 0 and s % STICK_Q_TILE == 0
    assert STICK_Q_TILE % ATTN_TILE == 0 and s % (DIFF_KEY_TILES * ATTN_TILE) == 0
    mod_all = _modulation(c, w_ada, b_ada).reshape(DEPTH, bsz, 6, d)
    bias_tiles = _diff_bias_tiles(rel_bias)
    n_even_main = 3072
    y = route = prev_mod = None
    for l in range(DEPTH):
        mod = mod_all[l]
        i = l // 2
        if l % 2 == 0:
            w_in = even_w_in[i]
            w_main = w_in[:, :n_even_main].astype(BF16)
            w_gate = jnp.zeros((d, LANES), BF16).at[:, :GATE_RANK].set(
                w_in[:, n_even_main:].astype(BF16))
            x, proj, gate = _in_projection(x, y, route, prev_mod, norm_mix[l], mod, w_main, w_gate)
            oa = _diff_attention(proj, bias_tiles, even_lambda[i], even_subln[i], l)
            ob = _gla(proj, gate, even_w_gk2[i], even_b_gk[i], even_gla_norm[i])
            w_out = even_w_out[i].astype(BF16)
            o_list = [oa, ob]
            w_list = [w_out[:H_A * DV_A], w_out[H_A * DV_A:]]
        else:
            x, proj, _ = _in_projection(x, y, route, prev_mod, norm_mix[l], mod,
                                        odd_w_in[i].astype(BF16), None)
            o_list = [_stick_breaking(proj)]
            w_list = [odd_w_out[i].astype(BF16)]
        w_router = jnp.zeros((d, LANES), F32)
        w_router = w_router.at[:, :N_EXPERTS].set(router_expert_w[l].astype(F32))
        w_router = w_router.at[:, N_EXPERTS:N_EXPERTS + N_GROUPS].set(router_group_w[l].astype(F32))
        b_router = jnp.zeros((1, LANES), F32)
        b_router = b_router.at[0, :N_EXPERTS].set(router_expert_b[l].astype(F32))
        b_router = b_router.at[0, N_EXPERTS:N_EXPERTS + N_GROUPS].set(router_group_b[l].astype(F32))
        x, h2, route, counts = _out_projection(o_list, w_list, x, mod, norm_ffn[l], w_router,
                                               b_router)
        row_tok, tile_expert, n_used, pos = _dispatch(route.reshape(n_tok, LANES), counts[0], n_tok)
        xs = h2.reshape(n_tok, d).at[row_tok].get(mode="promise_in_bounds")
        out = _grouped_experts(xs, tile_expert, n_used, expert_w_gate, expert_w_up,
                               expert_w_down, l)
        y = out.at[pos].get(mode="promise_in_bounds").reshape(2, bsz, s, d)
        prev_mod = mod
    return _final_norm(x, y, route, prev_mod, norm_final)
```

```python
import functools
import math

import numpy as np
import jax
import jax.numpy as jnp
from jax import lax
from jax.experimental import pallas as pl
from jax.experimental.pallas import tpu as pltpu

F32 = jnp.float32
BF16 = jnp.bfloat16

D_MODEL = 1024
DEPTH = 4
CHUNK = 64
EPS = 1e-6
H_A, DH_A, DV_A = 4, 64, 128
H_B, DK_B, DV_B = 4, 64, 128
GATE_RANK = 16
GATE_TAU = 16.0
H_C, DH_C = 16, 64
N_BUCKETS = 32
MAX_DISTANCE = 128
N_GROUPS = 4
EXPERTS_PER_GROUP = 8
N_EXPERTS = N_GROUPS * EXPERTS_PER_GROUP
D_EXPERT = 512
MIX = 1024

LANES = 128
V7X_VMEM_BYTES = 64 * 1024 * 1024
NEG_BIG = -1e30
EXP_ZERO_BELOW = -104.0

TOKEN_TILE = 512
ATTN_TILE = 128
STICK_PAIRS = 2
STICK_AHEAD = 2
DIFF_KEY_TILES = 4
DIFF_Q_TILES = 2
GLA_STEP = 256
EXPERT_TILE = 512


def _vmem_limit(block_bytes, scratch_bytes=0):
    est = 2 * block_bytes + scratch_bytes + 16 * 1024 * 1024
    return int(min(est, V7X_VMEM_BYTES - 8 * 1024 * 1024))


def _nt(a, b):
    return lax.dot_general(a, b, (((1,), (1,)), ((), ())), preferred_element_type=F32)


def _tn(a, b):
    return lax.dot_general(a, b, (((0,), (0,)), ((), ())), preferred_element_type=F32)


def _dot(a, b):
    return jnp.dot(a, b, preferred_element_type=F32)


def _split(x):
    hi = x.astype(BF16)
    lo = (x - hi.astype(F32)).astype(BF16)
    return hi, lo


def _dot3(a, b):
    ah, al = _split(a)
    bh, bl = _split(b)
    return _dot(ah, bh) + _dot(al, bh) + _dot(ah, bl)


def _rms(x, g):
    return x * lax.rsqrt(jnp.mean(x * x, axis=-1, keepdims=True) + EPS) * g


def _softplus(z):
    return jnp.maximum(z, 0.0) + jnp.log(1.0 + jnp.exp(-jnp.abs(z)))


def _silu(x):
    return x / (1.0 + jnp.exp(-x))


def _mod_kernel(c_ref, w_ref, b_ref, o_ref):
    c = c_ref[...]
    o_ref[0] = _dot3(_silu(c), w_ref[0]) + b_ref[0]


def _modulation(c, w_ada, b_ada):
    depth, d, n = w_ada.shape
    bsz = c.shape[0]
    tn = 1536
    return pl.pallas_call(
        _mod_kernel,
        out_shape=jax.ShapeDtypeStruct((depth, bsz, n), F32),
        grid=(depth, n // tn),
        in_specs=[
            pl.BlockSpec((bsz, d), lambda l, j: (0, 0)),
            pl.BlockSpec((1, d, tn), lambda l, j: (l, 0, j)),
            pl.BlockSpec((1, 1, tn), lambda l, j: (l, 0, j)),
        ],
        out_specs=pl.BlockSpec((1, bsz, tn), lambda l, j: (l, 0, j)),
        compiler_params=pltpu.CompilerParams(
            dimension_semantics=("arbitrary", "arbitrary"),
            vmem_limit_bytes=_vmem_limit(d * tn * 4 + bsz * (d + tn) * 4)),
        name="adaln_modulation",
    )(c, w_ada, b_ada.reshape(depth, 1, n))


def _expert_residual(x, y_ref, route_ref, mod_ref):
    route = route_ref[0]
    y = route[:, 2:3] * y_ref[0, 0].astype(F32) + route[:, 3:4] * y_ref[1, 0].astype(F32)
    return x + mod_ref[0, 5:6, :] * y


def _proj_kernel(*refs, has_res, n_main, has_gate):
    it = iter(refs)
    x_ref = next(it)
    if has_res:
        y_ref = next(it)
        route_ref = next(it)
        pmod_ref = next(it)
    g_ref = next(it)
    mod_ref = next(it)
    w_ref = next(it)
    wg_ref = next(it) if has_gate else None
    xo_ref = next(it) if has_res else None
    o_ref = next(it)
    og_ref = next(it) if has_gate else None

    x = x_ref[0]
    if has_res:
        x = _expert_residual(x, y_ref, route_ref, pmod_ref)
        xo_ref[0] = x
    h = _rms(x, g_ref[...]) * (1.0 + mod_ref[0, 1:2, :]) + mod_ref[0, 0:1, :]
    hb = h.astype(BF16)
    step = 512
    for n0 in range(0, n_main, step):
        o_ref[0, :, n0:n0 + step] = _dot(hb, w_ref[:, n0:n0 + step]).astype(BF16)
    if has_gate:
        og_ref[0] = _dot(hb, wg_ref[...])


def _in_projection(x, y, route, prev_mod, g, mod, w, w_gate):
    bsz, s, d = x.shape
    tm = TOKEN_TILE
    has_res = y is not None
    has_gate = w_gate is not None
    n_main = w.shape[1]
    tok = lambda b, i: (b, i, 0)
    per_b = lambda b, i: (b, 0, 0)
    const2 = lambda b, i: (0, 0)
    in_specs = [pl.BlockSpec((1, tm, d), tok)]
    args = [x]
    if has_res:
        in_specs += [pl.BlockSpec((2, 1, tm, d), lambda b, i: (0, b, i, 0)),
                     pl.BlockSpec((1, tm, LANES), tok), pl.BlockSpec((1, 6, d), per_b)]
        args += [y, route, prev_mod]
    in_specs += [pl.BlockSpec((1, d), const2), pl.BlockSpec((1, 6, d), per_b),
                 pl.BlockSpec((d, n_main), const2)]
    args += [g.reshape(1, d), mod, w]
    if has_gate:
        in_specs.append(pl.BlockSpec((d, LANES), const2))
        args.append(w_gate)
    out_shape, out_specs = [], []
    if has_res:
        out_shape.append(jax.ShapeDtypeStruct((bsz, s, d), F32))
        out_specs.append(pl.BlockSpec((1, tm, d), tok))
    out_shape.append(jax.ShapeDtypeStruct((bsz, s, n_main), BF16))
    out_specs.append(pl.BlockSpec((1, tm, n_main), tok))
    if has_gate:
        out_shape.append(jax.ShapeDtypeStruct((bsz, s, LANES), F32))
        out_specs.append(pl.BlockSpec((1, tm, LANES), tok))
    block_bytes = (tm * d * 4 * (4 if has_res else 1) + d * n_main * 2 + tm * n_main * 2
                   + d * LANES * 2 + 2 * tm * LANES * 4)
    outs = pl.pallas_call(
        functools.partial(_proj_kernel, has_res=has_res, n_main=n_main, has_gate=has_gate),
        out_shape=out_shape,
        grid=(bsz, s // tm),
        in_specs=in_specs,
        out_specs=out_specs,
        compiler_params=pltpu.CompilerParams(
            dimension_semantics=("arbitrary", "arbitrary"),
            vmem_limit_bytes=_vmem_limit(block_bytes, tm * d * 8)),
        name="norm_mod_in_projection",
    )(*args)
    outs = list(outs)
    x_new = outs.pop(0) if has_res else x
    proj = outs.pop(0)
    gate = outs.pop(0) if has_gate else None
    return x_new, proj, gate


def _t5_bucket(rel):
    nb = N_BUCKETS // 2
    max_exact = nb // 2
    ret = jnp.where(rel > 0, nb, 0)
    n = jnp.abs(rel)
    nf = jnp.maximum(n, 1).astype(F32)
    large = max_exact + (jnp.log(nf / max_exact) / math.log(MAX_DISTANCE / max_exact)
                         * (nb - max_exact)).astype(jnp.int32)
    large = jnp.minimum(large, nb - 1)
    return ret + jnp.where(n < max_exact, n, large)


def _diff_bias_tiles(rel_bias):
    t = ATTN_TILE
    assert t >= MAX_DISTANCE, "tiles two or more away must lie beyond the last bucket edge"
    qp = jnp.arange(t)[:, None]
    kp = jnp.arange(t)[None, :]
    tiles = []
    for off in (0, 1, 2):
        rel = (kp - off * t) - qp
        hit = _t5_bucket(rel)[None, :, :, None] == jnp.arange(N_BUCKETS)[None, None, None, :]
        b = jnp.sum(jnp.where(hit, jnp.transpose(rel_bias.astype(F32))[:, None, None, :], 0.0),
                    axis=-1)
        if off == 0:
            b = jnp.where((kp // CHUNK) <= (qp // CHUNK), b, NEG_BIG)
        tiles.append(b)
    tiles.append(jnp.full_like(tiles[0], NEG_BIG))
    tab = jnp.stack(tiles, axis=1)
    return jnp.concatenate([tab, tab], axis=2)


def _diff_attn_kernel(q_ref, k_ref, v_ref, bias_ref, lam_ref, g_ref, o_ref, *, lambda_init):
    t = ATTN_TILE
    chains = DIFF_Q_TILES
    i = pl.program_id(2)
    lane = lax.broadcasted_iota(jnp.int32, (t, LANES), 1)
    scale = jnp.asarray(DH_A ** -0.5, BF16)
    stacked = []
    for c in range(chains):
        q = q_ref[0, c * t:(c + 1) * t, :]
        zero = jnp.zeros_like(q)
        stacked += [jnp.where(lane < DH_A, q, zero), jnp.where(lane >= DH_A, q, zero)]
    qq = jnp.concatenate(stacked, axis=0) * scale

    group = DIFF_KEY_TILES
    span = group * t
    last_tile = i * chains + chains - 1

    def body(n, carry):
        r0 = pl.multiple_of(n * span, span)
        kb = k_ref[0, pl.ds(r0, span), :]
        vb = v_ref[0, pl.ds(r0, span), :]
        s_chain = [_nt(qq[c * 2 * t:(c + 1) * 2 * t], kb) for c in range(chains)]
        out = []
        for c in range(chains):
            m, l, acc = carry[c]

            def logits(u):
                back = (i * chains + c) - (n * group + u)
                which = jnp.where(back < 0, 3, jnp.minimum(back, 2))
                return s_chain[c][:, u * t:(u + 1) * t] + bias_ref[0, which]

            top = logits(0)
            for u in range(1, group):
                top = jnp.maximum(top, logits(u))
            m_new = jnp.maximum(m, jnp.max(top, axis=-1, keepdims=True))
            alpha = jnp.exp(m - m_new)
            ps = [jnp.exp(logits(u) - m_new) for u in range(group)]
            tot = ps[0]
            for u in range(1, group):
                tot = tot + ps[u]
            l = alpha * l + jnp.sum(tot, axis=-1, keepdims=True)
            p = jnp.concatenate([pu.astype(BF16) for pu in ps], axis=1)
            acc = alpha * acc + _dot(p, vb)
            out.append((m_new, l, acc))
        return tuple(out)

    init = tuple((jnp.full((2 * t, 1), NEG_BIG, F32), jnp.zeros((2 * t, 1), F32),
                  jnp.zeros((2 * t, DV_A), F32)) for _ in range(chains))
    final = lax.fori_loop(0, last_tile // group + 1, body, init)
    lp = lam_ref[...].astype(F32)
    lam = (jnp.exp(jnp.sum(lp[0:1] * lp[1:2], axis=-1, keepdims=True))
           - jnp.exp(jnp.sum(lp[2:3] * lp[3:4], axis=-1, keepdims=True)) + lambda_init)
    for c in range(chains):
        _, l, acc = final[c]
        o = acc / l
        w = o[:t] - lam * o[t:]
        o_ref[0, c * t:(c + 1) * t, :] = (_rms(w, g_ref[...]) * (1.0 - lambda_init)
                                          ).astype(o_ref.dtype)


def _diff_attention(proj, bias_tiles, lam_p, subln_g, layer_idx):
    bsz, s, _ = proj.shape
    t = ATTN_TILE
    tq = DIFF_Q_TILES * t
    lambda_init = 0.8 - 0.6 * math.exp(-0.3 * layer_idx)
    return pl.pallas_call(
        functools.partial(_diff_attn_kernel, lambda_init=lambda_init),
        out_shape=jax.ShapeDtypeStruct((bsz, s, H_A * DV_A), BF16),
        grid=(bsz, H_A, s // tq),
        in_specs=[
            pl.BlockSpec((1, tq, LANES), lambda b, h, i: (b, i, h)),
            pl.BlockSpec((1, s, LANES), lambda b, h, i: (b, 0, H_A + h)),
            pl.BlockSpec((1, s, LANES), lambda b, h, i: (b, 0, 2 * H_A + h)),
            pl.BlockSpec((1, 4, 2 * t, t), lambda b, h, i: (h, 0, 0, 0)),
            pl.BlockSpec((4, DH_A), lambda b, h, i: (0, 0)),
            pl.BlockSpec((1, DV_A), lambda b, h, i: (0, 0)),
        ],
        out_specs=pl.BlockSpec((1, tq, DV_A), lambda b, h, i: (b, i, h)),
        compiler_params=pltpu.CompilerParams(
            dimension_semantics=("arbitrary", "arbitrary", "arbitrary"),
            vmem_limit_bytes=_vmem_limit(2 * s * LANES * 2 + 4 * 2 * t * t * 4 + 4 * tq * LANES,
                                         4 * tq * DIFF_KEY_TILES * t * 4)),
        name="diff_attention",
    )(proj, proj, proj, bias_tiles, lam_p, subln_g.reshape(1, DV_A))


def _gla_kernel(q_ref, k_ref, v_ref, r_ref, bg_ref, wgk_ref, bgk_ref, g_ref, o_ref, st_ref):
    c = CHUNK
    ts = q_ref.shape[1]
    n_chunks = ts // c

    @pl.when(pl.program_id(1) == 0)
    def _():
        st_ref[...] = jnp.zeros_like(st_ref)

    row = lax.broadcasted_iota(jnp.int32, (ts, ts), 0)
    col = lax.broadcasted_iota(jnp.int32, (ts, ts), 1)
    same_chunk_before = ((row // c) == (col // c)) & (col <= row)
    tri = jnp.where(same_chunk_before, 1.0, 0.0).astype(BF16)
    causal = (lax.broadcasted_iota(jnp.int32, (c, c), 1)
              <= lax.broadcasted_iota(jnp.int32, (c, c), 0))
    lane = lax.broadcasted_iota(jnp.int32, (ts, LANES), 1)
    wgk = wgk_ref[...]
    bgk = bgk_ref[...]
    g = g_ref[...]
    bg = bg_ref[0]

    las = [-_softplus(-(_dot3(bg, wgk[:, hp * LANES:(hp + 1) * LANES])
                        + bgk[:, hp * LANES:(hp + 1) * LANES])) / GATE_TAU
           for hp in range(H_B // 2)]
    bs = []
    for la in las:
        la_hi, la_lo = _split(la)
        bs.append(_dot(tri, la_hi) + _dot(tri, la_lo))

    heads = []
    for hp in range(H_B // 2):
        cols = slice(hp * LANES, (hp + 1) * LANES)
        b = bs[hp]
        b_last = jnp.concatenate(
            [jnp.broadcast_to(b[(ci + 1) * c - 1:(ci + 1) * c, :], (c, LANES))
             for ci in range(n_chunks)], axis=0)
        qf = q_ref[0, :, cols].astype(F32) * (DK_B ** -0.5)
        kf = k_ref[0, :, cols].astype(F32)
        q_dec = qf * jnp.exp(b)
        k_inv = (kf * jnp.exp(-b)).astype(BF16)
        k_dec = kf * jnp.exp(b_last - b)
        decays = [jnp.exp(b[(ci + 1) * c - 1:(ci + 1) * c, :]) for ci in range(n_chunks)]
        for hh in range(2):
            own = (lane >= hh * DK_B) & (lane < (hh + 1) * DK_B)
            heads.append((jnp.where(own, q_dec, 0.0).astype(BF16), k_inv,
                          jnp.where(own, k_dec, 0.0).astype(BF16), decays))

    def rows(x, ci):
        return x[ci * c:(ci + 1) * c]

    scores, increments = [], []
    for h, (qd, k_inv, kd, _) in enumerate(heads):
        vh = v_ref[0, :, h * DV_B:(h + 1) * DV_B]
        scores.append([jnp.where(causal, _nt(rows(qd, ci), rows(k_inv, ci)), 0.0).astype(BF16)
                       for ci in range(n_chunks)])
        increments.append([_tn(rows(vh, ci), rows(kd, ci)) for ci in range(n_chunks)])

    for h, (qd, _, _, decays) in enumerate(heads):
        vh = v_ref[0, :, h * DV_B:(h + 1) * DV_B]
        st = st_ref[h]
        states = []
        for ci in range(n_chunks):
            states.append(st.astype(BF16))
            st = st * decays[ci] + increments[h][ci]
        st_ref[h] = st
        o = jnp.concatenate(
            [_dot(scores[h][ci], rows(vh, ci)) + _nt(rows(qd, ci), states[ci])
             for ci in range(n_chunks)], axis=0)
        rr = r_ref[0, :, h * DV_B:(h + 1) * DV_B].astype(F32)
        o_ref[0, :, h * DV_B:(h + 1) * DV_B] = (_rms(o, g) * _silu(rr)).astype(o_ref.dtype)


def _gla(proj, gate, w_gk2, b_gk, gla_g):
    bsz, s, _ = proj.shape
    ts = GLA_STEP
    nk = H_B * DK_B
    nv = H_B * DV_B
    w_pad = jnp.zeros((LANES, nk), F32).at[:GATE_RANK].set(w_gk2.astype(F32))
    return pl.pallas_call(
        _gla_kernel,
        out_shape=jax.ShapeDtypeStruct((bsz, s, nv), BF16),
        grid=(bsz, s // ts),
        in_specs=[
            pl.BlockSpec((1, ts, nk), lambda b, i: (b, i, 1536 // nk)),
            pl.BlockSpec((1, ts, nk), lambda b, i: (b, i, 1792 // nk)),
            pl.BlockSpec((1, ts, nv), lambda b, i: (b, i, 2048 // nv)),
            pl.BlockSpec((1, ts, nv), lambda b, i: (b, i, 2560 // nv)),
            pl.BlockSpec((1, ts, LANES), lambda b, i: (b, i, 0)),
            pl.BlockSpec((LANES, nk), lambda b, i: (0, 0)),
            pl.BlockSpec((1, nk), lambda b, i: (0, 0)),
            pl.BlockSpec((1, DV_B), lambda b, i: (0, 0)),
        ],
        out_specs=pl.BlockSpec((1, ts, nv), lambda b, i: (b, i, 0)),
        scratch_shapes=[pltpu.VMEM((H_B, DV_B, LANES), F32)],
        compiler_params=pltpu.CompilerParams(
            dimension_semantics=("arbitrary", "arbitrary"),
            vmem_limit_bytes=_vmem_limit(ts * (2 * nk + 3 * nv) * 2 + ts * LANES * 4,
                                         H_B * DV_B * LANES * 4)),
        name="gated_linear_attention",
    )(proj, proj, proj, proj, gate, w_pad, b_gk.reshape(1, nk), gla_g.reshape(1, DV_B))


def _stick_kernel(q_ref, k_ref, v_ref, o_ref, run_ref, acc_ref):
    t = ATTN_TILE
    chains = STICK_PAIRS
    i = pl.program_id(2)
    lane = lax.broadcasted_iota(jnp.int32, (t, LANES), 1)
    scale = jnp.asarray(DH_C ** -0.5, BF16)
    qqs = []
    for c in range(chains):
        q = q_ref[0, :, c * LANES:(c + 1) * LANES]
        zero = jnp.zeros_like(q)
        qqs.append(jnp.concatenate([jnp.where(lane < DH_C, q, zero),
                                    jnp.where(lane >= DH_C, q, zero)], axis=0) * scale)
    row = lax.broadcasted_iota(jnp.int32, (t, t), 0)
    col = lax.broadcasted_iota(jnp.int32, (t, t), 1)
    later = jnp.where(row > col, 1.0, 0.0).astype(BF16)
    below = (lax.broadcasted_iota(jnp.int32, (2 * t, t), 1)
             < lax.broadcasted_iota(jnp.int32, (2 * t, t), 0) % t)

    def logits(j):
        r0 = pl.multiple_of(jnp.maximum(j, 0) * t, t)
        return [_nt(qqs[c], k_ref[0, pl.ds(r0, t), c * LANES:(c + 1) * LANES])
                for c in range(chains)]

    def tiles(first, zs, runs, accs, diagonal_first):
        runs, accs = list(runs), list(accs)
        mids = []
        for u in range(len(zs)):
            for c in range(chains):
                z = zs[u][c]
                sp = _softplus(z)
                log_1m = -sp
                masked = diagonal_first and u == 0
                if masked:
                    log_1m = jnp.where(below, log_1m, 0.0)
                hi, lo = _split(log_1m)
                within = _dot(hi, later) + _dot(lo, later)
                mids.append((u, c, z - sp, log_1m, within, masked))
        for u, c, log_b, log_1m, within, masked in mids:
            r0 = pl.multiple_of((first - u) * t, t)
            vb = v_ref[0, pl.ds(r0, t), c * LANES:(c + 1) * LANES]
            a = jnp.exp(log_b + (within + runs[c]))
            if masked:
                a = jnp.where(below, a, 0.0)
            ab = a.astype(BF16)
            zero_v = jnp.zeros_like(vb)
            a2 = jnp.concatenate([ab[:t], ab[t:]], axis=1)
            v2 = jnp.concatenate([jnp.where(lane < DH_C, vb, zero_v),
                                  jnp.where(lane >= DH_C, vb, zero_v)], axis=0)
            accs[c] = accs[c] + _dot(a2, v2)
            runs[c] = runs[c] + jnp.sum(log_1m, axis=-1, keepdims=True)
        return runs, accs

    def first_block(n_tiles):
        zs = [logits(i - u) for u in range(n_tiles)]
        runs, accs = tiles(i, zs, [jnp.zeros((2 * t, 1), F32)] * chains,
                           [jnp.zeros((t, LANES), F32)] * chains, True)
        for c in range(chains):
            run_ref[c] = runs[c]
            acc_ref[c] = accs[c]

    ahead = STICK_AHEAD

    @pl.when(i >= ahead)
    def _():
        first_block(1 + ahead)

    @pl.when(i < ahead)
    def _():
        first_block(1)

    n_left = i
    done0 = jnp.where(i >= ahead, ahead, 0)

    def go_on(done, runs):
        top = runs[0]
        for c in range(1, chains):
            top = jnp.maximum(top, runs[c])
        return ((done < n_left) & (jnp.max(top) > EXP_ZERO_BELOW)).astype(jnp.int32)

    def body(carry):
        done, _, runs, accs = carry
        first = n_left - 1 - done
        runs, accs = tiles(first, [logits(first)], runs, accs, False)
        done = done + 1
        return done, go_on(done, runs), tuple(runs), tuple(accs)

    runs = tuple(run_ref[c] for c in range(chains))
    accs = tuple(acc_ref[c] for c in range(chains))
    carry = (done0, go_on(done0, runs), runs, accs)
    accs = lax.while_loop(lambda carry: carry[1] > 0, body, carry)[3]
    for c in range(chains):
        o_ref[0, :, c * LANES:(c + 1) * LANES] = accs[c].astype(o_ref.dtype)


def _stick_breaking(proj):
    bsz, s, _ = proj.shape
    t = ATTN_TILE
    width = STICK_PAIRS * LANES
    n_blocks = H_C * DH_C // width
    return pl.pallas_call(
        _stick_kernel,
        out_shape=jax.ShapeDtypeStruct((bsz, s, H_C * DH_C), BF16),
        grid=(bsz, n_blocks, s // t),
        in_specs=[
            pl.BlockSpec((1, t, width), lambda b, h, i: (b, i, h)),
            pl.BlockSpec((1, s, width), lambda b, h, i: (b, 0, n_blocks + h)),
            pl.BlockSpec((1, s, width), lambda b, h, i: (b, 0, 2 * n_blocks + h)),
        ],
        out_specs=pl.BlockSpec((1, t, width), lambda b, h, i: (b, i, h)),
        scratch_shapes=[pltpu.VMEM((STICK_PAIRS, 2 * t, 1), F32),
                        pltpu.VMEM((STICK_PAIRS, t, LANES), F32)],
        compiler_params=pltpu.CompilerParams(
            dimension_semantics=("arbitrary", "arbitrary", "arbitrary"),
            vmem_limit_bytes=_vmem_limit(2 * s * width * 2 + 4 * t * width,
                                         STICK_PAIRS * 3 * t * LANES * 4)),
        name="stick_breaking_attention",
    )(proj, proj, proj)


def _route(logits):
    lane_i = lax.broadcasted_iota(jnp.int32, logits.shape, 1)
    lane = lane_i.astype(F32)
    group_of_lane = (lane_i // EXPERTS_PER_GROUP).astype(F32)
    big = jnp.float32(1 << 20)
    neg = jnp.float32(-jnp.inf)
    is_group = (lane_i >= N_EXPERTS) & (lane_i < N_EXPERTS + N_GROUPS)
    gl = jnp.where(is_group, logits, neg)
    gmax = jnp.max(gl, axis=-1, keepdims=True)
    g_val = 1.0 / jnp.sum(jnp.exp(gl - gmax), axis=-1, keepdims=True)
    g_sel = jnp.min(jnp.where(gl == gmax, lane - N_EXPERTS, big), axis=-1, keepdims=True)
    in_group = (lane_i < N_EXPERTS) & (group_of_lane == g_sel)
    el = jnp.where(in_group, logits, neg)
    emax = jnp.max(el, axis=-1, keepdims=True)
    esum = jnp.sum(jnp.exp(el - emax), axis=-1, keepdims=True)
    i1 = jnp.min(jnp.where(el == emax, lane, big), axis=-1, keepdims=True)
    el2 = jnp.where(lane == i1, neg, el)
    emax2 = jnp.max(el2, axis=-1, keepdims=True)
    i2 = jnp.min(jnp.where(el2 == emax2, lane, big), axis=-1, keepdims=True)
    v1 = 1.0 / esum
    v2 = jnp.exp(emax2 - emax) / esum
    tot = v1 + v2
    w1 = g_val * (v1 / tot)
    w2 = g_val * (v2 / tot)
    out = jnp.where(lane_i == 0, i1, 0.0)
    out = jnp.where(lane_i == 1, i2, out)
    out = jnp.where(lane_i == 2, w1, out)
    out = jnp.where(lane_i == 3, w2, out)
    return out, (lane == i1), (lane == i2)


def _out_kernel(*refs, n_o):
    o_refs = refs[:n_o]
    w_refs = refs[n_o:2 * n_o]
    x_ref, mod_ref, g_ref, wr_ref, br_ref, xo_ref, h_ref, r_ref, cnt_ref, run_ref = refs[2 * n_o:]

    @pl.when((pl.program_id(0) == 0) & (pl.program_id(1) == 0))
    def _():
        run_ref[...] = jnp.zeros_like(run_ref)

    m = _dot(o_refs[0][0], w_refs[0][...])
    for a in range(1, n_o):
        m = m + _dot(o_refs[a][0], w_refs[a][...])
    x = x_ref[0] + mod_ref[0, 2:3, :] * m
    xo_ref[0] = x
    h = _rms(x, g_ref[...]) * (1.0 + mod_ref[0, 4:5, :]) + mod_ref[0, 3:4, :]
    h_ref[0] = h.astype(BF16)
    route, hit1, hit2 = _route(_dot3(h, wr_ref[...]) + br_ref[...])
    tm = route.shape[0]
    picks = jnp.where(hit1 | hit2, 1.0, 0.0)
    row = lax.broadcasted_iota(jnp.int32, (tm, tm), 0)
    col = lax.broadcasted_iota(jnp.int32, (tm, tm), 1)
    before = jnp.where(col < row, 1.0, 0.0).astype(BF16)
    earlier = _dot(before, picks.astype(BF16)) + run_ref[...]
    rank1 = jnp.sum(jnp.where(hit1, earlier, 0.0), axis=-1, keepdims=True)
    rank2 = jnp.sum(jnp.where(hit2, earlier, 0.0), axis=-1, keepdims=True)
    lane_i = lax.broadcasted_iota(jnp.int32, route.shape, 1)
    route = jnp.where(lane_i == 4, rank1, route)
    route = jnp.where(lane_i == 5, rank2, route)
    r_ref[0] = route
    run_ref[...] = run_ref[...] + jnp.sum(picks, axis=0, keepdims=True)
    cnt_ref[...] = run_ref[...]


def _out_projection(o_list, w_list, x, mod, g, w_router, b_router):
    bsz, s, d = x.shape
    tm = TOKEN_TILE
    n_o = len(o_list)
    tok = lambda b, i: (b, i, 0)
    const2 = lambda b, i: (0, 0)
    in_specs = ([pl.BlockSpec((1, tm, o.shape[2]), tok) for o in o_list]
                + [pl.BlockSpec(w.shape, const2) for w in w_list]
                + [pl.BlockSpec((1, tm, d), tok), pl.BlockSpec((1, 6, d), lambda b, i: (b, 0, 0)),
                   pl.BlockSpec((1, d), const2), pl.BlockSpec((d, LANES), const2),
                   pl.BlockSpec((1, LANES), const2)])
    block_bytes = (sum(tm * o.shape[2] * 2 for o in o_list) + sum(w.size * 2 for w in w_list)
                   + 2 * tm * d * 4 + tm * d * 2 + d * LANES * 4 + tm * LANES * 4)
    return pl.pallas_call(
        functools.partial(_out_kernel, n_o=n_o),
        out_shape=[jax.ShapeDtypeStruct((bsz, s, d), F32),
                   jax.ShapeDtypeStruct((bsz, s, d), BF16),
                   jax.ShapeDtypeStruct((bsz, s, LANES), F32),
                   jax.ShapeDtypeStruct((1, LANES), F32)],
        grid=(bsz, s // tm),
        in_specs=in_specs,
        out_specs=[pl.BlockSpec((1, tm, d), tok), pl.BlockSpec((1, tm, d), tok),
                   pl.BlockSpec((1, tm, LANES), tok), pl.BlockSpec((1, LANES), const2)],
        scratch_shapes=[pltpu.VMEM((1, LANES), F32)],
        compiler_params=pltpu.CompilerParams(
            dimension_semantics=("arbitrary", "arbitrary"),
            vmem_limit_bytes=_vmem_limit(block_bytes, tm * d * 12 + tm * tm * 4)),
        name="out_projection_norm_router",
    )(*o_list, *w_list, x, mod, g.reshape(1, d), w_router, b_router)


def _expert_kernel(te_ref, nu_ref, x_ref, wg_ref, wu_ref, wd_ref, o_ref, wgb_ref, wub_ref, wdb_ref):
    t = pl.program_id(0)
    prev = te_ref[jnp.maximum(t - 1, 0)]

    @pl.when((t == 0) | (te_ref[t] != prev))
    def _():
        wgb_ref[...] = wg_ref[0, 0].astype(BF16)
        wub_ref[...] = wu_ref[0, 0].astype(BF16)
        wdb_ref[...] = wd_ref[0, 0].astype(BF16)

    @pl.when(t < nu_ref[0])
    def _():
        x = x_ref[...]
        gt = _dot(x, wgb_ref[...])
        up = _dot(x, wub_ref[...])
        o_ref[...] = _dot((_silu(gt) * up).astype(BF16), wdb_ref[...]).astype(o_ref.dtype)

    @pl.when(t >= nu_ref[0])
    def _():
        o_ref[...] = jnp.zeros_like(o_ref)


def _grouped_experts(xs, tile_expert, n_used, w_gate, w_up, w_down, layer):
    p, d = xs.shape
    tm = EXPERT_TILE
    f = w_gate.shape[3]
    grid_spec = pltpu.PrefetchScalarGridSpec(
        num_scalar_prefetch=2,
        grid=(p // tm,),
        in_specs=[
            pl.BlockSpec((tm, d), lambda t, te, nu: (t, 0)),
            pl.BlockSpec((1, 1, d, f), lambda t, te, nu: (layer, te[t], 0, 0)),
            pl.BlockSpec((1, 1, d, f), lambda t, te, nu: (layer, te[t], 0, 0)),
            pl.BlockSpec((1, 1, f, d), lambda t, te, nu: (layer, te[t], 0, 0)),
        ],
        out_specs=pl.BlockSpec((tm, d), lambda t, te, nu: (t, 0)),
        scratch_shapes=[pltpu.VMEM((d, f), BF16), pltpu.VMEM((d, f), BF16),
                        pltpu.VMEM((f, d), BF16)],
    )
    block_bytes = tm * d * 2 + 3 * d * f * 4 + tm * d * 2
    return pl.pallas_call(
        _expert_kernel,
        out_shape=jax.ShapeDtypeStruct((p, d), BF16),
        grid_spec=grid_spec,
        compiler_params=pltpu.CompilerParams(
            dimension_semantics=("arbitrary",),
            vmem_limit_bytes=_vmem_limit(block_bytes, 3 * d * f * 2 + tm * f * 16)),
        name="grouped_swiglu_experts",
    )(tile_expert, n_used, xs, w_gate, w_up, w_down)


def _pos_kernel(route_ref, starts_ref, pos_ref):
    rt = jnp.transpose(route_ref[...])
    starts = starts_ref[...]
    ids = lax.broadcasted_iota(jnp.int32, (N_EXPERTS, rt.shape[1]), 0).astype(F32)
    for c in range(2):
        start_of = jnp.sum(jnp.where(ids == rt[c:c + 1, :], starts, 0.0), axis=0, keepdims=True)
        pos_ref[c:c + 1, :] = (start_of + rt[4 + c:5 + c, :]).astype(jnp.int32)


def _row_positions(route, starts):
    n_tok = route.shape[0]
    tm = TOKEN_TILE
    return pl.pallas_call(
        _pos_kernel,
        out_shape=jax.ShapeDtypeStruct((2, n_tok), jnp.int32),
        grid=(n_tok // tm,),
        in_specs=[pl.BlockSpec((tm, LANES), lambda i: (i, 0)),
                  pl.BlockSpec((N_EXPERTS, 1), lambda i: (0, 0))],
        out_specs=pl.BlockSpec((2, tm), lambda i: (0, i)),
        compiler_params=pltpu.CompilerParams(
            dimension_semantics=("arbitrary",),
            vmem_limit_bytes=_vmem_limit(tm * LANES * 4 + N_EXPERTS * LANES * 4 + 8 * tm * 4)),
        name="expert_row_positions",
    )(route, starts.astype(F32).reshape(N_EXPERTS, 1))


def _dispatch(route, counts, n_tok):
    tm = EXPERT_TILE
    counts = counts[:N_EXPERTS].astype(jnp.int32)
    padded = ((counts + tm - 1) // tm) * tm
    ends = jnp.cumsum(padded)
    starts = ends - padded
    pos = _row_positions(route, starts).reshape(2 * n_tok)
    p = 2 * n_tok + N_EXPERTS * tm
    tok = jnp.concatenate([jnp.arange(n_tok, dtype=jnp.int32)] * 2)
    row_tok = (jnp.arange(p, dtype=jnp.int32) % n_tok).at[pos].set(
        tok, indices_are_sorted=False, unique_indices=True, mode="promise_in_bounds")
    tile_start = jnp.arange(p // tm, dtype=jnp.int32) * tm
    tile_expert = jnp.minimum(jnp.sum(ends[None, :] <= tile_start[:, None], axis=1),
                              N_EXPERTS - 1).astype(jnp.int32)
    n_used = (ends[-1] // tm).astype(jnp.int32).reshape(1)
    return row_tok, tile_expert, n_used, pos


def _final_kernel(x_ref, y_ref, route_ref, mod_ref, g_ref, o_ref):
    o_ref[0] = _rms(_expert_residual(x_ref[0], y_ref, route_ref, mod_ref), g_ref[...])


def _final_norm(x, y, route, mod, g):
    bsz, s, d = x.shape
    tm = TOKEN_TILE
    tok = lambda b, i: (b, i, 0)
    return pl.pallas_call(
        _final_kernel,
        out_shape=jax.ShapeDtypeStruct((bsz, s, d), F32),
        grid=(bsz, s // tm),
        in_specs=[pl.BlockSpec((1, tm, d), tok),
                  pl.BlockSpec((2, 1, tm, d), lambda b, i: (0, b, i, 0)),
                  pl.BlockSpec((1, tm, LANES), tok),
                  pl.BlockSpec((1, 6, d), lambda b, i: (b, 0, 0)),
                  pl.BlockSpec((1, d), lambda b, i: (0, 0))],
        out_specs=pl.BlockSpec((1, tm, d), tok),
        compiler_params=pltpu.CompilerParams(
            dimension_semantics=("arbitrary", "arbitrary"),
            vmem_limit_bytes=_vmem_limit(4 * tm * d * 4 + tm * LANES * 4)),
        name="residual_final_norm",
    )(x, y, route, mod, g.reshape(1, d))


def kernel(x, c, w_ada, b_ada, norm_mix, norm_ffn, norm_final, rel_bias, even_w_in, even_lambda, even_subln, even_w_gk2, even_b_gk, even_gla_norm, even_w_out, odd_w_in, odd_w_out, router_group_w, router_group_b, router_expert_w, router_expert_b, expert_w_gate, expert_w_up, expert_w_down):
    bsz, s, d = x.shape
    n_tok = bsz * s
    assert s % TOKEN_TILE == 0 and s % GLA_STEP == 0
    assert s % (DIFF_KEY_TILES * ATTN_TILE) == 0 and s % (DIFF_Q_TILES * ATTN_TILE) == 0
    mod_all = _modulation(c, w_ada, b_ada).reshape(DEPTH, bsz, 6, d)
    bias_tiles = _diff_bias_tiles(rel_bias)
    n_even_main = 3072
    y = route = prev_mod = None
    for l in range(DEPTH):
        mod = mod_all[l]
        i = l // 2
        if l % 2 == 0:
            w_in = even_w_in[i]
            w_main = w_in[:, :n_even_main].astype(BF16)
            w_gate = jnp.zeros((d, LANES), BF16).at[:, :GATE_RANK].set(
                w_in[:, n_even_main:].astype(BF16))
            x, proj, gate = _in_projection(x, y, route, prev_mod, norm_mix[l], mod, w_main, w_gate)
            oa = _diff_attention(proj, bias_tiles, even_lambda[i], even_subln[i], l)
            ob = _gla(proj, gate, even_w_gk2[i], even_b_gk[i], even_gla_norm[i])
            w_out = even_w_out[i].astype(BF16)
            o_list = [oa, ob]
            w_list = [w_out[:H_A * DV_A], w_out[H_A * DV_A:]]
        else:
            x, proj, _ = _in_projection(x, y, route, prev_mod, norm_mix[l], mod,
                                        odd_w_in[i].astype(BF16), None)
            o_list = [_stick_breaking(proj)]
            w_list = [odd_w_out[i].astype(BF16)]
        w_router = jnp.zeros((d, LANES), F32)
        w_router = w_router.at[:, :N_EXPERTS].set(router_expert_w[l].astype(F32))
        w_router = w_router.at[:, N_EXPERTS:N_EXPERTS + N_GROUPS].set(router_group_w[l].astype(F32))
        b_router = jnp.zeros((1, LANES), F32)
        b_router = b_router.at[0, :N_EXPERTS].set(router_expert_b[l].astype(F32))
        b_router = b_router.at[0, N_EXPERTS:N_EXPERTS + N_GROUPS].set(router_group_b[l].astype(F32))
        x, h2, route, counts = _out_projection(o_list, w_list, x, mod, norm_ffn[l], w_router,
                                               b_router)
        row_tok, tile_expert, n_used, pos = _dispatch(route.reshape(n_tok, LANES), counts[0], n_tok)
        xs = h2.reshape(n_tok, d).at[row_tok].get(mode="promise_in_bounds")
        out = _grouped_experts(xs, tile_expert, n_used, expert_w_gate, expert_w_up,
                               expert_w_down, l)
        y = out.at[pos].get(mode="promise_in_bounds").reshape(2, bsz, s, d)
        prev_mod = mod
    return _final_norm(x, y, route, prev_mod, norm_final)
```

```python
import functools
import math

import numpy as np
import jax
import jax.numpy as jnp
from jax import lax
from jax.experimental import pallas as pl
from jax.experimental.pallas import tpu as pltpu
from jax.experimental.pallas import tpu_sc as plsc

F32 = jnp.float32
BF16 = jnp.bfloat16

D_MODEL = 1024
DEPTH = 4
CHUNK = 64
EPS = 1e-6
H_A, DH_A, DV_A = 4, 64, 128
H_B, DK_B, DV_B = 4, 64, 128
GATE_RANK = 16
GATE_TAU = 16.0
H_C, DH_C = 16, 64
N_BUCKETS = 32
MAX_DISTANCE = 128
N_GROUPS = 4
EXPERTS_PER_GROUP = 8
N_EXPERTS = N_GROUPS * EXPERTS_PER_GROUP
D_EXPERT = 512
MIX = 1024

LANES = 128
V7X_VMEM_BYTES = 64 * 1024 * 1024
NEG_BIG = -1e30
EXP_ZERO_BELOW = -104.0

TOKEN_TILE = 512
ATTN_TILE = 128
STICK_PAIRS = 2
STICK_AHEAD = 2
DIFF_KEY_TILES = 4
DIFF_Q_TILES = 2
GLA_STEP = 256
EXPERT_TILE = 512
SC_SCATTER_WINDOW = 128
SC_ROW_SPLIT = 2


def _vmem_limit(block_bytes, scratch_bytes=0):
    est = 2 * block_bytes + scratch_bytes + 16 * 1024 * 1024
    return int(min(est, V7X_VMEM_BYTES - 8 * 1024 * 1024))


def _nt(a, b):
    return lax.dot_general(a, b, (((1,), (1,)), ((), ())), preferred_element_type=F32)


def _tn(a, b):
    return lax.dot_general(a, b, (((0,), (0,)), ((), ())), preferred_element_type=F32)


def _dot(a, b):
    return jnp.dot(a, b, preferred_element_type=F32)


def _split(x):
    hi = x.astype(BF16)
    lo = (x - hi.astype(F32)).astype(BF16)
    return hi, lo


def _dot3(a, b):
    ah, al = _split(a)
    bh, bl = _split(b)
    return _dot(ah, bh) + _dot(al, bh) + _dot(ah, bl)


def _rms(x, g):
    return x * lax.rsqrt(jnp.mean(x * x, axis=-1, keepdims=True) + EPS) * g


def _softplus(z):
    return jnp.maximum(z, 0.0) + jnp.log(1.0 + jnp.exp(-jnp.abs(z)))


def _silu(x):
    return x / (1.0 + jnp.exp(-x))


def _mod_kernel(c_ref, w_ref, b_ref, o_ref):
    c = c_ref[...]
    o_ref[0] = _dot3(_silu(c), w_ref[0]) + b_ref[0]


def _modulation(c, w_ada, b_ada):
    depth, d, n = w_ada.shape
    bsz = c.shape[0]
    tn = 1536
    return pl.pallas_call(
        _mod_kernel,
        out_shape=jax.ShapeDtypeStruct((depth, bsz, n), F32),
        grid=(depth, n // tn),
        in_specs=[
            pl.BlockSpec((bsz, d), lambda l, j: (0, 0)),
            pl.BlockSpec((1, d, tn), lambda l, j: (l, 0, j)),
            pl.BlockSpec((1, 1, tn), lambda l, j: (l, 0, j)),
        ],
        out_specs=pl.BlockSpec((1, bsz, tn), lambda l, j: (l, 0, j)),
        compiler_params=pltpu.CompilerParams(
            dimension_semantics=("arbitrary", "arbitrary"),
            vmem_limit_bytes=_vmem_limit(d * tn * 4 + bsz * (d + tn) * 4)),
        name="adaln_modulation",
    )(c, w_ada, b_ada.reshape(depth, 1, n))


def _expert_residual(x, y_ref, route_ref, mod_ref):
    route = route_ref[0]
    y = route[:, 2:3] * y_ref[0, 0].astype(F32) + route[:, 3:4] * y_ref[1, 0].astype(F32)
    return x + mod_ref[0, 5:6, :] * y


def _proj_kernel(*refs, has_res, n_main, has_gate):
    it = iter(refs)
    x_ref = next(it)
    if has_res:
        y_ref = next(it)
        route_ref = next(it)
        pmod_ref = next(it)
    g_ref = next(it)
    mod_ref = next(it)
    w_ref = next(it)
    wg_ref = next(it) if has_gate else None
    xo_ref = next(it) if has_res else None
    o_ref = next(it)
    og_ref = next(it) if has_gate else None

    x = x_ref[0]
    if has_res:
        x = _expert_residual(x, y_ref, route_ref, pmod_ref)
        xo_ref[0] = x
    h = _rms(x, g_ref[...]) * (1.0 + mod_ref[0, 1:2, :]) + mod_ref[0, 0:1, :]
    hb = h.astype(BF16)
    step = 512
    for n0 in range(0, n_main, step):
        o_ref[0, :, n0:n0 + step] = _dot(hb, w_ref[:, n0:n0 + step]).astype(BF16)
    if has_gate:
        og_ref[0] = _dot(hb, wg_ref[...])


def _in_projection(x, y, route, prev_mod, g, mod, w, w_gate):
    bsz, s, d = x.shape
    tm = TOKEN_TILE
    has_res = y is not None
    has_gate = w_gate is not None
    n_main = w.shape[1]
    tok = lambda b, i: (b, i, 0)
    per_b = lambda b, i: (b, 0, 0)
    const2 = lambda b, i: (0, 0)
    in_specs = [pl.BlockSpec((1, tm, d), tok)]
    args = [x]
    if has_res:
        in_specs += [pl.BlockSpec((2, 1, tm, d), lambda b, i: (0, b, i, 0)),
                     pl.BlockSpec((1, tm, LANES), tok), pl.BlockSpec((1, 6, d), per_b)]
        args += [y, route, prev_mod]
    in_specs += [pl.BlockSpec((1, d), const2), pl.BlockSpec((1, 6, d), per_b),
                 pl.BlockSpec((d, n_main), const2)]
    args += [g.reshape(1, d), mod, w]
    if has_gate:
        in_specs.append(pl.BlockSpec((d, LANES), const2))
        args.append(w_gate)
    out_shape, out_specs = [], []
    if has_res:
        out_shape.append(jax.ShapeDtypeStruct((bsz, s, d), F32))
        out_specs.append(pl.BlockSpec((1, tm, d), tok))
    out_shape.append(jax.ShapeDtypeStruct((bsz, s, n_main), BF16))
    out_specs.append(pl.BlockSpec((1, tm, n_main), tok))
    if has_gate:
        out_shape.append(jax.ShapeDtypeStruct((bsz, s, LANES), F32))
        out_specs.append(pl.BlockSpec((1, tm, LANES), tok))
    block_bytes = (tm * d * 4 * (4 if has_res else 1) + d * n_main * 2 + tm * n_main * 2
                   + d * LANES * 2 + 2 * tm * LANES * 4)
    outs = pl.pallas_call(
        functools.partial(_proj_kernel, has_res=has_res, n_main=n_main, has_gate=has_gate),
        out_shape=out_shape,
        grid=(bsz, s // tm),
        in_specs=in_specs,
        out_specs=out_specs,
        compiler_params=pltpu.CompilerParams(
            dimension_semantics=("arbitrary", "arbitrary"),
            vmem_limit_bytes=_vmem_limit(block_bytes, tm * d * 8)),
        name="norm_mod_in_projection",
    )(*args)
    outs = list(outs)
    x_new = outs.pop(0) if has_res else x
    proj = outs.pop(0)
    gate = outs.pop(0) if has_gate else None
    return x_new, proj, gate


def _t5_bucket(rel):
    nb = N_BUCKETS // 2
    max_exact = nb // 2
    ret = jnp.where(rel > 0, nb, 0)
    n = jnp.abs(rel)
    nf = jnp.maximum(n, 1).astype(F32)
    large = max_exact + (jnp.log(nf / max_exact) / math.log(MAX_DISTANCE / max_exact)
                         * (nb - max_exact)).astype(jnp.int32)
    large = jnp.minimum(large, nb - 1)
    return ret + jnp.where(n < max_exact, n, large)


def _diff_bias_tiles(rel_bias):
    t = ATTN_TILE
    assert t >= MAX_DISTANCE, "tiles two or more away must lie beyond the last bucket edge"
    qp = jnp.arange(t)[:, None]
    kp = jnp.arange(t)[None, :]
    tiles = []
    for off in (0, 1, 2):
        rel = (kp - off * t) - qp
        hit = _t5_bucket(rel)[None, :, :, None] == jnp.arange(N_BUCKETS)[None, None, None, :]
        b = jnp.sum(jnp.where(hit, jnp.transpose(rel_bias.astype(F32))[:, None, None, :], 0.0),
                    axis=-1)
        if off == 0:
            b = jnp.where((kp // CHUNK) <= (qp // CHUNK), b, NEG_BIG)
        tiles.append(b)
    tiles.append(jnp.full_like(tiles[0], NEG_BIG))
    tab = jnp.stack(tiles, axis=1)
    return jnp.concatenate([tab, tab], axis=2)


def _diff_attn_kernel(q_ref, k_ref, v_ref, bias_ref, lam_ref, g_ref, o_ref, *, lambda_init):
    t = ATTN_TILE
    chains = DIFF_Q_TILES
    i = pl.program_id(2)
    lane = lax.broadcasted_iota(jnp.int32, (t, LANES), 1)
    scale = jnp.asarray(DH_A ** -0.5, BF16)
    stacked = []
    for c in range(chains):
        q = q_ref[0, c * t:(c + 1) * t, :]
        zero = jnp.zeros_like(q)
        stacked += [jnp.where(lane < DH_A, q, zero), jnp.where(lane >= DH_A, q, zero)]
    qq = jnp.concatenate(stacked, axis=0) * scale

    group = DIFF_KEY_TILES
    span = group * t
    last_tile = i * chains + chains - 1

    def body(n, carry):
        r0 = pl.multiple_of(n * span, span)
        kb = k_ref[0, pl.ds(r0, span), :]
        vb = v_ref[0, pl.ds(r0, span), :]
        s_chain = [_nt(qq[c * 2 * t:(c + 1) * 2 * t], kb) for c in range(chains)]
        out = []
        for c in range(chains):
            m, l, acc = carry[c]

            def logits(u):
                back = (i * chains + c) - (n * group + u)
                which = jnp.where(back < 0, 3, jnp.minimum(back, 2))
                return s_chain[c][:, u * t:(u + 1) * t] + bias_ref[0, which]

            top = logits(0)
            for u in range(1, group):
                top = jnp.maximum(top, logits(u))
            m_new = jnp.maximum(m, jnp.max(top, axis=-1, keepdims=True))
            alpha = jnp.exp(m - m_new)
            ps = [jnp.exp(logits(u) - m_new) for u in range(group)]
            tot = ps[0]
            for u in range(1, group):
                tot = tot + ps[u]
            l = alpha * l + jnp.sum(tot, axis=-1, keepdims=True)
            p = jnp.concatenate([pu.astype(BF16) for pu in ps], axis=1)
            acc = alpha * acc + _dot(p, vb)
            out.append((m_new, l, acc))
        return tuple(out)

    init = tuple((jnp.full((2 * t, 1), NEG_BIG, F32), jnp.zeros((2 * t, 1), F32),
                  jnp.zeros((2 * t, DV_A), F32)) for _ in range(chains))
    final = lax.fori_loop(0, last_tile // group + 1, body, init)
    lp = lam_ref[...].astype(F32)
    lam = (jnp.exp(jnp.sum(lp[0:1] * lp[1:2], axis=-1, keepdims=True))
           - jnp.exp(jnp.sum(lp[2:3] * lp[3:4], axis=-1, keepdims=True)) + lambda_init)
    for c in range(chains):
        _, l, acc = final[c]
        o = acc / l
        w = o[:t] - lam * o[t:]
        o_ref[0, c * t:(c + 1) * t, :] = (_rms(w, g_ref[...]) * (1.0 - lambda_init)
                                          ).astype(o_ref.dtype)


def _diff_attention(proj, bias_tiles, lam_p, subln_g, layer_idx):
    bsz, s, _ = proj.shape
    t = ATTN_TILE
    tq = DIFF_Q_TILES * t
    lambda_init = 0.8 - 0.6 * math.exp(-0.3 * layer_idx)
    return pl.pallas_call(
        functools.partial(_diff_attn_kernel, lambda_init=lambda_init),
        out_shape=jax.ShapeDtypeStruct((bsz, s, H_A * DV_A), BF16),
        grid=(bsz, H_A, s // tq),
        in_specs=[
            pl.BlockSpec((1, tq, LANES), lambda b, h, i: (b, i, h)),
            pl.BlockSpec((1, s, LANES), lambda b, h, i: (b, 0, H_A + h)),
            pl.BlockSpec((1, s, LANES), lambda b, h, i: (b, 0, 2 * H_A + h)),
            pl.BlockSpec((1, 4, 2 * t, t), lambda b, h, i: (h, 0, 0, 0)),
            pl.BlockSpec((4, DH_A), lambda b, h, i: (0, 0)),
            pl.BlockSpec((1, DV_A), lambda b, h, i: (0, 0)),
        ],
        out_specs=pl.BlockSpec((1, tq, DV_A), lambda b, h, i: (b, i, h)),
        compiler_params=pltpu.CompilerParams(
            dimension_semantics=("arbitrary", "arbitrary", "arbitrary"),
            vmem_limit_bytes=_vmem_limit(2 * s * LANES * 2 + 4 * 2 * t * t * 4 + 4 * tq * LANES,
                                         4 * tq * DIFF_KEY_TILES * t * 4)),
        name="diff_attention",
    )(proj, proj, proj, bias_tiles, lam_p, subln_g.reshape(1, DV_A))


def _gla_kernel(q_ref, k_ref, v_ref, r_ref, bg_ref, wgk_ref, bgk_ref, g_ref, o_ref, st_ref):
    c = CHUNK
    ts = q_ref.shape[1]
    n_chunks = ts // c

    @pl.when(pl.program_id(1) == 0)
    def _():
        st_ref[...] = jnp.zeros_like(st_ref)

    row = lax.broadcasted_iota(jnp.int32, (ts, ts), 0)
    col = lax.broadcasted_iota(jnp.int32, (ts, ts), 1)
    same_chunk_before = ((row // c) == (col // c)) & (col <= row)
    tri = jnp.where(same_chunk_before, 1.0, 0.0).astype(BF16)
    causal = (lax.broadcasted_iota(jnp.int32, (c, c), 1)
              <= lax.broadcasted_iota(jnp.int32, (c, c), 0))
    lane = lax.broadcasted_iota(jnp.int32, (ts, LANES), 1)
    wgk = wgk_ref[...]
    bgk = bgk_ref[...]
    g = g_ref[...]
    bg = bg_ref[0]

    las = [-_softplus(-(_dot3(bg, wgk[:, hp * LANES:(hp + 1) * LANES])
                        + bgk[:, hp * LANES:(hp + 1) * LANES])) / GATE_TAU
           for hp in range(H_B // 2)]
    bs = []
    for la in las:
        la_hi, la_lo = _split(la)
        bs.append(_dot(tri, la_hi) + _dot(tri, la_lo))

    heads = []
    for hp in range(H_B // 2):
        cols = slice(hp * LANES, (hp + 1) * LANES)
        b = bs[hp]
        b_last = jnp.concatenate(
            [jnp.broadcast_to(b[(ci + 1) * c - 1:(ci + 1) * c, :], (c, LANES))
             for ci in range(n_chunks)], axis=0)
        qf = q_ref[0, :, cols].astype(F32) * (DK_B ** -0.5)
        kf = k_ref[0, :, cols].astype(F32)
        q_dec = qf * jnp.exp(b)
        k_inv = (kf * jnp.exp(-b)).astype(BF16)
        k_dec = kf * jnp.exp(b_last - b)
        decays = [jnp.exp(b[(ci + 1) * c - 1:(ci + 1) * c, :]) for ci in range(n_chunks)]
        for hh in range(2):
            own = (lane >= hh * DK_B) & (lane < (hh + 1) * DK_B)
            heads.append((jnp.where(own, q_dec, 0.0).astype(BF16), k_inv,
                          jnp.where(own, k_dec, 0.0).astype(BF16), decays))

    def rows(x, ci):
        return x[ci * c:(ci + 1) * c]

    scores, increments = [], []
    for h, (qd, k_inv, kd, _) in enumerate(heads):
        vh = v_ref[0, :, h * DV_B:(h + 1) * DV_B]
        scores.append([jnp.where(causal, _nt(rows(qd, ci), rows(k_inv, ci)), 0.0).astype(BF16)
                       for ci in range(n_chunks)])
        increments.append([_tn(rows(vh, ci), rows(kd, ci)) for ci in range(n_chunks)])

    for h, (qd, _, _, decays) in enumerate(heads):
        vh = v_ref[0, :, h * DV_B:(h + 1) * DV_B]
        st = st_ref[h]
        states = []
        for ci in range(n_chunks):
            states.append(st.astype(BF16))
            st = st * decays[ci] + increments[h][ci]
        st_ref[h] = st
        o = jnp.concatenate(
            [_dot(scores[h][ci], rows(vh, ci)) + _nt(rows(qd, ci), states[ci])
             for ci in range(n_chunks)], axis=0)
        rr = r_ref[0, :, h * DV_B:(h + 1) * DV_B].astype(F32)
        o_ref[0, :, h * DV_B:(h + 1) * DV_B] = (_rms(o, g) * _silu(rr)).astype(o_ref.dtype)


def _gla(proj, gate, w_gk2, b_gk, gla_g):
    bsz, s, _ = proj.shape
    ts = GLA_STEP
    nk = H_B * DK_B
    nv = H_B * DV_B
    w_pad = jnp.zeros((LANES, nk), F32).at[:GATE_RANK].set(w_gk2.astype(F32))
    return pl.pallas_call(
        _gla_kernel,
        out_shape=jax.ShapeDtypeStruct((bsz, s, nv), BF16),
        grid=(bsz, s // ts),
        in_specs=[
            pl.BlockSpec((1, ts, nk), lambda b, i: (b, i, 1536 // nk)),
            pl.BlockSpec((1, ts, nk), lambda b, i: (b, i, 1792 // nk)),
            pl.BlockSpec((1, ts, nv), lambda b, i: (b, i, 2048 // nv)),
            pl.BlockSpec((1, ts, nv), lambda b, i: (b, i, 2560 // nv)),
            pl.BlockSpec((1, ts, LANES), lambda b, i: (b, i, 0)),
            pl.BlockSpec((LANES, nk), lambda b, i: (0, 0)),
            pl.BlockSpec((1, nk), lambda b, i: (0, 0)),
            pl.BlockSpec((1, DV_B), lambda b, i: (0, 0)),
        ],
        out_specs=pl.BlockSpec((1, ts, nv), lambda b, i: (b, i, 0)),
        scratch_shapes=[pltpu.VMEM((H_B, DV_B, LANES), F32)],
        compiler_params=pltpu.CompilerParams(
            dimension_semantics=("arbitrary", "arbitrary"),
            vmem_limit_bytes=_vmem_limit(ts * (2 * nk + 3 * nv) * 2 + ts * LANES * 4,
                                         H_B * DV_B * LANES * 4)),
        name="gated_linear_attention",
    )(proj, proj, proj, proj, gate, w_pad, b_gk.reshape(1, nk), gla_g.reshape(1, DV_B))


def _stick_kernel(q_ref, k_ref, v_ref, o_ref, run_ref, acc_ref):
    t = ATTN_TILE
    chains = STICK_PAIRS
    i = pl.program_id(2)
    lane = lax.broadcasted_iota(jnp.int32, (t, LANES), 1)
    scale = jnp.asarray(DH_C ** -0.5, BF16)
    qqs = []
    for c in range(chains):
        q = q_ref[0, :, c * LANES:(c + 1) * LANES]
        zero = jnp.zeros_like(q)
        qqs.append(jnp.concatenate([jnp.where(lane < DH_C, q, zero),
                                    jnp.where(lane >= DH_C, q, zero)], axis=0) * scale)
    row = lax.broadcasted_iota(jnp.int32, (t, t), 0)
    col = lax.broadcasted_iota(jnp.int32, (t, t), 1)
    later = jnp.where(row > col, 1.0, 0.0).astype(BF16)
    below = (lax.broadcasted_iota(jnp.int32, (2 * t, t), 1)
             < lax.broadcasted_iota(jnp.int32, (2 * t, t), 0) % t)

    def logits(j):
        r0 = pl.multiple_of(jnp.maximum(j, 0) * t, t)
        return [_nt(qqs[c], k_ref[0, pl.ds(r0, t), c * LANES:(c + 1) * LANES])
                for c in range(chains)]

    def tiles(first, zs, runs, accs, diagonal_first):
        runs, accs = list(runs), list(accs)
        mids = []
        for u in range(len(zs)):
            for c in range(chains):
                z = zs[u][c]
                sp = _softplus(z)
                log_1m = -sp
                masked = diagonal_first and u == 0
                if masked:
                    log_1m = jnp.where(below, log_1m, 0.0)
                hi, lo = _split(log_1m)
                within = _dot(hi, later) + _dot(lo, later)
                mids.append((u, c, z - sp, log_1m, within, masked))
        for u, c, log_b, log_1m, within, masked in mids:
            r0 = pl.multiple_of((first - u) * t, t)
            vb = v_ref[0, pl.ds(r0, t), c * LANES:(c + 1) * LANES]
            a = jnp.exp(log_b + (within + runs[c]))
            if masked:
                a = jnp.where(below, a, 0.0)
            ab = a.astype(BF16)
            zero_v = jnp.zeros_like(vb)
            a2 = jnp.concatenate([ab[:t], ab[t:]], axis=1)
            v2 = jnp.concatenate([jnp.where(lane < DH_C, vb, zero_v),
                                  jnp.where(lane >= DH_C, vb, zero_v)], axis=0)
            accs[c] = accs[c] + _dot(a2, v2)
            runs[c] = runs[c] + jnp.sum(log_1m, axis=-1, keepdims=True)
        return runs, accs

    def first_block(n_tiles):
        zs = [logits(i - u) for u in range(n_tiles)]
        runs, accs = tiles(i, zs, [jnp.zeros((2 * t, 1), F32)] * chains,
                           [jnp.zeros((t, LANES), F32)] * chains, True)
        for c in range(chains):
            run_ref[c] = runs[c]
            acc_ref[c] = accs[c]

    ahead = STICK_AHEAD

    @pl.when(i >= ahead)
    def _():
        first_block(1 + ahead)

    @pl.when(i < ahead)
    def _():
        first_block(1)

    n_left = i
    done0 = jnp.where(i >= ahead, ahead, 0)

    def go_on(done, runs):
        top = runs[0]
        for c in range(1, chains):
            top = jnp.maximum(top, runs[c])
        return ((done < n_left) & (jnp.max(top) > EXP_ZERO_BELOW)).astype(jnp.int32)

    def body(carry):
        done, _, runs, accs = carry
        first = n_left - 1 - done
        runs, accs = tiles(first, [logits(first)], runs, accs, False)
        done = done + 1
        return done, go_on(done, runs), tuple(runs), tuple(accs)

    runs = tuple(run_ref[c] for c in range(chains))
    accs = tuple(acc_ref[c] for c in range(chains))
    carry = (done0, go_on(done0, runs), runs, accs)
    accs = lax.while_loop(lambda carry: carry[1] > 0, body, carry)[3]
    for c in range(chains):
        o_ref[0, :, c * LANES:(c + 1) * LANES] = accs[c].astype(o_ref.dtype)


def _stick_breaking(proj):
    bsz, s, _ = proj.shape
    t = ATTN_TILE
    width = STICK_PAIRS * LANES
    n_blocks = H_C * DH_C // width
    return pl.pallas_call(
        _stick_kernel,
        out_shape=jax.ShapeDtypeStruct((bsz, s, H_C * DH_C), BF16),
        grid=(bsz, n_blocks, s // t),
        in_specs=[
            pl.BlockSpec((1, t, width), lambda b, h, i: (b, i, h)),
            pl.BlockSpec((1, s, width), lambda b, h, i: (b, 0, n_blocks + h)),
            pl.BlockSpec((1, s, width), lambda b, h, i: (b, 0, 2 * n_blocks + h)),
        ],
        out_specs=pl.BlockSpec((1, t, width), lambda b, h, i: (b, i, h)),
        scratch_shapes=[pltpu.VMEM((STICK_PAIRS, 2 * t, 1), F32),
                        pltpu.VMEM((STICK_PAIRS, t, LANES), F32)],
        compiler_params=pltpu.CompilerParams(
            dimension_semantics=("arbitrary", "arbitrary", "arbitrary"),
            vmem_limit_bytes=_vmem_limit(2 * s * width * 2 + 4 * t * width,
                                         STICK_PAIRS * 3 * t * LANES * 4)),
        name="stick_breaking_attention",
    )(proj, proj, proj)


def _route(logits):
    lane_i = lax.broadcasted_iota(jnp.int32, logits.shape, 1)
    lane = lane_i.astype(F32)
    group_of_lane = (lane_i // EXPERTS_PER_GROUP).astype(F32)
    big = jnp.float32(1 << 20)
    neg = jnp.float32(-jnp.inf)
    is_group = (lane_i >= N_EXPERTS) & (lane_i < N_EXPERTS + N_GROUPS)
    gl = jnp.where(is_group, logits, neg)
    gmax = jnp.max(gl, axis=-1, keepdims=True)
    g_val = 1.0 / jnp.sum(jnp.exp(gl - gmax), axis=-1, keepdims=True)
    g_sel = jnp.min(jnp.where(gl == gmax, lane - N_EXPERTS, big), axis=-1, keepdims=True)
    in_group = (lane_i < N_EXPERTS) & (group_of_lane == g_sel)
    el = jnp.where(in_group, logits, neg)
    emax = jnp.max(el, axis=-1, keepdims=True)
    esum = jnp.sum(jnp.exp(el - emax), axis=-1, keepdims=True)
    i1 = jnp.min(jnp.where(el == emax, lane, big), axis=-1, keepdims=True)
    el2 = jnp.where(lane == i1, neg, el)
    emax2 = jnp.max(el2, axis=-1, keepdims=True)
    i2 = jnp.min(jnp.where(el2 == emax2, lane, big), axis=-1, keepdims=True)
    v1 = 1.0 / esum
    v2 = jnp.exp(emax2 - emax) / esum
    tot = v1 + v2
    w1 = g_val * (v1 / tot)
    w2 = g_val * (v2 / tot)
    out = jnp.where(lane_i == 0, i1, 0.0)
    out = jnp.where(lane_i == 1, i2, out)
    out = jnp.where(lane_i == 2, w1, out)
    out = jnp.where(lane_i == 3, w2, out)
    return out, (lane == i1), (lane == i2)


def _pack_halves(h):
    half = h.shape[1] // 2
    lo = pltpu.bitcast(h[:, :half].astype(BF16).astype(F32), jnp.uint32)
    hi = pltpu.bitcast(h[:, half:].astype(BF16).astype(F32), jnp.uint32)
    return lax.shift_right_logical(lo, jnp.uint32(16)) | hi


def _unpack_halves(p):
    lo = pltpu.bitcast(lax.shift_left(p, jnp.uint32(16)), F32).astype(BF16)
    hi = pltpu.bitcast(p & jnp.uint32(0xFFFF0000), F32).astype(BF16)
    return lo, hi


def _out_kernel(*refs, n_o):
    o_refs = refs[:n_o]
    w_refs = refs[n_o:2 * n_o]
    x_ref, mod_ref, g_ref, wr_ref, br_ref, xo_ref, h_ref, r_ref, cnt_ref, run_ref = refs[2 * n_o:]

    @pl.when((pl.program_id(0) == 0) & (pl.program_id(1) == 0))
    def _():
        run_ref[...] = jnp.zeros_like(run_ref)

    m = _dot(o_refs[0][0], w_refs[0][...])
    for a in range(1, n_o):
        m = m + _dot(o_refs[a][0], w_refs[a][...])
    x = x_ref[0] + mod_ref[0, 2:3, :] * m
    xo_ref[0] = x
    h = _rms(x, g_ref[...]) * (1.0 + mod_ref[0, 4:5, :]) + mod_ref[0, 3:4, :]
    h_ref[0] = _pack_halves(h)
    route, hit1, hit2 = _route(_dot3(h, wr_ref[...]) + br_ref[...])
    tm = route.shape[0]
    picks = jnp.where(hit1 | hit2, 1.0, 0.0)
    row = lax.broadcasted_iota(jnp.int32, (tm, tm), 0)
    col = lax.broadcasted_iota(jnp.int32, (tm, tm), 1)
    before = jnp.where(col < row, 1.0, 0.0).astype(BF16)
    earlier = _dot(before, picks.astype(BF16)) + run_ref[...]
    rank1 = jnp.sum(jnp.where(hit1, earlier, 0.0), axis=-1, keepdims=True)
    rank2 = jnp.sum(jnp.where(hit2, earlier, 0.0), axis=-1, keepdims=True)
    lane_i = lax.broadcasted_iota(jnp.int32, route.shape, 1)
    route = jnp.where(lane_i == 4, rank1, route)
    route = jnp.where(lane_i == 5, rank2, route)
    r_ref[0] = route
    run_ref[...] = run_ref[...] + jnp.sum(picks, axis=0, keepdims=True)
    cnt_ref[...] = run_ref[...]


def _out_projection(o_list, w_list, x, mod, g, w_router, b_router):
    bsz, s, d = x.shape
    tm = TOKEN_TILE
    n_o = len(o_list)
    tok = lambda b, i: (b, i, 0)
    const2 = lambda b, i: (0, 0)
    in_specs = ([pl.BlockSpec((1, tm, o.shape[2]), tok) for o in o_list]
                + [pl.BlockSpec(w.shape, const2) for w in w_list]
                + [pl.BlockSpec((1, tm, d), tok), pl.BlockSpec((1, 6, d), lambda b, i: (b, 0, 0)),
                   pl.BlockSpec((1, d), const2), pl.BlockSpec((d, LANES), const2),
                   pl.BlockSpec((1, LANES), const2)])
    block_bytes = (sum(tm * o.shape[2] * 2 for o in o_list) + sum(w.size * 2 for w in w_list)
                   + 2 * tm * d * 4 + tm * d * 2 + d * LANES * 4 + tm * LANES * 4)
    return pl.pallas_call(
        functools.partial(_out_kernel, n_o=n_o),
        out_shape=[jax.ShapeDtypeStruct((bsz, s, d), F32),
                   jax.ShapeDtypeStruct((bsz, s, d // 2), jnp.uint32),
                   jax.ShapeDtypeStruct((bsz, s, LANES), F32),
                   jax.ShapeDtypeStruct((1, LANES), F32)],
        grid=(bsz, s // tm),
        in_specs=in_specs,
        out_specs=[pl.BlockSpec((1, tm, d), tok), pl.BlockSpec((1, tm, d // 2), tok),
                   pl.BlockSpec((1, tm, LANES), tok), pl.BlockSpec((1, LANES), const2)],
        scratch_shapes=[pltpu.VMEM((1, LANES), F32)],
        compiler_params=pltpu.CompilerParams(
            dimension_semantics=("arbitrary", "arbitrary"),
            vmem_limit_bytes=_vmem_limit(block_bytes, tm * d * 12 + tm * tm * 4)),
        name="out_projection_norm_router",
    )(*o_list, *w_list, x, mod, g.reshape(1, d), w_router, b_router)


def _expert_kernel(te_ref, nu_ref, x_ref, wg_ref, wu_ref, wd_ref, o_ref, wgb_ref, wub_ref, wdb_ref):
    t = pl.program_id(0)
    prev = te_ref[jnp.maximum(t - 1, 0)]

    @pl.when((t == 0) | (te_ref[t] != prev))
    def _():
        wgb_ref[...] = wg_ref[0, 0].astype(BF16)
        wub_ref[...] = wu_ref[0, 0].astype(BF16)
        wdb_ref[...] = wd_ref[0, 0].astype(BF16)

    @pl.when(t < nu_ref[0])
    def _():
        lo, hi = _unpack_halves(x_ref[...])
        half = lo.shape[1]
        gt = _dot(lo, wgb_ref[:half, :]) + _dot(hi, wgb_ref[half:, :])
        up = _dot(lo, wub_ref[:half, :]) + _dot(hi, wub_ref[half:, :])
        o_ref[...] = _dot((_silu(gt) * up).astype(BF16), wdb_ref[...]).astype(o_ref.dtype)

    @pl.when(t >= nu_ref[0])
    def _():
        o_ref[...] = jnp.zeros_like(o_ref)


def _grouped_experts(xs, tile_expert, n_used, w_gate, w_up, w_down, layer):
    p = xs.shape[0]
    d = w_gate.shape[2]
    tm = EXPERT_TILE
    f = w_gate.shape[3]
    grid_spec = pltpu.PrefetchScalarGridSpec(
        num_scalar_prefetch=2,
        grid=(p // tm,),
        in_specs=[
            pl.BlockSpec((tm, d // 2), lambda t, te, nu: (t, 0)),
            pl.BlockSpec((1, 1, d, f), lambda t, te, nu: (layer, te[t], 0, 0)),
            pl.BlockSpec((1, 1, d, f), lambda t, te, nu: (layer, te[t], 0, 0)),
            pl.BlockSpec((1, 1, f, d), lambda t, te, nu: (layer, te[t], 0, 0)),
        ],
        out_specs=pl.BlockSpec((tm, d), lambda t, te, nu: (t, 0)),
        scratch_shapes=[pltpu.VMEM((d, f), BF16), pltpu.VMEM((d, f), BF16),
                        pltpu.VMEM((f, d), BF16)],
    )
    block_bytes = tm * d * 2 + 3 * d * f * 4 + tm * d * 2
    return pl.pallas_call(
        _expert_kernel,
        out_shape=jax.ShapeDtypeStruct((p, d), BF16),
        grid_spec=grid_spec,
        compiler_params=pltpu.CompilerParams(
            dimension_semantics=("arbitrary",),
            vmem_limit_bytes=_vmem_limit(block_bytes, 3 * d * f * 2 + tm * f * 16)),
        name="grouped_swiglu_experts",
    )(tile_expert, n_used, xs, w_gate, w_up, w_down)


def _pos_kernel(route_ref, starts_ref, pos_ref):
    rt = jnp.transpose(route_ref[...])
    starts = starts_ref[...]
    ids = lax.broadcasted_iota(jnp.int32, (N_EXPERTS, rt.shape[1]), 0).astype(F32)
    for c in range(2):
        start_of = jnp.sum(jnp.where(ids == rt[c:c + 1, :], starts, 0.0), axis=0, keepdims=True)
        pos_ref[c:c + 1, :] = (start_of + rt[4 + c:5 + c, :]).astype(jnp.int32)


def _row_positions(route, starts):
    n_tok = route.shape[0]
    tm = TOKEN_TILE
    return pl.pallas_call(
        _pos_kernel,
        out_shape=jax.ShapeDtypeStruct((2, n_tok), jnp.int32),
        grid=(n_tok // tm,),
        in_specs=[pl.BlockSpec((tm, LANES), lambda i: (i, 0)),
                  pl.BlockSpec((N_EXPERTS, 1), lambda i: (0, 0))],
        out_specs=pl.BlockSpec((2, tm), lambda i: (0, i)),
        compiler_params=pltpu.CompilerParams(
            dimension_semantics=("arbitrary",),
            vmem_limit_bytes=_vmem_limit(tm * LANES * 4 + N_EXPERTS * LANES * 4 + 8 * tm * 4)),
        name="expert_row_positions",
    )(route, starts.astype(F32).reshape(N_EXPERTS, 1))


def _scatter_rows(rows, pos, n_out):
    split = SC_ROW_SPLIT
    n_tok, d = rows.shape[0] * split, rows.shape[1] // split
    rows = rows.reshape(n_tok, d)
    pos = (pos[:, None] * split + jnp.arange(split, dtype=pos.dtype)[None, :]).reshape(-1)
    n_out = n_out * split
    n_idx = pos.shape[0]
    w = SC_SCATTER_WINDOW
    per_pass = n_tok // w
    mesh = plsc.VectorSubcoreMesh(core_axis_name="core", subcore_axis_name="subcore")

    @pl.kernel(out_type=jax.ShapeDtypeStruct((n_out, d), rows.dtype), mesh=mesh, scratch_types=[])
    def scatter_kernel(x_hbm, i_hbm, o_hbm):
        def body(x_vmem, i_vmem):
            pltpu.sync_copy(x_vmem, o_hbm.at[i_vmem.at[0]])

        pltpu.emit_pipeline(
            body,
            grid=(n_idx // w,),
            in_specs=[pl.BlockSpec((w, d), lambda i: (i % per_pass, 0)),
                      pl.BlockSpec((1, w), lambda i: (0, i))],
            out_specs=[],
            core_axis_name=("core", "subcore"),
            dimension_semantics=(pltpu.PARALLEL,),
        )(x_hbm, i_hbm)

    return scatter_kernel(rows, pos.reshape(1, n_idx)).reshape(n_out // split, d * split)


def _dispatch(route, counts, n_tok):
    tm = EXPERT_TILE
    counts = counts[:N_EXPERTS].astype(jnp.int32)
    padded = ((counts + tm - 1) // tm) * tm
    ends = jnp.cumsum(padded)
    starts = ends - padded
    pos = _row_positions(route, starts).reshape(2 * n_tok)
    p = 2 * n_tok + N_EXPERTS * tm
    tile_start = jnp.arange(p // tm, dtype=jnp.int32) * tm
    tile_expert = jnp.minimum(jnp.sum(ends[None, :] <= tile_start[:, None], axis=1),
                              N_EXPERTS - 1).astype(jnp.int32)
    n_used = (ends[-1] // tm).astype(jnp.int32).reshape(1)
    return p, tile_expert, n_used, pos


def _final_kernel(x_ref, y_ref, route_ref, mod_ref, g_ref, o_ref):
    o_ref[0] = _rms(_expert_residual(x_ref[0], y_ref, route_ref, mod_ref), g_ref[...])


def _final_norm(x, y, route, mod, g):
    bsz, s, d = x.shape
    tm = TOKEN_TILE
    tok = lambda b, i: (b, i, 0)
    return pl.pallas_call(
        _final_kernel,
        out_shape=jax.ShapeDtypeStruct((bsz, s, d), F32),
        grid=(bsz, s // tm),
        in_specs=[pl.BlockSpec((1, tm, d), tok),
                  pl.BlockSpec((2, 1, tm, d), lambda b, i: (0, b, i, 0)),
                  pl.BlockSpec((1, tm, LANES), tok),
                  pl.BlockSpec((1, 6, d), lambda b, i: (b, 0, 0)),
                  pl.BlockSpec((1, d), lambda b, i: (0, 0))],
        out_specs=pl.BlockSpec((1, tm, d), tok),
        compiler_params=pltpu.CompilerParams(
            dimension_semantics=("arbitrary", "arbitrary"),
            vmem_limit_bytes=_vmem_limit(4 * tm * d * 4 + tm * LANES * 4)),
        name="residual_final_norm",
    )(x, y, route, mod, g.reshape(1, d))


def kernel(x, c, w_ada, b_ada, norm_mix, norm_ffn, norm_final, rel_bias, even_w_in, even_lambda, even_subln, even_w_gk2, even_b_gk, even_gla_norm, even_w_out, odd_w_in, odd_w_out, router_group_w, router_group_b, router_expert_w, router_expert_b, expert_w_gate, expert_w_up, expert_w_down):
    bsz, s, d = x.shape
    n_tok = bsz * s
    assert s % TOKEN_TILE == 0 and s % GLA_STEP == 0
    assert s % (DIFF_KEY_TILES * ATTN_TILE) == 0 and s % (DIFF_Q_TILES * ATTN_TILE) == 0
    mod_all = _modulation(c, w_ada, b_ada).reshape(DEPTH, bsz, 6, d)
    bias_tiles = _diff_bias_tiles(rel_bias)
    n_even_main = 3072
    y = route = prev_mod = None
    for l in range(DEPTH):
        mod = mod_all[l]
        i = l // 2
        if l % 2 == 0:
            w_in = even_w_in[i]
            w_main = w_in[:, :n_even_main].astype(BF16)
            w_gate = jnp.zeros((d, LANES), BF16).at[:, :GATE_RANK].set(
                w_in[:, n_even_main:].astype(BF16))
            x, proj, gate = _in_projection(x, y, route, prev_mod, norm_mix[l], mod, w_main, w_gate)
            oa = _diff_attention(proj, bias_tiles, even_lambda[i], even_subln[i], l)
            ob = _gla(proj, gate, even_w_gk2[i], even_b_gk[i], even_gla_norm[i])
            w_out = even_w_out[i].astype(BF16)
            o_list = [oa, ob]
            w_list = [w_out[:H_A * DV_A], w_out[H_A * DV_A:]]
        else:
            x, proj, _ = _in_projection(x, y, route, prev_mod, norm_mix[l], mod,
                                        odd_w_in[i].astype(BF16), None)
            o_list = [_stick_breaking(proj)]
            w_list = [odd_w_out[i].astype(BF16)]
        w_router = jnp.zeros((d, LANES), F32)
        w_router = w_router.at[:, :N_EXPERTS].set(router_expert_w[l].astype(F32))
        w_router = w_router.at[:, N_EXPERTS:N_EXPERTS + N_GROUPS].set(router_group_w[l].astype(F32))
        b_router = jnp.zeros((1, LANES), F32)
        b_router = b_router.at[0, :N_EXPERTS].set(router_expert_b[l].astype(F32))
        b_router = b_router.at[0, N_EXPERTS:N_EXPERTS + N_GROUPS].set(router_group_b[l].astype(F32))
        x, h2, route, counts = _out_projection(o_list, w_list, x, mod, norm_ffn[l], w_router,
                                               b_router)
        n_rows, tile_expert, n_used, pos = _dispatch(route.reshape(n_tok, LANES), counts[0], n_tok)
        xs = _scatter_rows(h2.reshape(n_tok, d // 2), pos, n_rows)
        out = _grouped_experts(xs, tile_expert, n_used, expert_w_gate, expert_w_up,
                               expert_w_down, l)
        y = out.at[pos].get(mode="promise_in_bounds").reshape(2, bsz, s, d)
        prev_mod = mod
    return _final_norm(x, y, route, prev_mod, norm_final)
```

```python
import functools
import math

import numpy as np
import jax
import jax.numpy as jnp
from jax import lax
from jax.experimental import pallas as pl
from jax.experimental.pallas import tpu as pltpu
from jax.experimental.pallas import tpu_sc as plsc

F32 = jnp.float32
BF16 = jnp.bfloat16

D_MODEL = 1024
DEPTH = 4
CHUNK = 64
EPS = 1e-6
H_A, DH_A, DV_A = 4, 64, 128
H_B, DK_B, DV_B = 4, 64, 128
GATE_RANK = 16
GATE_TAU = 16.0
H_C, DH_C = 16, 64
N_BUCKETS = 32
MAX_DISTANCE = 128
N_GROUPS = 4
EXPERTS_PER_GROUP = 8
N_EXPERTS = N_GROUPS * EXPERTS_PER_GROUP
D_EXPERT = 512
MIX = 1024

LANES = 128
V7X_VMEM_BYTES = 64 * 1024 * 1024
NEG_BIG = -1e30
EXP_ZERO_BELOW = -104.0

TOKEN_TILE = 512
ATTN_TILE = 128
STICK_PAIRS = 2
STICK_AHEAD = 2
DIFF_KEY_TILES = 4
DIFF_Q_TILES = 2
GLA_STEP = 256
EXPERT_TILE = 512
SC_SCATTER_WINDOW = 128
MOE_PIECES = 2


def _vmem_limit(block_bytes, scratch_bytes=0):
    est = 2 * block_bytes + scratch_bytes + 16 * 1024 * 1024
    return int(min(est, V7X_VMEM_BYTES - 8 * 1024 * 1024))


def _nt(a, b):
    return lax.dot_general(a, b, (((1,), (1,)), ((), ())), preferred_element_type=F32)


def _tn(a, b):
    return lax.dot_general(a, b, (((0,), (0,)), ((), ())), preferred_element_type=F32)


def _dot(a, b):
    return jnp.dot(a, b, preferred_element_type=F32)


def _split(x):
    hi = x.astype(BF16)
    lo = (x - hi.astype(F32)).astype(BF16)
    return hi, lo


def _dot3(a, b):
    ah, al = _split(a)
    bh, bl = _split(b)
    return _dot(ah, bh) + _dot(al, bh) + _dot(ah, bl)


def _rms(x, g):
    return x * lax.rsqrt(jnp.mean(x * x, axis=-1, keepdims=True) + EPS) * g


def _softplus(z):
    return jnp.maximum(z, 0.0) + jnp.log(1.0 + jnp.exp(-jnp.abs(z)))


def _silu(x):
    return x / (1.0 + jnp.exp(-x))


def _mod_kernel(c_ref, w_ref, b_ref, o_ref):
    c = c_ref[...]
    o_ref[0] = _dot3(_silu(c), w_ref[0]) + b_ref[0]


def _modulation(c, w_ada, b_ada):
    depth, d, n = w_ada.shape
    bsz = c.shape[0]
    tn = 1536
    return pl.pallas_call(
        _mod_kernel,
        out_shape=jax.ShapeDtypeStruct((depth, bsz, n), F32),
        grid=(depth, n // tn),
        in_specs=[
            pl.BlockSpec((bsz, d), lambda l, j: (0, 0)),
            pl.BlockSpec((1, d, tn), lambda l, j: (l, 0, j)),
            pl.BlockSpec((1, 1, tn), lambda l, j: (l, 0, j)),
        ],
        out_specs=pl.BlockSpec((1, bsz, tn), lambda l, j: (l, 0, j)),
        compiler_params=pltpu.CompilerParams(
            dimension_semantics=("arbitrary", "arbitrary"),
            vmem_limit_bytes=_vmem_limit(d * tn * 4 + bsz * (d + tn) * 4)),
        name="adaln_modulation",
    )(c, w_ada, b_ada.reshape(depth, 1, n))


def _expert_residual(x, y_ref, route_ref, mod_ref):
    route = route_ref[0]
    y = route[:, 2:3] * y_ref[0, 0].astype(F32) + route[:, 3:4] * y_ref[1, 0].astype(F32)
    return x + mod_ref[0, 5:6, :] * y


def _proj_kernel(*refs, has_res, n_main, has_gate):
    it = iter(refs)
    x_ref = next(it)
    if has_res:
        y_ref = next(it)
        route_ref = next(it)
        pmod_ref = next(it)
    g_ref = next(it)
    mod_ref = next(it)
    w_ref = next(it)
    wg_ref = next(it) if has_gate else None
    xo_ref = next(it) if has_res else None
    o_ref = next(it)
    og_ref = next(it) if has_gate else None

    x = x_ref[0]
    if has_res:
        x = _expert_residual(x, y_ref, route_ref, pmod_ref)
        xo_ref[0] = x
    h = _rms(x, g_ref[...]) * (1.0 + mod_ref[0, 1:2, :]) + mod_ref[0, 0:1, :]
    hb = h.astype(BF16)
    step = 512
    for n0 in range(0, n_main, step):
        o_ref[0, :, n0:n0 + step] = _dot(hb, w_ref[:, n0:n0 + step]).astype(BF16)
    if has_gate:
        og_ref[0] = _dot(hb, wg_ref[...])


def _in_projection(x, y, route, prev_mod, g, mod, w, w_gate):
    bsz, s, d = x.shape
    tm = TOKEN_TILE
    has_res = y is not None
    has_gate = w_gate is not None
    n_main = w.shape[1]
    tok = lambda b, i: (b, i, 0)
    per_b = lambda b, i: (b, 0, 0)
    const2 = lambda b, i: (0, 0)
    in_specs = [pl.BlockSpec((1, tm, d), tok)]
    args = [x]
    if has_res:
        in_specs += [pl.BlockSpec((2, 1, tm, d), lambda b, i: (0, b, i, 0)),
                     pl.BlockSpec((1, tm, LANES), tok), pl.BlockSpec((1, 6, d), per_b)]
        args += [y, route, prev_mod]
    in_specs += [pl.BlockSpec((1, d), const2), pl.BlockSpec((1, 6, d), per_b),
                 pl.BlockSpec((d, n_main), const2)]
    args += [g.reshape(1, d), mod, w]
    if has_gate:
        in_specs.append(pl.BlockSpec((d, LANES), const2))
        args.append(w_gate)
    out_shape, out_specs = [], []
    if has_res:
        out_shape.append(jax.ShapeDtypeStruct((bsz, s, d), F32))
        out_specs.append(pl.BlockSpec((1, tm, d), tok))
    out_shape.append(jax.ShapeDtypeStruct((bsz, s, n_main), BF16))
    out_specs.append(pl.BlockSpec((1, tm, n_main), tok))
    if has_gate:
        out_shape.append(jax.ShapeDtypeStruct((bsz, s, LANES), F32))
        out_specs.append(pl.BlockSpec((1, tm, LANES), tok))
    block_bytes = (tm * d * 4 * (4 if has_res else 1) + d * n_main * 2 + tm * n_main * 2
                   + d * LANES * 2 + 2 * tm * LANES * 4)
    outs = pl.pallas_call(
        functools.partial(_proj_kernel, has_res=has_res, n_main=n_main, has_gate=has_gate),
        out_shape=out_shape,
        grid=(bsz, s // tm),
        in_specs=in_specs,
        out_specs=out_specs,
        compiler_params=pltpu.CompilerParams(
            dimension_semantics=("arbitrary", "arbitrary"),
            vmem_limit_bytes=_vmem_limit(block_bytes, tm * d * 8)),
        name="norm_mod_in_projection",
    )(*args)
    outs = list(outs)
    x_new = outs.pop(0) if has_res else x
    proj = outs.pop(0)
    gate = outs.pop(0) if has_gate else None
    return x_new, proj, gate


def _t5_bucket(rel):
    nb = N_BUCKETS // 2
    max_exact = nb // 2
    ret = jnp.where(rel > 0, nb, 0)
    n = jnp.abs(rel)
    nf = jnp.maximum(n, 1).astype(F32)
    large = max_exact + (jnp.log(nf / max_exact) / math.log(MAX_DISTANCE / max_exact)
                         * (nb - max_exact)).astype(jnp.int32)
    large = jnp.minimum(large, nb - 1)
    return ret + jnp.where(n < max_exact, n, large)


def _diff_bias_tiles(rel_bias):
    t = ATTN_TILE
    assert t >= MAX_DISTANCE, "tiles two or more away must lie beyond the last bucket edge"
    qp = jnp.arange(t)[:, None]
    kp = jnp.arange(t)[None, :]
    tiles = []
    for off in (0, 1, 2):
        rel = (kp - off * t) - qp
        hit = _t5_bucket(rel)[None, :, :, None] == jnp.arange(N_BUCKETS)[None, None, None, :]
        b = jnp.sum(jnp.where(hit, jnp.transpose(rel_bias.astype(F32))[:, None, None, :], 0.0),
                    axis=-1)
        if off == 0:
            b = jnp.where((kp // CHUNK) <= (qp // CHUNK), b, NEG_BIG)
        tiles.append(b)
    tiles.append(jnp.full_like(tiles[0], NEG_BIG))
    tab = jnp.stack(tiles, axis=1)
    return jnp.concatenate([tab, tab], axis=2)


def _diff_attn_kernel(q_ref, k_ref, v_ref, bias_ref, lam_ref, g_ref, o_ref, *, lambda_init):
    t = ATTN_TILE
    chains = DIFF_Q_TILES
    i = pl.program_id(2)
    lane = lax.broadcasted_iota(jnp.int32, (t, LANES), 1)
    scale = jnp.asarray(DH_A ** -0.5, BF16)
    stacked = []
    for c in range(chains):
        q = q_ref[0, c * t:(c + 1) * t, :]
        zero = jnp.zeros_like(q)
        stacked += [jnp.where(lane < DH_A, q, zero), jnp.where(lane >= DH_A, q, zero)]
    qq = jnp.concatenate(stacked, axis=0) * scale

    group = DIFF_KEY_TILES
    span = group * t
    last_tile = i * chains + chains - 1

    def body(n, carry):
        r0 = pl.multiple_of(n * span, span)
        kb = k_ref[0, pl.ds(r0, span), :]
        vb = v_ref[0, pl.ds(r0, span), :]
        s_chain = [_nt(qq[c * 2 * t:(c + 1) * 2 * t], kb) for c in range(chains)]
        out = []
        for c in range(chains):
            m, l, acc = carry[c]

            def logits(u):
                back = (i * chains + c) - (n * group + u)
                which = jnp.where(back < 0, 3, jnp.minimum(back, 2))
                return s_chain[c][:, u * t:(u + 1) * t] + bias_ref[0, which]

            top = logits(0)
            for u in range(1, group):
                top = jnp.maximum(top, logits(u))
            m_new = jnp.maximum(m, jnp.max(top, axis=-1, keepdims=True))
            alpha = jnp.exp(m - m_new)
            ps = [jnp.exp(logits(u) - m_new) for u in range(group)]
            tot = ps[0]
            for u in range(1, group):
                tot = tot + ps[u]
            l = alpha * l + jnp.sum(tot, axis=-1, keepdims=True)
            p = jnp.concatenate([pu.astype(BF16) for pu in ps], axis=1)
            acc = alpha * acc + _dot(p, vb)
            out.append((m_new, l, acc))
        return tuple(out)

    init = tuple((jnp.full((2 * t, 1), NEG_BIG, F32), jnp.zeros((2 * t, 1), F32),
                  jnp.zeros((2 * t, DV_A), F32)) for _ in range(chains))
    final = lax.fori_loop(0, last_tile // group + 1, body, init)
    lp = lam_ref[...].astype(F32)
    lam = (jnp.exp(jnp.sum(lp[0:1] * lp[1:2], axis=-1, keepdims=True))
           - jnp.exp(jnp.sum(lp[2:3] * lp[3:4], axis=-1, keepdims=True)) + lambda_init)
    for c in range(chains):
        _, l, acc = final[c]
        o = acc / l
        w = o[:t] - lam * o[t:]
        o_ref[0, c * t:(c + 1) * t, :] = (_rms(w, g_ref[...]) * (1.0 - lambda_init)
                                          ).astype(o_ref.dtype)


def _diff_attention(proj, bias_tiles, lam_p, subln_g, layer_idx):
    bsz, s, _ = proj.shape
    t = ATTN_TILE
    tq = DIFF_Q_TILES * t
    lambda_init = 0.8 - 0.6 * math.exp(-0.3 * layer_idx)
    return pl.pallas_call(
        functools.partial(_diff_attn_kernel, lambda_init=lambda_init),
        out_shape=jax.ShapeDtypeStruct((bsz, s, H_A * DV_A), BF16),
        grid=(bsz, H_A, s // tq),
        in_specs=[
            pl.BlockSpec((1, tq, LANES), lambda b, h, i: (b, i, h)),
            pl.BlockSpec((1, s, LANES), lambda b, h, i: (b, 0, H_A + h)),
            pl.BlockSpec((1, s, LANES), lambda b, h, i: (b, 0, 2 * H_A + h)),
            pl.BlockSpec((1, 4, 2 * t, t), lambda b, h, i: (h, 0, 0, 0)),
            pl.BlockSpec((4, DH_A), lambda b, h, i: (0, 0)),
            pl.BlockSpec((1, DV_A), lambda b, h, i: (0, 0)),
        ],
        out_specs=pl.BlockSpec((1, tq, DV_A), lambda b, h, i: (b, i, h)),
        compiler_params=pltpu.CompilerParams(
            dimension_semantics=("arbitrary", "arbitrary", "arbitrary"),
            vmem_limit_bytes=_vmem_limit(2 * s * LANES * 2 + 4 * 2 * t * t * 4 + 4 * tq * LANES,
                                         4 * tq * DIFF_KEY_TILES * t * 4)),
        name="diff_attention",
    )(proj, proj, proj, bias_tiles, lam_p, subln_g.reshape(1, DV_A))


def _gla_kernel(q_ref, k_ref, v_ref, r_ref, bg_ref, wgk_ref, bgk_ref, g_ref, o_ref, st_ref):
    c = CHUNK
    ts = q_ref.shape[1]
    n_chunks = ts // c

    @pl.when(pl.program_id(1) == 0)
    def _():
        st_ref[...] = jnp.zeros_like(st_ref)

    row = lax.broadcasted_iota(jnp.int32, (ts, ts), 0)
    col = lax.broadcasted_iota(jnp.int32, (ts, ts), 1)
    same_chunk_before = ((row // c) == (col // c)) & (col <= row)
    tri = jnp.where(same_chunk_before, 1.0, 0.0).astype(BF16)
    causal = (lax.broadcasted_iota(jnp.int32, (c, c), 1)
              <= lax.broadcasted_iota(jnp.int32, (c, c), 0))
    lane = lax.broadcasted_iota(jnp.int32, (ts, LANES), 1)
    wgk = wgk_ref[...]
    bgk = bgk_ref[...]
    g = g_ref[...]
    bg = bg_ref[0]

    las = [-_softplus(-(_dot3(bg, wgk[:, hp * LANES:(hp + 1) * LANES])
                        + bgk[:, hp * LANES:(hp + 1) * LANES])) / GATE_TAU
           for hp in range(H_B // 2)]
    bs = []
    for la in las:
        la_hi, la_lo = _split(la)
        bs.append(_dot(tri, la_hi) + _dot(tri, la_lo))

    heads = []
    for hp in range(H_B // 2):
        cols = slice(hp * LANES, (hp + 1) * LANES)
        b = bs[hp]
        b_last = jnp.concatenate(
            [jnp.broadcast_to(b[(ci + 1) * c - 1:(ci + 1) * c, :], (c, LANES))
             for ci in range(n_chunks)], axis=0)
        qf = q_ref[0, :, cols].astype(F32) * (DK_B ** -0.5)
        kf = k_ref[0, :, cols].astype(F32)
        q_dec = qf * jnp.exp(b)
        k_inv = (kf * jnp.exp(-b)).astype(BF16)
        k_dec = kf * jnp.exp(b_last - b)
        decays = [jnp.exp(b[(ci + 1) * c - 1:(ci + 1) * c, :]) for ci in range(n_chunks)]
        for hh in range(2):
            own = (lane >= hh * DK_B) & (lane < (hh + 1) * DK_B)
            heads.append((jnp.where(own, q_dec, 0.0).astype(BF16), k_inv,
                          jnp.where(own, k_dec, 0.0).astype(BF16), decays))

    def rows(x, ci):
        return x[ci * c:(ci + 1) * c]

    scores, increments = [], []
    for h, (qd, k_inv, kd, _) in enumerate(heads):
        vh = v_ref[0, :, h * DV_B:(h + 1) * DV_B]
        scores.append([jnp.where(causal, _nt(rows(qd, ci), rows(k_inv, ci)), 0.0).astype(BF16)
                       for ci in range(n_chunks)])
        increments.append([_tn(rows(vh, ci), rows(kd, ci)) for ci in range(n_chunks)])

    for h, (qd, _, _, decays) in enumerate(heads):
        vh = v_ref[0, :, h * DV_B:(h + 1) * DV_B]
        st = st_ref[h]
        states = []
        for ci in range(n_chunks):
            states.append(st.astype(BF16))
            st = st * decays[ci] + increments[h][ci]
        st_ref[h] = st
        o = jnp.concatenate(
            [_dot(scores[h][ci], rows(vh, ci)) + _nt(rows(qd, ci), states[ci])
             for ci in range(n_chunks)], axis=0)
        rr = r_ref[0, :, h * DV_B:(h + 1) * DV_B].astype(F32)
        o_ref[0, :, h * DV_B:(h + 1) * DV_B] = (_rms(o, g) * _silu(rr)).astype(o_ref.dtype)


def _gla(proj, gate, w_gk2, b_gk, gla_g):
    bsz, s, _ = proj.shape
    ts = GLA_STEP
    nk = H_B * DK_B
    nv = H_B * DV_B
    w_pad = jnp.zeros((LANES, nk), F32).at[:GATE_RANK].set(w_gk2.astype(F32))
    return pl.pallas_call(
        _gla_kernel,
        out_shape=jax.ShapeDtypeStruct((bsz, s, nv), BF16),
        grid=(bsz, s // ts),
        in_specs=[
            pl.BlockSpec((1, ts, nk), lambda b, i: (b, i, 1536 // nk)),
            pl.BlockSpec((1, ts, nk), lambda b, i: (b, i, 1792 // nk)),
            pl.BlockSpec((1, ts, nv), lambda b, i: (b, i, 2048 // nv)),
            pl.BlockSpec((1, ts, nv), lambda b, i: (b, i, 2560 // nv)),
            pl.BlockSpec((1, ts, LANES), lambda b, i: (b, i, 0)),
            pl.BlockSpec((LANES, nk), lambda b, i: (0, 0)),
            pl.BlockSpec((1, nk), lambda b, i: (0, 0)),
            pl.BlockSpec((1, DV_B), lambda b, i: (0, 0)),
        ],
        out_specs=pl.BlockSpec((1, ts, nv), lambda b, i: (b, i, 0)),
        scratch_shapes=[pltpu.VMEM((H_B, DV_B, LANES), F32)],
        compiler_params=pltpu.CompilerParams(
            dimension_semantics=("arbitrary", "arbitrary"),
            vmem_limit_bytes=_vmem_limit(ts * (2 * nk + 3 * nv) * 2 + ts * LANES * 4,
                                         H_B * DV_B * LANES * 4)),
        name="gated_linear_attention",
    )(proj, proj, proj, proj, gate, w_pad, b_gk.reshape(1, nk), gla_g.reshape(1, DV_B))


def _stick_kernel(q_ref, k_ref, v_ref, o_ref, run_ref, acc_ref):
    t = ATTN_TILE
    chains = STICK_PAIRS
    i = pl.program_id(2)
    lane = lax.broadcasted_iota(jnp.int32, (t, LANES), 1)
    scale = jnp.asarray(DH_C ** -0.5, BF16)
    qqs = []
    for c in range(chains):
        q = q_ref[0, :, c * LANES:(c + 1) * LANES]
        zero = jnp.zeros_like(q)
        qqs.append(jnp.concatenate([jnp.where(lane < DH_C, q, zero),
                                    jnp.where(lane >= DH_C, q, zero)], axis=0) * scale)
    row = lax.broadcasted_iota(jnp.int32, (t, t), 0)
    col = lax.broadcasted_iota(jnp.int32, (t, t), 1)
    later = jnp.where(row > col, 1.0, 0.0).astype(BF16)
    below = (lax.broadcasted_iota(jnp.int32, (2 * t, t), 1)
             < lax.broadcasted_iota(jnp.int32, (2 * t, t), 0) % t)

    def logits(j):
        r0 = pl.multiple_of(jnp.maximum(j, 0) * t, t)
        return [_nt(qqs[c], k_ref[0, pl.ds(r0, t), c * LANES:(c + 1) * LANES])
                for c in range(chains)]

    def tiles(first, zs, runs, accs, diagonal_first):
        runs, accs = list(runs), list(accs)
        mids = []
        for u in range(len(zs)):
            for c in range(chains):
                z = zs[u][c]
                sp = _softplus(z)
                log_1m = -sp
                masked = diagonal_first and u == 0
                if masked:
                    log_1m = jnp.where(below, log_1m, 0.0)
                hi, lo = _split(log_1m)
                within = _dot(hi, later) + _dot(lo, later)
                mids.append((u, c, z - sp, log_1m, within, masked))
        for u, c, log_b, log_1m, within, masked in mids:
            r0 = pl.multiple_of((first - u) * t, t)
            vb = v_ref[0, pl.ds(r0, t), c * LANES:(c + 1) * LANES]
            a = jnp.exp(log_b + (within + runs[c]))
            if masked:
                a = jnp.where(below, a, 0.0)
            ab = a.astype(BF16)
            zero_v = jnp.zeros_like(vb)
            a2 = jnp.concatenate([ab[:t], ab[t:]], axis=1)
            v2 = jnp.concatenate([jnp.where(lane < DH_C, vb, zero_v),
                                  jnp.where(lane >= DH_C, vb, zero_v)], axis=0)
            accs[c] = accs[c] + _dot(a2, v2)
            runs[c] = runs[c] + jnp.sum(log_1m, axis=-1, keepdims=True)
        return runs, accs

    def first_block(n_tiles):
        zs = [logits(i - u) for u in range(n_tiles)]
        runs, accs = tiles(i, zs, [jnp.zeros((2 * t, 1), F32)] * chains,
                           [jnp.zeros((t, LANES), F32)] * chains, True)
        for c in range(chains):
            run_ref[c] = runs[c]
            acc_ref[c] = accs[c]

    ahead = STICK_AHEAD

    @pl.when(i >= ahead)
    def _():
        first_block(1 + ahead)

    @pl.when(i < ahead)
    def _():
        first_block(1)

    n_left = i
    done0 = jnp.where(i >= ahead, ahead, 0)

    def go_on(done, runs):
        top = runs[0]
        for c in range(1, chains):
            top = jnp.maximum(top, runs[c])
        return ((done < n_left) & (jnp.max(top) > EXP_ZERO_BELOW)).astype(jnp.int32)

    def body(carry):
        done, _, runs, accs = carry
        first = n_left - 1 - done
        runs, accs = tiles(first, [logits(first)], runs, accs, False)
        done = done + 1
        return done, go_on(done, runs), tuple(runs), tuple(accs)

    runs = tuple(run_ref[c] for c in range(chains))
    accs = tuple(acc_ref[c] for c in range(chains))
    carry = (done0, go_on(done0, runs), runs, accs)
    accs = lax.while_loop(lambda carry: carry[1] > 0, body, carry)[3]
    for c in range(chains):
        o_ref[0, :, c * LANES:(c + 1) * LANES] = accs[c].astype(o_ref.dtype)


def _stick_breaking(proj):
    bsz, s, _ = proj.shape
    t = ATTN_TILE
    width = STICK_PAIRS * LANES
    n_blocks = H_C * DH_C // width
    return pl.pallas_call(
        _stick_kernel,
        out_shape=jax.ShapeDtypeStruct((bsz, s, H_C * DH_C), BF16),
        grid=(bsz, n_blocks, s // t),
        in_specs=[
            pl.BlockSpec((1, t, width), lambda b, h, i: (b, i, h)),
            pl.BlockSpec((1, s, width), lambda b, h, i: (b, 0, n_blocks + h)),
            pl.BlockSpec((1, s, width), lambda b, h, i: (b, 0, 2 * n_blocks + h)),
        ],
        out_specs=pl.BlockSpec((1, t, width), lambda b, h, i: (b, i, h)),
        scratch_shapes=[pltpu.VMEM((STICK_PAIRS, 2 * t, 1), F32),
                        pltpu.VMEM((STICK_PAIRS, t, LANES), F32)],
        compiler_params=pltpu.CompilerParams(
            dimension_semantics=("arbitrary", "arbitrary", "arbitrary"),
            vmem_limit_bytes=_vmem_limit(2 * s * width * 2 + 4 * t * width,
                                         STICK_PAIRS * 3 * t * LANES * 4)),
        name="stick_breaking_attention",
    )(proj, proj, proj)


def _route(logits):
    lane_i = lax.broadcasted_iota(jnp.int32, logits.shape, 1)
    lane = lane_i.astype(F32)
    group_of_lane = (lane_i // EXPERTS_PER_GROUP).astype(F32)
    big = jnp.float32(1 << 20)
    neg = jnp.float32(-jnp.inf)
    is_group = (lane_i >= N_EXPERTS) & (lane_i < N_EXPERTS + N_GROUPS)
    gl = jnp.where(is_group, logits, neg)
    gmax = jnp.max(gl, axis=-1, keepdims=True)
    g_val = 1.0 / jnp.sum(jnp.exp(gl - gmax), axis=-1, keepdims=True)
    g_sel = jnp.min(jnp.where(gl == gmax, lane - N_EXPERTS, big), axis=-1, keepdims=True)
    in_group = (lane_i < N_EXPERTS) & (group_of_lane == g_sel)
    el = jnp.where(in_group, logits, neg)
    emax = jnp.max(el, axis=-1, keepdims=True)
    esum = jnp.sum(jnp.exp(el - emax), axis=-1, keepdims=True)
    i1 = jnp.min(jnp.where(el == emax, lane, big), axis=-1, keepdims=True)
    el2 = jnp.where(lane == i1, neg, el)
    emax2 = jnp.max(el2, axis=-1, keepdims=True)
    i2 = jnp.min(jnp.where(el2 == emax2, lane, big), axis=-1, keepdims=True)
    v1 = 1.0 / esum
    v2 = jnp.exp(emax2 - emax) / esum
    tot = v1 + v2
    w1 = g_val * (v1 / tot)
    w2 = g_val * (v2 / tot)
    out = jnp.where(lane_i == 0, i1, 0.0)
    out = jnp.where(lane_i == 1, i2, out)
    out = jnp.where(lane_i == 2, w1, out)
    out = jnp.where(lane_i == 3, w2, out)
    return out, (lane == i1), (lane == i2)


def _pack_halves(h):
    half = h.shape[1] // 2
    lo = pltpu.bitcast(h[:, :half].astype(BF16).astype(F32), jnp.uint32)
    hi = pltpu.bitcast(h[:, half:].astype(BF16).astype(F32), jnp.uint32)
    return lax.shift_right_logical(lo, jnp.uint32(16)) | hi


def _unpack_halves(p):
    lo = pltpu.bitcast(lax.shift_left(p, jnp.uint32(16)), F32).astype(BF16)
    hi = pltpu.bitcast(p & jnp.uint32(0xFFFF0000), F32).astype(BF16)
    return lo, hi


def _out_kernel(*refs, n_o):
    o_refs = refs[:n_o]
    w_refs = refs[n_o:2 * n_o]
    x_ref, mod_ref, g_ref, wr_ref, br_ref, xo_ref = refs[2 * n_o:2 * n_o + 6]
    h_refs = refs[2 * n_o + 6:2 * n_o + 6 + MOE_PIECES]
    r_ref, cnt_ref, run_ref = refs[2 * n_o + 6 + MOE_PIECES:]

    @pl.when((pl.program_id(0) == 0) & (pl.program_id(1) == 0))
    def _():
        run_ref[...] = jnp.zeros_like(run_ref)

    m = _dot(o_refs[0][0], w_refs[0][...])
    for a in range(1, n_o):
        m = m + _dot(o_refs[a][0], w_refs[a][...])
    x = x_ref[0] + mod_ref[0, 2:3, :] * m
    xo_ref[0] = x
    h = _rms(x, g_ref[...]) * (1.0 + mod_ref[0, 4:5, :]) + mod_ref[0, 3:4, :]
    piece = h.shape[1] // MOE_PIECES
    for k in range(MOE_PIECES):
        h_refs[k][0] = _pack_halves(h[:, k * piece:(k + 1) * piece])
    route, hit1, hit2 = _route(_dot3(h, wr_ref[...]) + br_ref[...])
    tm = route.shape[0]
    picks = jnp.where(hit1 | hit2, 1.0, 0.0)
    row = lax.broadcasted_iota(jnp.int32, (tm, tm), 0)
    col = lax.broadcasted_iota(jnp.int32, (tm, tm), 1)
    before = jnp.where(col < row, 1.0, 0.0).astype(BF16)
    earlier = _dot(before, picks.astype(BF16)) + run_ref[...]
    rank1 = jnp.sum(jnp.where(hit1, earlier, 0.0), axis=-1, keepdims=True)
    rank2 = jnp.sum(jnp.where(hit2, earlier, 0.0), axis=-1, keepdims=True)
    lane_i = lax.broadcasted_iota(jnp.int32, route.shape, 1)
    route = jnp.where(lane_i == 4, rank1, route)
    route = jnp.where(lane_i == 5, rank2, route)
    r_ref[0] = route
    run_ref[...] = run_ref[...] + jnp.sum(picks, axis=0, keepdims=True)
    cnt_ref[...] = run_ref[...]


def _out_projection(o_list, w_list, x, mod, g, w_router, b_router):
    bsz, s, d = x.shape
    tm = TOKEN_TILE
    n_o = len(o_list)
    packed = d // (2 * MOE_PIECES)
    tok = lambda b, i: (b, i, 0)
    const2 = lambda b, i: (0, 0)
    in_specs = ([pl.BlockSpec((1, tm, o.shape[2]), tok) for o in o_list]
                + [pl.BlockSpec(w.shape, const2) for w in w_list]
                + [pl.BlockSpec((1, tm, d), tok), pl.BlockSpec((1, 6, d), lambda b, i: (b, 0, 0)),
                   pl.BlockSpec((1, d), const2), pl.BlockSpec((d, LANES), const2),
                   pl.BlockSpec((1, LANES), const2)])
    block_bytes = (sum(tm * o.shape[2] * 2 for o in o_list) + sum(w.size * 2 for w in w_list)
                   + 2 * tm * d * 4 + tm * d * 2 + d * LANES * 4 + tm * LANES * 4)
    return pl.pallas_call(
        functools.partial(_out_kernel, n_o=n_o),
        out_shape=([jax.ShapeDtypeStruct((bsz, s, d), F32)]
                   + [jax.ShapeDtypeStruct((bsz, s, packed), jnp.uint32)] * MOE_PIECES
                   + [jax.ShapeDtypeStruct((bsz, s, LANES), F32),
                      jax.ShapeDtypeStruct((1, LANES), F32)]),
        grid=(bsz, s // tm),
        in_specs=in_specs,
        out_specs=([pl.BlockSpec((1, tm, d), tok)]
                   + [pl.BlockSpec((1, tm, packed), tok)] * MOE_PIECES
                   + [pl.BlockSpec((1, tm, LANES), tok), pl.BlockSpec((1, LANES), const2)]),
        scratch_shapes=[pltpu.VMEM((1, LANES), F32)],
        compiler_params=pltpu.CompilerParams(
            dimension_semantics=("arbitrary", "arbitrary"),
            vmem_limit_bytes=_vmem_limit(block_bytes, tm * d * 12 + tm * tm * 4)),
        name="out_projection_norm_router",
    )(*o_list, *w_list, x, mod, g.reshape(1, d), w_router, b_router)


def _expert_kernel(te_ref, nu_ref, *refs):
    x_refs = refs[:MOE_PIECES]
    wg_ref, wu_ref, wd_ref, o_ref, wgb_ref, wub_ref, wdb_ref = refs[MOE_PIECES:]
    t = pl.program_id(0)
    prev = te_ref[jnp.maximum(t - 1, 0)]

    @pl.when((t == 0) | (te_ref[t] != prev))
    def _():
        wgb_ref[...] = wg_ref[0, 0].astype(BF16)
        wub_ref[...] = wu_ref[0, 0].astype(BF16)
        wdb_ref[...] = wd_ref[0, 0].astype(BF16)

    @pl.when(t < nu_ref[0])
    def _():
        cols = []
        for x_ref in x_refs:
            cols += list(_unpack_halves(x_ref[...]))
        w = cols[0].shape[1]
        gt = _dot(cols[0], wgb_ref[:w, :])
        up = _dot(cols[0], wub_ref[:w, :])
        for k in range(1, len(cols)):
            gt = gt + _dot(cols[k], wgb_ref[k * w:(k + 1) * w, :])
            up = up + _dot(cols[k], wub_ref[k * w:(k + 1) * w, :])
        o_ref[...] = _dot((_silu(gt) * up).astype(BF16), wdb_ref[...]).astype(o_ref.dtype)

    @pl.when(t >= nu_ref[0])
    def _():
        o_ref[...] = jnp.zeros_like(o_ref)


def _grouped_experts(xs, tile_expert, n_used, w_gate, w_up, w_down, layer):
    p, packed = xs[0].shape
    d = w_gate.shape[2]
    tm = EXPERT_TILE
    f = w_gate.shape[3]
    grid_spec = pltpu.PrefetchScalarGridSpec(
        num_scalar_prefetch=2,
        grid=(p // tm,),
        in_specs=[pl.BlockSpec((tm, packed), lambda t, te, nu: (t, 0))] * MOE_PIECES + [
            pl.BlockSpec((1, 1, d, f), lambda t, te, nu: (layer, te[t], 0, 0)),
            pl.BlockSpec((1, 1, d, f), lambda t, te, nu: (layer, te[t], 0, 0)),
            pl.BlockSpec((1, 1, f, d), lambda t, te, nu: (layer, te[t], 0, 0)),
        ],
        out_specs=pl.BlockSpec((tm, d), lambda t, te, nu: (t, 0)),
        scratch_shapes=[pltpu.VMEM((d, f), BF16), pltpu.VMEM((d, f), BF16),
                        pltpu.VMEM((f, d), BF16)],
    )
    block_bytes = tm * d * 2 + 3 * d * f * 4 + tm * d * 2
    return pl.pallas_call(
        _expert_kernel,
        out_shape=jax.ShapeDtypeStruct((p, d), BF16),
        grid_spec=grid_spec,
        compiler_params=pltpu.CompilerParams(
            dimension_semantics=("arbitrary",),
            vmem_limit_bytes=_vmem_limit(block_bytes, 3 * d * f * 2 + tm * f * 16)),
        name="grouped_swiglu_experts",
    )(tile_expert, n_used, *xs, w_gate, w_up, w_down)


def _pos_kernel(route_ref, starts_ref, pos_ref):
    rt = jnp.transpose(route_ref[...])
    starts = starts_ref[...]
    ids = lax.broadcasted_iota(jnp.int32, (N_EXPERTS, rt.shape[1]), 0).astype(F32)
    for c in range(2):
        start_of = jnp.sum(jnp.where(ids == rt[c:c + 1, :], starts, 0.0), axis=0, keepdims=True)
        pos_ref[c:c + 1, :] = (start_of + rt[4 + c:5 + c, :]).astype(jnp.int32)


def _row_positions(route, starts):
    n_tok = route.shape[0]
    tm = TOKEN_TILE
    return pl.pallas_call(
        _pos_kernel,
        out_shape=jax.ShapeDtypeStruct((2, n_tok), jnp.int32),
        grid=(n_tok // tm,),
        in_specs=[pl.BlockSpec((tm, LANES), lambda i: (i, 0)),
                  pl.BlockSpec((N_EXPERTS, 1), lambda i: (0, 0))],
        out_specs=pl.BlockSpec((2, tm), lambda i: (0, i)),
        compiler_params=pltpu.CompilerParams(
            dimension_semantics=("arbitrary",),
            vmem_limit_bytes=_vmem_limit(tm * LANES * 4 + N_EXPERTS * LANES * 4 + 8 * tm * 4)),
        name="expert_row_positions",
    )(route, starts.astype(F32).reshape(N_EXPERTS, 1))


def _scatter_rows(rows, pos, n_out):
    n_tok, d = rows.shape
    n_idx = pos.shape[0]
    w = SC_SCATTER_WINDOW
    per_pass = n_tok // w
    mesh = plsc.VectorSubcoreMesh(core_axis_name="core", subcore_axis_name="subcore")

    @pl.kernel(out_type=jax.ShapeDtypeStruct((n_out, d), rows.dtype), mesh=mesh, scratch_types=[])
    def scatter_kernel(x_hbm, i_hbm, o_hbm):
        def body(x_vmem, i_vmem):
            pltpu.sync_copy(x_vmem, o_hbm.at[i_vmem.at[0]])

        pltpu.emit_pipeline(
            body,
            grid=(n_idx // w,),
            in_specs=[pl.BlockSpec((w, d), lambda i: (i % per_pass, 0)),
                      pl.BlockSpec((1, w), lambda i: (0, i))],
            out_specs=[],
            core_axis_name=("core", "subcore"),
            dimension_semantics=(pltpu.PARALLEL,),
        )(x_hbm, i_hbm)

    return scatter_kernel(rows, pos.reshape(1, n_idx))


def _dispatch(route, counts, n_tok):
    tm = EXPERT_TILE
    counts = counts[:N_EXPERTS].astype(jnp.int32)
    padded = ((counts + tm - 1) // tm) * tm
    ends = jnp.cumsum(padded)
    starts = ends - padded
    pos = _row_positions(route, starts).reshape(2 * n_tok)
    p = 2 * n_tok + N_EXPERTS * tm
    tile_start = jnp.arange(p // tm, dtype=jnp.int32) * tm
    tile_expert = jnp.minimum(jnp.sum(ends[None, :] <= tile_start[:, None], axis=1),
                              N_EXPERTS - 1).astype(jnp.int32)
    n_used = (ends[-1] // tm).astype(jnp.int32).reshape(1)
    return p, tile_expert, n_used, pos


def _final_kernel(x_ref, y_ref, route_ref, mod_ref, g_ref, o_ref):
    o_ref[0] = _rms(_expert_residual(x_ref[0], y_ref, route_ref, mod_ref), g_ref[...])


def _final_norm(x, y, route, mod, g):
    bsz, s, d = x.shape
    tm = TOKEN_TILE
    tok = lambda b, i: (b, i, 0)
    return pl.pallas_call(
        _final_kernel,
        out_shape=jax.ShapeDtypeStruct((bsz, s, d), F32),
        grid=(bsz, s // tm),
        in_specs=[pl.BlockSpec((1, tm, d), tok),
                  pl.BlockSpec((2, 1, tm, d), lambda b, i: (0, b, i, 0)),
                  pl.BlockSpec((1, tm, LANES), tok),
                  pl.BlockSpec((1, 6, d), lambda b, i: (b, 0, 0)),
                  pl.BlockSpec((1, d), lambda b, i: (0, 0))],
        out_specs=pl.BlockSpec((1, tm, d), tok),
        compiler_params=pltpu.CompilerParams(
            dimension_semantics=("arbitrary", "arbitrary"),
            vmem_limit_bytes=_vmem_limit(4 * tm * d * 4 + tm * LANES * 4)),
        name="residual_final_norm",
    )(x, y, route, mod, g.reshape(1, d))


def kernel(x, c, w_ada, b_ada, norm_mix, norm_ffn, norm_final, rel_bias, even_w_in, even_lambda, even_subln, even_w_gk2, even_b_gk, even_gla_norm, even_w_out, odd_w_in, odd_w_out, router_group_w, router_group_b, router_expert_w, router_expert_b, expert_w_gate, expert_w_up, expert_w_down):
    bsz, s, d = x.shape
    n_tok = bsz * s
    assert s % TOKEN_TILE == 0 and s % GLA_STEP == 0
    assert s % (DIFF_KEY_TILES * ATTN_TILE) == 0 and s % (DIFF_Q_TILES * ATTN_TILE) == 0
    mod_all = _modulation(c, w_ada, b_ada).reshape(DEPTH, bsz, 6, d)
    bias_tiles = _diff_bias_tiles(rel_bias)
    n_even_main = 3072
    y = route = prev_mod = None
    for l in range(DEPTH):
        mod = mod_all[l]
        i = l // 2
        if l % 2 == 0:
            w_in = even_w_in[i]
            w_main = w_in[:, :n_even_main].astype(BF16)
            w_gate = jnp.zeros((d, LANES), BF16).at[:, :GATE_RANK].set(
                w_in[:, n_even_main:].astype(BF16))
            x, proj, gate = _in_projection(x, y, route, prev_mod, norm_mix[l], mod, w_main, w_gate)
            oa = _diff_attention(proj, bias_tiles, even_lambda[i], even_subln[i], l)
            ob = _gla(proj, gate, even_w_gk2[i], even_b_gk[i], even_gla_norm[i])
            w_out = even_w_out[i].astype(BF16)
            o_list = [oa, ob]
            w_list = [w_out[:H_A * DV_A], w_out[H_A * DV_A:]]
        else:
            x, proj, _ = _in_projection(x, y, route, prev_mod, norm_mix[l], mod,
                                        odd_w_in[i].astype(BF16), None)
            o_list = [_stick_breaking(proj)]
            w_list = [odd_w_out[i].astype(BF16)]
        w_router = jnp.zeros((d, LANES), F32)
        w_router = w_router.at[:, :N_EXPERTS].set(router_expert_w[l].astype(F32))
        w_router = w_router.at[:, N_EXPERTS:N_EXPERTS + N_GROUPS].set(router_group_w[l].astype(F32))
        b_router = jnp.zeros((1, LANES), F32)
        b_router = b_router.at[0, :N_EXPERTS].set(router_expert_b[l].astype(F32))
        b_router = b_router.at[0, N_EXPERTS:N_EXPERTS + N_GROUPS].set(router_group_b[l].astype(F32))
        outs = _out_projection(o_list, w_list, x, mod, norm_ffn[l], w_router, b_router)
        x, pieces, route, counts = outs[0], outs[1:1 + MOE_PIECES], outs[-2], outs[-1]
        n_rows, tile_expert, n_used, pos = _dispatch(route.reshape(n_tok, LANES), counts[0], n_tok)
        xs = [_scatter_rows(piece.reshape(n_tok, piece.shape[2]), pos, n_rows) for piece in pieces]
        out = _grouped_experts(xs, tile_expert, n_used, expert_w_gate, expert_w_up,
                               expert_w_down, l)
        y = out.at[pos].get(mode="promise_in_bounds").reshape(2, bsz, s, d)
        prev_mod = mod
    return _final_norm(x, y, route, prev_mod, norm_final)
```

```python
import functools
import math

import numpy as np
import jax
import jax.numpy as jnp
from jax import lax
from jax.experimental import pallas as pl
from jax.experimental.pallas import tpu as pltpu
from jax.experimental.pallas import tpu_sc as plsc

F32 = jnp.float32
BF16 = jnp.bfloat16

D_MODEL = 1024
DEPTH = 4
CHUNK = 64
EPS = 1e-6
H_A, DH_A, DV_A = 4, 64, 128
H_B, DK_B, DV_B = 4, 64, 128
GATE_RANK = 16
GATE_TAU = 16.0
H_C, DH_C = 16, 64
N_BUCKETS = 32
MAX_DISTANCE = 128
N_GROUPS = 4
EXPERTS_PER_GROUP = 8
N_EXPERTS = N_GROUPS * EXPERTS_PER_GROUP
D_EXPERT = 512
MIX = 1024

LANES = 128
V7X_VMEM_BYTES = 64 * 1024 * 1024
NEG_BIG = -1e30
EXP_ZERO_BELOW = -104.0

TOKEN_TILE = 512
ATTN_TILE = 128
STICK_PAIRS = 4
STICK_AHEAD = 2
DIFF_KEY_TILES = 8
DIFF_Q_TILES = 4
GLA_STEP = 256
EXPERT_TILE = 512
SC_SCATTER_WINDOW = 128
MOE_PIECES = 2


def _vmem_limit(block_bytes, scratch_bytes=0):
    est = 2 * block_bytes + scratch_bytes + 16 * 1024 * 1024
    return int(min(est, V7X_VMEM_BYTES - 8 * 1024 * 1024))


def _nt(a, b):
    return lax.dot_general(a, b, (((1,), (1,)), ((), ())), preferred_element_type=F32)


def _tn(a, b):
    return lax.dot_general(a, b, (((0,), (0,)), ((), ())), preferred_element_type=F32)


def _dot(a, b):
    return jnp.dot(a, b, preferred_element_type=F32)


def _split(x):
    hi = x.astype(BF16)
    lo = (x - hi.astype(F32)).astype(BF16)
    return hi, lo


def _dot3(a, b):
    ah, al = _split(a)
    bh, bl = _split(b)
    return _dot(ah, bh) + _dot(al, bh) + _dot(ah, bl)


def _rms(x, g):
    return x * lax.rsqrt(jnp.mean(x * x, axis=-1, keepdims=True) + EPS) * g


def _softplus(z):
    return jnp.maximum(z, 0.0) + jnp.log(1.0 + jnp.exp(-jnp.abs(z)))


def _silu(x):
    return x / (1.0 + jnp.exp(-x))


def _mod_kernel(c_ref, w_ref, b_ref, o_ref):
    c = c_ref[...]
    o_ref[0] = _dot3(_silu(c), w_ref[0]) + b_ref[0]


def _modulation(c, w_ada, b_ada):
    depth, d, n = w_ada.shape
    bsz = c.shape[0]
    tn = 1536
    return pl.pallas_call(
        _mod_kernel,
        out_shape=jax.ShapeDtypeStruct((depth, bsz, n), F32),
        grid=(depth, n // tn),
        in_specs=[
            pl.BlockSpec((bsz, d), lambda l, j: (0, 0)),
            pl.BlockSpec((1, d, tn), lambda l, j: (l, 0, j)),
            pl.BlockSpec((1, 1, tn), lambda l, j: (l, 0, j)),
        ],
        out_specs=pl.BlockSpec((1, bsz, tn), lambda l, j: (l, 0, j)),
        compiler_params=pltpu.CompilerParams(
            dimension_semantics=("arbitrary", "arbitrary"),
            vmem_limit_bytes=_vmem_limit(d * tn * 4 + bsz * (d + tn) * 4)),
        name="adaln_modulation",
    )(c, w_ada, b_ada.reshape(depth, 1, n))


def _expert_residual(x, y_ref, route_ref, mod_ref):
    route = route_ref[0]
    y = route[:, 2:3] * y_ref[0, 0].astype(F32) + route[:, 3:4] * y_ref[1, 0].astype(F32)
    return x + mod_ref[0, 5:6, :] * y


def _proj_kernel(*refs, has_res, n_main, has_gate):
    it = iter(refs)
    x_ref = next(it)
    if has_res:
        y_ref = next(it)
        route_ref = next(it)
        pmod_ref = next(it)
    g_ref = next(it)
    mod_ref = next(it)
    w_ref = next(it)
    wg_ref = next(it) if has_gate else None
    xo_ref = next(it) if has_res else None
    o_ref = next(it)
    og_ref = next(it) if has_gate else None

    x = x_ref[0]
    if has_res:
        x = _expert_residual(x, y_ref, route_ref, pmod_ref)
        xo_ref[0] = x
    h = _rms(x, g_ref[...]) * (1.0 + mod_ref[0, 1:2, :]) + mod_ref[0, 0:1, :]
    hb = h.astype(BF16)
    step = 512
    for n0 in range(0, n_main, step):
        o_ref[0, :, n0:n0 + step] = _dot(hb, w_ref[:, n0:n0 + step]).astype(BF16)
    if has_gate:
        og_ref[0] = _dot(hb, wg_ref[...])


def _in_projection(x, y, route, prev_mod, g, mod, w, w_gate):
    bsz, s, d = x.shape
    tm = TOKEN_TILE
    has_res = y is not None
    has_gate = w_gate is not None
    n_main = w.shape[1]
    tok = lambda b, i: (b, i, 0)
    per_b = lambda b, i: (b, 0, 0)
    const2 = lambda b, i: (0, 0)
    in_specs = [pl.BlockSpec((1, tm, d), tok)]
    args = [x]
    if has_res:
        in_specs += [pl.BlockSpec((2, 1, tm, d), lambda b, i: (0, b, i, 0)),
                     pl.BlockSpec((1, tm, LANES), tok), pl.BlockSpec((1, 6, d), per_b)]
        args += [y, route, prev_mod]
    in_specs += [pl.BlockSpec((1, d), const2), pl.BlockSpec((1, 6, d), per_b),
                 pl.BlockSpec((d, n_main), const2)]
    args += [g.reshape(1, d), mod, w]
    if has_gate:
        in_specs.append(pl.BlockSpec((d, LANES), const2))
        args.append(w_gate)
    out_shape, out_specs = [], []
    if has_res:
        out_shape.append(jax.ShapeDtypeStruct((bsz, s, d), F32))
        out_specs.append(pl.BlockSpec((1, tm, d), tok))
    out_shape.append(jax.ShapeDtypeStruct((bsz, s, n_main), BF16))
    out_specs.append(pl.BlockSpec((1, tm, n_main), tok))
    if has_gate:
        out_shape.append(jax.ShapeDtypeStruct((bsz, s, LANES), F32))
        out_specs.append(pl.BlockSpec((1, tm, LANES), tok))
    block_bytes = (tm * d * 4 * (4 if has_res else 1) + d * n_main * 2 + tm * n_main * 2
                   + d * LANES * 2 + 2 * tm * LANES * 4)
    outs = pl.pallas_call(
        functools.partial(_proj_kernel, has_res=has_res, n_main=n_main, has_gate=has_gate),
        out_shape=out_shape,
        grid=(bsz, s // tm),
        in_specs=in_specs,
        out_specs=out_specs,
        compiler_params=pltpu.CompilerParams(
            dimension_semantics=("arbitrary", "arbitrary"),
            vmem_limit_bytes=_vmem_limit(block_bytes, tm * d * 8)),
        name="norm_mod_in_projection",
    )(*args)
    outs = list(outs)
    x_new = outs.pop(0) if has_res else x
    proj = outs.pop(0)
    gate = outs.pop(0) if has_gate else None
    return x_new, proj, gate


def _t5_bucket(rel):
    nb = N_BUCKETS // 2
    max_exact = nb // 2
    ret = jnp.where(rel > 0, nb, 0)
    n = jnp.abs(rel)
    nf = jnp.maximum(n, 1).astype(F32)
    large = max_exact + (jnp.log(nf / max_exact) / math.log(MAX_DISTANCE / max_exact)
                         * (nb - max_exact)).astype(jnp.int32)
    large = jnp.minimum(large, nb - 1)
    return ret + jnp.where(n < max_exact, n, large)


def _diff_bias_tiles(rel_bias):
    t = ATTN_TILE
    assert t >= MAX_DISTANCE, "tiles two or more away must lie beyond the last bucket edge"
    qp = jnp.arange(t)[:, None]
    kp = jnp.arange(t)[None, :]
    tiles = []
    for off in (0, 1, 2):
        rel = (kp - off * t) - qp
        hit = _t5_bucket(rel)[None, :, :, None] == jnp.arange(N_BUCKETS)[None, None, None, :]
        b = jnp.sum(jnp.where(hit, jnp.transpose(rel_bias.astype(F32))[:, None, None, :], 0.0),
                    axis=-1)
        if off == 0:
            b = jnp.where((kp // CHUNK) <= (qp // CHUNK), b, NEG_BIG)
        tiles.append(b)
    tiles.append(jnp.full_like(tiles[0], NEG_BIG))
    tab = jnp.stack(tiles, axis=1)
    return jnp.concatenate([tab, tab], axis=2)


def _diff_attn_kernel(q_ref, k_ref, v_ref, bias_ref, lam_ref, g_ref, o_ref, *, lambda_init):
    t = ATTN_TILE
    chains = DIFF_Q_TILES
    i = pl.program_id(2)
    lane = lax.broadcasted_iota(jnp.int32, (t, LANES), 1)
    scale = jnp.asarray(DH_A ** -0.5, BF16)
    stacked = []
    for c in range(chains):
        q = q_ref[0, c * t:(c + 1) * t, :]
        zero = jnp.zeros_like(q)
        stacked += [jnp.where(lane < DH_A, q, zero), jnp.where(lane >= DH_A, q, zero)]
    qq = jnp.concatenate(stacked, axis=0) * scale

    group = DIFF_KEY_TILES
    span = group * t
    last_tile = i * chains + chains - 1

    def body(n, carry):
        r0 = pl.multiple_of(n * span, span)
        kb = k_ref[0, pl.ds(r0, span), :]
        vb = v_ref[0, pl.ds(r0, span), :]
        s_chain = [_nt(qq[c * 2 * t:(c + 1) * 2 * t], kb) for c in range(chains)]
        out = []
        for c in range(chains):
            m, l, acc = carry[c]

            def logits(u):
                back = (i * chains + c) - (n * group + u)
                which = jnp.where(back < 0, 3, jnp.minimum(back, 2))
                return s_chain[c][:, u * t:(u + 1) * t] + bias_ref[0, which]

            top = logits(0)
            for u in range(1, group):
                top = jnp.maximum(top, logits(u))
            m_new = jnp.maximum(m, jnp.max(top, axis=-1, keepdims=True))
            alpha = jnp.exp(m - m_new)
            ps = [jnp.exp(logits(u) - m_new) for u in range(group)]
            tot = ps[0]
            for u in range(1, group):
                tot = tot + ps[u]
            l = alpha * l + jnp.sum(tot, axis=-1, keepdims=True)
            p = jnp.concatenate([pu.astype(BF16) for pu in ps], axis=1)
            acc = alpha * acc + _dot(p, vb)
            out.append((m_new, l, acc))
        return tuple(out)

    init = tuple((jnp.full((2 * t, 1), NEG_BIG, F32), jnp.zeros((2 * t, 1), F32),
                  jnp.zeros((2 * t, DV_A), F32)) for _ in range(chains))
    final = lax.fori_loop(0, last_tile // group + 1, body, init)
    lp = lam_ref[...].astype(F32)
    lam = (jnp.exp(jnp.sum(lp[0:1] * lp[1:2], axis=-1, keepdims=True))
           - jnp.exp(jnp.sum(lp[2:3] * lp[3:4], axis=-1, keepdims=True)) + lambda_init)
    for c in range(chains):
        _, l, acc = final[c]
        o = acc / l
        w = o[:t] - lam * o[t:]
        o_ref[0, c * t:(c + 1) * t, :] = (_rms(w, g_ref[...]) * (1.0 - lambda_init)
                                          ).astype(o_ref.dtype)


def _diff_attention(proj, bias_tiles, lam_p, subln_g, layer_idx):
    bsz, s, _ = proj.shape
    t = ATTN_TILE
    tq = DIFF_Q_TILES * t
    lambda_init = 0.8 - 0.6 * math.exp(-0.3 * layer_idx)
    return pl.pallas_call(
        functools.partial(_diff_attn_kernel, lambda_init=lambda_init),
        out_shape=jax.ShapeDtypeStruct((bsz, s, H_A * DV_A), BF16),
        grid=(bsz, H_A, s // tq),
        in_specs=[
            pl.BlockSpec((1, tq, LANES), lambda b, h, i: (b, i, h)),
            pl.BlockSpec((1, s, LANES), lambda b, h, i: (b, 0, H_A + h)),
            pl.BlockSpec((1, s, LANES), lambda b, h, i: (b, 0, 2 * H_A + h)),
            pl.BlockSpec((1, 4, 2 * t, t), lambda b, h, i: (h, 0, 0, 0)),
            pl.BlockSpec((4, DH_A), lambda b, h, i: (0, 0)),
            pl.BlockSpec((1, DV_A), lambda b, h, i: (0, 0)),
        ],
        out_specs=pl.BlockSpec((1, tq, DV_A), lambda b, h, i: (b, i, h)),
        compiler_params=pltpu.CompilerParams(
            dimension_semantics=("arbitrary", "arbitrary", "arbitrary"),
            vmem_limit_bytes=_vmem_limit(2 * s * LANES * 2 + 4 * 2 * t * t * 4 + 4 * tq * LANES,
                                         4 * tq * DIFF_KEY_TILES * t * 4)),
        name="diff_attention",
    )(proj, proj, proj, bias_tiles, lam_p, subln_g.reshape(1, DV_A))


def _gla_kernel(q_ref, k_ref, v_ref, r_ref, bg_ref, wgk_ref, bgk_ref, g_ref, o_ref, st_ref):
    c = CHUNK
    ts = q_ref.shape[1]
    n_chunks = ts // c

    @pl.when(pl.program_id(1) == 0)
    def _():
        st_ref[...] = jnp.zeros_like(st_ref)

    row = lax.broadcasted_iota(jnp.int32, (ts, ts), 0)
    col = lax.broadcasted_iota(jnp.int32, (ts, ts), 1)
    same_chunk_before = ((row // c) == (col // c)) & (col <= row)
    tri = jnp.where(same_chunk_before, 1.0, 0.0).astype(BF16)
    causal = (lax.broadcasted_iota(jnp.int32, (c, c), 1)
              <= lax.broadcasted_iota(jnp.int32, (c, c), 0))
    lane = lax.broadcasted_iota(jnp.int32, (ts, LANES), 1)
    wgk = wgk_ref[...]
    bgk = bgk_ref[...]
    g = g_ref[...]
    bg = bg_ref[0]

    las = [-_softplus(-(_dot3(bg, wgk[:, hp * LANES:(hp + 1) * LANES])
                        + bgk[:, hp * LANES:(hp + 1) * LANES])) / GATE_TAU
           for hp in range(H_B // 2)]
    bs = []
    for la in las:
        la_hi, la_lo = _split(la)
        bs.append(_dot(tri, la_hi) + _dot(tri, la_lo))

    heads = []
    for hp in range(H_B // 2):
        cols = slice(hp * LANES, (hp + 1) * LANES)
        b = bs[hp]
        b_last = jnp.concatenate(
            [jnp.broadcast_to(b[(ci + 1) * c - 1:(ci + 1) * c, :], (c, LANES))
             for ci in range(n_chunks)], axis=0)
        qf = q_ref[0, :, cols].astype(F32) * (DK_B ** -0.5)
        kf = k_ref[0, :, cols].astype(F32)
        q_dec = qf * jnp.exp(b)
        k_inv = (kf * jnp.exp(-b)).astype(BF16)
        k_dec = kf * jnp.exp(b_last - b)
        decays = [jnp.exp(b[(ci + 1) * c - 1:(ci + 1) * c, :]) for ci in range(n_chunks)]
        for hh in range(2):
            own = (lane >= hh * DK_B) & (lane < (hh + 1) * DK_B)
            heads.append((jnp.where(own, q_dec, 0.0).astype(BF16), k_inv,
                          jnp.where(own, k_dec, 0.0).astype(BF16), decays))

    def rows(x, ci):
        return x[ci * c:(ci + 1) * c]

    scores, increments = [], []
    for h, (qd, k_inv, kd, _) in enumerate(heads):
        vh = v_ref[0, :, h * DV_B:(h + 1) * DV_B]
        scores.append([jnp.where(causal, _nt(rows(qd, ci), rows(k_inv, ci)), 0.0).astype(BF16)
                       for ci in range(n_chunks)])
        increments.append([_tn(rows(vh, ci), rows(kd, ci)) for ci in range(n_chunks)])

    for h, (qd, _, _, decays) in enumerate(heads):
        vh = v_ref[0, :, h * DV_B:(h + 1) * DV_B]
        st = st_ref[h]
        states = []
        for ci in range(n_chunks):
            states.append(st.astype(BF16))
            st = st * decays[ci] + increments[h][ci]
        st_ref[h] = st
        o = jnp.concatenate(
            [_dot(scores[h][ci], rows(vh, ci)) + _nt(rows(qd, ci), states[ci])
             for ci in range(n_chunks)], axis=0)
        rr = r_ref[0, :, h * DV_B:(h + 1) * DV_B].astype(F32)
        o_ref[0, :, h * DV_B:(h + 1) * DV_B] = (_rms(o, g) * _silu(rr)).astype(o_ref.dtype)


def _gla(proj, gate, w_gk2, b_gk, gla_g):
    bsz, s, _ = proj.shape
    ts = GLA_STEP
    nk = H_B * DK_B
    nv = H_B * DV_B
    w_pad = jnp.zeros((LANES, nk), F32).at[:GATE_RANK].set(w_gk2.astype(F32))
    return pl.pallas_call(
        _gla_kernel,
        out_shape=jax.ShapeDtypeStruct((bsz, s, nv), BF16),
        grid=(bsz, s // ts),
        in_specs=[
            pl.BlockSpec((1, ts, nk), lambda b, i: (b, i, 1536 // nk)),
            pl.BlockSpec((1, ts, nk), lambda b, i: (b, i, 1792 // nk)),
            pl.BlockSpec((1, ts, nv), lambda b, i: (b, i, 2048 // nv)),
            pl.BlockSpec((1, ts, nv), lambda b, i: (b, i, 2560 // nv)),
            pl.BlockSpec((1, ts, LANES), lambda b, i: (b, i, 0)),
            pl.BlockSpec((LANES, nk), lambda b, i: (0, 0)),
            pl.BlockSpec((1, nk), lambda b, i: (0, 0)),
            pl.BlockSpec((1, DV_B), lambda b, i: (0, 0)),
        ],
        out_specs=pl.BlockSpec((1, ts, nv), lambda b, i: (b, i, 0)),
        scratch_shapes=[pltpu.VMEM((H_B, DV_B, LANES), F32)],
        compiler_params=pltpu.CompilerParams(
            dimension_semantics=("arbitrary", "arbitrary"),
            vmem_limit_bytes=_vmem_limit(ts * (2 * nk + 3 * nv) * 2 + ts * LANES * 4,
                                         H_B * DV_B * LANES * 4)),
        name="gated_linear_attention",
    )(proj, proj, proj, proj, gate, w_pad, b_gk.reshape(1, nk), gla_g.reshape(1, DV_B))


def _stick_kernel(q_ref, k_ref, v_ref, o_ref, run_ref, acc_ref):
    t = ATTN_TILE
    chains = STICK_PAIRS
    i = pl.program_id(2)
    lane = lax.broadcasted_iota(jnp.int32, (t, LANES), 1)
    scale = jnp.asarray(DH_C ** -0.5, BF16)
    qqs = []
    for c in range(chains):
        q = q_ref[0, :, c * LANES:(c + 1) * LANES]
        zero = jnp.zeros_like(q)
        qqs.append(jnp.concatenate([jnp.where(lane < DH_C, q, zero),
                                    jnp.where(lane >= DH_C, q, zero)], axis=0) * scale)
    row = lax.broadcasted_iota(jnp.int32, (t, t), 0)
    col = lax.broadcasted_iota(jnp.int32, (t, t), 1)
    later = jnp.where(row > col, 1.0, 0.0).astype(BF16)
    below = (lax.broadcasted_iota(jnp.int32, (2 * t, t), 1)
             < lax.broadcasted_iota(jnp.int32, (2 * t, t), 0) % t)

    def logits(j):
        r0 = pl.multiple_of(jnp.maximum(j, 0) * t, t)
        return [_nt(qqs[c], k_ref[0, pl.ds(r0, t), c * LANES:(c + 1) * LANES])
                for c in range(chains)]

    def tiles(first, zs, runs, accs, diagonal_first):
        runs, accs = list(runs), list(accs)
        mids = []
        for u in range(len(zs)):
            for c in range(chains):
                z = zs[u][c]
                sp = _softplus(z)
                log_1m = -sp
                masked = diagonal_first and u == 0
                if masked:
                    log_1m = jnp.where(below, log_1m, 0.0)
                hi, lo = _split(log_1m)
                within = _dot(hi, later) + _dot(lo, later)
                mids.append((u, c, z - sp, log_1m, within, masked))
        for u, c, log_b, log_1m, within, masked in mids:
            r0 = pl.multiple_of((first - u) * t, t)
            vb = v_ref[0, pl.ds(r0, t), c * LANES:(c + 1) * LANES]
            a = jnp.exp(log_b + (within + runs[c]))
            if masked:
                a = jnp.where(below, a, 0.0)
            ab = a.astype(BF16)
            zero_v = jnp.zeros_like(vb)
            a2 = jnp.concatenate([ab[:t], ab[t:]], axis=1)
            v2 = jnp.concatenate([jnp.where(lane < DH_C, vb, zero_v),
                                  jnp.where(lane >= DH_C, vb, zero_v)], axis=0)
            accs[c] = accs[c] + _dot(a2, v2)
            runs[c] = runs[c] + jnp.sum(log_1m, axis=-1, keepdims=True)
        return runs, accs

    def first_block(n_tiles):
        zs = [logits(i - u) for u in range(n_tiles)]
        runs, accs = tiles(i, zs, [jnp.zeros((2 * t, 1), F32)] * chains,
                           [jnp.zeros((t, LANES), F32)] * chains, True)
        for c in range(chains):
            run_ref[c] = runs[c]
            acc_ref[c] = accs[c]

    ahead = STICK_AHEAD

    @pl.when(i >= ahead)
    def _():
        first_block(1 + ahead)

    @pl.when(i < ahead)
    def _():
        first_block(1)

    n_left = i
    done0 = jnp.where(i >= ahead, ahead, 0)

    def go_on(done, runs):
        top = runs[0]
        for c in range(1, chains):
            top = jnp.maximum(top, runs[c])
        return ((done < n_left) & (jnp.max(top) > EXP_ZERO_BELOW)).astype(jnp.int32)

    def body(carry):
        done, _, runs, accs = carry
        first = n_left - 1 - done
        runs, accs = tiles(first, [logits(first)], runs, accs, False)
        done = done + 1
        return done, go_on(done, runs), tuple(runs), tuple(accs)

    runs = tuple(run_ref[c] for c in range(chains))
    accs = tuple(acc_ref[c] for c in range(chains))
    carry = (done0, go_on(done0, runs), runs, accs)
    accs = lax.while_loop(lambda carry: carry[1] > 0, body, carry)[3]
    for c in range(chains):
        o_ref[0, :, c * LANES:(c + 1) * LANES] = accs[c].astype(o_ref.dtype)


def _stick_breaking(proj):
    bsz, s, _ = proj.shape
    t = ATTN_TILE
    width = STICK_PAIRS * LANES
    n_blocks = H_C * DH_C // width
    return pl.pallas_call(
        _stick_kernel,
        out_shape=jax.ShapeDtypeStruct((bsz, s, H_C * DH_C), BF16),
        grid=(bsz, n_blocks, s // t),
        in_specs=[
            pl.BlockSpec((1, t, width), lambda b, h, i: (b, i, h)),
            pl.BlockSpec((1, s, width), lambda b, h, i: (b, 0, n_blocks + h)),
            pl.BlockSpec((1, s, width), lambda b, h, i: (b, 0, 2 * n_blocks + h)),
        ],
        out_specs=pl.BlockSpec((1, t, width), lambda b, h, i: (b, i, h)),
        scratch_shapes=[pltpu.VMEM((STICK_PAIRS, 2 * t, 1), F32),
                        pltpu.VMEM((STICK_PAIRS, t, LANES), F32)],
        compiler_params=pltpu.CompilerParams(
            dimension_semantics=("arbitrary", "arbitrary", "arbitrary"),
            vmem_limit_bytes=_vmem_limit(2 * s * width * 2 + 4 * t * width,
                                         STICK_PAIRS * 3 * t * LANES * 4)),
        name="stick_breaking_attention",
    )(proj, proj, proj)


def _route(logits):
    lane_i = lax.broadcasted_iota(jnp.int32, logits.shape, 1)
    lane = lane_i.astype(F32)
    group_of_lane = (lane_i // EXPERTS_PER_GROUP).astype(F32)
    big = jnp.float32(1 << 20)
    neg = jnp.float32(-jnp.inf)
    is_group = (lane_i >= N_EXPERTS) & (lane_i < N_EXPERTS + N_GROUPS)
    gl = jnp.where(is_group, logits, neg)
    gmax = jnp.max(gl, axis=-1, keepdims=True)
    g_val = 1.0 / jnp.sum(jnp.exp(gl - gmax), axis=-1, keepdims=True)
    g_sel = jnp.min(jnp.where(gl == gmax, lane - N_EXPERTS, big), axis=-1, keepdims=True)
    in_group = (lane_i < N_EXPERTS) & (group_of_lane == g_sel)
    el = jnp.where(in_group, logits, neg)
    emax = jnp.max(el, axis=-1, keepdims=True)
    esum = jnp.sum(jnp.exp(el - emax), axis=-1, keepdims=True)
    i1 = jnp.min(jnp.where(el == emax, lane, big), axis=-1, keepdims=True)
    el2 = jnp.where(lane == i1, neg, el)
    emax2 = jnp.max(el2, axis=-1, keepdims=True)
    i2 = jnp.min(jnp.where(el2 == emax2, lane, big), axis=-1, keepdims=True)
    v1 = 1.0 / esum
    v2 = jnp.exp(emax2 - emax) / esum
    tot = v1 + v2
    w1 = g_val * (v1 / tot)
    w2 = g_val * (v2 / tot)
    out = jnp.where(lane_i == 0, i1, 0.0)
    out = jnp.where(lane_i == 1, i2, out)
    out = jnp.where(lane_i == 2, w1, out)
    out = jnp.where(lane_i == 3, w2, out)
    return out, (lane == i1), (lane == i2)


def _pack_halves(h):
    half = h.shape[1] // 2
    lo = pltpu.bitcast(h[:, :half].astype(BF16).astype(F32), jnp.uint32)
    hi = pltpu.bitcast(h[:, half:].astype(BF16).astype(F32), jnp.uint32)
    return lax.shift_right_logical(lo, jnp.uint32(16)) | hi


def _unpack_halves(p):
    lo = pltpu.bitcast(lax.shift_left(p, jnp.uint32(16)), F32).astype(BF16)
    hi = pltpu.bitcast(p & jnp.uint32(0xFFFF0000), F32).astype(BF16)
    return lo, hi


def _out_kernel(*refs, n_o):
    o_refs = refs[:n_o]
    w_refs = refs[n_o:2 * n_o]
    x_ref, mod_ref, g_ref, wr_ref, br_ref, xo_ref = refs[2 * n_o:2 * n_o + 6]
    h_refs = refs[2 * n_o + 6:2 * n_o + 6 + MOE_PIECES]
    r_ref, cnt_ref, run_ref = refs[2 * n_o + 6 + MOE_PIECES:]

    @pl.when((pl.program_id(0) == 0) & (pl.program_id(1) == 0))
    def _():
        run_ref[...] = jnp.zeros_like(run_ref)

    m = _dot(o_refs[0][0], w_refs[0][...])
    for a in range(1, n_o):
        m = m + _dot(o_refs[a][0], w_refs[a][...])
    x = x_ref[0] + mod_ref[0, 2:3, :] * m
    xo_ref[0] = x
    h = _rms(x, g_ref[...]) * (1.0 + mod_ref[0, 4:5, :]) + mod_ref[0, 3:4, :]
    piece = h.shape[1] // MOE_PIECES
    for k in range(MOE_PIECES):
        h_refs[k][0] = _pack_halves(h[:, k * piece:(k + 1) * piece])
    route, hit1, hit2 = _route(_dot3(h, wr_ref[...]) + br_ref[...])
    tm = route.shape[0]
    picks = jnp.where(hit1 | hit2, 1.0, 0.0)
    row = lax.broadcasted_iota(jnp.int32, (tm, tm), 0)
    col = lax.broadcasted_iota(jnp.int32, (tm, tm), 1)
    before = jnp.where(col < row, 1.0, 0.0).astype(BF16)
    earlier = _dot(before, picks.astype(BF16)) + run_ref[...]
    rank1 = jnp.sum(jnp.where(hit1, earlier, 0.0), axis=-1, keepdims=True)
    rank2 = jnp.sum(jnp.where(hit2, earlier, 0.0), axis=-1, keepdims=True)
    lane_i = lax.broadcasted_iota(jnp.int32, route.shape, 1)
    route = jnp.where(lane_i == 4, rank1, route)
    route = jnp.where(lane_i == 5, rank2, route)
    r_ref[0] = route
    run_ref[...] = run_ref[...] + jnp.sum(picks, axis=0, keepdims=True)
    cnt_ref[...] = run_ref[...]


def _out_projection(o_list, w_list, x, mod, g, w_router, b_router):
    bsz, s, d = x.shape
    tm = TOKEN_TILE
    n_o = len(o_list)
    packed = d // (2 * MOE_PIECES)
    tok = lambda b, i: (b, i, 0)
    const2 = lambda b, i: (0, 0)
    in_specs = ([pl.BlockSpec((1, tm, o.shape[2]), tok) for o in o_list]
                + [pl.BlockSpec(w.shape, const2) for w in w_list]
                + [pl.BlockSpec((1, tm, d), tok), pl.BlockSpec((1, 6, d), lambda b, i: (b, 0, 0)),
                   pl.BlockSpec((1, d), const2), pl.BlockSpec((d, LANES), const2),
                   pl.BlockSpec((1, LANES), const2)])
    block_bytes = (sum(tm * o.shape[2] * 2 for o in o_list) + sum(w.size * 2 for w in w_list)
                   + 2 * tm * d * 4 + tm * d * 2 + d * LANES * 4 + tm * LANES * 4)
    return pl.pallas_call(
        functools.partial(_out_kernel, n_o=n_o),
        out_shape=([jax.ShapeDtypeStruct((bsz, s, d), F32)]
                   + [jax.ShapeDtypeStruct((bsz, s, packed), jnp.uint32)] * MOE_PIECES
                   + [jax.ShapeDtypeStruct((bsz, s, LANES), F32),
                      jax.ShapeDtypeStruct((1, LANES), F32)]),
        grid=(bsz, s // tm),
        in_specs=in_specs,
        out_specs=([pl.BlockSpec((1, tm, d), tok)]
                   + [pl.BlockSpec((1, tm, packed), tok)] * MOE_PIECES
                   + [pl.BlockSpec((1, tm, LANES), tok), pl.BlockSpec((1, LANES), const2)]),
        scratch_shapes=[pltpu.VMEM((1, LANES), F32)],
        compiler_params=pltpu.CompilerParams(
            dimension_semantics=("arbitrary", "arbitrary"),
            vmem_limit_bytes=_vmem_limit(block_bytes, tm * d * 12 + tm * tm * 4)),
        name="out_projection_norm_router",
    )(*o_list, *w_list, x, mod, g.reshape(1, d), w_router, b_router)


def _expert_kernel(te_ref, nu_ref, *refs):
    x_refs = refs[:MOE_PIECES]
    wg_ref, wu_ref, wd_ref, o_ref, wgb_ref, wub_ref, wdb_ref = refs[MOE_PIECES:]
    t = pl.program_id(0)
    prev = te_ref[jnp.maximum(t - 1, 0)]

    @pl.when((t == 0) | (te_ref[t] != prev))
    def _():
        wgb_ref[...] = wg_ref[0, 0].astype(BF16)
        wub_ref[...] = wu_ref[0, 0].astype(BF16)
        wdb_ref[...] = wd_ref[0, 0].astype(BF16)

    @pl.when(t < nu_ref[0])
    def _():
        cols = []
        for x_ref in x_refs:
            cols += list(_unpack_halves(x_ref[...]))
        w = cols[0].shape[1]
        gt = _dot(cols[0], wgb_ref[:w, :])
        up = _dot(cols[0], wub_ref[:w, :])
        for k in range(1, len(cols)):
            gt = gt + _dot(cols[k], wgb_ref[k * w:(k + 1) * w, :])
            up = up + _dot(cols[k], wub_ref[k * w:(k + 1) * w, :])
        o_ref[...] = _dot((_silu(gt) * up).astype(BF16), wdb_ref[...]).astype(o_ref.dtype)

    @pl.when(t >= nu_ref[0])
    def _():
        o_ref[...] = jnp.zeros_like(o_ref)


def _grouped_experts(xs, tile_expert, n_used, w_gate, w_up, w_down, layer):
    p, packed = xs[0].shape
    d = w_gate.shape[2]
    tm = EXPERT_TILE
    f = w_gate.shape[3]
    grid_spec = pltpu.PrefetchScalarGridSpec(
        num_scalar_prefetch=2,
        grid=(p // tm,),
        in_specs=[pl.BlockSpec((tm, packed), lambda t, te, nu: (t, 0))] * MOE_PIECES + [
            pl.BlockSpec((1, 1, d, f), lambda t, te, nu: (layer, te[t], 0, 0)),
            pl.BlockSpec((1, 1, d, f), lambda t, te, nu: (layer, te[t], 0, 0)),
            pl.BlockSpec((1, 1, f, d), lambda t, te, nu: (layer, te[t], 0, 0)),
        ],
        out_specs=pl.BlockSpec((tm, d), lambda t, te, nu: (t, 0)),
        scratch_shapes=[pltpu.VMEM((d, f), BF16), pltpu.VMEM((d, f), BF16),
                        pltpu.VMEM((f, d), BF16)],
    )
    block_bytes = tm * d * 2 + 3 * d * f * 4 + tm * d * 2
    return pl.pallas_call(
        _expert_kernel,
        out_shape=jax.ShapeDtypeStruct((p, d), BF16),
        grid_spec=grid_spec,
        compiler_params=pltpu.CompilerParams(
            dimension_semantics=("arbitrary",),
            vmem_limit_bytes=_vmem_limit(block_bytes, 3 * d * f * 2 + tm * f * 16)),
        name="grouped_swiglu_experts",
    )(tile_expert, n_used, *xs, w_gate, w_up, w_down)


def _pos_kernel(route_ref, starts_ref, pos_ref):
    rt = jnp.transpose(route_ref[...])
    starts = starts_ref[...]
    ids = lax.broadcasted_iota(jnp.int32, (N_EXPERTS, rt.shape[1]), 0).astype(F32)
    for c in range(2):
        start_of = jnp.sum(jnp.where(ids == rt[c:c + 1, :], starts, 0.0), axis=0, keepdims=True)
        pos_ref[c:c + 1, :] = (start_of + rt[4 + c:5 + c, :]).astype(jnp.int32)


def _row_positions(route, starts):
    n_tok = route.shape[0]
    tm = TOKEN_TILE
    return pl.pallas_call(
        _pos_kernel,
        out_shape=jax.ShapeDtypeStruct((2, n_tok), jnp.int32),
        grid=(n_tok // tm,),
        in_specs=[pl.BlockSpec((tm, LANES), lambda i: (i, 0)),
                  pl.BlockSpec((N_EXPERTS, 1), lambda i: (0, 0))],
        out_specs=pl.BlockSpec((2, tm), lambda i: (0, i)),
        compiler_params=pltpu.CompilerParams(
            dimension_semantics=("arbitrary",),
            vmem_limit_bytes=_vmem_limit(tm * LANES * 4 + N_EXPERTS * LANES * 4 + 8 * tm * 4)),
        name="expert_row_positions",
    )(route, starts.astype(F32).reshape(N_EXPERTS, 1))


def _scatter_rows(rows, pos, n_out):
    n_tok, d = rows.shape
    n_idx = pos.shape[0]
    w = SC_SCATTER_WINDOW
    per_pass = n_tok // w
    mesh = plsc.VectorSubcoreMesh(core_axis_name="core", subcore_axis_name="subcore")

    @pl.kernel(out_type=jax.ShapeDtypeStruct((n_out, d), rows.dtype), mesh=mesh, scratch_types=[])
    def scatter_kernel(x_hbm, i_hbm, o_hbm):
        def body(x_vmem, i_vmem):
            pltpu.sync_copy(x_vmem, o_hbm.at[i_vmem.at[0]])

        pltpu.emit_pipeline(
            body,
            grid=(n_idx // w,),
            in_specs=[pl.BlockSpec((w, d), lambda i: (i % per_pass, 0)),
                      pl.BlockSpec((1, w), lambda i: (0, i))],
            out_specs=[],
            core_axis_name=("core", "subcore"),
            dimension_semantics=(pltpu.PARALLEL,),
        )(x_hbm, i_hbm)

    return scatter_kernel(rows, pos.reshape(1, n_idx))


def _dispatch(route, counts, n_tok):
    tm = EXPERT_TILE
    counts = counts[:N_EXPERTS].astype(jnp.int32)
    padded = ((counts + tm - 1) // tm) * tm
    ends = jnp.cumsum(padded)
    starts = ends - padded
    pos = _row_positions(route, starts).reshape(2 * n_tok)
    p = 2 * n_tok + N_EXPERTS * tm
    tile_start = jnp.arange(p // tm, dtype=jnp.int32) * tm
    tile_expert = jnp.minimum(jnp.sum(ends[None, :] <= tile_start[:, None], axis=1),
                              N_EXPERTS - 1).astype(jnp.int32)
    n_used = (ends[-1] // tm).astype(jnp.int32).reshape(1)
    return p, tile_expert, n_used, pos


def _final_kernel(x_ref, y_ref, route_ref, mod_ref, g_ref, o_ref):
    o_ref[0] = _rms(_expert_residual(x_ref[0], y_ref, route_ref, mod_ref), g_ref[...])


def _final_norm(x, y, route, mod, g):
    bsz, s, d = x.shape
    tm = TOKEN_TILE
    tok = lambda b, i: (b, i, 0)
    return pl.pallas_call(
        _final_kernel,
        out_shape=jax.ShapeDtypeStruct((bsz, s, d), F32),
        grid=(bsz, s // tm),
        in_specs=[pl.BlockSpec((1, tm, d), tok),
                  pl.BlockSpec((2, 1, tm, d), lambda b, i: (0, b, i, 0)),
                  pl.BlockSpec((1, tm, LANES), tok),
                  pl.BlockSpec((1, 6, d), lambda b, i: (b, 0, 0)),
                  pl.BlockSpec((1, d), lambda b, i: (0, 0))],
        out_specs=pl.BlockSpec((1, tm, d), tok),
        compiler_params=pltpu.CompilerParams(
            dimension_semantics=("arbitrary", "arbitrary"),
            vmem_limit_bytes=_vmem_limit(4 * tm * d * 4 + tm * LANES * 4)),
        name="residual_final_norm",
    )(x, y, route, mod, g.reshape(1, d))


def kernel(x, c, w_ada, b_ada, norm_mix, norm_ffn, norm_final, rel_bias, even_w_in, even_lambda, even_subln, even_w_gk2, even_b_gk, even_gla_norm, even_w_out, odd_w_in, odd_w_out, router_group_w, router_group_b, router_expert_w, router_expert_b, expert_w_gate, expert_w_up, expert_w_down):
    bsz, s, d = x.shape
    n_tok = bsz * s
    assert s % TOKEN_TILE == 0 and s % GLA_STEP == 0
    assert s % (DIFF_KEY_TILES * ATTN_TILE) == 0 and s % (DIFF_Q_TILES * ATTN_TILE) == 0
    mod_all = _modulation(c, w_ada, b_ada).reshape(DEPTH, bsz, 6, d)
    bias_tiles = _diff_bias_tiles(rel_bias)
    n_even_main = 3072
    y = route = prev_mod = None
    for l in range(DEPTH):
        mod = mod_all[l]
        i = l // 2
        if l % 2 == 0:
            w_in = even_w_in[i]
            w_main = w_in[:, :n_even_main].astype(BF16)
            w_gate = jnp.zeros((d, LANES), BF16).at[:, :GATE_RANK].set(
                w_in[:, n_even_main:].astype(BF16))
            x, proj, gate = _in_projection(x, y, route, prev_mod, norm_mix[l], mod, w_main, w_gate)
            oa = _diff_attention(proj, bias_tiles, even_lambda[i], even_subln[i], l)
            ob = _gla(proj, gate, even_w_gk2[i], even_b_gk[i], even_gla_norm[i])
            w_out = even_w_out[i].astype(BF16)
            o_list = [oa, ob]
            w_list = [w_out[:H_A * DV_A], w_out[H_A * DV_A:]]
        else:
            x, proj, _ = _in_projection(x, y, route, prev_mod, norm_mix[l], mod,
                                        odd_w_in[i].astype(BF16), None)
            o_list = [_stick_breaking(proj)]
            w_list = [odd_w_out[i].astype(BF16)]
        w_router = jnp.zeros((d, LANES), F32)
        w_router = w_router.at[:, :N_EXPERTS].set(router_expert_w[l].astype(F32))
        w_router = w_router.at[:, N_EXPERTS:N_EXPERTS + N_GROUPS].set(router_group_w[l].astype(F32))
        b_router = jnp.zeros((1, LANES), F32)
        b_router = b_router.at[0, :N_EXPERTS].set(router_expert_b[l].astype(F32))
        b_router = b_router.at[0, N_EXPERTS:N_EXPERTS + N_GROUPS].set(router_group_b[l].astype(F32))
        outs = _out_projection(o_list, w_list, x, mod, norm_ffn[l], w_router, b_router)
        x, pieces, route, counts = outs[0], outs[1:1 + MOE_PIECES], outs[-2], outs[-1]
        n_rows, tile_expert, n_used, pos = _dispatch(route.reshape(n_tok, LANES), counts[0], n_tok)
        xs = [_scatter_rows(piece.reshape(n_tok, piece.shape[2]), pos, n_rows) for piece in pieces]
        out = _grouped_experts(xs, tile_expert, n_used, expert_w_gate, expert_w_up,
                               expert_w_down, l)
        y = out.at[pos].get(mode="promise_in_bounds").reshape(2, bsz, s, d)
        prev_mod = mod
    return _final_norm(x, y, route, prev_mod, norm_final)
```

```python
import functools
import math

import numpy as np
import jax
import jax.numpy as jnp
from jax import lax
from jax.experimental import pallas as pl
from jax.experimental.pallas import tpu as pltpu
from jax.experimental.pallas import tpu_sc as plsc

F32 = jnp.float32
BF16 = jnp.bfloat16

D_MODEL = 1024
DEPTH = 4
CHUNK = 64
EPS = 1e-6
H_A, DH_A, DV_A = 4, 64, 128
H_B, DK_B, DV_B = 4, 64, 128
GATE_RANK = 16
GATE_TAU = 16.0
H_C, DH_C = 16, 64
N_BUCKETS = 32
MAX_DISTANCE = 128
N_GROUPS = 4
EXPERTS_PER_GROUP = 8
N_EXPERTS = N_GROUPS * EXPERTS_PER_GROUP
D_EXPERT = 512
MIX = 1024

LANES = 128
V7X_VMEM_BYTES = 64 * 1024 * 1024
NEG_BIG = -1e30
EXP_ZERO_BELOW = -104.0

TOKEN_TILE = 512
ATTN_TILE = 128
STICK_PAIRS = 8
STICK_AHEAD = 2
DIFF_KEY_TILES = 8
DIFF_Q_TILES = 8
GLA_STEP = 256
EXPERT_TILE = 512
SC_SCATTER_WINDOW = 128
MOE_PIECES = 2


def _vmem_limit(block_bytes, scratch_bytes=0):
    est = 2 * block_bytes + scratch_bytes + 16 * 1024 * 1024
    return int(min(est, V7X_VMEM_BYTES - 8 * 1024 * 1024))


def _nt(a, b):
    return lax.dot_general(a, b, (((1,), (1,)), ((), ())), preferred_element_type=F32)


def _tn(a, b):
    return lax.dot_general(a, b, (((0,), (0,)), ((), ())), preferred_element_type=F32)


def _dot(a, b):
    return jnp.dot(a, b, preferred_element_type=F32)


def _split(x):
    hi = x.astype(BF16)
    lo = (x - hi.astype(F32)).astype(BF16)
    return hi, lo


def _dot3(a, b):
    ah, al = _split(a)
    bh, bl = _split(b)
    return _dot(ah, bh) + _dot(al, bh) + _dot(ah, bl)


def _dot3_narrow(a, b):
    ah, al = _split(a)
    bh, bl = _split(b)
    n = b.shape[1]
    both = _dot(ah, jnp.concatenate([bh, bl], axis=1))
    return both[:, :n] + both[:, n:] + _dot(al, bh)


def _rms(x, g):
    return x * lax.rsqrt(jnp.mean(x * x, axis=-1, keepdims=True) + EPS) * g


def _softplus(z):
    return jnp.maximum(z, 0.0) + jnp.log(1.0 + jnp.exp(-jnp.abs(z)))


def _silu(x):
    return x / (1.0 + jnp.exp(-x))


def _mod_kernel(c_ref, w_ref, b_ref, o_ref):
    c = c_ref[...]
    o_ref[0] = _dot3(_silu(c), w_ref[0]) + b_ref[0]


def _modulation(c, w_ada, b_ada):
    depth, d, n = w_ada.shape
    bsz = c.shape[0]
    tn = 1536
    return pl.pallas_call(
        _mod_kernel,
        out_shape=jax.ShapeDtypeStruct((depth, bsz, n), F32),
        grid=(depth, n // tn),
        in_specs=[
            pl.BlockSpec((bsz, d), lambda l, j: (0, 0)),
            pl.BlockSpec((1, d, tn), lambda l, j: (l, 0, j)),
            pl.BlockSpec((1, 1, tn), lambda l, j: (l, 0, j)),
        ],
        out_specs=pl.BlockSpec((1, bsz, tn), lambda l, j: (l, 0, j)),
        compiler_params=pltpu.CompilerParams(
            dimension_semantics=("arbitrary", "arbitrary"),
            vmem_limit_bytes=_vmem_limit(d * tn * 4 + bsz * (d + tn) * 4)),
        name="adaln_modulation",
    )(c, w_ada, b_ada.reshape(depth, 1, n))


def _expert_residual(x, y_ref, route_ref, mod_ref):
    route = route_ref[0]
    y = route[:, 2:3] * y_ref[0, 0].astype(F32) + route[:, 3:4] * y_ref[1, 0].astype(F32)
    return x + mod_ref[0, 5:6, :] * y


def _proj_kernel(*refs, has_res, n_main, has_gate):
    it = iter(refs)
    x_ref = next(it)
    if has_res:
        y_ref = next(it)
        route_ref = next(it)
        pmod_ref = next(it)
    g_ref = next(it)
    mod_ref = next(it)
    w_ref = next(it)
    wg_ref = next(it) if has_gate else None
    xo_ref = next(it) if has_res else None
    o_ref = next(it)
    og_ref = next(it) if has_gate else None

    x = x_ref[0]
    if has_res:
        x = _expert_residual(x, y_ref, route_ref, pmod_ref)
        xo_ref[0] = x
    h = _rms(x, g_ref[...]) * (1.0 + mod_ref[0, 1:2, :]) + mod_ref[0, 0:1, :]
    hb = h.astype(BF16)
    step = 512
    for n0 in range(0, n_main, step):
        o_ref[0, :, n0:n0 + step] = _dot(hb, w_ref[:, n0:n0 + step]).astype(BF16)
    if has_gate:
        og_ref[0] = _dot(hb, wg_ref[...])


def _in_projection(x, y, route, prev_mod, g, mod, w, w_gate):
    bsz, s, d = x.shape
    tm = TOKEN_TILE
    has_res = y is not None
    has_gate = w_gate is not None
    n_main = w.shape[1]
    tok = lambda b, i: (b, i, 0)
    per_b = lambda b, i: (b, 0, 0)
    const2 = lambda b, i: (0, 0)
    in_specs = [pl.BlockSpec((1, tm, d), tok)]
    args = [x]
    if has_res:
        in_specs += [pl.BlockSpec((2, 1, tm, d), lambda b, i: (0, b, i, 0)),
                     pl.BlockSpec((1, tm, LANES), tok), pl.BlockSpec((1, 6, d), per_b)]
        args += [y, route, prev_mod]
    in_specs += [pl.BlockSpec((1, d), const2), pl.BlockSpec((1, 6, d), per_b),
                 pl.BlockSpec((d, n_main), const2)]
    args += [g.reshape(1, d), mod, w]
    if has_gate:
        in_specs.append(pl.BlockSpec((d, LANES), const2))
        args.append(w_gate)
    out_shape, out_specs = [], []
    if has_res:
        out_shape.append(jax.ShapeDtypeStruct((bsz, s, d), F32))
        out_specs.append(pl.BlockSpec((1, tm, d), tok))
    out_shape.append(jax.ShapeDtypeStruct((bsz, s, n_main), BF16))
    out_specs.append(pl.BlockSpec((1, tm, n_main), tok))
    if has_gate:
        out_shape.append(jax.ShapeDtypeStruct((bsz, s, LANES), F32))
        out_specs.append(pl.BlockSpec((1, tm, LANES), tok))
    block_bytes = (tm * d * 4 * (4 if has_res else 1) + d * n_main * 2 + tm * n_main * 2
                   + d * LANES * 2 + 2 * tm * LANES * 4)
    outs = pl.pallas_call(
        functools.partial(_proj_kernel, has_res=has_res, n_main=n_main, has_gate=has_gate),
        out_shape=out_shape,
        grid=(bsz, s // tm),
        in_specs=in_specs,
        out_specs=out_specs,
        compiler_params=pltpu.CompilerParams(
            dimension_semantics=("arbitrary", "arbitrary"),
            vmem_limit_bytes=_vmem_limit(block_bytes, tm * d * 8)),
        name="norm_mod_in_projection",
    )(*args)
    outs = list(outs)
    x_new = outs.pop(0) if has_res else x
    proj = outs.pop(0)
    gate = outs.pop(0) if has_gate else None
    return x_new, proj, gate


def _t5_bucket(rel):
    nb = N_BUCKETS // 2
    max_exact = nb // 2
    ret = jnp.where(rel > 0, nb, 0)
    n = jnp.abs(rel)
    nf = jnp.maximum(n, 1).astype(F32)
    large = max_exact + (jnp.log(nf / max_exact) / math.log(MAX_DISTANCE / max_exact)
                         * (nb - max_exact)).astype(jnp.int32)
    large = jnp.minimum(large, nb - 1)
    return ret + jnp.where(n < max_exact, n, large)


def _diff_bias_tiles(rel_bias):
    t = ATTN_TILE
    assert t >= MAX_DISTANCE, "tiles two or more away must lie beyond the last bucket edge"
    qp = jnp.arange(t)[:, None]
    kp = jnp.arange(t)[None, :]
    tiles = []
    for off in (0, 1, 2):
        rel = (kp - off * t) - qp
        hit = _t5_bucket(rel)[None, :, :, None] == jnp.arange(N_BUCKETS)[None, None, None, :]
        b = jnp.sum(jnp.where(hit, jnp.transpose(rel_bias.astype(F32))[:, None, None, :], 0.0),
                    axis=-1)
        if off == 0:
            b = jnp.where((kp // CHUNK) <= (qp // CHUNK), b, NEG_BIG)
        tiles.append(b)
    tiles.append(jnp.full_like(tiles[0], NEG_BIG))
    tab = jnp.stack(tiles, axis=1)
    return jnp.concatenate([tab, tab], axis=2)


def _diff_attn_kernel(q_ref, k_ref, v_ref, bias_ref, lam_ref, g_ref, o_ref, *, lambda_init):
    t = ATTN_TILE
    chains = DIFF_Q_TILES
    i = pl.program_id(2)
    lane = lax.broadcasted_iota(jnp.int32, (t, LANES), 1)
    scale = jnp.asarray(DH_A ** -0.5, BF16)
    stacked = []
    for c in range(chains):
        q = q_ref[0, c * t:(c + 1) * t, :]
        zero = jnp.zeros_like(q)
        stacked += [jnp.where(lane < DH_A, q, zero), jnp.where(lane >= DH_A, q, zero)]
    qq = jnp.concatenate(stacked, axis=0) * scale

    group = DIFF_KEY_TILES
    span = group * t
    last_tile = i * chains + chains - 1

    def body(n, carry):
        r0 = pl.multiple_of(n * span, span)
        kb = k_ref[0, pl.ds(r0, span), :]
        vb = v_ref[0, pl.ds(r0, span), :]
        s_chain = [_nt(qq[c * 2 * t:(c + 1) * 2 * t], kb) for c in range(chains)]
        out = []
        for c in range(chains):
            m, l, acc = carry[c]

            def logits(u):
                back = (i * chains + c) - (n * group + u)
                which = jnp.where(back < 0, 3, jnp.minimum(back, 2))
                return s_chain[c][:, u * t:(u + 1) * t] + bias_ref[0, which]

            top = logits(0)
            for u in range(1, group):
                top = jnp.maximum(top, logits(u))
            m_new = jnp.maximum(m, jnp.max(top, axis=-1, keepdims=True))
            alpha = jnp.exp(m - m_new)
            ps = [jnp.exp(logits(u) - m_new) for u in range(group)]
            tot = ps[0]
            for u in range(1, group):
                tot = tot + ps[u]
            l = alpha * l + jnp.sum(tot, axis=-1, keepdims=True)
            p = jnp.concatenate([pu.astype(BF16) for pu in ps], axis=1)
            acc = alpha * acc + _dot(p, vb)
            out.append((m_new, l, acc))
        return tuple(out)

    init = tuple((jnp.full((2 * t, 1), NEG_BIG, F32), jnp.zeros((2 * t, 1), F32),
                  jnp.zeros((2 * t, DV_A), F32)) for _ in range(chains))
    final = lax.fori_loop(0, last_tile // group + 1, body, init)
    lp = lam_ref[...].astype(F32)
    lam = (jnp.exp(jnp.sum(lp[0:1] * lp[1:2], axis=-1, keepdims=True))
           - jnp.exp(jnp.sum(lp[2:3] * lp[3:4], axis=-1, keepdims=True)) + lambda_init)
    for c in range(chains):
        _, l, acc = final[c]
        o = acc / l
        w = o[:t] - lam * o[t:]
        o_ref[0, c * t:(c + 1) * t, :] = (_rms(w, g_ref[...]) * (1.0 - lambda_init)
                                          ).astype(o_ref.dtype)


def _diff_attention(proj, bias_tiles, lam_p, subln_g, layer_idx):
    bsz, s, _ = proj.shape
    t = ATTN_TILE
    tq = DIFF_Q_TILES * t
    lambda_init = 0.8 - 0.6 * math.exp(-0.3 * layer_idx)
    return pl.pallas_call(
        functools.partial(_diff_attn_kernel, lambda_init=lambda_init),
        out_shape=jax.ShapeDtypeStruct((bsz, s, H_A * DV_A), BF16),
        grid=(bsz, H_A, s // tq),
        in_specs=[
            pl.BlockSpec((1, tq, LANES), lambda b, h, i: (b, i, h)),
            pl.BlockSpec((1, s, LANES), lambda b, h, i: (b, 0, H_A + h)),
            pl.BlockSpec((1, s, LANES), lambda b, h, i: (b, 0, 2 * H_A + h)),
            pl.BlockSpec((1, 4, 2 * t, t), lambda b, h, i: (h, 0, 0, 0)),
            pl.BlockSpec((4, DH_A), lambda b, h, i: (0, 0)),
            pl.BlockSpec((1, DV_A), lambda b, h, i: (0, 0)),
        ],
        out_specs=pl.BlockSpec((1, tq, DV_A), lambda b, h, i: (b, i, h)),
        compiler_params=pltpu.CompilerParams(
            dimension_semantics=("arbitrary", "arbitrary", "arbitrary"),
            vmem_limit_bytes=_vmem_limit(2 * s * LANES * 2 + 4 * 2 * t * t * 4 + 4 * tq * LANES,
                                         4 * tq * DIFF_KEY_TILES * t * 4)),
        name="diff_attention",
    )(proj, proj, proj, bias_tiles, lam_p, subln_g.reshape(1, DV_A))


def _gla_kernel(q_ref, k_ref, v_ref, r_ref, bg_ref, wgk_ref, bgk_ref, g_ref, o_ref, st_ref):
    c = CHUNK
    ts = q_ref.shape[1]
    n_chunks = ts // c

    @pl.when(pl.program_id(1) == 0)
    def _():
        st_ref[...] = jnp.zeros_like(st_ref)

    row = lax.broadcasted_iota(jnp.int32, (ts, ts), 0)
    col = lax.broadcasted_iota(jnp.int32, (ts, ts), 1)
    same_chunk_before = ((row // c) == (col // c)) & (col <= row)
    tri = jnp.where(same_chunk_before, 1.0, 0.0).astype(BF16)
    causal = (lax.broadcasted_iota(jnp.int32, (c, c), 1)
              <= lax.broadcasted_iota(jnp.int32, (c, c), 0))
    lane = lax.broadcasted_iota(jnp.int32, (ts, LANES), 1)
    wgk = wgk_ref[...]
    bgk = bgk_ref[...]
    g = g_ref[...]
    bg = bg_ref[0]

    las = [-_softplus(-(_dot3(bg, wgk[:, hp * LANES:(hp + 1) * LANES])
                        + bgk[:, hp * LANES:(hp + 1) * LANES])) / GATE_TAU
           for hp in range(H_B // 2)]
    bs = []
    for la in las:
        la_hi, la_lo = _split(la)
        bs.append(_dot(tri, la_hi) + _dot(tri, la_lo))

    heads = []
    for hp in range(H_B // 2):
        cols = slice(hp * LANES, (hp + 1) * LANES)
        b = bs[hp]
        b_last = jnp.concatenate(
            [jnp.broadcast_to(b[(ci + 1) * c - 1:(ci + 1) * c, :], (c, LANES))
             for ci in range(n_chunks)], axis=0)
        qf = q_ref[0, :, cols].astype(F32) * (DK_B ** -0.5)
        kf = k_ref[0, :, cols].astype(F32)
        q_dec = qf * jnp.exp(b)
        k_inv = (kf * jnp.exp(-b)).astype(BF16)
        k_dec = kf * jnp.exp(b_last - b)
        decays = [jnp.exp(b[(ci + 1) * c - 1:(ci + 1) * c, :]) for ci in range(n_chunks)]
        for hh in range(2):
            own = (lane >= hh * DK_B) & (lane < (hh + 1) * DK_B)
            heads.append((jnp.where(own, q_dec, 0.0).astype(BF16), k_inv,
                          jnp.where(own, k_dec, 0.0).astype(BF16), decays))

    def rows(x, ci):
        return x[ci * c:(ci + 1) * c]

    scores, increments = [], []
    for h, (qd, k_inv, kd, _) in enumerate(heads):
        vh = v_ref[0, :, h * DV_B:(h + 1) * DV_B]
        scores.append([jnp.where(causal, _nt(rows(qd, ci), rows(k_inv, ci)), 0.0).astype(BF16)
                       for ci in range(n_chunks)])
        increments.append([_tn(rows(vh, ci), rows(kd, ci)) for ci in range(n_chunks)])

    for h, (qd, _, _, decays) in enumerate(heads):
        vh = v_ref[0, :, h * DV_B:(h + 1) * DV_B]
        st = st_ref[h]
        states = []
        for ci in range(n_chunks):
            states.append(st.astype(BF16))
            st = st * decays[ci] + increments[h][ci]
        st_ref[h] = st
        o = jnp.concatenate(
            [_dot(scores[h][ci], rows(vh, ci)) + _nt(rows(qd, ci), states[ci])
             for ci in range(n_chunks)], axis=0)
        rr = r_ref[0, :, h * DV_B:(h + 1) * DV_B].astype(F32)
        o_ref[0, :, h * DV_B:(h + 1) * DV_B] = (_rms(o, g) * _silu(rr)).astype(o_ref.dtype)


def _gla(proj, gate, w_gk2, b_gk, gla_g):
    bsz, s, _ = proj.shape
    ts = GLA_STEP
    nk = H_B * DK_B
    nv = H_B * DV_B
    w_pad = jnp.zeros((LANES, nk), F32).at[:GATE_RANK].set(w_gk2.astype(F32))
    return pl.pallas_call(
        _gla_kernel,
        out_shape=jax.ShapeDtypeStruct((bsz, s, nv), BF16),
        grid=(bsz, s // ts),
        in_specs=[
            pl.BlockSpec((1, ts, nk), lambda b, i: (b, i, 1536 // nk)),
            pl.BlockSpec((1, ts, nk), lambda b, i: (b, i, 1792 // nk)),
            pl.BlockSpec((1, ts, nv), lambda b, i: (b, i, 2048 // nv)),
            pl.BlockSpec((1, ts, nv), lambda b, i: (b, i, 2560 // nv)),
            pl.BlockSpec((1, ts, LANES), lambda b, i: (b, i, 0)),
            pl.BlockSpec((LANES, nk), lambda b, i: (0, 0)),
            pl.BlockSpec((1, nk), lambda b, i: (0, 0)),
            pl.BlockSpec((1, DV_B), lambda b, i: (0, 0)),
        ],
        out_specs=pl.BlockSpec((1, ts, nv), lambda b, i: (b, i, 0)),
        scratch_shapes=[pltpu.VMEM((H_B, DV_B, LANES), F32)],
        compiler_params=pltpu.CompilerParams(
            dimension_semantics=("arbitrary", "arbitrary"),
            vmem_limit_bytes=_vmem_limit(ts * (2 * nk + 3 * nv) * 2 + ts * LANES * 4,
                                         H_B * DV_B * LANES * 4)),
        name="gated_linear_attention",
    )(proj, proj, proj, proj, gate, w_pad, b_gk.reshape(1, nk), gla_g.reshape(1, DV_B))


def _stick_kernel(q_ref, k_ref, v_ref, o_ref, run_ref, acc_ref):
    t = ATTN_TILE
    chains = STICK_PAIRS
    i = pl.program_id(2)
    lane = lax.broadcasted_iota(jnp.int32, (t, LANES), 1)
    scale = jnp.asarray(DH_C ** -0.5, BF16)
    qqs = []
    for c in range(chains):
        q = q_ref[0, :, c * LANES:(c + 1) * LANES]
        zero = jnp.zeros_like(q)
        qqs.append(jnp.concatenate([jnp.where(lane < DH_C, q, zero),
                                    jnp.where(lane >= DH_C, q, zero)], axis=0) * scale)
    row = lax.broadcasted_iota(jnp.int32, (t, t), 0)
    col = lax.broadcasted_iota(jnp.int32, (t, t), 1)
    later = jnp.where(row > col, 1.0, 0.0).astype(BF16)
    below = (lax.broadcasted_iota(jnp.int32, (2 * t, t), 1)
             < lax.broadcasted_iota(jnp.int32, (2 * t, t), 0) % t)

    def logits(j):
        r0 = pl.multiple_of(jnp.maximum(j, 0) * t, t)
        return [_nt(qqs[c], k_ref[0, pl.ds(r0, t), c * LANES:(c + 1) * LANES])
                for c in range(chains)]

    def tiles(first, zs, runs, accs, diagonal_first):
        runs, accs = list(runs), list(accs)
        mids = []
        for u in range(len(zs)):
            for c in range(chains):
                z = zs[u][c]
                sp = _softplus(z)
                log_1m = -sp
                masked = diagonal_first and u == 0
                if masked:
                    log_1m = jnp.where(below, log_1m, 0.0)
                hi, lo = _split(log_1m)
                within = _dot(hi, later) + _dot(lo, later)
                mids.append((u, c, z - sp, log_1m, within, masked))
        for u, c, log_b, log_1m, within, masked in mids:
            r0 = pl.multiple_of((first - u) * t, t)
            vb = v_ref[0, pl.ds(r0, t), c * LANES:(c + 1) * LANES]
            a = jnp.exp(log_b + (within + runs[c]))
            if masked:
                a = jnp.where(below, a, 0.0)
            ab = a.astype(BF16)
            zero_v = jnp.zeros_like(vb)
            a2 = jnp.concatenate([ab[:t], ab[t:]], axis=1)
            v2 = jnp.concatenate([jnp.where(lane < DH_C, vb, zero_v),
                                  jnp.where(lane >= DH_C, vb, zero_v)], axis=0)
            accs[c] = accs[c] + _dot(a2, v2)
            runs[c] = runs[c] + jnp.sum(log_1m, axis=-1, keepdims=True)
        return runs, accs

    def first_block(n_tiles):
        zs = [logits(i - u) for u in range(n_tiles)]
        runs, accs = tiles(i, zs, [jnp.zeros((2 * t, 1), F32)] * chains,
                           [jnp.zeros((t, LANES), F32)] * chains, True)
        for c in range(chains):
            run_ref[c] = runs[c]
            acc_ref[c] = accs[c]

    ahead = STICK_AHEAD

    @pl.when(i >= ahead)
    def _():
        first_block(1 + ahead)

    @pl.when(i < ahead)
    def _():
        first_block(1)

    n_left = i
    done0 = jnp.where(i >= ahead, ahead, 0)

    def go_on(done, runs):
        top = runs[0]
        for c in range(1, chains):
            top = jnp.maximum(top, runs[c])
        return ((done < n_left) & (jnp.max(top) > EXP_ZERO_BELOW)).astype(jnp.int32)

    def body(carry):
        done, _, runs, accs = carry
        first = n_left - 1 - done
        runs, accs = tiles(first, [logits(first)], runs, accs, False)
        done = done + 1
        return done, go_on(done, runs), tuple(runs), tuple(accs)

    runs = tuple(run_ref[c] for c in range(chains))
    accs = tuple(acc_ref[c] for c in range(chains))
    carry = (done0, go_on(done0, runs), runs, accs)
    accs = lax.while_loop(lambda carry: carry[1] > 0, body, carry)[3]
    for c in range(chains):
        o_ref[0, :, c * LANES:(c + 1) * LANES] = accs[c].astype(o_ref.dtype)


def _stick_breaking(proj):
    bsz, s, _ = proj.shape
    t = ATTN_TILE
    width = STICK_PAIRS * LANES
    n_blocks = H_C * DH_C // width
    return pl.pallas_call(
        _stick_kernel,
        out_shape=jax.ShapeDtypeStruct((bsz, s, H_C * DH_C), BF16),
        grid=(bsz, n_blocks, s // t),
        in_specs=[
            pl.BlockSpec((1, t, width), lambda b, h, i: (b, i, h)),
            pl.BlockSpec((1, s, width), lambda b, h, i: (b, 0, n_blocks + h)),
            pl.BlockSpec((1, s, width), lambda b, h, i: (b, 0, 2 * n_blocks + h)),
        ],
        out_specs=pl.BlockSpec((1, t, width), lambda b, h, i: (b, i, h)),
        scratch_shapes=[pltpu.VMEM((STICK_PAIRS, 2 * t, 1), F32),
                        pltpu.VMEM((STICK_PAIRS, t, LANES), F32)],
        compiler_params=pltpu.CompilerParams(
            dimension_semantics=("arbitrary", "arbitrary", "arbitrary"),
            vmem_limit_bytes=_vmem_limit(2 * s * width * 2 + 4 * t * width,
                                         STICK_PAIRS * 3 * t * LANES * 4)),
        name="stick_breaking_attention",
    )(proj, proj, proj)


def _route(logits):
    lane_i = lax.broadcasted_iota(jnp.int32, logits.shape, 1)
    lane = lane_i.astype(F32)
    group_of_lane = (lane_i // EXPERTS_PER_GROUP).astype(F32)
    big = jnp.float32(1 << 20)
    neg = jnp.float32(-jnp.inf)
    is_group = (lane_i >= N_EXPERTS) & (lane_i < N_EXPERTS + N_GROUPS)
    gl = jnp.where(is_group, logits, neg)
    gmax = jnp.max(gl, axis=-1, keepdims=True)
    g_val = 1.0 / jnp.sum(jnp.exp(gl - gmax), axis=-1, keepdims=True)
    g_sel = jnp.min(jnp.where(gl == gmax, lane - N_EXPERTS, big), axis=-1, keepdims=True)
    in_group = (lane_i < N_EXPERTS) & (group_of_lane == g_sel)
    el = jnp.where(in_group, logits, neg)
    emax = jnp.max(el, axis=-1, keepdims=True)
    esum = jnp.sum(jnp.exp(el - emax), axis=-1, keepdims=True)
    i1 = jnp.min(jnp.where(el == emax, lane, big), axis=-1, keepdims=True)
    el2 = jnp.where(lane == i1, neg, el)
    emax2 = jnp.max(el2, axis=-1, keepdims=True)
    i2 = jnp.min(jnp.where(el2 == emax2, lane, big), axis=-1, keepdims=True)
    v1 = 1.0 / esum
    v2 = jnp.exp(emax2 - emax) / esum
    tot = v1 + v2
    w1 = g_val * (v1 / tot)
    w2 = g_val * (v2 / tot)
    out = jnp.where(lane_i == 0, i1, 0.0)
    out = jnp.where(lane_i == 1, i2, out)
    out = jnp.where(lane_i == 2, w1, out)
    out = jnp.where(lane_i == 3, w2, out)
    return out, (lane == i1), (lane == i2)


def _pack_halves(h):
    half = h.shape[1] // 2
    lo = pltpu.bitcast(h[:, :half].astype(BF16).astype(F32), jnp.uint32)
    hi = pltpu.bitcast(h[:, half:].astype(BF16).astype(F32), jnp.uint32)
    return lax.shift_right_logical(lo, jnp.uint32(16)) | hi


def _unpack_halves(p):
    lo = pltpu.bitcast(lax.shift_left(p, jnp.uint32(16)), F32).astype(BF16)
    hi = pltpu.bitcast(p & jnp.uint32(0xFFFF0000), F32).astype(BF16)
    return lo, hi


def _out_kernel(*refs, n_o):
    o_refs = refs[:n_o]
    w_refs = refs[n_o:2 * n_o]
    x_ref, mod_ref, g_ref, wr_ref, br_ref, xo_ref = refs[2 * n_o:2 * n_o + 6]
    h_refs = refs[2 * n_o + 6:2 * n_o + 6 + MOE_PIECES]
    r_ref, cnt_ref, run_ref = refs[2 * n_o + 6 + MOE_PIECES:]

    @pl.when((pl.program_id(0) == 0) & (pl.program_id(1) == 0))
    def _():
        run_ref[...] = jnp.zeros_like(run_ref)

    m = _dot(o_refs[0][0], w_refs[0][...])
    for a in range(1, n_o):
        m = m + _dot(o_refs[a][0], w_refs[a][...])
    x = x_ref[0] + mod_ref[0, 2:3, :] * m
    xo_ref[0] = x
    h = _rms(x, g_ref[...]) * (1.0 + mod_ref[0, 4:5, :]) + mod_ref[0, 3:4, :]
    piece = h.shape[1] // MOE_PIECES
    for k in range(MOE_PIECES):
        h_refs[k][0] = _pack_halves(h[:, k * piece:(k + 1) * piece])
    route, hit1, hit2 = _route(_dot3_narrow(h, wr_ref[...]) + br_ref[...])
    tm = route.shape[0]
    picks = jnp.where(hit1 | hit2, 1.0, 0.0)
    row = lax.broadcasted_iota(jnp.int32, (tm, tm), 0)
    col = lax.broadcasted_iota(jnp.int32, (tm, tm), 1)
    before = jnp.where(col < row, 1.0, 0.0).astype(BF16)
    earlier = _dot(before, picks.astype(BF16)) + run_ref[...]
    rank1 = jnp.sum(jnp.where(hit1, earlier, 0.0), axis=-1, keepdims=True)
    rank2 = jnp.sum(jnp.where(hit2, earlier, 0.0), axis=-1, keepdims=True)
    lane_i = lax.broadcasted_iota(jnp.int32, route.shape, 1)
    route = jnp.where(lane_i == 4, rank1, route)
    route = jnp.where(lane_i == 5, rank2, route)
    r_ref[0] = route
    run_ref[...] = run_ref[...] + jnp.sum(picks, axis=0, keepdims=True)
    cnt_ref[...] = run_ref[...]


def _out_projection(o_list, w_list, x, mod, g, w_router, b_router):
    bsz, s, d = x.shape
    tm = TOKEN_TILE
    n_o = len(o_list)
    packed = d // (2 * MOE_PIECES)
    tok = lambda b, i: (b, i, 0)
    const2 = lambda b, i: (0, 0)
    in_specs = ([pl.BlockSpec((1, tm, o.shape[2]), tok) for o in o_list]
                + [pl.BlockSpec(w.shape, const2) for w in w_list]
                + [pl.BlockSpec((1, tm, d), tok), pl.BlockSpec((1, 6, d), lambda b, i: (b, 0, 0)),
                   pl.BlockSpec((1, d), const2), pl.BlockSpec((d, LANES), const2),
                   pl.BlockSpec((1, LANES), const2)])
    block_bytes = (sum(tm * o.shape[2] * 2 for o in o_list) + sum(w.size * 2 for w in w_list)
                   + 2 * tm * d * 4 + tm * d * 2 + d * LANES * 4 + tm * LANES * 4)
    return pl.pallas_call(
        functools.partial(_out_kernel, n_o=n_o),
        out_shape=([jax.ShapeDtypeStruct((bsz, s, d), F32)]
                   + [jax.ShapeDtypeStruct((bsz, s, packed), jnp.uint32)] * MOE_PIECES
                   + [jax.ShapeDtypeStruct((bsz, s, LANES), F32),
                      jax.ShapeDtypeStruct((1, LANES), F32)]),
        grid=(bsz, s // tm),
        in_specs=in_specs,
        out_specs=([pl.BlockSpec((1, tm, d), tok)]
                   + [pl.BlockSpec((1, tm, packed), tok)] * MOE_PIECES
                   + [pl.BlockSpec((1, tm, LANES), tok), pl.BlockSpec((1, LANES), const2)]),
        scratch_shapes=[pltpu.VMEM((1, LANES), F32)],
        compiler_params=pltpu.CompilerParams(
            dimension_semantics=("arbitrary", "arbitrary"),
            vmem_limit_bytes=_vmem_limit(block_bytes, tm * d * 12 + tm * tm * 4)),
        name="out_projection_norm_router",
    )(*o_list, *w_list, x, mod, g.reshape(1, d), w_router, b_router)


def _expert_kernel(te_ref, nu_ref, *refs):
    x_refs = refs[:MOE_PIECES]
    wg_ref, wu_ref, wd_ref, o_ref, wgb_ref, wub_ref, wdb_ref = refs[MOE_PIECES:]
    t = pl.program_id(0)
    prev = te_ref[jnp.maximum(t - 1, 0)]

    @pl.when((t == 0) | (te_ref[t] != prev))
    def _():
        wgb_ref[...] = wg_ref[0, 0].astype(BF16)
        wub_ref[...] = wu_ref[0, 0].astype(BF16)
        wdb_ref[...] = wd_ref[0, 0].astype(BF16)

    @pl.when(t < nu_ref[0])
    def _():
        cols = []
        for x_ref in x_refs:
            cols += list(_unpack_halves(x_ref[...]))
        w = cols[0].shape[1]
        gt = _dot(cols[0], wgb_ref[:w, :])
        up = _dot(cols[0], wub_ref[:w, :])
        for k in range(1, len(cols)):
            gt = gt + _dot(cols[k], wgb_ref[k * w:(k + 1) * w, :])
            up = up + _dot(cols[k], wub_ref[k * w:(k + 1) * w, :])
        o_ref[...] = _dot((_silu(gt) * up).astype(BF16), wdb_ref[...]).astype(o_ref.dtype)

    @pl.when(t >= nu_ref[0])
    def _():
        o_ref[...] = jnp.zeros_like(o_ref)


def _grouped_experts(xs, tile_expert, n_used, w_gate, w_up, w_down, layer):
    p, packed = xs[0].shape
    d = w_gate.shape[2]
    tm = EXPERT_TILE
    f = w_gate.shape[3]
    grid_spec = pltpu.PrefetchScalarGridSpec(
        num_scalar_prefetch=2,
        grid=(p // tm,),
        in_specs=[pl.BlockSpec((tm, packed), lambda t, te, nu: (t, 0))] * MOE_PIECES + [
            pl.BlockSpec((1, 1, d, f), lambda t, te, nu: (layer, te[t], 0, 0)),
            pl.BlockSpec((1, 1, d, f), lambda t, te, nu: (layer, te[t], 0, 0)),
            pl.BlockSpec((1, 1, f, d), lambda t, te, nu: (layer, te[t], 0, 0)),
        ],
        out_specs=pl.BlockSpec((tm, d), lambda t, te, nu: (t, 0)),
        scratch_shapes=[pltpu.VMEM((d, f), BF16), pltpu.VMEM((d, f), BF16),
                        pltpu.VMEM((f, d), BF16)],
    )
    block_bytes = tm * d * 2 + 3 * d * f * 4 + tm * d * 2
    return pl.pallas_call(
        _expert_kernel,
        out_shape=jax.ShapeDtypeStruct((p, d), BF16),
        grid_spec=grid_spec,
        compiler_params=pltpu.CompilerParams(
            dimension_semantics=("arbitrary",),
            vmem_limit_bytes=_vmem_limit(block_bytes, 3 * d * f * 2 + tm * f * 16)),
        name="grouped_swiglu_experts",
    )(tile_expert, n_used, *xs, w_gate, w_up, w_down)


def _pos_kernel(route_ref, starts_ref, pos_ref):
    rt = jnp.transpose(route_ref[...])
    starts = starts_ref[...]
    ids = lax.broadcasted_iota(jnp.int32, (N_EXPERTS, rt.shape[1]), 0).astype(F32)
    for c in range(2):
        start_of = jnp.sum(jnp.where(ids == rt[c:c + 1, :], starts, 0.0), axis=0, keepdims=True)
        pos_ref[c:c + 1, :] = (start_of + rt[4 + c:5 + c, :]).astype(jnp.int32)


def _row_positions(route, starts):
    n_tok = route.shape[0]
    tm = TOKEN_TILE
    return pl.pallas_call(
        _pos_kernel,
        out_shape=jax.ShapeDtypeStruct((2, n_tok), jnp.int32),
        grid=(n_tok // tm,),
        in_specs=[pl.BlockSpec((tm, LANES), lambda i: (i, 0)),
                  pl.BlockSpec((N_EXPERTS, 1), lambda i: (0, 0))],
        out_specs=pl.BlockSpec((2, tm), lambda i: (0, i)),
        compiler_params=pltpu.CompilerParams(
            dimension_semantics=("arbitrary",),
            vmem_limit_bytes=_vmem_limit(tm * LANES * 4 + N_EXPERTS * LANES * 4 + 8 * tm * 4)),
        name="expert_row_positions",
    )(route, starts.astype(F32).reshape(N_EXPERTS, 1))


def _scatter_rows(rows, pos, n_out):
    n_tok, d = rows.shape
    n_idx = pos.shape[0]
    w = SC_SCATTER_WINDOW
    per_pass = n_tok // w
    mesh = plsc.VectorSubcoreMesh(core_axis_name="core", subcore_axis_name="subcore")

    @pl.kernel(out_type=jax.ShapeDtypeStruct((n_out, d), rows.dtype), mesh=mesh, scratch_types=[])
    def scatter_kernel(x_hbm, i_hbm, o_hbm):
        def body(x_vmem, i_vmem):
            pltpu.sync_copy(x_vmem, o_hbm.at[i_vmem.at[0]])

        pltpu.emit_pipeline(
            body,
            grid=(n_idx // w,),
            in_specs=[pl.BlockSpec((w, d), lambda i: (i % per_pass, 0)),
                      pl.BlockSpec((1, w), lambda i: (0, i))],
            out_specs=[],
            core_axis_name=("core", "subcore"),
            dimension_semantics=(pltpu.PARALLEL,),
        )(x_hbm, i_hbm)

    return scatter_kernel(rows, pos.reshape(1, n_idx))


def _dispatch(route, counts, n_tok):
    tm = EXPERT_TILE
    counts = counts[:N_EXPERTS].astype(jnp.int32)
    padded = ((counts + tm - 1) // tm) * tm
    ends = jnp.cumsum(padded)
    starts = ends - padded
    pos = _row_positions(route, starts).reshape(2 * n_tok)
    p = 2 * n_tok + N_EXPERTS * tm
    tile_start = jnp.arange(p // tm, dtype=jnp.int32) * tm
    tile_expert = jnp.minimum(jnp.sum(ends[None, :] <= tile_start[:, None], axis=1),
                              N_EXPERTS - 1).astype(jnp.int32)
    n_used = (ends[-1] // tm).astype(jnp.int32).reshape(1)
    return p, tile_expert, n_used, pos


def _final_kernel(x_ref, y_ref, route_ref, mod_ref, g_ref, o_ref):
    o_ref[0] = _rms(_expert_residual(x_ref[0], y_ref, route_ref, mod_ref), g_ref[...])


def _final_norm(x, y, route, mod, g):
    bsz, s, d = x.shape
    tm = TOKEN_TILE
    tok = lambda b, i: (b, i, 0)
    return pl.pallas_call(
        _final_kernel,
        out_shape=jax.ShapeDtypeStruct((bsz, s, d), F32),
        grid=(bsz, s // tm),
        in_specs=[pl.BlockSpec((1, tm, d), tok),
                  pl.BlockSpec((2, 1, tm, d), lambda b, i: (0, b, i, 0)),
                  pl.BlockSpec((1, tm, LANES), tok),
                  pl.BlockSpec((1, 6, d), lambda b, i: (b, 0, 0)),
                  pl.BlockSpec((1, d), lambda b, i: (0, 0))],
        out_specs=pl.BlockSpec((1, tm, d), tok),
        compiler_params=pltpu.CompilerParams(
            dimension_semantics=("arbitrary", "arbitrary"),
            vmem_limit_bytes=_vmem_limit(4 * tm * d * 4 + tm * LANES * 4)),
        name="residual_final_norm",
    )(x, y, route, mod, g.reshape(1, d))


def kernel(x, c, w_ada, b_ada, norm_mix, norm_ffn, norm_final, rel_bias, even_w_in, even_lambda, even_subln, even_w_gk2, even_b_gk, even_gla_norm, even_w_out, odd_w_in, odd_w_out, router_group_w, router_group_b, router_expert_w, router_expert_b, expert_w_gate, expert_w_up, expert_w_down):
    bsz, s, d = x.shape
    n_tok = bsz * s
    assert s % TOKEN_TILE == 0 and s % GLA_STEP == 0
    assert s % (DIFF_KEY_TILES * ATTN_TILE) == 0 and s % (DIFF_Q_TILES * ATTN_TILE) == 0
    mod_all = _modulation(c, w_ada, b_ada).reshape(DEPTH, bsz, 6, d)
    bias_tiles = _diff_bias_tiles(rel_bias)
    n_even_main = 3072
    y = route = prev_mod = None
    for l in range(DEPTH):
        mod = mod_all[l]
        i = l // 2
        if l % 2 == 0:
            w_in = even_w_in[i]
            w_main = w_in[:, :n_even_main].astype(BF16)
            w_gate = jnp.zeros((d, LANES), BF16).at[:, :GATE_RANK].set(
                w_in[:, n_even_main:].astype(BF16))
            x, proj, gate = _in_projection(x, y, route, prev_mod, norm_mix[l], mod, w_main, w_gate)
            oa = _diff_attention(proj, bias_tiles, even_lambda[i], even_subln[i], l)
            ob = _gla(proj, gate, even_w_gk2[i], even_b_gk[i], even_gla_norm[i])
            w_out = even_w_out[i].astype(BF16)
            o_list = [oa, ob]
            w_list = [w_out[:H_A * DV_A], w_out[H_A * DV_A:]]
        else:
            x, proj, _ = _in_projection(x, y, route, prev_mod, norm_mix[l], mod,
                                        odd_w_in[i].astype(BF16), None)
            o_list = [_stick_breaking(proj)]
            w_list = [odd_w_out[i].astype(BF16)]
        w_router = jnp.zeros((d, LANES), F32)
        w_router = w_router.at[:, :N_EXPERTS].set(router_expert_w[l].astype(F32))
        w_router = w_router.at[:, N_EXPERTS:N_EXPERTS + N_GROUPS].set(router_group_w[l].astype(F32))
        b_router = jnp.zeros((1, LANES), F32)
        b_router = b_router.at[0, :N_EXPERTS].set(router_expert_b[l].astype(F32))
        b_router = b_router.at[0, N_EXPERTS:N_EXPERTS + N_GROUPS].set(router_group_b[l].astype(F32))
        outs = _out_projection(o_list, w_list, x, mod, norm_ffn[l], w_router, b_router)
        x, pieces, route, counts = outs[0], outs[1:1 + MOE_PIECES], outs[-2], outs[-1]
        n_rows, tile_expert, n_used, pos = _dispatch(route.reshape(n_tok, LANES), counts[0], n_tok)
        xs = [_scatter_rows(piece.reshape(n_tok, piece.shape[2]), pos, n_rows) for piece in pieces]
        out = _grouped_experts(xs, tile_expert, n_used, expert_w_gate, expert_w_up,
                               expert_w_down, l)
        y = out.at[pos].get(mode="promise_in_bounds").reshape(2, bsz, s, d)
        prev_mod = mod
    return _final_norm(x, y, route, prev_mod, norm_final)
```

```python
import functools
import math

import numpy as np
import jax
import jax.numpy as jnp
from jax import lax
from jax.experimental import pallas as pl
from jax.experimental.pallas import tpu as pltpu
from jax.experimental.pallas import tpu_sc as plsc

F32 = jnp.float32
BF16 = jnp.bfloat16

D_MODEL = 1024
DEPTH = 4
CHUNK = 64
EPS = 1e-6
H_A, DH_A, DV_A = 4, 64, 128
H_B, DK_B, DV_B = 4, 64, 128
GATE_RANK = 16
GATE_TAU = 16.0
H_C, DH_C = 16, 64
N_BUCKETS = 32
MAX_DISTANCE = 128
N_GROUPS = 4
EXPERTS_PER_GROUP = 8
N_EXPERTS = N_GROUPS * EXPERTS_PER_GROUP
D_EXPERT = 512
MIX = 1024

LANES = 128
V7X_VMEM_BYTES = 64 * 1024 * 1024
NEG_BIG = -1e30
EXP_ZERO_BELOW = -104.0

TOKEN_TILE = 512
ATTN_TILE = 128
STICK_PAIRS = 8
STICK_AHEAD = 2
DIFF_KEY_TILES = 8
DIFF_Q_TILES = 8
GLA_STEP = 256
EXPERT_TILE = 512
SC_SCATTER_WINDOW = 128
MOE_PIECES = 2


def _vmem_limit(block_bytes, scratch_bytes=0):
    est = 2 * block_bytes + scratch_bytes + 16 * 1024 * 1024
    return int(min(est, V7X_VMEM_BYTES - 8 * 1024 * 1024))


def _nt(a, b):
    return lax.dot_general(a, b, (((1,), (1,)), ((), ())), preferred_element_type=F32)


def _tn(a, b):
    return lax.dot_general(a, b, (((0,), (0,)), ((), ())), preferred_element_type=F32)


def _dot(a, b):
    return jnp.dot(a, b, preferred_element_type=F32)


def _split(x):
    hi = x.astype(BF16)
    lo = (x - hi.astype(F32)).astype(BF16)
    return hi, lo


def _dot3(a, b):
    ah, al = _split(a)
    bh, bl = _split(b)
    return _dot(ah, bh) + _dot(al, bh) + _dot(ah, bl)


def _dot3_narrow(a, b):
    ah, al = _split(a)
    bh, bl = _split(b)
    n = b.shape[1]
    both = _dot(ah, jnp.concatenate([bh, bl], axis=1))
    return both[:, :n] + both[:, n:] + _dot(al, bh)


def _rms(x, g):
    return x * lax.rsqrt(jnp.mean(x * x, axis=-1, keepdims=True) + EPS) * g


def _softplus(z):
    return jnp.maximum(z, 0.0) + jnp.log(1.0 + jnp.exp(-jnp.abs(z)))


def _silu(x):
    return x / (1.0 + jnp.exp(-x))


def _mod_kernel(c_ref, w_ref, b_ref, o_ref):
    c = c_ref[...]
    o_ref[0] = _dot3(_silu(c), w_ref[0]) + b_ref[0]


def _modulation(c, w_ada, b_ada):
    depth, d, n = w_ada.shape
    bsz = c.shape[0]
    tn = 1536
    return pl.pallas_call(
        _mod_kernel,
        out_shape=jax.ShapeDtypeStruct((depth, bsz, n), F32),
        grid=(depth, n // tn),
        in_specs=[
            pl.BlockSpec((bsz, d), lambda l, j: (0, 0)),
            pl.BlockSpec((1, d, tn), lambda l, j: (l, 0, j)),
            pl.BlockSpec((1, 1, tn), lambda l, j: (l, 0, j)),
        ],
        out_specs=pl.BlockSpec((1, bsz, tn), lambda l, j: (l, 0, j)),
        compiler_params=pltpu.CompilerParams(
            dimension_semantics=("arbitrary", "arbitrary"),
            vmem_limit_bytes=_vmem_limit(d * tn * 4 + bsz * (d + tn) * 4)),
        name="adaln_modulation",
    )(c, w_ada, b_ada.reshape(depth, 1, n))


def _expert_residual(x, y_ref, route_ref, mod_ref):
    route = route_ref[0]
    y = route[:, 2:3] * y_ref[0, 0].astype(F32) + route[:, 3:4] * y_ref[1, 0].astype(F32)
    return x + mod_ref[0, 5:6, :] * y


def _proj_kernel(*refs, has_res, n_main, has_gate):
    it = iter(refs)
    x_ref = next(it)
    if has_res:
        y_ref = next(it)
        route_ref = next(it)
        pmod_ref = next(it)
    g_ref = next(it)
    mod_ref = next(it)
    w_ref = next(it)
    wg_ref = next(it) if has_gate else None
    xo_ref = next(it) if has_res else None
    o_ref = next(it)
    og_ref = next(it) if has_gate else None

    x = x_ref[0]
    if has_res:
        x = _expert_residual(x, y_ref, route_ref, pmod_ref)
        xo_ref[0] = x
    h = _rms(x, g_ref[...]) * (1.0 + mod_ref[0, 1:2, :]) + mod_ref[0, 0:1, :]
    hb = h.astype(BF16)
    step = 512
    for n0 in range(0, n_main, step):
        o_ref[0, :, n0:n0 + step] = _dot(hb, w_ref[:, n0:n0 + step]).astype(BF16)
    if has_gate:
        og_ref[0] = _dot(hb, wg_ref[...])


def _in_projection(x, y, route, prev_mod, g, mod, w, w_gate):
    bsz, s, d = x.shape
    tm = TOKEN_TILE
    has_res = y is not None
    has_gate = w_gate is not None
    n_main = w.shape[1]
    tok = lambda b, i: (b, i, 0)
    per_b = lambda b, i: (b, 0, 0)
    const2 = lambda b, i: (0, 0)
    in_specs = [pl.BlockSpec((1, tm, d), tok)]
    args = [x]
    if has_res:
        in_specs += [pl.BlockSpec((2, 1, tm, d), lambda b, i: (0, b, i, 0)),
                     pl.BlockSpec((1, tm, LANES), tok), pl.BlockSpec((1, 6, d), per_b)]
        args += [y, route, prev_mod]
    in_specs += [pl.BlockSpec((1, d), const2), pl.BlockSpec((1, 6, d), per_b),
                 pl.BlockSpec((d, n_main), const2)]
    args += [g.reshape(1, d), mod, w]
    if has_gate:
        in_specs.append(pl.BlockSpec((d, LANES), const2))
        args.append(w_gate)
    out_shape, out_specs = [], []
    if has_res:
        out_shape.append(jax.ShapeDtypeStruct((bsz, s, d), F32))
        out_specs.append(pl.BlockSpec((1, tm, d), tok))
    out_shape.append(jax.ShapeDtypeStruct((bsz, s, n_main), BF16))
    out_specs.append(pl.BlockSpec((1, tm, n_main), tok))
    if has_gate:
        out_shape.append(jax.ShapeDtypeStruct((bsz, s, LANES), F32))
        out_specs.append(pl.BlockSpec((1, tm, LANES), tok))
    block_bytes = (tm * d * 4 * (4 if has_res else 1) + d * n_main * 2 + tm * n_main * 2
                   + d * LANES * 2 + 2 * tm * LANES * 4)
    outs = pl.pallas_call(
        functools.partial(_proj_kernel, has_res=has_res, n_main=n_main, has_gate=has_gate),
        out_shape=out_shape,
        grid=(bsz, s // tm),
        in_specs=in_specs,
        out_specs=out_specs,
        compiler_params=pltpu.CompilerParams(
            dimension_semantics=("arbitrary", "arbitrary"),
            vmem_limit_bytes=_vmem_limit(block_bytes, tm * d * 8)),
        name="norm_mod_in_projection",
    )(*args)
    outs = list(outs)
    x_new = outs.pop(0) if has_res else x
    proj = outs.pop(0)
    gate = outs.pop(0) if has_gate else None
    return x_new, proj, gate


def _t5_bucket(rel):
    nb = N_BUCKETS // 2
    max_exact = nb // 2
    ret = jnp.where(rel > 0, nb, 0)
    n = jnp.abs(rel)
    nf = jnp.maximum(n, 1).astype(F32)
    large = max_exact + (jnp.log(nf / max_exact) / math.log(MAX_DISTANCE / max_exact)
                         * (nb - max_exact)).astype(jnp.int32)
    large = jnp.minimum(large, nb - 1)
    return ret + jnp.where(n < max_exact, n, large)


def _diff_bias_tiles(rel_bias):
    t = ATTN_TILE
    assert t >= MAX_DISTANCE, "tiles two or more away must lie beyond the last bucket edge"
    qp = jnp.arange(t)[:, None]
    kp = jnp.arange(t)[None, :]
    tiles = []
    for off in (0, 1, 2):
        rel = (kp - off * t) - qp
        hit = _t5_bucket(rel)[None, :, :, None] == jnp.arange(N_BUCKETS)[None, None, None, :]
        b = jnp.sum(jnp.where(hit, jnp.transpose(rel_bias.astype(F32))[:, None, None, :], 0.0),
                    axis=-1)
        if off == 0:
            b = jnp.where((kp // CHUNK) <= (qp // CHUNK), b, NEG_BIG)
        tiles.append(b)
    tab = jnp.stack(tiles, axis=1)
    return jnp.concatenate([tab, tab], axis=2)


def _diff_attn_kernel(q_ref, k_ref, v_ref, bias_ref, lam_ref, g_ref, o_ref, *, lambda_init):
    t = ATTN_TILE
    chains = DIFF_Q_TILES
    i = pl.program_id(2)
    lane = lax.broadcasted_iota(jnp.int32, (t, LANES), 1)
    scale = jnp.asarray(DH_A ** -0.5, BF16)
    stacked = []
    for c in range(chains):
        q = q_ref[0, c * t:(c + 1) * t, :]
        zero = jnp.zeros_like(q)
        stacked += [jnp.where(lane < DH_A, q, zero), jnp.where(lane >= DH_A, q, zero)]
    qq = jnp.concatenate(stacked, axis=0) * scale

    group = DIFF_KEY_TILES
    span = group * t
    assert chains == group, "a grid step's query tiles must be exactly one key block"

    def softmax_step(carry_c, s, which_of, vb):
        m, l, acc = carry_c
        width = s.shape[1] // t

        def logits(u):
            return s[:, u * t:(u + 1) * t] + bias_ref[0, which_of(u)]

        top = logits(0)
        for u in range(1, width):
            top = jnp.maximum(top, logits(u))
        m_new = jnp.maximum(m, jnp.max(top, axis=-1, keepdims=True))
        alpha = jnp.exp(m - m_new)
        ps = [jnp.exp(logits(u) - m_new) for u in range(width)]
        tot = ps[0]
        for u in range(1, width):
            tot = tot + ps[u]
        l = alpha * l + jnp.sum(tot, axis=-1, keepdims=True)
        p = jnp.concatenate([pu.astype(BF16) for pu in ps], axis=1)
        return m_new, l, alpha * acc + _dot(p, vb)

    def body(n, carry):
        r0 = pl.multiple_of(n * span, span)
        kb = k_ref[0, pl.ds(r0, span), :]
        vb = v_ref[0, pl.ds(r0, span), :]
        s_chain = [_nt(qq[c * 2 * t:(c + 1) * 2 * t], kb) for c in range(chains)]
        out = []
        for c in range(chains):
            def which_of(u, c=c):
                return jnp.minimum((i * chains + c) - (n * group + u), 2)
            out.append(softmax_step(carry[c], s_chain[c], which_of, vb))
        return tuple(out)

    init = tuple((jnp.full((2 * t, 1), NEG_BIG, F32), jnp.zeros((2 * t, 1), F32),
                  jnp.zeros((2 * t, DV_A), F32)) for _ in range(chains))
    before = lax.fori_loop(0, i, body, init)
    r0 = pl.multiple_of(i * span, span)
    diag = [(_nt(qq[c * 2 * t:(c + 1) * 2 * t], k_ref[0, pl.ds(r0, (c + 1) * t), :]),
             v_ref[0, pl.ds(r0, (c + 1) * t), :]) for c in range(chains)]
    final = [softmax_step(before[c], diag[c][0], lambda u, c=c: min(c - u, 2), diag[c][1])
             for c in range(chains)]
    lp = lam_ref[...].astype(F32)
    lam = (jnp.exp(jnp.sum(lp[0:1] * lp[1:2], axis=-1, keepdims=True))
           - jnp.exp(jnp.sum(lp[2:3] * lp[3:4], axis=-1, keepdims=True)) + lambda_init)
    for c in range(chains):
        _, l, acc = final[c]
        o = acc / l
        w = o[:t] - lam * o[t:]
        o_ref[0, c * t:(c + 1) * t, :] = (_rms(w, g_ref[...]) * (1.0 - lambda_init)
                                          ).astype(o_ref.dtype)


def _diff_attention(proj, bias_tiles, lam_p, subln_g, layer_idx):
    bsz, s, _ = proj.shape
    t = ATTN_TILE
    tq = DIFF_Q_TILES * t
    lambda_init = 0.8 - 0.6 * math.exp(-0.3 * layer_idx)
    return pl.pallas_call(
        functools.partial(_diff_attn_kernel, lambda_init=lambda_init),
        out_shape=jax.ShapeDtypeStruct((bsz, s, H_A * DV_A), BF16),
        grid=(bsz, H_A, s // tq),
        in_specs=[
            pl.BlockSpec((1, tq, LANES), lambda b, h, i: (b, i, h)),
            pl.BlockSpec((1, s, LANES), lambda b, h, i: (b, 0, H_A + h)),
            pl.BlockSpec((1, s, LANES), lambda b, h, i: (b, 0, 2 * H_A + h)),
            pl.BlockSpec((1, 3, 2 * t, t), lambda b, h, i: (h, 0, 0, 0)),
            pl.BlockSpec((4, DH_A), lambda b, h, i: (0, 0)),
            pl.BlockSpec((1, DV_A), lambda b, h, i: (0, 0)),
        ],
        out_specs=pl.BlockSpec((1, tq, DV_A), lambda b, h, i: (b, i, h)),
        compiler_params=pltpu.CompilerParams(
            dimension_semantics=("arbitrary", "arbitrary", "arbitrary"),
            vmem_limit_bytes=_vmem_limit(2 * s * LANES * 2 + 3 * 2 * t * t * 4 + 4 * tq * LANES,
                                         4 * tq * DIFF_KEY_TILES * t * 4)),
        name="diff_attention",
    )(proj, proj, proj, bias_tiles, lam_p, subln_g.reshape(1, DV_A))


def _gla_kernel(q_ref, k_ref, v_ref, r_ref, bg_ref, wgk_ref, bgk_ref, g_ref, o_ref, st_ref):
    c = CHUNK
    ts = q_ref.shape[1]
    n_chunks = ts // c

    @pl.when(pl.program_id(1) == 0)
    def _():
        st_ref[...] = jnp.zeros_like(st_ref)

    row = lax.broadcasted_iota(jnp.int32, (ts, ts), 0)
    col = lax.broadcasted_iota(jnp.int32, (ts, ts), 1)
    same_chunk_before = ((row // c) == (col // c)) & (col <= row)
    tri = jnp.where(same_chunk_before, 1.0, 0.0).astype(BF16)
    causal = (lax.broadcasted_iota(jnp.int32, (c, c), 1)
              <= lax.broadcasted_iota(jnp.int32, (c, c), 0))
    lane = lax.broadcasted_iota(jnp.int32, (ts, LANES), 1)
    wgk = wgk_ref[...]
    bgk = bgk_ref[...]
    g = g_ref[...]
    bg = bg_ref[0]

    las = [-_softplus(-(_dot3(bg, wgk[:, hp * LANES:(hp + 1) * LANES])
                        + bgk[:, hp * LANES:(hp + 1) * LANES])) / GATE_TAU
           for hp in range(H_B // 2)]
    bs = []
    for la in las:
        la_hi, la_lo = _split(la)
        bs.append(_dot(tri, la_hi) + _dot(tri, la_lo))

    heads = []
    for hp in range(H_B // 2):
        cols = slice(hp * LANES, (hp + 1) * LANES)
        b = bs[hp]
        b_last = jnp.concatenate(
            [jnp.broadcast_to(b[(ci + 1) * c - 1:(ci + 1) * c, :], (c, LANES))
             for ci in range(n_chunks)], axis=0)
        qf = q_ref[0, :, cols].astype(F32) * (DK_B ** -0.5)
        kf = k_ref[0, :, cols].astype(F32)
        q_dec = qf * jnp.exp(b)
        k_inv = (kf * jnp.exp(-b)).astype(BF16)
        k_dec = kf * jnp.exp(b_last - b)
        decays = [jnp.exp(b[(ci + 1) * c - 1:(ci + 1) * c, :]) for ci in range(n_chunks)]
        for hh in range(2):
            own = (lane >= hh * DK_B) & (lane < (hh + 1) * DK_B)
            heads.append((jnp.where(own, q_dec, 0.0).astype(BF16), k_inv,
                          jnp.where(own, k_dec, 0.0).astype(BF16), decays))

    def rows(x, ci):
        return x[ci * c:(ci + 1) * c]

    scores, increments = [], []
    for h, (qd, k_inv, kd, _) in enumerate(heads):
        vh = v_ref[0, :, h * DV_B:(h + 1) * DV_B]
        scores.append([jnp.where(causal, _nt(rows(qd, ci), rows(k_inv, ci)), 0.0).astype(BF16)
                       for ci in range(n_chunks)])
        increments.append([_tn(rows(vh, ci), rows(kd, ci)) for ci in range(n_chunks)])

    for h, (qd, _, _, decays) in enumerate(heads):
        vh = v_ref[0, :, h * DV_B:(h + 1) * DV_B]
        st = st_ref[h]
        states = []
        for ci in range(n_chunks):
            states.append(st.astype(BF16))
            st = st * decays[ci] + increments[h][ci]
        st_ref[h] = st
        o = jnp.concatenate(
            [_dot(scores[h][ci], rows(vh, ci)) + _nt(rows(qd, ci), states[ci])
             for ci in range(n_chunks)], axis=0)
        rr = r_ref[0, :, h * DV_B:(h + 1) * DV_B].astype(F32)
        o_ref[0, :, h * DV_B:(h + 1) * DV_B] = (_rms(o, g) * _silu(rr)).astype(o_ref.dtype)


def _gla(proj, gate, w_gk2, b_gk, gla_g):
    bsz, s, _ = proj.shape
    ts = GLA_STEP
    nk = H_B * DK_B
    nv = H_B * DV_B
    w_pad = jnp.zeros((LANES, nk), F32).at[:GATE_RANK].set(w_gk2.astype(F32))
    return pl.pallas_call(
        _gla_kernel,
        out_shape=jax.ShapeDtypeStruct((bsz, s, nv), BF16),
        grid=(bsz, s // ts),
        in_specs=[
            pl.BlockSpec((1, ts, nk), lambda b, i: (b, i, 1536 // nk)),
            pl.BlockSpec((1, ts, nk), lambda b, i: (b, i, 1792 // nk)),
            pl.BlockSpec((1, ts, nv), lambda b, i: (b, i, 2048 // nv)),
            pl.BlockSpec((1, ts, nv), lambda b, i: (b, i, 2560 // nv)),
            pl.BlockSpec((1, ts, LANES), lambda b, i: (b, i, 0)),
            pl.BlockSpec((LANES, nk), lambda b, i: (0, 0)),
            pl.BlockSpec((1, nk), lambda b, i: (0, 0)),
            pl.BlockSpec((1, DV_B), lambda b, i: (0, 0)),
        ],
        out_specs=pl.BlockSpec((1, ts, nv), lambda b, i: (b, i, 0)),
        scratch_shapes=[pltpu.VMEM((H_B, DV_B, LANES), F32)],
        compiler_params=pltpu.CompilerParams(
            dimension_semantics=("arbitrary", "arbitrary"),
            vmem_limit_bytes=_vmem_limit(ts * (2 * nk + 3 * nv) * 2 + ts * LANES * 4,
                                         H_B * DV_B * LANES * 4)),
        name="gated_linear_attention",
    )(proj, proj, proj, proj, gate, w_pad, b_gk.reshape(1, nk), gla_g.reshape(1, DV_B))


def _stick_kernel(q_ref, k_ref, v_ref, o_ref, run_ref, acc_ref):
    t = ATTN_TILE
    chains = STICK_PAIRS
    i = pl.program_id(2)
    lane = lax.broadcasted_iota(jnp.int32, (t, LANES), 1)
    scale = jnp.asarray(DH_C ** -0.5, BF16)
    qqs = []
    for c in range(chains):
        q = q_ref[0, :, c * LANES:(c + 1) * LANES]
        zero = jnp.zeros_like(q)
        qqs.append(jnp.concatenate([jnp.where(lane < DH_C, q, zero),
                                    jnp.where(lane >= DH_C, q, zero)], axis=0) * scale)
    row = lax.broadcasted_iota(jnp.int32, (t, t), 0)
    col = lax.broadcasted_iota(jnp.int32, (t, t), 1)
    later = jnp.where(row > col, 1.0, 0.0).astype(BF16)
    below = (lax.broadcasted_iota(jnp.int32, (2 * t, t), 1)
             < lax.broadcasted_iota(jnp.int32, (2 * t, t), 0) % t)

    def logits(j):
        r0 = pl.multiple_of(jnp.maximum(j, 0) * t, t)
        return [_nt(qqs[c], k_ref[0, pl.ds(r0, t), c * LANES:(c + 1) * LANES])
                for c in range(chains)]

    def tiles(first, zs, runs, accs, diagonal_first):
        runs, accs = list(runs), list(accs)
        mids = []
        for u in range(len(zs)):
            for c in range(chains):
                z = zs[u][c]
                sp = _softplus(z)
                log_1m = -sp
                masked = diagonal_first and u == 0
                if masked:
                    log_1m = jnp.where(below, log_1m, 0.0)
                hi, lo = _split(log_1m)
                within = _dot(hi, later) + _dot(lo, later)
                mids.append((u, c, z - sp, log_1m, within, masked))
        for u, c, log_b, log_1m, within, masked in mids:
            r0 = pl.multiple_of((first - u) * t, t)
            vb = v_ref[0, pl.ds(r0, t), c * LANES:(c + 1) * LANES]
            a = jnp.exp(log_b + (within + runs[c]))
            if masked:
                a = jnp.where(below, a, 0.0)
            ab = a.astype(BF16)
            zero_v = jnp.zeros_like(vb)
            a2 = jnp.concatenate([ab[:t], ab[t:]], axis=1)
            v2 = jnp.concatenate([jnp.where(lane < DH_C, vb, zero_v),
                                  jnp.where(lane >= DH_C, vb, zero_v)], axis=0)
            accs[c] = accs[c] + _dot(a2, v2)
            runs[c] = runs[c] + jnp.sum(log_1m, axis=-1, keepdims=True)
        return runs, accs

    def first_block(n_tiles):
        zs = [logits(i - u) for u in range(n_tiles)]
        runs, accs = tiles(i, zs, [jnp.zeros((2 * t, 1), F32)] * chains,
                           [jnp.zeros((t, LANES), F32)] * chains, True)
        for c in range(chains):
            run_ref[c] = runs[c]
            acc_ref[c] = accs[c]

    ahead = STICK_AHEAD

    @pl.when(i >= ahead)
    def _():
        first_block(1 + ahead)

    @pl.when(i < ahead)
    def _():
        first_block(1)

    n_left = i
    done0 = jnp.where(i >= ahead, ahead, 0)

    def go_on(done, runs):
        top = runs[0]
        for c in range(1, chains):
            top = jnp.maximum(top, runs[c])
        return ((done < n_left) & (jnp.max(top) > EXP_ZERO_BELOW)).astype(jnp.int32)

    def body(carry):
        done, _, runs, accs = carry
        first = n_left - 1 - done
        runs, accs = tiles(first, [logits(first)], runs, accs, False)
        done = done + 1
        return done, go_on(done, runs), tuple(runs), tuple(accs)

    runs = tuple(run_ref[c] for c in range(chains))
    accs = tuple(acc_ref[c] for c in range(chains))
    carry = (done0, go_on(done0, runs), runs, accs)
    accs = lax.while_loop(lambda carry: carry[1] > 0, body, carry)[3]
    for c in range(chains):
        o_ref[0, :, c * LANES:(c + 1) * LANES] = accs[c].astype(o_ref.dtype)


def _stick_breaking(proj):
    bsz, s, _ = proj.shape
    t = ATTN_TILE
    width = STICK_PAIRS * LANES
    n_blocks = H_C * DH_C // width
    return pl.pallas_call(
        _stick_kernel,
        out_shape=jax.ShapeDtypeStruct((bsz, s, H_C * DH_C), BF16),
        grid=(bsz, n_blocks, s // t),
        in_specs=[
            pl.BlockSpec((1, t, width), lambda b, h, i: (b, i, h)),
            pl.BlockSpec((1, s, width), lambda b, h, i: (b, 0, n_blocks + h)),
            pl.BlockSpec((1, s, width), lambda b, h, i: (b, 0, 2 * n_blocks + h)),
        ],
        out_specs=pl.BlockSpec((1, t, width), lambda b, h, i: (b, i, h)),
        scratch_shapes=[pltpu.VMEM((STICK_PAIRS, 2 * t, 1), F32),
                        pltpu.VMEM((STICK_PAIRS, t, LANES), F32)],
        compiler_params=pltpu.CompilerParams(
            dimension_semantics=("arbitrary", "arbitrary", "arbitrary"),
            vmem_limit_bytes=_vmem_limit(2 * s * width * 2 + 4 * t * width,
                                         STICK_PAIRS * 3 * t * LANES * 4)),
        name="stick_breaking_attention",
    )(proj, proj, proj)


def _route(logits):
    lane_i = lax.broadcasted_iota(jnp.int32, logits.shape, 1)
    lane = lane_i.astype(F32)
    group_of_lane = (lane_i // EXPERTS_PER_GROUP).astype(F32)
    big = jnp.float32(1 << 20)
    neg = jnp.float32(-jnp.inf)
    is_group = (lane_i >= N_EXPERTS) & (lane_i < N_EXPERTS + N_GROUPS)
    gl = jnp.where(is_group, logits, neg)
    gmax = jnp.max(gl, axis=-1, keepdims=True)
    g_val = 1.0 / jnp.sum(jnp.exp(gl - gmax), axis=-1, keepdims=True)
    g_sel = jnp.min(jnp.where(gl == gmax, lane - N_EXPERTS, big), axis=-1, keepdims=True)
    in_group = (lane_i < N_EXPERTS) & (group_of_lane == g_sel)
    el = jnp.where(in_group, logits, neg)
    emax = jnp.max(el, axis=-1, keepdims=True)
    esum = jnp.sum(jnp.exp(el - emax), axis=-1, keepdims=True)
    i1 = jnp.min(jnp.where(el == emax, lane, big), axis=-1, keepdims=True)
    el2 = jnp.where(lane == i1, neg, el)
    emax2 = jnp.max(el2, axis=-1, keepdims=True)
    i2 = jnp.min(jnp.where(el2 == emax2, lane, big), axis=-1, keepdims=True)
    v1 = 1.0 / esum
    v2 = jnp.exp(emax2 - emax) / esum
    tot = v1 + v2
    w1 = g_val * (v1 / tot)
    w2 = g_val * (v2 / tot)
    out = jnp.where(lane_i == 0, i1, 0.0)
    out = jnp.where(lane_i == 1, i2, out)
    out = jnp.where(lane_i == 2, w1, out)
    out = jnp.where(lane_i == 3, w2, out)
    return out, (lane == i1), (lane == i2)


def _pack_halves(h):
    half = h.shape[1] // 2
    lo = pltpu.bitcast(h[:, :half].astype(BF16).astype(F32), jnp.uint32)
    hi = pltpu.bitcast(h[:, half:].astype(BF16).astype(F32), jnp.uint32)
    return lax.shift_right_logical(lo, jnp.uint32(16)) | hi


def _unpack_halves(p):
    lo = pltpu.bitcast(lax.shift_left(p, jnp.uint32(16)), F32).astype(BF16)
    hi = pltpu.bitcast(p & jnp.uint32(0xFFFF0000), F32).astype(BF16)
    return lo, hi


def _out_kernel(*refs, n_o):
    o_refs = refs[:n_o]
    w_refs = refs[n_o:2 * n_o]
    x_ref, mod_ref, g_ref, wr_ref, br_ref, xo_ref = refs[2 * n_o:2 * n_o + 6]
    h_refs = refs[2 * n_o + 6:2 * n_o + 6 + MOE_PIECES]
    r_ref, cnt_ref, run_ref = refs[2 * n_o + 6 + MOE_PIECES:]

    @pl.when((pl.program_id(0) == 0) & (pl.program_id(1) == 0))
    def _():
        run_ref[...] = jnp.zeros_like(run_ref)

    m = _dot(o_refs[0][0], w_refs[0][...])
    for a in range(1, n_o):
        m = m + _dot(o_refs[a][0], w_refs[a][...])
    x = x_ref[0] + mod_ref[0, 2:3, :] * m
    xo_ref[0] = x
    h = _rms(x, g_ref[...]) * (1.0 + mod_ref[0, 4:5, :]) + mod_ref[0, 3:4, :]
    piece = h.shape[1] // MOE_PIECES
    for k in range(MOE_PIECES):
        h_refs[k][0] = _pack_halves(h[:, k * piece:(k + 1) * piece])
    route, hit1, hit2 = _route(_dot3_narrow(h, wr_ref[...]) + br_ref[...])
    tm = route.shape[0]
    picks = jnp.where(hit1 | hit2, 1.0, 0.0)
    row = lax.broadcasted_iota(jnp.int32, (tm, tm), 0)
    col = lax.broadcasted_iota(jnp.int32, (tm, tm), 1)
    before = jnp.where(col < row, 1.0, 0.0).astype(BF16)
    earlier = _dot(before, picks.astype(BF16)) + run_ref[...]
    rank1 = jnp.sum(jnp.where(hit1, earlier, 0.0), axis=-1, keepdims=True)
    rank2 = jnp.sum(jnp.where(hit2, earlier, 0.0), axis=-1, keepdims=True)
    lane_i = lax.broadcasted_iota(jnp.int32, route.shape, 1)
    route = jnp.where(lane_i == 4, rank1, route)
    route = jnp.where(lane_i == 5, rank2, route)
    r_ref[0] = route
    run_ref[...] = run_ref[...] + jnp.sum(picks, axis=0, keepdims=True)
    cnt_ref[...] = run_ref[...]


def _out_projection(o_list, w_list, x, mod, g, w_router, b_router):
    bsz, s, d = x.shape
    tm = TOKEN_TILE
    n_o = len(o_list)
    packed = d // (2 * MOE_PIECES)
    tok = lambda b, i: (b, i, 0)
    const2 = lambda b, i: (0, 0)
    in_specs = ([pl.BlockSpec((1, tm, o.shape[2]), tok) for o in o_list]
                + [pl.BlockSpec(w.shape, const2) for w in w_list]
                + [pl.BlockSpec((1, tm, d), tok), pl.BlockSpec((1, 6, d), lambda b, i: (b, 0, 0)),
                   pl.BlockSpec((1, d), const2), pl.BlockSpec((d, LANES), const2),
                   pl.BlockSpec((1, LANES), const2)])
    block_bytes = (sum(tm * o.shape[2] * 2 for o in o_list) + sum(w.size * 2 for w in w_list)
                   + 2 * tm * d * 4 + tm * d * 2 + d * LANES * 4 + tm * LANES * 4)
    return pl.pallas_call(
        functools.partial(_out_kernel, n_o=n_o),
        out_shape=([jax.ShapeDtypeStruct((bsz, s, d), F32)]
                   + [jax.ShapeDtypeStruct((bsz, s, packed), jnp.uint32)] * MOE_PIECES
                   + [jax.ShapeDtypeStruct((bsz, s, LANES), F32),
                      jax.ShapeDtypeStruct((1, LANES), F32)]),
        grid=(bsz, s // tm),
        in_specs=in_specs,
        out_specs=([pl.BlockSpec((1, tm, d), tok)]
                   + [pl.BlockSpec((1, tm, packed), tok)] * MOE_PIECES
                   + [pl.BlockSpec((1, tm, LANES), tok), pl.BlockSpec((1, LANES), const2)]),
        scratch_shapes=[pltpu.VMEM((1, LANES), F32)],
        compiler_params=pltpu.CompilerParams(
            dimension_semantics=("arbitrary", "arbitrary"),
            vmem_limit_bytes=_vmem_limit(block_bytes, tm * d * 12 + tm * tm * 4)),
        name="out_projection_norm_router",
    )(*o_list, *w_list, x, mod, g.reshape(1, d), w_router, b_router)


def _expert_kernel(te_ref, tv_ref, nu_ref, *refs):
    x_refs = refs[:MOE_PIECES]
    wg_ref, wu_ref, wd_ref, o_ref, wgb_ref, wub_ref, wdb_ref = refs[MOE_PIECES:]
    t = pl.program_id(0)
    prev = te_ref[jnp.maximum(t - 1, 0)]

    @pl.when((t == 0) | (te_ref[t] != prev))
    def _():
        wgb_ref[...] = wg_ref[0, 0].astype(BF16)
        wub_ref[...] = wu_ref[0, 0].astype(BF16)
        wdb_ref[...] = wd_ref[0, 0].astype(BF16)

    @pl.when(t < nu_ref[0])
    def _():
        cols = []
        for x_ref in x_refs:
            row = lax.broadcasted_iota(jnp.int32, x_ref.shape, 0)
            cols += list(_unpack_halves(jnp.where(row < tv_ref[t], x_ref[...], jnp.uint32(0))))
        w = cols[0].shape[1]
        gt = _dot(cols[0], wgb_ref[:w, :])
        up = _dot(cols[0], wub_ref[:w, :])
        for k in range(1, len(cols)):
            gt = gt + _dot(cols[k], wgb_ref[k * w:(k + 1) * w, :])
            up = up + _dot(cols[k], wub_ref[k * w:(k + 1) * w, :])
        o_ref[...] = _dot((_silu(gt) * up).astype(BF16), wdb_ref[...]).astype(o_ref.dtype)

    @pl.when(t >= nu_ref[0])
    def _():
        o_ref[...] = jnp.zeros_like(o_ref)


def _grouped_experts(xs, tile_expert, tile_valid, n_used, w_gate, w_up, w_down, layer):
    p, packed = xs[0].shape
    d = w_gate.shape[2]
    tm = EXPERT_TILE
    f = w_gate.shape[3]
    grid_spec = pltpu.PrefetchScalarGridSpec(
        num_scalar_prefetch=3,
        grid=(p // tm,),
        in_specs=[pl.BlockSpec((tm, packed), lambda t, te, tv, nu: (t, 0))] * MOE_PIECES + [
            pl.BlockSpec((1, 1, d, f), lambda t, te, tv, nu: (layer, te[t], 0, 0)),
            pl.BlockSpec((1, 1, d, f), lambda t, te, tv, nu: (layer, te[t], 0, 0)),
            pl.BlockSpec((1, 1, f, d), lambda t, te, tv, nu: (layer, te[t], 0, 0)),
        ],
        out_specs=pl.BlockSpec((tm, d), lambda t, te, tv, nu: (t, 0)),
        scratch_shapes=[pltpu.VMEM((d, f), BF16), pltpu.VMEM((d, f), BF16),
                        pltpu.VMEM((f, d), BF16)],
    )
    block_bytes = tm * d * 2 + 3 * d * f * 4 + tm * d * 2
    return pl.pallas_call(
        _expert_kernel,
        out_shape=jax.ShapeDtypeStruct((p, d), BF16),
        grid_spec=grid_spec,
        compiler_params=pltpu.CompilerParams(
            dimension_semantics=("arbitrary",),
            vmem_limit_bytes=_vmem_limit(block_bytes, 3 * d * f * 2 + tm * f * 16)),
        name="grouped_swiglu_experts",
    )(tile_expert, tile_valid, n_used, *xs, w_gate, w_up, w_down)


def _pos_kernel(route_ref, starts_ref, pos_ref):
    rt = jnp.transpose(route_ref[...])
    starts = starts_ref[...]
    ids = lax.broadcasted_iota(jnp.int32, (N_EXPERTS, rt.shape[1]), 0).astype(F32)
    for c in range(2):
        start_of = jnp.sum(jnp.where(ids == rt[c:c + 1, :], starts, 0.0), axis=0, keepdims=True)
        pos_ref[c:c + 1, :] = (start_of + rt[4 + c:5 + c, :]).astype(jnp.int32)


def _row_positions(route, starts):
    n_tok = route.shape[0]
    tm = TOKEN_TILE
    return pl.pallas_call(
        _pos_kernel,
        out_shape=jax.ShapeDtypeStruct((2, n_tok), jnp.int32),
        grid=(n_tok // tm,),
        in_specs=[pl.BlockSpec((tm, LANES), lambda i: (i, 0)),
                  pl.BlockSpec((N_EXPERTS, 1), lambda i: (0, 0))],
        out_specs=pl.BlockSpec((2, tm), lambda i: (0, i)),
        compiler_params=pltpu.CompilerParams(
            dimension_semantics=("arbitrary",),
            vmem_limit_bytes=_vmem_limit(tm * LANES * 4 + N_EXPERTS * LANES * 4 + 8 * tm * 4)),
        name="expert_row_positions",
    )(route, starts.astype(F32).reshape(N_EXPERTS, 1))


def _scatter_rows(rows, pos, n_out):
    n_tok, d = rows.shape
    n_idx = pos.shape[0]
    w = SC_SCATTER_WINDOW
    per_pass = n_tok // w
    mesh = plsc.VectorSubcoreMesh(core_axis_name="core", subcore_axis_name="subcore")

    @pl.kernel(out_type=jax.ShapeDtypeStruct((n_out, d), rows.dtype), mesh=mesh, scratch_types=[])
    def scatter_kernel(x_hbm, i_hbm, o_hbm):
        def body(x_vmem, i_vmem):
            pltpu.sync_copy(x_vmem, o_hbm.at[i_vmem.at[0]])

        pltpu.emit_pipeline(
            body,
            grid=(n_idx // w,),
            in_specs=[pl.BlockSpec((w, d), lambda i: (i % per_pass, 0)),
                      pl.BlockSpec((1, w), lambda i: (0, i))],
            out_specs=[],
            core_axis_name=("core", "subcore"),
            dimension_semantics=(pltpu.PARALLEL,),
        )(x_hbm, i_hbm)

    return scatter_kernel(rows, pos.reshape(1, n_idx))


def _dispatch(route, counts, n_tok):
    tm = EXPERT_TILE
    counts = counts[:N_EXPERTS].astype(jnp.int32)
    padded = ((counts + tm - 1) // tm) * tm
    ends = jnp.cumsum(padded)
    starts = ends - padded
    pos = _row_positions(route, starts).reshape(2 * n_tok)
    p = 2 * n_tok + N_EXPERTS * tm
    tile_start = jnp.arange(p // tm, dtype=jnp.int32) * tm
    tile_expert = jnp.minimum(jnp.sum(ends[None, :] <= tile_start[:, None], axis=1),
                              N_EXPERTS - 1).astype(jnp.int32)
    n_used = (ends[-1] // tm).astype(jnp.int32).reshape(1)
    row_end = starts + counts
    tile_valid = jnp.clip(jnp.sum(jnp.where(tile_expert[:, None] == jnp.arange(N_EXPERTS)[None, :],
                                            row_end[None, :], 0), axis=1) - tile_start, 0, tm)
    return p, tile_expert, tile_valid.astype(jnp.int32), n_used, pos


def _final_kernel(x_ref, y_ref, route_ref, mod_ref, g_ref, o_ref):
    o_ref[0] = _rms(_expert_residual(x_ref[0], y_ref, route_ref, mod_ref), g_ref[...])


def _final_norm(x, y, route, mod, g):
    bsz, s, d = x.shape
    tm = TOKEN_TILE
    tok = lambda b, i: (b, i, 0)
    return pl.pallas_call(
        _final_kernel,
        out_shape=jax.ShapeDtypeStruct((bsz, s, d), F32),
        grid=(bsz, s // tm),
        in_specs=[pl.BlockSpec((1, tm, d), tok),
                  pl.BlockSpec((2, 1, tm, d), lambda b, i: (0, b, i, 0)),
                  pl.BlockSpec((1, tm, LANES), tok),
                  pl.BlockSpec((1, 6, d), lambda b, i: (b, 0, 0)),
                  pl.BlockSpec((1, d), lambda b, i: (0, 0))],
        out_specs=pl.BlockSpec((1, tm, d), tok),
        compiler_params=pltpu.CompilerParams(
            dimension_semantics=("arbitrary", "arbitrary"),
            vmem_limit_bytes=_vmem_limit(4 * tm * d * 4 + tm * LANES * 4)),
        name="residual_final_norm",
    )(x, y, route, mod, g.reshape(1, d))


def kernel(x, c, w_ada, b_ada, norm_mix, norm_ffn, norm_final, rel_bias, even_w_in, even_lambda, even_subln, even_w_gk2, even_b_gk, even_gla_norm, even_w_out, odd_w_in, odd_w_out, router_group_w, router_group_b, router_expert_w, router_expert_b, expert_w_gate, expert_w_up, expert_w_down):
    bsz, s, d = x.shape
    n_tok = bsz * s
    assert s % TOKEN_TILE == 0 and s % GLA_STEP == 0
    assert s % (DIFF_KEY_TILES * ATTN_TILE) == 0 and s % (DIFF_Q_TILES * ATTN_TILE) == 0
    mod_all = _modulation(c, w_ada, b_ada).reshape(DEPTH, bsz, 6, d)
    bias_tiles = _diff_bias_tiles(rel_bias)
    n_even_main = 3072
    y = route = prev_mod = None
    for l in range(DEPTH):
        mod = mod_all[l]
        i = l // 2
        if l % 2 == 0:
            w_in = even_w_in[i]
            w_main = w_in[:, :n_even_main].astype(BF16)
            w_gate = jnp.zeros((d, LANES), BF16).at[:, :GATE_RANK].set(
                w_in[:, n_even_main:].astype(BF16))
            x, proj, gate = _in_projection(x, y, route, prev_mod, norm_mix[l], mod, w_main, w_gate)
            oa = _diff_attention(proj, bias_tiles, even_lambda[i], even_subln[i], l)
            ob = _gla(proj, gate, even_w_gk2[i], even_b_gk[i], even_gla_norm[i])
            w_out = even_w_out[i].astype(BF16)
            o_list = [oa, ob]
            w_list = [w_out[:H_A * DV_A], w_out[H_A * DV_A:]]
        else:
            x, proj, _ = _in_projection(x, y, route, prev_mod, norm_mix[l], mod,
                                        odd_w_in[i].astype(BF16), None)
            o_list = [_stick_breaking(proj)]
            w_list = [odd_w_out[i].astype(BF16)]
        w_router = jnp.zeros((d, LANES), F32)
        w_router = w_router.at[:, :N_EXPERTS].set(router_expert_w[l].astype(F32))
        w_router = w_router.at[:, N_EXPERTS:N_EXPERTS + N_GROUPS].set(router_group_w[l].astype(F32))
        b_router = jnp.zeros((1, LANES), F32)
        b_router = b_router.at[0, :N_EXPERTS].set(router_expert_b[l].astype(F32))
        b_router = b_router.at[0, N_EXPERTS:N_EXPERTS + N_GROUPS].set(router_group_b[l].astype(F32))
        outs = _out_projection(o_list, w_list, x, mod, norm_ffn[l], w_router, b_router)
        x, pieces, route, counts = outs[0], outs[1:1 + MOE_PIECES], outs[-2], outs[-1]
        n_rows, tile_expert, tile_valid, n_used, pos = _dispatch(route.reshape(n_tok, LANES),
                                                                 counts[0], n_tok)
        xs = [_scatter_rows(piece.reshape(n_tok, piece.shape[2]), pos, n_rows) for piece in pieces]
        out = _grouped_experts(xs, tile_expert, tile_valid, n_used, expert_w_gate, expert_w_up,
                               expert_w_down, l)
        y = out.at[pos].get(mode="promise_in_bounds").reshape(2, bsz, s, d)
        prev_mod = mod
    return _final_norm(x, y, route, prev_mod, norm_final)
```

```python
import functools
import math

import numpy as np
import jax
import jax.numpy as jnp
from jax import lax
from jax.experimental import pallas as pl
from jax.experimental.pallas import tpu as pltpu
from jax.experimental.pallas import tpu_sc as plsc

F32 = jnp.float32
BF16 = jnp.bfloat16

D_MODEL = 1024
DEPTH = 4
CHUNK = 64
EPS = 1e-6
H_A, DH_A, DV_A = 4, 64, 128
H_B, DK_B, DV_B = 4, 64, 128
GATE_RANK = 16
GATE_TAU = 16.0
H_C, DH_C = 16, 64
N_BUCKETS = 32
MAX_DISTANCE = 128
N_GROUPS = 4
EXPERTS_PER_GROUP = 8
N_EXPERTS = N_GROUPS * EXPERTS_PER_GROUP
D_EXPERT = 512
MIX = 1024

LANES = 128
V7X_VMEM_BYTES = 64 * 1024 * 1024
NEG_BIG = -1e30
EXP_ZERO_BELOW = -104.0

TOKEN_TILE = 512
PROJ_ROW_PARTS = 2
ATTN_TILE = 128
STICK_PAIRS = 8
STICK_AHEAD = 2
DIFF_KEY_TILES = 8
DIFF_Q_TILES = 8
GLA_STEP = 256
EXPERT_TILE = 512
SC_SCATTER_WINDOW = 128
MOE_PIECES = 2


def _vmem_limit(block_bytes, scratch_bytes=0):
    est = 2 * block_bytes + scratch_bytes + 16 * 1024 * 1024
    return int(min(est, V7X_VMEM_BYTES - 8 * 1024 * 1024))


def _nt(a, b):
    return lax.dot_general(a, b, (((1,), (1,)), ((), ())), preferred_element_type=F32)


def _tn(a, b):
    return lax.dot_general(a, b, (((0,), (0,)), ((), ())), preferred_element_type=F32)


def _dot(a, b):
    return jnp.dot(a, b, preferred_element_type=F32)


def _split(x):
    hi = x.astype(BF16)
    lo = (x - hi.astype(F32)).astype(BF16)
    return hi, lo


def _dot3(a, b):
    ah, al = _split(a)
    bh, bl = _split(b)
    return _dot(ah, bh) + _dot(al, bh) + _dot(ah, bl)


def _dot3_narrow(a, b):
    ah, al = _split(a)
    bh, bl = _split(b)
    n = b.shape[1]
    both = _dot(ah, jnp.concatenate([bh, bl], axis=1))
    return both[:, :n] + both[:, n:] + _dot(al, bh)


def _rms(x, g):
    return x * lax.rsqrt(jnp.mean(x * x, axis=-1, keepdims=True) + EPS) * g


def _softplus(z):
    return jnp.maximum(z, 0.0) + jnp.log(1.0 + jnp.exp(-jnp.abs(z)))


def _silu(x):
    return x / (1.0 + jnp.exp(-x))


def _mod_kernel(c_ref, w_ref, b_ref, o_ref):
    c = c_ref[...]
    o_ref[0] = _dot3(_silu(c), w_ref[0]) + b_ref[0]


def _modulation(c, w_ada, b_ada):
    depth, d, n = w_ada.shape
    bsz = c.shape[0]
    tn = 1536
    return pl.pallas_call(
        _mod_kernel,
        out_shape=jax.ShapeDtypeStruct((depth, bsz, n), F32),
        grid=(depth, n // tn),
        in_specs=[
            pl.BlockSpec((bsz, d), lambda l, j: (0, 0)),
            pl.BlockSpec((1, d, tn), lambda l, j: (l, 0, j)),
            pl.BlockSpec((1, 1, tn), lambda l, j: (l, 0, j)),
        ],
        out_specs=pl.BlockSpec((1, bsz, tn), lambda l, j: (l, 0, j)),
        compiler_params=pltpu.CompilerParams(
            dimension_semantics=("arbitrary", "arbitrary"),
            vmem_limit_bytes=_vmem_limit(d * tn * 4 + bsz * (d + tn) * 4)),
        name="adaln_modulation",
    )(c, w_ada, b_ada.reshape(depth, 1, n))


def _expert_residual(x, y_ref, route_ref, mod_ref, rows=slice(None)):
    route = route_ref[0, rows, :]
    y = (route[:, 2:3] * y_ref[0, 0, rows, :].astype(F32)
         + route[:, 3:4] * y_ref[1, 0, rows, :].astype(F32))
    return x + mod_ref[0, 5:6, :] * y


def _proj_kernel(*refs, has_res, n_main, has_gate):
    it = iter(refs)
    x_ref = next(it)
    if has_res:
        y_ref = next(it)
        route_ref = next(it)
        pmod_ref = next(it)
    g_ref = next(it)
    mod_ref = next(it)
    w_ref = next(it)
    wg_ref = next(it) if has_gate else None
    xo_ref = next(it) if has_res else None
    o_ref = next(it)
    og_ref = next(it) if has_gate else None

    th = x_ref.shape[1] // PROJ_ROW_PARTS
    hbs = []
    for r in range(PROJ_ROW_PARTS):
        rows = slice(r * th, (r + 1) * th)
        x = x_ref[0, rows, :]
        if has_res:
            x = _expert_residual(x, y_ref, route_ref, pmod_ref, rows)
            xo_ref[0, rows, :] = x
        h = _rms(x, g_ref[...]) * (1.0 + mod_ref[0, 1:2, :]) + mod_ref[0, 0:1, :]
        hbs.append(h.astype(BF16))
    step = 512
    for r in range(PROJ_ROW_PARTS):
        rows = slice(r * th, (r + 1) * th)
        for n0 in range(0, n_main, step):
            o_ref[0, rows, n0:n0 + step] = _dot(hbs[r], w_ref[:, n0:n0 + step]).astype(BF16)
        if has_gate:
            og_ref[0, rows, :] = _dot(hbs[r], wg_ref[...])


def _in_projection(x, y, route, prev_mod, g, mod, w, w_gate):
    bsz, s, d = x.shape
    tm = TOKEN_TILE
    has_res = y is not None
    has_gate = w_gate is not None
    n_main = w.shape[1]
    tok = lambda b, i: (b, i, 0)
    per_b = lambda b, i: (b, 0, 0)
    const2 = lambda b, i: (0, 0)
    in_specs = [pl.BlockSpec((1, tm, d), tok)]
    args = [x]
    if has_res:
        in_specs += [pl.BlockSpec((2, 1, tm, d), lambda b, i: (0, b, i, 0)),
                     pl.BlockSpec((1, tm, LANES), tok), pl.BlockSpec((1, 6, d), per_b)]
        args += [y, route, prev_mod]
    in_specs += [pl.BlockSpec((1, d), const2), pl.BlockSpec((1, 6, d), per_b),
                 pl.BlockSpec((d, n_main), const2)]
    args += [g.reshape(1, d), mod, w]
    if has_gate:
        in_specs.append(pl.BlockSpec((d, LANES), const2))
        args.append(w_gate)
    out_shape, out_specs = [], []
    if has_res:
        out_shape.append(jax.ShapeDtypeStruct((bsz, s, d), F32))
        out_specs.append(pl.BlockSpec((1, tm, d), tok))
    out_shape.append(jax.ShapeDtypeStruct((bsz, s, n_main), BF16))
    out_specs.append(pl.BlockSpec((1, tm, n_main), tok))
    if has_gate:
        out_shape.append(jax.ShapeDtypeStruct((bsz, s, LANES), F32))
        out_specs.append(pl.BlockSpec((1, tm, LANES), tok))
    block_bytes = (tm * d * 4 * (4 if has_res else 1) + d * n_main * 2 + tm * n_main * 2
                   + d * LANES * 2 + 2 * tm * LANES * 4)
    outs = pl.pallas_call(
        functools.partial(_proj_kernel, has_res=has_res, n_main=n_main, has_gate=has_gate),
        out_shape=out_shape,
        grid=(bsz, s // tm),
        in_specs=in_specs,
        out_specs=out_specs,
        compiler_params=pltpu.CompilerParams(
            dimension_semantics=("arbitrary", "arbitrary"),
            vmem_limit_bytes=_vmem_limit(block_bytes, tm * d * 8)),
        name="norm_mod_in_projection",
    )(*args)
    outs = list(outs)
    x_new = outs.pop(0) if has_res else x
    proj = outs.pop(0)
    gate = outs.pop(0) if has_gate else None
    return x_new, proj, gate


def _t5_bucket(rel):
    nb = N_BUCKETS // 2
    max_exact = nb // 2
    ret = jnp.where(rel > 0, nb, 0)
    n = jnp.abs(rel)
    nf = jnp.maximum(n, 1).astype(F32)
    large = max_exact + (jnp.log(nf / max_exact) / math.log(MAX_DISTANCE / max_exact)
                         * (nb - max_exact)).astype(jnp.int32)
    large = jnp.minimum(large, nb - 1)
    return ret + jnp.where(n < max_exact, n, large)


def _diff_bias_tiles(rel_bias):
    t = ATTN_TILE
    assert t >= MAX_DISTANCE, "tiles two or more away must lie beyond the last bucket edge"
    qp = jnp.arange(t)[:, None]
    kp = jnp.arange(t)[None, :]
    tiles = []
    for off in (0, 1, 2):
        rel = (kp - off * t) - qp
        hit = _t5_bucket(rel)[None, :, :, None] == jnp.arange(N_BUCKETS)[None, None, None, :]
        b = jnp.sum(jnp.where(hit, jnp.transpose(rel_bias.astype(F32))[:, None, None, :], 0.0),
                    axis=-1)
        if off == 0:
            b = jnp.where((kp // CHUNK) <= (qp // CHUNK), b, NEG_BIG)
        tiles.append(b)
    tab = jnp.stack(tiles, axis=1)
    return jnp.concatenate([tab, tab], axis=2)


def _diff_attn_kernel(q_ref, k_ref, v_ref, bias_ref, lam_ref, g_ref, o_ref, *, lambda_init):
    t = ATTN_TILE
    chains = DIFF_Q_TILES
    i = pl.program_id(2)
    lane = lax.broadcasted_iota(jnp.int32, (t, LANES), 1)
    scale = jnp.asarray(DH_A ** -0.5, BF16)
    stacked = []
    for c in range(chains):
        q = q_ref[0, c * t:(c + 1) * t, :]
        zero = jnp.zeros_like(q)
        stacked += [jnp.where(lane < DH_A, q, zero), jnp.where(lane >= DH_A, q, zero)]
    qq = jnp.concatenate(stacked, axis=0) * scale

    group = DIFF_KEY_TILES
    span = group * t
    assert chains == group, "a grid step's query tiles must be exactly one key block"

    def softmax_step(carry_c, s, which_of, vb):
        m, l, acc = carry_c
        width = s.shape[1] // t

        def logits(u):
            return s[:, u * t:(u + 1) * t] + bias_ref[0, which_of(u)]

        top = logits(0)
        for u in range(1, width):
            top = jnp.maximum(top, logits(u))
        m_new = jnp.maximum(m, jnp.max(top, axis=-1, keepdims=True))
        alpha = jnp.exp(m - m_new)
        ps = [jnp.exp(logits(u) - m_new) for u in range(width)]
        tot = ps[0]
        for u in range(1, width):
            tot = tot + ps[u]
        l = alpha * l + jnp.sum(tot, axis=-1, keepdims=True)
        p = jnp.concatenate([pu.astype(BF16) for pu in ps], axis=1)
        return m_new, l, alpha * acc + _dot(p, vb)

    def body(n, carry):
        r0 = pl.multiple_of(n * span, span)
        kb = k_ref[0, pl.ds(r0, span), :]
        vb = v_ref[0, pl.ds(r0, span), :]
        s_chain = [_nt(qq[c * 2 * t:(c + 1) * 2 * t], kb) for c in range(chains)]
        out = []
        for c in range(chains):
            def which_of(u, c=c):
                return jnp.minimum((i * chains + c) - (n * group + u), 2)
            out.append(softmax_step(carry[c], s_chain[c], which_of, vb))
        return tuple(out)

    init = tuple((jnp.full((2 * t, 1), NEG_BIG, F32), jnp.zeros((2 * t, 1), F32),
                  jnp.zeros((2 * t, DV_A), F32)) for _ in range(chains))
    before = lax.fori_loop(0, i, body, init)
    r0 = pl.multiple_of(i * span, span)
    diag = [(_nt(qq[c * 2 * t:(c + 1) * 2 * t], k_ref[0, pl.ds(r0, (c + 1) * t), :]),
             v_ref[0, pl.ds(r0, (c + 1) * t), :]) for c in range(chains)]
    final = [softmax_step(before[c], diag[c][0], lambda u, c=c: min(c - u, 2), diag[c][1])
             for c in range(chains)]
    lp = lam_ref[...].astype(F32)
    lam = (jnp.exp(jnp.sum(lp[0:1] * lp[1:2], axis=-1, keepdims=True))
           - jnp.exp(jnp.sum(lp[2:3] * lp[3:4], axis=-1, keepdims=True)) + lambda_init)
    for c in range(chains):
        _, l, acc = final[c]
        o = acc / l
        w = o[:t] - lam * o[t:]
        o_ref[0, c * t:(c + 1) * t, :] = (_rms(w, g_ref[...]) * (1.0 - lambda_init)
                                          ).astype(o_ref.dtype)


def _diff_attention(proj, bias_tiles, lam_p, subln_g, layer_idx):
    bsz, s, _ = proj.shape
    t = ATTN_TILE
    tq = DIFF_Q_TILES * t
    lambda_init = 0.8 - 0.6 * math.exp(-0.3 * layer_idx)
    return pl.pallas_call(
        functools.partial(_diff_attn_kernel, lambda_init=lambda_init),
        out_shape=jax.ShapeDtypeStruct((bsz, s, H_A * DV_A), BF16),
        grid=(bsz, H_A, s // tq),
        in_specs=[
            pl.BlockSpec((1, tq, LANES), lambda b, h, i: (b, i, h)),
            pl.BlockSpec((1, s, LANES), lambda b, h, i: (b, 0, H_A + h)),
            pl.BlockSpec((1, s, LANES), lambda b, h, i: (b, 0, 2 * H_A + h)),
            pl.BlockSpec((1, 3, 2 * t, t), lambda b, h, i: (h, 0, 0, 0)),
            pl.BlockSpec((4, DH_A), lambda b, h, i: (0, 0)),
            pl.BlockSpec((1, DV_A), lambda b, h, i: (0, 0)),
        ],
        out_specs=pl.BlockSpec((1, tq, DV_A), lambda b, h, i: (b, i, h)),
        compiler_params=pltpu.CompilerParams(
            dimension_semantics=("arbitrary", "arbitrary", "arbitrary"),
            vmem_limit_bytes=_vmem_limit(2 * s * LANES * 2 + 3 * 2 * t * t * 4 + 4 * tq * LANES,
                                         4 * tq * DIFF_KEY_TILES * t * 4)),
        name="diff_attention",
    )(proj, proj, proj, bias_tiles, lam_p, subln_g.reshape(1, DV_A))


def _gla_kernel(q_ref, k_ref, v_ref, r_ref, bg_ref, wgk_ref, bgk_ref, g_ref, o_ref, st_ref):
    c = CHUNK
    ts = q_ref.shape[1]
    n_chunks = ts // c

    @pl.when(pl.program_id(1) == 0)
    def _():
        st_ref[...] = jnp.zeros_like(st_ref)

    row = lax.broadcasted_iota(jnp.int32, (ts, ts), 0)
    col = lax.broadcasted_iota(jnp.int32, (ts, ts), 1)
    same_chunk_before = ((row // c) == (col // c)) & (col <= row)
    tri = jnp.where(same_chunk_before, 1.0, 0.0).astype(BF16)
    causal = (lax.broadcasted_iota(jnp.int32, (c, c), 1)
              <= lax.broadcasted_iota(jnp.int32, (c, c), 0))
    lane = lax.broadcasted_iota(jnp.int32, (ts, LANES), 1)
    wgk = wgk_ref[...]
    bgk = bgk_ref[...]
    g = g_ref[...]
    bg = bg_ref[0]

    las = [-_softplus(-(_dot3(bg, wgk[:, hp * LANES:(hp + 1) * LANES])
                        + bgk[:, hp * LANES:(hp + 1) * LANES])) / GATE_TAU
           for hp in range(H_B // 2)]
    bs = []
    for la in las:
        la_hi, la_lo = _split(la)
        bs.append(_dot(tri, la_hi) + _dot(tri, la_lo))

    heads = []
    for hp in range(H_B // 2):
        cols = slice(hp * LANES, (hp + 1) * LANES)
        b = bs[hp]
        b_last = jnp.concatenate(
            [jnp.broadcast_to(b[(ci + 1) * c - 1:(ci + 1) * c, :], (c, LANES))
             for ci in range(n_chunks)], axis=0)
        qf = q_ref[0, :, cols].astype(F32) * (DK_B ** -0.5)
        kf = k_ref[0, :, cols].astype(F32)
        q_dec = qf * jnp.exp(b)
        k_inv = (kf * jnp.exp(-b)).astype(BF16)
        k_dec = kf * jnp.exp(b_last - b)
        decays = [jnp.exp(b[(ci + 1) * c - 1:(ci + 1) * c, :]) for ci in range(n_chunks)]
        for hh in range(2):
            own = (lane >= hh * DK_B) & (lane < (hh + 1) * DK_B)
            heads.append((jnp.where(own, q_dec, 0.0).astype(BF16), k_inv,
                          jnp.where(own, k_dec, 0.0).astype(BF16), decays))

    def rows(x, ci):
        return x[ci * c:(ci + 1) * c]

    scores, increments = [], []
    for h, (qd, k_inv, kd, _) in enumerate(heads):
        vh = v_ref[0, :, h * DV_B:(h + 1) * DV_B]
        scores.append([jnp.where(causal, _nt(rows(qd, ci), rows(k_inv, ci)), 0.0).astype(BF16)
                       for ci in range(n_chunks)])
        increments.append([_tn(rows(vh, ci), rows(kd, ci)) for ci in range(n_chunks)])

    for h, (qd, _, _, decays) in enumerate(heads):
        vh = v_ref[0, :, h * DV_B:(h + 1) * DV_B]
        st = st_ref[h]
        states = []
        for ci in range(n_chunks):
            states.append(st.astype(BF16))
            st = st * decays[ci] + increments[h][ci]
        st_ref[h] = st
        o = jnp.concatenate(
            [_dot(scores[h][ci], rows(vh, ci)) + _nt(rows(qd, ci), states[ci])
             for ci in range(n_chunks)], axis=0)
        rr = r_ref[0, :, h * DV_B:(h + 1) * DV_B].astype(F32)
        o_ref[0, :, h * DV_B:(h + 1) * DV_B] = (_rms(o, g) * _silu(rr)).astype(o_ref.dtype)


def _gla(proj, gate, w_gk2, b_gk, gla_g):
    bsz, s, _ = proj.shape
    ts = GLA_STEP
    nk = H_B * DK_B
    nv = H_B * DV_B
    w_pad = jnp.zeros((LANES, nk), F32).at[:GATE_RANK].set(w_gk2.astype(F32))
    return pl.pallas_call(
        _gla_kernel,
        out_shape=jax.ShapeDtypeStruct((bsz, s, nv), BF16),
        grid=(bsz, s // ts),
        in_specs=[
            pl.BlockSpec((1, ts, nk), lambda b, i: (b, i, 1536 // nk)),
            pl.BlockSpec((1, ts, nk), lambda b, i: (b, i, 1792 // nk)),
            pl.BlockSpec((1, ts, nv), lambda b, i: (b, i, 2048 // nv)),
            pl.BlockSpec((1, ts, nv), lambda b, i: (b, i, 2560 // nv)),
            pl.BlockSpec((1, ts, LANES), lambda b, i: (b, i, 0)),
            pl.BlockSpec((LANES, nk), lambda b, i: (0, 0)),
            pl.BlockSpec((1, nk), lambda b, i: (0, 0)),
            pl.BlockSpec((1, DV_B), lambda b, i: (0, 0)),
        ],
        out_specs=pl.BlockSpec((1, ts, nv), lambda b, i: (b, i, 0)),
        scratch_shapes=[pltpu.VMEM((H_B, DV_B, LANES), F32)],
        compiler_params=pltpu.CompilerParams(
            dimension_semantics=("arbitrary", "arbitrary"),
            vmem_limit_bytes=_vmem_limit(ts * (2 * nk + 3 * nv) * 2 + ts * LANES * 4,
                                         H_B * DV_B * LANES * 4)),
        name="gated_linear_attention",
    )(proj, proj, proj, proj, gate, w_pad, b_gk.reshape(1, nk), gla_g.reshape(1, DV_B))


def _stick_kernel(q_ref, k_ref, v_ref, o_ref, run_ref, acc_ref):
    t = ATTN_TILE
    chains = STICK_PAIRS
    i = pl.program_id(2)
    lane = lax.broadcasted_iota(jnp.int32, (t, LANES), 1)
    scale = jnp.asarray(DH_C ** -0.5, BF16)
    qqs = []
    for c in range(chains):
        q = q_ref[0, :, c * LANES:(c + 1) * LANES]
        zero = jnp.zeros_like(q)
        qqs.append(jnp.concatenate([jnp.where(lane < DH_C, q, zero),
                                    jnp.where(lane >= DH_C, q, zero)], axis=0) * scale)
    row = lax.broadcasted_iota(jnp.int32, (t, t), 0)
    col = lax.broadcasted_iota(jnp.int32, (t, t), 1)
    later = jnp.where(row > col, 1.0, 0.0).astype(BF16)
    below = (lax.broadcasted_iota(jnp.int32, (2 * t, t), 1)
             < lax.broadcasted_iota(jnp.int32, (2 * t, t), 0) % t)

    def logits(j):
        r0 = pl.multiple_of(jnp.maximum(j, 0) * t, t)
        return [_nt(qqs[c], k_ref[0, pl.ds(r0, t), c * LANES:(c + 1) * LANES])
                for c in range(chains)]

    def tiles(first, zs, runs, accs, diagonal_first):
        runs, accs = list(runs), list(accs)
        mids = []
        for u in range(len(zs)):
            for c in range(chains):
                z = zs[u][c]
                sp = _softplus(z)
                log_1m = -sp
                masked = diagonal_first and u == 0
                if masked:
                    log_1m = jnp.where(below, log_1m, 0.0)
                hi, lo = _split(log_1m)
                within = _dot(hi, later) + _dot(lo, later)
                mids.append((u, c, z - sp, log_1m, within, masked))
        for u, c, log_b, log_1m, within, masked in mids:
            r0 = pl.multiple_of((first - u) * t, t)
            vb = v_ref[0, pl.ds(r0, t), c * LANES:(c + 1) * LANES]
            a = jnp.exp(log_b + (within + runs[c]))
            if masked:
                a = jnp.where(below, a, 0.0)
            ab = a.astype(BF16)
            zero_v = jnp.zeros_like(vb)
            a2 = jnp.concatenate([ab[:t], ab[t:]], axis=1)
            v2 = jnp.concatenate([jnp.where(lane < DH_C, vb, zero_v),
                                  jnp.where(lane >= DH_C, vb, zero_v)], axis=0)
            accs[c] = accs[c] + _dot(a2, v2)
            runs[c] = runs[c] + jnp.sum(log_1m, axis=-1, keepdims=True)
        return runs, accs

    def first_block(n_tiles):
        zs = [logits(i - u) for u in range(n_tiles)]
        runs, accs = tiles(i, zs, [jnp.zeros((2 * t, 1), F32)] * chains,
                           [jnp.zeros((t, LANES), F32)] * chains, True)
        for c in range(chains):
            run_ref[c] = runs[c]
            acc_ref[c] = accs[c]

    ahead = STICK_AHEAD

    @pl.when(i >= ahead)
    def _():
        first_block(1 + ahead)

    @pl.when(i < ahead)
    def _():
        first_block(1)

    n_left = i
    done0 = jnp.where(i >= ahead, ahead, 0)

    def go_on(done, runs):
        top = runs[0]
        for c in range(1, chains):
            top = jnp.maximum(top, runs[c])
        return ((done < n_left) & (jnp.max(top) > EXP_ZERO_BELOW)).astype(jnp.int32)

    def body(carry):
        done, _, runs, accs = carry
        first = n_left - 1 - done
        runs, accs = tiles(first, [logits(first)], runs, accs, False)
        done = done + 1
        return done, go_on(done, runs), tuple(runs), tuple(accs)

    runs = tuple(run_ref[c] for c in range(chains))
    accs = tuple(acc_ref[c] for c in range(chains))
    carry = (done0, go_on(done0, runs), runs, accs)
    accs = lax.while_loop(lambda carry: carry[1] > 0, body, carry)[3]
    for c in range(chains):
        o_ref[0, :, c * LANES:(c + 1) * LANES] = accs[c].astype(o_ref.dtype)


def _stick_breaking(proj):
    bsz, s, _ = proj.shape
    t = ATTN_TILE
    width = STICK_PAIRS * LANES
    n_blocks = H_C * DH_C // width
    return pl.pallas_call(
        _stick_kernel,
        out_shape=jax.ShapeDtypeStruct((bsz, s, H_C * DH_C), BF16),
        grid=(bsz, n_blocks, s // t),
        in_specs=[
            pl.BlockSpec((1, t, width), lambda b, h, i: (b, i, h)),
            pl.BlockSpec((1, s, width), lambda b, h, i: (b, 0, n_blocks + h)),
            pl.BlockSpec((1, s, width), lambda b, h, i: (b, 0, 2 * n_blocks + h)),
        ],
        out_specs=pl.BlockSpec((1, t, width), lambda b, h, i: (b, i, h)),
        scratch_shapes=[pltpu.VMEM((STICK_PAIRS, 2 * t, 1), F32),
                        pltpu.VMEM((STICK_PAIRS, t, LANES), F32)],
        compiler_params=pltpu.CompilerParams(
            dimension_semantics=("arbitrary", "arbitrary", "arbitrary"),
            vmem_limit_bytes=_vmem_limit(2 * s * width * 2 + 4 * t * width,
                                         STICK_PAIRS * 3 * t * LANES * 4)),
        name="stick_breaking_attention",
    )(proj, proj, proj)


def _route(logits):
    lane_i = lax.broadcasted_iota(jnp.int32, logits.shape, 1)
    lane = lane_i.astype(F32)
    group_of_lane = (lane_i // EXPERTS_PER_GROUP).astype(F32)
    big = jnp.float32(1 << 20)
    neg = jnp.float32(-jnp.inf)
    is_group = (lane_i >= N_EXPERTS) & (lane_i < N_EXPERTS + N_GROUPS)
    gl = jnp.where(is_group, logits, neg)
    gmax = jnp.max(gl, axis=-1, keepdims=True)
    g_val = 1.0 / jnp.sum(jnp.exp(gl - gmax), axis=-1, keepdims=True)
    g_sel = jnp.min(jnp.where(gl == gmax, lane - N_EXPERTS, big), axis=-1, keepdims=True)
    in_group = (lane_i < N_EXPERTS) & (group_of_lane == g_sel)
    el = jnp.where(in_group, logits, neg)
    emax = jnp.max(el, axis=-1, keepdims=True)
    esum = jnp.sum(jnp.exp(el - emax), axis=-1, keepdims=True)
    i1 = jnp.min(jnp.where(el == emax, lane, big), axis=-1, keepdims=True)
    el2 = jnp.where(lane == i1, neg, el)
    emax2 = jnp.max(el2, axis=-1, keepdims=True)
    i2 = jnp.min(jnp.where(el2 == emax2, lane, big), axis=-1, keepdims=True)
    v1 = 1.0 / esum
    v2 = jnp.exp(emax2 - emax) / esum
    tot = v1 + v2
    w1 = g_val * (v1 / tot)
    w2 = g_val * (v2 / tot)
    out = jnp.where(lane_i == 0, i1, 0.0)
    out = jnp.where(lane_i == 1, i2, out)
    out = jnp.where(lane_i == 2, w1, out)
    out = jnp.where(lane_i == 3, w2, out)
    return out, (lane == i1), (lane == i2)


def _pack_halves(h):
    half = h.shape[1] // 2
    lo = pltpu.bitcast(h[:, :half].astype(BF16).astype(F32), jnp.uint32)
    hi = pltpu.bitcast(h[:, half:].astype(BF16).astype(F32), jnp.uint32)
    return lax.shift_right_logical(lo, jnp.uint32(16)) | hi


def _unpack_halves(p):
    lo = pltpu.bitcast(lax.shift_left(p, jnp.uint32(16)), F32).astype(BF16)
    hi = pltpu.bitcast(p & jnp.uint32(0xFFFF0000), F32).astype(BF16)
    return lo, hi


def _out_kernel(*refs, n_o):
    o_refs = refs[:n_o]
    w_refs = refs[n_o:2 * n_o]
    x_ref, mod_ref, g_ref, wr_ref, br_ref, xo_ref = refs[2 * n_o:2 * n_o + 6]
    h_refs = refs[2 * n_o + 6:2 * n_o + 6 + MOE_PIECES]
    r_ref, cnt_ref, run_ref = refs[2 * n_o + 6 + MOE_PIECES:]

    @pl.when((pl.program_id(0) == 0) & (pl.program_id(1) == 0))
    def _():
        run_ref[...] = jnp.zeros_like(run_ref)

    th = x_ref.shape[1] // PROJ_ROW_PARTS
    ms = []
    for r in range(PROJ_ROW_PARTS):
        rows = slice(r * th, (r + 1) * th)
        m = _dot(o_refs[0][0, rows, :], w_refs[0][...])
        for a in range(1, n_o):
            m = m + _dot(o_refs[a][0, rows, :], w_refs[a][...])
        ms.append(m)
    routes, hits1, hits2 = [], [], []
    for r in range(PROJ_ROW_PARTS):
        rows = slice(r * th, (r + 1) * th)
        x = x_ref[0, rows, :] + mod_ref[0, 2:3, :] * ms[r]
        xo_ref[0, rows, :] = x
        h = _rms(x, g_ref[...]) * (1.0 + mod_ref[0, 4:5, :]) + mod_ref[0, 3:4, :]
        piece = h.shape[1] // MOE_PIECES
        for k in range(MOE_PIECES):
            h_refs[k][0, rows, :] = _pack_halves(h[:, k * piece:(k + 1) * piece])
        part_route, part_hit1, part_hit2 = _route(_dot3_narrow(h, wr_ref[...]) + br_ref[...])
        routes.append(part_route)
        hits1.append(part_hit1)
        hits2.append(part_hit2)
    route = jnp.concatenate(routes, axis=0)
    hit1 = jnp.concatenate(hits1, axis=0)
    hit2 = jnp.concatenate(hits2, axis=0)
    tm = route.shape[0]
    picks = jnp.where(hit1 | hit2, 1.0, 0.0)
    row = lax.broadcasted_iota(jnp.int32, (tm, tm), 0)
    col = lax.broadcasted_iota(jnp.int32, (tm, tm), 1)
    before = jnp.where(col < row, 1.0, 0.0).astype(BF16)
    earlier = _dot(before, picks.astype(BF16)) + run_ref[...]
    rank1 = jnp.sum(jnp.where(hit1, earlier, 0.0), axis=-1, keepdims=True)
    rank2 = jnp.sum(jnp.where(hit2, earlier, 0.0), axis=-1, keepdims=True)
    lane_i = lax.broadcasted_iota(jnp.int32, route.shape, 1)
    route = jnp.where(lane_i == 4, rank1, route)
    route = jnp.where(lane_i == 5, rank2, route)
    r_ref[0] = route
    run_ref[...] = run_ref[...] + jnp.sum(picks, axis=0, keepdims=True)
    cnt_ref[...] = run_ref[...]


def _out_projection(o_list, w_list, x, mod, g, w_router, b_router):
    bsz, s, d = x.shape
    tm = TOKEN_TILE
    n_o = len(o_list)
    packed = d // (2 * MOE_PIECES)
    tok = lambda b, i: (b, i, 0)
    const2 = lambda b, i: (0, 0)
    in_specs = ([pl.BlockSpec((1, tm, o.shape[2]), tok) for o in o_list]
                + [pl.BlockSpec(w.shape, const2) for w in w_list]
                + [pl.BlockSpec((1, tm, d), tok), pl.BlockSpec((1, 6, d), lambda b, i: (b, 0, 0)),
                   pl.BlockSpec((1, d), const2), pl.BlockSpec((d, LANES), const2),
                   pl.BlockSpec((1, LANES), const2)])
    block_bytes = (sum(tm * o.shape[2] * 2 for o in o_list) + sum(w.size * 2 for w in w_list)
                   + 2 * tm * d * 4 + tm * d * 2 + d * LANES * 4 + tm * LANES * 4)
    return pl.pallas_call(
        functools.partial(_out_kernel, n_o=n_o),
        out_shape=([jax.ShapeDtypeStruct((bsz, s, d), F32)]
                   + [jax.ShapeDtypeStruct((bsz, s, packed), jnp.uint32)] * MOE_PIECES
                   + [jax.ShapeDtypeStruct((bsz, s, LANES), F32),
                      jax.ShapeDtypeStruct((1, LANES), F32)]),
        grid=(bsz, s // tm),
        in_specs=in_specs,
        out_specs=([pl.BlockSpec((1, tm, d), tok)]
                   + [pl.BlockSpec((1, tm, packed), tok)] * MOE_PIECES
                   + [pl.BlockSpec((1, tm, LANES), tok), pl.BlockSpec((1, LANES), const2)]),
        scratch_shapes=[pltpu.VMEM((1, LANES), F32)],
        compiler_params=pltpu.CompilerParams(
            dimension_semantics=("arbitrary", "arbitrary"),
            vmem_limit_bytes=_vmem_limit(block_bytes, tm * d * 12 + tm * tm * 4)),
        name="out_projection_norm_router",
    )(*o_list, *w_list, x, mod, g.reshape(1, d), w_router, b_router)


def _expert_kernel(te_ref, tv_ref, nu_ref, *refs):
    x_refs = refs[:MOE_PIECES]
    wg_ref, wu_ref, wd_ref, o_ref, wgb_ref, wub_ref, wdb_ref = refs[MOE_PIECES:]
    t = pl.program_id(0)
    prev = te_ref[jnp.maximum(t - 1, 0)]

    @pl.when((t == 0) | (te_ref[t] != prev))
    def _():
        wgb_ref[...] = wg_ref[0, 0].astype(BF16)
        wub_ref[...] = wu_ref[0, 0].astype(BF16)
        wdb_ref[...] = wd_ref[0, 0].astype(BF16)

    @pl.when(t < nu_ref[0])
    def _():
        cols = []
        for x_ref in x_refs:
            row = lax.broadcasted_iota(jnp.int32, x_ref.shape, 0)
            cols += list(_unpack_halves(jnp.where(row < tv_ref[t], x_ref[...], jnp.uint32(0))))
        w = cols[0].shape[1]
        gt = _dot(cols[0], wgb_ref[:w, :])
        up = _dot(cols[0], wub_ref[:w, :])
        for k in range(1, len(cols)):
            gt = gt + _dot(cols[k], wgb_ref[k * w:(k + 1) * w, :])
            up = up + _dot(cols[k], wub_ref[k * w:(k + 1) * w, :])
        o_ref[...] = _dot((_silu(gt) * up).astype(BF16), wdb_ref[...]).astype(o_ref.dtype)

    @pl.when(t >= nu_ref[0])
    def _():
        o_ref[...] = jnp.zeros_like(o_ref)


def _grouped_experts(xs, tile_expert, tile_valid, n_used, w_gate, w_up, w_down, layer):
    p, packed = xs[0].shape
    d = w_gate.shape[2]
    tm = EXPERT_TILE
    f = w_gate.shape[3]
    grid_spec = pltpu.PrefetchScalarGridSpec(
        num_scalar_prefetch=3,
        grid=(p // tm,),
        in_specs=[pl.BlockSpec((tm, packed), lambda t, te, tv, nu: (t, 0))] * MOE_PIECES + [
            pl.BlockSpec((1, 1, d, f), lambda t, te, tv, nu: (layer, te[t], 0, 0)),
            pl.BlockSpec((1, 1, d, f), lambda t, te, tv, nu: (layer, te[t], 0, 0)),
            pl.BlockSpec((1, 1, f, d), lambda t, te, tv, nu: (layer, te[t], 0, 0)),
        ],
        out_specs=pl.BlockSpec((tm, d), lambda t, te, tv, nu: (t, 0)),
        scratch_shapes=[pltpu.VMEM((d, f), BF16), pltpu.VMEM((d, f), BF16),
                        pltpu.VMEM((f, d), BF16)],
    )
    block_bytes = tm * d * 2 + 3 * d * f * 4 + tm * d * 2
    return pl.pallas_call(
        _expert_kernel,
        out_shape=jax.ShapeDtypeStruct((p, d), BF16),
        grid_spec=grid_spec,
        compiler_params=pltpu.CompilerParams(
            dimension_semantics=("arbitrary",),
            vmem_limit_bytes=_vmem_limit(block_bytes, 3 * d * f * 2 + tm * f * 16)),
        name="grouped_swiglu_experts",
    )(tile_expert, tile_valid, n_used, *xs, w_gate, w_up, w_down)


def _pos_kernel(route_ref, starts_ref, pos_ref):
    rt = jnp.transpose(route_ref[...])
    starts = starts_ref[...]
    ids = lax.broadcasted_iota(jnp.int32, (N_EXPERTS, rt.shape[1]), 0).astype(F32)
    for c in range(2):
        start_of = jnp.sum(jnp.where(ids == rt[c:c + 1, :], starts, 0.0), axis=0, keepdims=True)
        pos_ref[c:c + 1, :] = (start_of + rt[4 + c:5 + c, :]).astype(jnp.int32)


def _row_positions(route, starts):
    n_tok = route.shape[0]
    tm = TOKEN_TILE
    return pl.pallas_call(
        _pos_kernel,
        out_shape=jax.ShapeDtypeStruct((2, n_tok), jnp.int32),
        grid=(n_tok // tm,),
        in_specs=[pl.BlockSpec((tm, LANES), lambda i: (i, 0)),
                  pl.BlockSpec((N_EXPERTS, 1), lambda i: (0, 0))],
        out_specs=pl.BlockSpec((2, tm), lambda i: (0, i)),
        compiler_params=pltpu.CompilerParams(
            dimension_semantics=("arbitrary",),
            vmem_limit_bytes=_vmem_limit(tm * LANES * 4 + N_EXPERTS * LANES * 4 + 8 * tm * 4)),
        name="expert_row_positions",
    )(route, starts.astype(F32).reshape(N_EXPERTS, 1))


def _scatter_rows(rows, pos, n_out):
    n_tok, d = rows.shape
    n_idx = pos.shape[0]
    w = SC_SCATTER_WINDOW
    per_pass = n_tok // w
    mesh = plsc.VectorSubcoreMesh(core_axis_name="core", subcore_axis_name="subcore")

    @pl.kernel(out_type=jax.ShapeDtypeStruct((n_out, d), rows.dtype), mesh=mesh, scratch_types=[])
    def scatter_kernel(x_hbm, i_hbm, o_hbm):
        def body(x_vmem, i_vmem):
            pltpu.sync_copy(x_vmem, o_hbm.at[i_vmem.at[0]])

        pltpu.emit_pipeline(
            body,
            grid=(n_idx // w,),
            in_specs=[pl.BlockSpec((w, d), lambda i: (i % per_pass, 0)),
                      pl.BlockSpec((1, w), lambda i: (0, i))],
            out_specs=[],
            core_axis_name=("core", "subcore"),
            dimension_semantics=(pltpu.PARALLEL,),
        )(x_hbm, i_hbm)

    return scatter_kernel(rows, pos.reshape(1, n_idx))


def _dispatch(route, counts, n_tok):
    tm = EXPERT_TILE
    counts = counts[:N_EXPERTS].astype(jnp.int32)
    padded = ((counts + tm - 1) // tm) * tm
    ends = jnp.cumsum(padded)
    starts = ends - padded
    pos = _row_positions(route, starts).reshape(2 * n_tok)
    p = 2 * n_tok + N_EXPERTS * tm
    tile_start = jnp.arange(p // tm, dtype=jnp.int32) * tm
    tile_expert = jnp.minimum(jnp.sum(ends[None, :] <= tile_start[:, None], axis=1),
                              N_EXPERTS - 1).astype(jnp.int32)
    n_used = (ends[-1] // tm).astype(jnp.int32).reshape(1)
    row_end = starts + counts
    tile_valid = jnp.clip(jnp.sum(jnp.where(tile_expert[:, None] == jnp.arange(N_EXPERTS)[None, :],
                                            row_end[None, :], 0), axis=1) - tile_start, 0, tm)
    return p, tile_expert, tile_valid.astype(jnp.int32), n_used, pos


def _final_kernel(x_ref, y_ref, route_ref, mod_ref, g_ref, o_ref):
    o_ref[0] = _rms(_expert_residual(x_ref[0], y_ref, route_ref, mod_ref), g_ref[...])


def _final_norm(x, y, route, mod, g):
    bsz, s, d = x.shape
    tm = TOKEN_TILE
    tok = lambda b, i: (b, i, 0)
    return pl.pallas_call(
        _final_kernel,
        out_shape=jax.ShapeDtypeStruct((bsz, s, d), F32),
        grid=(bsz, s // tm),
        in_specs=[pl.BlockSpec((1, tm, d), tok),
                  pl.BlockSpec((2, 1, tm, d), lambda b, i: (0, b, i, 0)),
                  pl.BlockSpec((1, tm, LANES), tok),
                  pl.BlockSpec((1, 6, d), lambda b, i: (b, 0, 0)),
                  pl.BlockSpec((1, d), lambda b, i: (0, 0))],
        out_specs=pl.BlockSpec((1, tm, d), tok),
        compiler_params=pltpu.CompilerParams(
            dimension_semantics=("arbitrary", "arbitrary"),
            vmem_limit_bytes=_vmem_limit(4 * tm * d * 4 + tm * LANES * 4)),
        name="residual_final_norm",
    )(x, y, route, mod, g.reshape(1, d))


def kernel(x, c, w_ada, b_ada, norm_mix, norm_ffn, norm_final, rel_bias, even_w_in, even_lambda, even_subln, even_w_gk2, even_b_gk, even_gla_norm, even_w_out, odd_w_in, odd_w_out, router_group_w, router_group_b, router_expert_w, router_expert_b, expert_w_gate, expert_w_up, expert_w_down):
    bsz, s, d = x.shape
    n_tok = bsz * s
    assert s % TOKEN_TILE == 0 and s % GLA_STEP == 0
    assert s % (DIFF_KEY_TILES * ATTN_TILE) == 0 and s % (DIFF_Q_TILES * ATTN_TILE) == 0
    mod_all = _modulation(c, w_ada, b_ada).reshape(DEPTH, bsz, 6, d)
    bias_tiles = _diff_bias_tiles(rel_bias)
    n_even_main = 3072
    y = route = prev_mod = None
    for l in range(DEPTH):
        mod = mod_all[l]
        i = l // 2
        if l % 2 == 0:
            w_in = even_w_in[i]
            w_main = w_in[:, :n_even_main].astype(BF16)
            w_gate = jnp.zeros((d, LANES), BF16).at[:, :GATE_RANK].set(
                w_in[:, n_even_main:].astype(BF16))
            x, proj, gate = _in_projection(x, y, route, prev_mod, norm_mix[l], mod, w_main, w_gate)
            oa = _diff_attention(proj, bias_tiles, even_lambda[i], even_subln[i], l)
            ob = _gla(proj, gate, even_w_gk2[i], even_b_gk[i], even_gla_norm[i])
            w_out = even_w_out[i].astype(BF16)
            o_list = [oa, ob]
            w_list = [w_out[:H_A * DV_A], w_out[H_A * DV_A:]]
        else:
            x, proj, _ = _in_projection(x, y, route, prev_mod, norm_mix[l], mod,
                                        odd_w_in[i].astype(BF16), None)
            o_list = [_stick_breaking(proj)]
            w_list = [odd_w_out[i].astype(BF16)]
        w_router = jnp.zeros((d, LANES), F32)
        w_router = w_router.at[:, :N_EXPERTS].set(router_expert_w[l].astype(F32))
        w_router = w_router.at[:, N_EXPERTS:N_EXPERTS + N_GROUPS].set(router_group_w[l].astype(F32))
        b_router = jnp.zeros((1, LANES), F32)
        b_router = b_router.at[0, :N_EXPERTS].set(router_expert_b[l].astype(F32))
        b_router = b_router.at[0, N_EXPERTS:N_EXPERTS + N_GROUPS].set(router_group_b[l].astype(F32))
        outs = _out_projection(o_list, w_list, x, mod, norm_ffn[l], w_router, b_router)
        x, pieces, route, counts = outs[0], outs[1:1 + MOE_PIECES], outs[-2], outs[-1]
        n_rows, tile_expert, tile_valid, n_used, pos = _dispatch(route.reshape(n_tok, LANES),
                                                                 counts[0], n_tok)
        xs = [_scatter_rows(piece.reshape(n_tok, piece.shape[2]), pos, n_rows) for piece in pieces]
        out = _grouped_experts(xs, tile_expert, tile_valid, n_used, expert_w_gate, expert_w_up,
                               expert_w_down, l)
        y = out.at[pos].get(mode="promise_in_bounds").reshape(2, bsz, s, d)
        prev_mod = mod
    return _final_norm(x, y, route, prev_mod, norm_final)
```

```python
import functools
import math

import numpy as np
import jax
import jax.numpy as jnp
from jax import lax
from jax.experimental import pallas as pl
from jax.experimental.pallas import tpu as pltpu
from jax.experimental.pallas import tpu_sc as plsc

F32 = jnp.float32
BF16 = jnp.bfloat16

D_MODEL = 1024
DEPTH = 4
CHUNK = 64
EPS = 1e-6
H_A, DH_A, DV_A = 4, 64, 128
H_B, DK_B, DV_B = 4, 64, 128
GATE_RANK = 16
GATE_TAU = 16.0
H_C, DH_C = 16, 64
N_BUCKETS = 32
MAX_DISTANCE = 128
N_GROUPS = 4
EXPERTS_PER_GROUP = 8
N_EXPERTS = N_GROUPS * EXPERTS_PER_GROUP
D_EXPERT = 512
MIX = 1024

LANES = 128
V7X_VMEM_BYTES = 64 * 1024 * 1024
NEG_BIG = -1e30
EXP_ZERO_BELOW = -104.0

TOKEN_TILE = 512
PROJ_ROW_PARTS = 2
ATTN_TILE = 128
STICK_PAIRS = 8
STICK_AHEAD = 2
DIFF_KEY_TILES = 8
DIFF_Q_TILES = 8
GLA_STEP = 256
EXPERT_TILE = 512
EXPERT_ROW_PARTS = 4
SC_SCATTER_WINDOW = 128
MOE_PIECES = 2


def _vmem_limit(block_bytes, scratch_bytes=0):
    est = 2 * block_bytes + scratch_bytes + 16 * 1024 * 1024
    return int(min(est, V7X_VMEM_BYTES - 8 * 1024 * 1024))


def _nt(a, b):
    return lax.dot_general(a, b, (((1,), (1,)), ((), ())), preferred_element_type=F32)


def _tn(a, b):
    return lax.dot_general(a, b, (((0,), (0,)), ((), ())), preferred_element_type=F32)


def _dot(a, b):
    return jnp.dot(a, b, preferred_element_type=F32)


def _split(x):
    hi = x.astype(BF16)
    lo = (x - hi.astype(F32)).astype(BF16)
    return hi, lo


def _dot3(a, b):
    ah, al = _split(a)
    bh, bl = _split(b)
    return _dot(ah, bh) + _dot(al, bh) + _dot(ah, bl)


def _dot3_narrow(a, b):
    ah, al = _split(a)
    bh, bl = _split(b)
    n = b.shape[1]
    both = _dot(ah, jnp.concatenate([bh, bl], axis=1))
    return both[:, :n] + both[:, n:] + _dot(al, bh)


def _rms(x, g):
    return x * lax.rsqrt(jnp.mean(x * x, axis=-1, keepdims=True) + EPS) * g


def _softplus(z):
    return jnp.maximum(z, 0.0) + jnp.log(1.0 + jnp.exp(-jnp.abs(z)))


def _silu(x):
    return x / (1.0 + jnp.exp(-x))


def _mod_kernel(c_ref, w_ref, b_ref, o_ref):
    c = c_ref[...]
    o_ref[0] = _dot3(_silu(c), w_ref[0]) + b_ref[0]


def _modulation(c, w_ada, b_ada):
    depth, d, n = w_ada.shape
    bsz = c.shape[0]
    tn = 1536
    return pl.pallas_call(
        _mod_kernel,
        out_shape=jax.ShapeDtypeStruct((depth, bsz, n), F32),
        grid=(depth, n // tn),
        in_specs=[
            pl.BlockSpec((bsz, d), lambda l, j: (0, 0)),
            pl.BlockSpec((1, d, tn), lambda l, j: (l, 0, j)),
            pl.BlockSpec((1, 1, tn), lambda l, j: (l, 0, j)),
        ],
        out_specs=pl.BlockSpec((1, bsz, tn), lambda l, j: (l, 0, j)),
        compiler_params=pltpu.CompilerParams(
            dimension_semantics=("arbitrary", "arbitrary"),
            vmem_limit_bytes=_vmem_limit(d * tn * 4 + bsz * (d + tn) * 4)),
        name="adaln_modulation",
    )(c, w_ada, b_ada.reshape(depth, 1, n))


def _expert_residual(x, y_ref, route_ref, mod_ref, rows=slice(None)):
    route = route_ref[0, rows, :]
    y = (route[:, 2:3] * y_ref[0, 0, rows, :].astype(F32)
         + route[:, 3:4] * y_ref[1, 0, rows, :].astype(F32))
    return x + mod_ref[0, 5:6, :] * y


def _proj_kernel(*refs, has_res, n_main, has_gate):
    it = iter(refs)
    x_ref = next(it)
    if has_res:
        y_ref = next(it)
        route_ref = next(it)
        pmod_ref = next(it)
    g_ref = next(it)
    mod_ref = next(it)
    w_ref = next(it)
    wg_ref = next(it) if has_gate else None
    xo_ref = next(it) if has_res else None
    o_ref = next(it)
    og_ref = next(it) if has_gate else None

    th = x_ref.shape[1] // PROJ_ROW_PARTS
    hbs = []
    for r in range(PROJ_ROW_PARTS):
        rows = slice(r * th, (r + 1) * th)
        x = x_ref[0, rows, :]
        if has_res:
            x = _expert_residual(x, y_ref, route_ref, pmod_ref, rows)
            xo_ref[0, rows, :] = x
        h = _rms(x, g_ref[...]) * (1.0 + mod_ref[0, 1:2, :]) + mod_ref[0, 0:1, :]
        hbs.append(h.astype(BF16))
    step = 512
    for r in range(PROJ_ROW_PARTS):
        rows = slice(r * th, (r + 1) * th)
        for n0 in range(0, n_main, step):
            o_ref[0, rows, n0:n0 + step] = _dot(hbs[r], w_ref[:, n0:n0 + step]).astype(BF16)
        if has_gate:
            og_ref[0, rows, :] = _dot(hbs[r], wg_ref[...])


def _in_projection(x, y, route, prev_mod, g, mod, w, w_gate):
    bsz, s, d = x.shape
    tm = TOKEN_TILE
    has_res = y is not None
    has_gate = w_gate is not None
    n_main = w.shape[1]
    tok = lambda b, i: (b, i, 0)
    per_b = lambda b, i: (b, 0, 0)
    const2 = lambda b, i: (0, 0)
    in_specs = [pl.BlockSpec((1, tm, d), tok)]
    args = [x]
    if has_res:
        in_specs += [pl.BlockSpec((2, 1, tm, d), lambda b, i: (0, b, i, 0)),
                     pl.BlockSpec((1, tm, LANES), tok), pl.BlockSpec((1, 6, d), per_b)]
        args += [y, route, prev_mod]
    in_specs += [pl.BlockSpec((1, d), const2), pl.BlockSpec((1, 6, d), per_b),
                 pl.BlockSpec((d, n_main), const2)]
    args += [g.reshape(1, d), mod, w]
    if has_gate:
        in_specs.append(pl.BlockSpec((d, LANES), const2))
        args.append(w_gate)
    out_shape, out_specs = [], []
    if has_res:
        out_shape.append(jax.ShapeDtypeStruct((bsz, s, d), F32))
        out_specs.append(pl.BlockSpec((1, tm, d), tok))
    out_shape.append(jax.ShapeDtypeStruct((bsz, s, n_main), BF16))
    out_specs.append(pl.BlockSpec((1, tm, n_main), tok))
    if has_gate:
        out_shape.append(jax.ShapeDtypeStruct((bsz, s, LANES), F32))
        out_specs.append(pl.BlockSpec((1, tm, LANES), tok))
    block_bytes = (tm * d * 4 * (4 if has_res else 1) + d * n_main * 2 + tm * n_main * 2
                   + d * LANES * 2 + 2 * tm * LANES * 4)
    outs = pl.pallas_call(
        functools.partial(_proj_kernel, has_res=has_res, n_main=n_main, has_gate=has_gate),
        out_shape=out_shape,
        grid=(bsz, s // tm),
        in_specs=in_specs,
        out_specs=out_specs,
        compiler_params=pltpu.CompilerParams(
            dimension_semantics=("arbitrary", "arbitrary"),
            vmem_limit_bytes=_vmem_limit(block_bytes, tm * d * 8)),
        name="norm_mod_in_projection",
    )(*args)
    outs = list(outs)
    x_new = outs.pop(0) if has_res else x
    proj = outs.pop(0)
    gate = outs.pop(0) if has_gate else None
    return x_new, proj, gate


def _t5_bucket(rel):
    nb = N_BUCKETS // 2
    max_exact = nb // 2
    ret = jnp.where(rel > 0, nb, 0)
    n = jnp.abs(rel)
    nf = jnp.maximum(n, 1).astype(F32)
    large = max_exact + (jnp.log(nf / max_exact) / math.log(MAX_DISTANCE / max_exact)
                         * (nb - max_exact)).astype(jnp.int32)
    large = jnp.minimum(large, nb - 1)
    return ret + jnp.where(n < max_exact, n, large)


def _diff_bias_tiles(rel_bias):
    t = ATTN_TILE
    assert t >= MAX_DISTANCE, "tiles two or more away must lie beyond the last bucket edge"
    qp = jnp.arange(t)[:, None]
    kp = jnp.arange(t)[None, :]
    tiles = []
    for off in (0, 1, 2):
        rel = (kp - off * t) - qp
        hit = _t5_bucket(rel)[None, :, :, None] == jnp.arange(N_BUCKETS)[None, None, None, :]
        b = jnp.sum(jnp.where(hit, jnp.transpose(rel_bias.astype(F32))[:, None, None, :], 0.0),
                    axis=-1)
        if off == 0:
            b = jnp.where((kp // CHUNK) <= (qp // CHUNK), b, NEG_BIG)
        tiles.append(b)
    tab = jnp.stack(tiles, axis=1)
    return jnp.concatenate([tab, tab], axis=2)


def _diff_attn_kernel(q_ref, k_ref, v_ref, bias_ref, lam_ref, g_ref, o_ref, *, lambda_init):
    t = ATTN_TILE
    chains = DIFF_Q_TILES
    i = pl.program_id(2)
    lane = lax.broadcasted_iota(jnp.int32, (t, LANES), 1)
    scale = jnp.asarray(DH_A ** -0.5, BF16)
    stacked = []
    for c in range(chains):
        q = q_ref[0, c * t:(c + 1) * t, :]
        zero = jnp.zeros_like(q)
        stacked += [jnp.where(lane < DH_A, q, zero), jnp.where(lane >= DH_A, q, zero)]
    qq = jnp.concatenate(stacked, axis=0) * scale

    group = DIFF_KEY_TILES
    span = group * t
    assert chains == group, "a grid step's query tiles must be exactly one key block"

    def softmax_step(carry_c, s, which_of, vb):
        m, l, acc = carry_c
        width = s.shape[1] // t

        def logits(u):
            return s[:, u * t:(u + 1) * t] + bias_ref[0, which_of(u)]

        top = logits(0)
        for u in range(1, width):
            top = jnp.maximum(top, logits(u))
        m_new = jnp.maximum(m, jnp.max(top, axis=-1, keepdims=True))
        alpha = jnp.exp(m - m_new)
        ps = [jnp.exp(logits(u) - m_new) for u in range(width)]
        tot = ps[0]
        for u in range(1, width):
            tot = tot + ps[u]
        l = alpha * l + jnp.sum(tot, axis=-1, keepdims=True)
        p = jnp.concatenate([pu.astype(BF16) for pu in ps], axis=1)
        return m_new, l, alpha * acc + _dot(p, vb)

    def body(n, carry):
        r0 = pl.multiple_of(n * span, span)
        kb = k_ref[0, pl.ds(r0, span), :]
        vb = v_ref[0, pl.ds(r0, span), :]
        s_chain = [_nt(qq[c * 2 * t:(c + 1) * 2 * t], kb) for c in range(chains)]
        out = []
        for c in range(chains):
            def which_of(u, c=c):
                return jnp.minimum((i * chains + c) - (n * group + u), 2)
            out.append(softmax_step(carry[c], s_chain[c], which_of, vb))
        return tuple(out)

    init = tuple((jnp.full((2 * t, 1), NEG_BIG, F32), jnp.zeros((2 * t, 1), F32),
                  jnp.zeros((2 * t, DV_A), F32)) for _ in range(chains))
    before = lax.fori_loop(0, i, body, init)
    r0 = pl.multiple_of(i * span, span)
    diag = [(_nt(qq[c * 2 * t:(c + 1) * 2 * t], k_ref[0, pl.ds(r0, (c + 1) * t), :]),
             v_ref[0, pl.ds(r0, (c + 1) * t), :]) for c in range(chains)]
    final = [softmax_step(before[c], diag[c][0], lambda u, c=c: min(c - u, 2), diag[c][1])
             for c in range(chains)]
    lp = lam_ref[...].astype(F32)
    lam = (jnp.exp(jnp.sum(lp[0:1] * lp[1:2], axis=-1, keepdims=True))
           - jnp.exp(jnp.sum(lp[2:3] * lp[3:4], axis=-1, keepdims=True)) + lambda_init)
    for c in range(chains):
        _, l, acc = final[c]
        o = acc / l
        w = o[:t] - lam * o[t:]
        o_ref[0, c * t:(c + 1) * t, :] = (_rms(w, g_ref[...]) * (1.0 - lambda_init)
                                          ).astype(o_ref.dtype)


def _diff_attention(proj, bias_tiles, lam_p, subln_g, layer_idx):
    bsz, s, _ = proj.shape
    t = ATTN_TILE
    tq = DIFF_Q_TILES * t
    lambda_init = 0.8 - 0.6 * math.exp(-0.3 * layer_idx)
    return pl.pallas_call(
        functools.partial(_diff_attn_kernel, lambda_init=lambda_init),
        out_shape=jax.ShapeDtypeStruct((bsz, s, H_A * DV_A), BF16),
        grid=(bsz, H_A, s // tq),
        in_specs=[
            pl.BlockSpec((1, tq, LANES), lambda b, h, i: (b, i, h)),
            pl.BlockSpec((1, s, LANES), lambda b, h, i: (b, 0, H_A + h)),
            pl.BlockSpec((1, s, LANES), lambda b, h, i: (b, 0, 2 * H_A + h)),
            pl.BlockSpec((1, 3, 2 * t, t), lambda b, h, i: (h, 0, 0, 0)),
            pl.BlockSpec((4, DH_A), lambda b, h, i: (0, 0)),
            pl.BlockSpec((1, DV_A), lambda b, h, i: (0, 0)),
        ],
        out_specs=pl.BlockSpec((1, tq, DV_A), lambda b, h, i: (b, i, h)),
        compiler_params=pltpu.CompilerParams(
            dimension_semantics=("arbitrary", "arbitrary", "arbitrary"),
            vmem_limit_bytes=_vmem_limit(2 * s * LANES * 2 + 3 * 2 * t * t * 4 + 4 * tq * LANES,
                                         4 * tq * DIFF_KEY_TILES * t * 4)),
        name="diff_attention",
    )(proj, proj, proj, bias_tiles, lam_p, subln_g.reshape(1, DV_A))


def _gla_kernel(q_ref, k_ref, v_ref, r_ref, bg_ref, wgk_ref, bgk_ref, g_ref, o_ref, st_ref):
    c = CHUNK
    ts = q_ref.shape[1]
    n_chunks = ts // c

    @pl.when(pl.program_id(1) == 0)
    def _():
        st_ref[...] = jnp.zeros_like(st_ref)

    row = lax.broadcasted_iota(jnp.int32, (ts, ts), 0)
    col = lax.broadcasted_iota(jnp.int32, (ts, ts), 1)
    same_chunk_before = ((row // c) == (col // c)) & (col <= row)
    tri = jnp.where(same_chunk_before, 1.0, 0.0).astype(BF16)
    causal = (lax.broadcasted_iota(jnp.int32, (c, c), 1)
              <= lax.broadcasted_iota(jnp.int32, (c, c), 0))
    lane = lax.broadcasted_iota(jnp.int32, (ts, LANES), 1)
    wgk = wgk_ref[...]
    bgk = bgk_ref[...]
    g = g_ref[...]
    bg = bg_ref[0]

    las = [-_softplus(-(_dot3(bg, wgk[:, hp * LANES:(hp + 1) * LANES])
                        + bgk[:, hp * LANES:(hp + 1) * LANES])) / GATE_TAU
           for hp in range(H_B // 2)]
    bs = []
    for la in las:
        la_hi, la_lo = _split(la)
        bs.append(_dot(tri, la_hi) + _dot(tri, la_lo))

    heads = []
    for hp in range(H_B // 2):
        cols = slice(hp * LANES, (hp + 1) * LANES)
        b = bs[hp]
        b_last = jnp.concatenate(
            [jnp.broadcast_to(b[(ci + 1) * c - 1:(ci + 1) * c, :], (c, LANES))
             for ci in range(n_chunks)], axis=0)
        qf = q_ref[0, :, cols].astype(F32) * (DK_B ** -0.5)
        kf = k_ref[0, :, cols].astype(F32)
        q_dec = qf * jnp.exp(b)
        k_inv = (kf * jnp.exp(-b)).astype(BF16)
        k_dec = kf * jnp.exp(b_last - b)
        decays = [jnp.exp(b[(ci + 1) * c - 1:(ci + 1) * c, :]) for ci in range(n_chunks)]
        for hh in range(2):
            own = (lane >= hh * DK_B) & (lane < (hh + 1) * DK_B)
            heads.append((jnp.where(own, q_dec, 0.0).astype(BF16), k_inv,
                          jnp.where(own, k_dec, 0.0).astype(BF16), decays))

    def rows(x, ci):
        return x[ci * c:(ci + 1) * c]

    scores, increments = [], []
    for h, (qd, k_inv, kd, _) in enumerate(heads):
        vh = v_ref[0, :, h * DV_B:(h + 1) * DV_B]
        scores.append([jnp.where(causal, _nt(rows(qd, ci), rows(k_inv, ci)), 0.0).astype(BF16)
                       for ci in range(n_chunks)])
        increments.append([_tn(rows(vh, ci), rows(kd, ci)) for ci in range(n_chunks)])

    for h, (qd, _, _, decays) in enumerate(heads):
        vh = v_ref[0, :, h * DV_B:(h + 1) * DV_B]
        st = st_ref[h]
        states = []
        for ci in range(n_chunks):
            states.append(st.astype(BF16))
            st = st * decays[ci] + increments[h][ci]
        st_ref[h] = st
        o = jnp.concatenate(
            [_dot(scores[h][ci], rows(vh, ci)) + _nt(rows(qd, ci), states[ci])
             for ci in range(n_chunks)], axis=0)
        rr = r_ref[0, :, h * DV_B:(h + 1) * DV_B].astype(F32)
        o_ref[0, :, h * DV_B:(h + 1) * DV_B] = (_rms(o, g) * _silu(rr)).astype(o_ref.dtype)


def _gla(proj, gate, w_gk2, b_gk, gla_g):
    bsz, s, _ = proj.shape
    ts = GLA_STEP
    nk = H_B * DK_B
    nv = H_B * DV_B
    w_pad = jnp.zeros((LANES, nk), F32).at[:GATE_RANK].set(w_gk2.astype(F32))
    return pl.pallas_call(
        _gla_kernel,
        out_shape=jax.ShapeDtypeStruct((bsz, s, nv), BF16),
        grid=(bsz, s // ts),
        in_specs=[
            pl.BlockSpec((1, ts, nk), lambda b, i: (b, i, 1536 // nk)),
            pl.BlockSpec((1, ts, nk), lambda b, i: (b, i, 1792 // nk)),
            pl.BlockSpec((1, ts, nv), lambda b, i: (b, i, 2048 // nv)),
            pl.BlockSpec((1, ts, nv), lambda b, i: (b, i, 2560 // nv)),
            pl.BlockSpec((1, ts, LANES), lambda b, i: (b, i, 0)),
            pl.BlockSpec((LANES, nk), lambda b, i: (0, 0)),
            pl.BlockSpec((1, nk), lambda b, i: (0, 0)),
            pl.BlockSpec((1, DV_B), lambda b, i: (0, 0)),
        ],
        out_specs=pl.BlockSpec((1, ts, nv), lambda b, i: (b, i, 0)),
        scratch_shapes=[pltpu.VMEM((H_B, DV_B, LANES), F32)],
        compiler_params=pltpu.CompilerParams(
            dimension_semantics=("arbitrary", "arbitrary"),
            vmem_limit_bytes=_vmem_limit(ts * (2 * nk + 3 * nv) * 2 + ts * LANES * 4,
                                         H_B * DV_B * LANES * 4)),
        name="gated_linear_attention",
    )(proj, proj, proj, proj, gate, w_pad, b_gk.reshape(1, nk), gla_g.reshape(1, DV_B))


def _stick_kernel(q_ref, k_ref, v_ref, o_ref, run_ref, acc_ref):
    t = ATTN_TILE
    chains = STICK_PAIRS
    i = pl.program_id(2)
    lane = lax.broadcasted_iota(jnp.int32, (t, LANES), 1)
    scale = jnp.asarray(DH_C ** -0.5, BF16)
    qqs = []
    for c in range(chains):
        q = q_ref[0, :, c * LANES:(c + 1) * LANES]
        zero = jnp.zeros_like(q)
        qqs.append(jnp.concatenate([jnp.where(lane < DH_C, q, zero),
                                    jnp.where(lane >= DH_C, q, zero)], axis=0) * scale)
    row = lax.broadcasted_iota(jnp.int32, (t, t), 0)
    col = lax.broadcasted_iota(jnp.int32, (t, t), 1)
    later = jnp.where(row > col, 1.0, 0.0).astype(BF16)
    below = (lax.broadcasted_iota(jnp.int32, (2 * t, t), 1)
             < lax.broadcasted_iota(jnp.int32, (2 * t, t), 0) % t)

    def logits(j):
        r0 = pl.multiple_of(jnp.maximum(j, 0) * t, t)
        return [_nt(qqs[c], k_ref[0, pl.ds(r0, t), c * LANES:(c + 1) * LANES])
                for c in range(chains)]

    def tiles(first, zs, runs, accs, diagonal_first):
        runs, accs = list(runs), list(accs)
        mids = []
        for u in range(len(zs)):
            for c in range(chains):
                z = zs[u][c]
                sp = _softplus(z)
                log_1m = -sp
                masked = diagonal_first and u == 0
                if masked:
                    log_1m = jnp.where(below, log_1m, 0.0)
                hi, lo = _split(log_1m)
                within = _dot(hi, later) + _dot(lo, later)
                mids.append((u, c, z - sp, log_1m, within, masked))
        for u, c, log_b, log_1m, within, masked in mids:
            r0 = pl.multiple_of((first - u) * t, t)
            vb = v_ref[0, pl.ds(r0, t), c * LANES:(c + 1) * LANES]
            a = jnp.exp(log_b + (within + runs[c]))
            if masked:
                a = jnp.where(below, a, 0.0)
            ab = a.astype(BF16)
            zero_v = jnp.zeros_like(vb)
            a2 = jnp.concatenate([ab[:t], ab[t:]], axis=1)
            v2 = jnp.concatenate([jnp.where(lane < DH_C, vb, zero_v),
                                  jnp.where(lane >= DH_C, vb, zero_v)], axis=0)
            accs[c] = accs[c] + _dot(a2, v2)
            runs[c] = runs[c] + jnp.sum(log_1m, axis=-1, keepdims=True)
        return runs, accs

    def first_block(n_tiles):
        zs = [logits(i - u) for u in range(n_tiles)]
        runs, accs = tiles(i, zs, [jnp.zeros((2 * t, 1), F32)] * chains,
                           [jnp.zeros((t, LANES), F32)] * chains, True)
        for c in range(chains):
            run_ref[c] = runs[c]
            acc_ref[c] = accs[c]

    ahead = STICK_AHEAD

    @pl.when(i >= ahead)
    def _():
        first_block(1 + ahead)

    @pl.when(i < ahead)
    def _():
        first_block(1)

    n_left = i
    done0 = jnp.where(i >= ahead, ahead, 0)

    def go_on(done, runs):
        top = runs[0]
        for c in range(1, chains):
            top = jnp.maximum(top, runs[c])
        return ((done < n_left) & (jnp.max(top) > EXP_ZERO_BELOW)).astype(jnp.int32)

    def body(carry):
        done, _, runs, accs = carry
        first = n_left - 1 - done
        runs, accs = tiles(first, [logits(first)], runs, accs, False)
        done = done + 1
        return done, go_on(done, runs), tuple(runs), tuple(accs)

    runs = tuple(run_ref[c] for c in range(chains))
    accs = tuple(acc_ref[c] for c in range(chains))
    carry = (done0, go_on(done0, runs), runs, accs)
    accs = lax.while_loop(lambda carry: carry[1] > 0, body, carry)[3]
    for c in range(chains):
        o_ref[0, :, c * LANES:(c + 1) * LANES] = accs[c].astype(o_ref.dtype)


def _stick_breaking(proj):
    bsz, s, _ = proj.shape
    t = ATTN_TILE
    width = STICK_PAIRS * LANES
    n_blocks = H_C * DH_C // width
    return pl.pallas_call(
        _stick_kernel,
        out_shape=jax.ShapeDtypeStruct((bsz, s, H_C * DH_C), BF16),
        grid=(bsz, n_blocks, s // t),
        in_specs=[
            pl.BlockSpec((1, t, width), lambda b, h, i: (b, i, h)),
            pl.BlockSpec((1, s, width), lambda b, h, i: (b, 0, n_blocks + h)),
            pl.BlockSpec((1, s, width), lambda b, h, i: (b, 0, 2 * n_blocks + h)),
        ],
        out_specs=pl.BlockSpec((1, t, width), lambda b, h, i: (b, i, h)),
        scratch_shapes=[pltpu.VMEM((STICK_PAIRS, 2 * t, 1), F32),
                        pltpu.VMEM((STICK_PAIRS, t, LANES), F32)],
        compiler_params=pltpu.CompilerParams(
            dimension_semantics=("arbitrary", "arbitrary", "arbitrary"),
            vmem_limit_bytes=_vmem_limit(2 * s * width * 2 + 4 * t * width,
                                         STICK_PAIRS * 3 * t * LANES * 4)),
        name="stick_breaking_attention",
    )(proj, proj, proj)


def _route(logits):
    lane_i = lax.broadcasted_iota(jnp.int32, logits.shape, 1)
    lane = lane_i.astype(F32)
    group_of_lane = (lane_i // EXPERTS_PER_GROUP).astype(F32)
    big = jnp.float32(1 << 20)
    neg = jnp.float32(-jnp.inf)
    is_group = (lane_i >= N_EXPERTS) & (lane_i < N_EXPERTS + N_GROUPS)
    gl = jnp.where(is_group, logits, neg)
    gmax = jnp.max(gl, axis=-1, keepdims=True)
    g_val = 1.0 / jnp.sum(jnp.exp(gl - gmax), axis=-1, keepdims=True)
    g_sel = jnp.min(jnp.where(gl == gmax, lane - N_EXPERTS, big), axis=-1, keepdims=True)
    in_group = (lane_i < N_EXPERTS) & (group_of_lane == g_sel)
    el = jnp.where(in_group, logits, neg)
    emax = jnp.max(el, axis=-1, keepdims=True)
    esum = jnp.sum(jnp.exp(el - emax), axis=-1, keepdims=True)
    i1 = jnp.min(jnp.where(el == emax, lane, big), axis=-1, keepdims=True)
    el2 = jnp.where(lane == i1, neg, el)
    emax2 = jnp.max(el2, axis=-1, keepdims=True)
    i2 = jnp.min(jnp.where(el2 == emax2, lane, big), axis=-1, keepdims=True)
    v1 = 1.0 / esum
    v2 = jnp.exp(emax2 - emax) / esum
    tot = v1 + v2
    w1 = g_val * (v1 / tot)
    w2 = g_val * (v2 / tot)
    out = jnp.where(lane_i == 0, i1, 0.0)
    out = jnp.where(lane_i == 1, i2, out)
    out = jnp.where(lane_i == 2, w1, out)
    out = jnp.where(lane_i == 3, w2, out)
    return out, (lane == i1), (lane == i2)


def _pack_halves(h):
    half = h.shape[1] // 2
    lo = pltpu.bitcast(h[:, :half].astype(BF16).astype(F32), jnp.uint32)
    hi = pltpu.bitcast(h[:, half:].astype(BF16).astype(F32), jnp.uint32)
    return lax.shift_right_logical(lo, jnp.uint32(16)) | hi


def _unpack_halves(p):
    lo = pltpu.bitcast(lax.shift_left(p, jnp.uint32(16)), F32).astype(BF16)
    hi = pltpu.bitcast(p & jnp.uint32(0xFFFF0000), F32).astype(BF16)
    return lo, hi


def _out_kernel(*refs, n_o):
    o_refs = refs[:n_o]
    w_refs = refs[n_o:2 * n_o]
    x_ref, mod_ref, g_ref, wr_ref, br_ref, xo_ref = refs[2 * n_o:2 * n_o + 6]
    h_refs = refs[2 * n_o + 6:2 * n_o + 6 + MOE_PIECES]
    r_ref, cnt_ref, run_ref = refs[2 * n_o + 6 + MOE_PIECES:]

    @pl.when((pl.program_id(0) == 0) & (pl.program_id(1) == 0))
    def _():
        run_ref[...] = jnp.zeros_like(run_ref)

    th = x_ref.shape[1] // PROJ_ROW_PARTS
    ms = []
    for r in range(PROJ_ROW_PARTS):
        rows = slice(r * th, (r + 1) * th)
        m = _dot(o_refs[0][0, rows, :], w_refs[0][...])
        for a in range(1, n_o):
            m = m + _dot(o_refs[a][0, rows, :], w_refs[a][...])
        ms.append(m)
    routes, hits1, hits2 = [], [], []
    for r in range(PROJ_ROW_PARTS):
        rows = slice(r * th, (r + 1) * th)
        x = x_ref[0, rows, :] + mod_ref[0, 2:3, :] * ms[r]
        xo_ref[0, rows, :] = x
        h = _rms(x, g_ref[...]) * (1.0 + mod_ref[0, 4:5, :]) + mod_ref[0, 3:4, :]
        piece = h.shape[1] // MOE_PIECES
        for k in range(MOE_PIECES):
            h_refs[k][0, rows, :] = _pack_halves(h[:, k * piece:(k + 1) * piece])
        part_route, part_hit1, part_hit2 = _route(_dot3_narrow(h, wr_ref[...]) + br_ref[...])
        routes.append(part_route)
        hits1.append(part_hit1)
        hits2.append(part_hit2)
    route = jnp.concatenate(routes, axis=0)
    hit1 = jnp.concatenate(hits1, axis=0)
    hit2 = jnp.concatenate(hits2, axis=0)
    tm = route.shape[0]
    picks = jnp.where(hit1 | hit2, 1.0, 0.0)
    row = lax.broadcasted_iota(jnp.int32, (tm, tm), 0)
    col = lax.broadcasted_iota(jnp.int32, (tm, tm), 1)
    before = jnp.where(col < row, 1.0, 0.0).astype(BF16)
    earlier = _dot(before, picks.astype(BF16)) + run_ref[...]
    rank1 = jnp.sum(jnp.where(hit1, earlier, 0.0), axis=-1, keepdims=True)
    rank2 = jnp.sum(jnp.where(hit2, earlier, 0.0), axis=-1, keepdims=True)
    lane_i = lax.broadcasted_iota(jnp.int32, route.shape, 1)
    route = jnp.where(lane_i == 4, rank1, route)
    route = jnp.where(lane_i == 5, rank2, route)
    r_ref[0] = route
    run_ref[...] = run_ref[...] + jnp.sum(picks, axis=0, keepdims=True)
    cnt_ref[...] = run_ref[...]


def _out_projection(o_list, w_list, x, mod, g, w_router, b_router):
    bsz, s, d = x.shape
    tm = TOKEN_TILE
    n_o = len(o_list)
    packed = d // (2 * MOE_PIECES)
    tok = lambda b, i: (b, i, 0)
    const2 = lambda b, i: (0, 0)
    in_specs = ([pl.BlockSpec((1, tm, o.shape[2]), tok) for o in o_list]
                + [pl.BlockSpec(w.shape, const2) for w in w_list]
                + [pl.BlockSpec((1, tm, d), tok), pl.BlockSpec((1, 6, d), lambda b, i: (b, 0, 0)),
                   pl.BlockSpec((1, d), const2), pl.BlockSpec((d, LANES), const2),
                   pl.BlockSpec((1, LANES), const2)])
    block_bytes = (sum(tm * o.shape[2] * 2 for o in o_list) + sum(w.size * 2 for w in w_list)
                   + 2 * tm * d * 4 + tm * d * 2 + d * LANES * 4 + tm * LANES * 4)
    return pl.pallas_call(
        functools.partial(_out_kernel, n_o=n_o),
        out_shape=([jax.ShapeDtypeStruct((bsz, s, d), F32)]
                   + [jax.ShapeDtypeStruct((bsz, s, packed), jnp.uint32)] * MOE_PIECES
                   + [jax.ShapeDtypeStruct((bsz, s, LANES), F32),
                      jax.ShapeDtypeStruct((1, LANES), F32)]),
        grid=(bsz, s // tm),
        in_specs=in_specs,
        out_specs=([pl.BlockSpec((1, tm, d), tok)]
                   + [pl.BlockSpec((1, tm, packed), tok)] * MOE_PIECES
                   + [pl.BlockSpec((1, tm, LANES), tok), pl.BlockSpec((1, LANES), const2)]),
        scratch_shapes=[pltpu.VMEM((1, LANES), F32)],
        compiler_params=pltpu.CompilerParams(
            dimension_semantics=("arbitrary", "arbitrary"),
            vmem_limit_bytes=_vmem_limit(block_bytes, tm * d * 12 + tm * tm * 4)),
        name="out_projection_norm_router",
    )(*o_list, *w_list, x, mod, g.reshape(1, d), w_router, b_router)


def _expert_kernel(te_ref, tv_ref, nu_ref, *refs):
    x_refs = refs[:MOE_PIECES]
    wg_ref, wu_ref, wd_ref, o_ref, wgb_ref, wub_ref, wdb_ref = refs[MOE_PIECES:]
    t = pl.program_id(0)
    prev = te_ref[jnp.maximum(t - 1, 0)]

    @pl.when((t == 0) | (te_ref[t] != prev))
    def _():
        wgb_ref[...] = wg_ref[0, 0].astype(BF16)
        wub_ref[...] = wu_ref[0, 0].astype(BF16)
        wdb_ref[...] = wd_ref[0, 0].astype(BF16)

    @pl.when(t < nu_ref[0])
    def _():
        th = o_ref.shape[0] // EXPERT_ROW_PARTS
        hidden = []
        for r in range(EXPERT_ROW_PARTS):
            rows = slice(r * th, (r + 1) * th)
            cols = []
            for x_ref in x_refs:
                row = lax.broadcasted_iota(jnp.int32, (th, x_ref.shape[1]), 0) + r * th
                cols += list(_unpack_halves(jnp.where(row < tv_ref[t], x_ref[rows, :],
                                                      jnp.uint32(0))))
            w = cols[0].shape[1]
            gt = _dot(cols[0], wgb_ref[:w, :])
            up = _dot(cols[0], wub_ref[:w, :])
            for k in range(1, len(cols)):
                gt = gt + _dot(cols[k], wgb_ref[k * w:(k + 1) * w, :])
                up = up + _dot(cols[k], wub_ref[k * w:(k + 1) * w, :])
            hidden.append((gt, up))
        for r in range(EXPERT_ROW_PARTS):
            gt, up = hidden[r]
            o_ref[r * th:(r + 1) * th, :] = _dot((_silu(gt) * up).astype(BF16),
                                                 wdb_ref[...]).astype(o_ref.dtype)

    @pl.when(t >= nu_ref[0])
    def _():
        o_ref[...] = jnp.zeros_like(o_ref)


def _grouped_experts(xs, tile_expert, tile_valid, n_used, w_gate, w_up, w_down, layer):
    p, packed = xs[0].shape
    d = w_gate.shape[2]
    tm = EXPERT_TILE
    f = w_gate.shape[3]
    grid_spec = pltpu.PrefetchScalarGridSpec(
        num_scalar_prefetch=3,
        grid=(p // tm,),
        in_specs=[pl.BlockSpec((tm, packed), lambda t, te, tv, nu: (t, 0))] * MOE_PIECES + [
            pl.BlockSpec((1, 1, d, f), lambda t, te, tv, nu: (layer, te[t], 0, 0)),
            pl.BlockSpec((1, 1, d, f), lambda t, te, tv, nu: (layer, te[t], 0, 0)),
            pl.BlockSpec((1, 1, f, d), lambda t, te, tv, nu: (layer, te[t], 0, 0)),
        ],
        out_specs=pl.BlockSpec((tm, d), lambda t, te, tv, nu: (t, 0)),
        scratch_shapes=[pltpu.VMEM((d, f), BF16), pltpu.VMEM((d, f), BF16),
                        pltpu.VMEM((f, d), BF16)],
    )
    block_bytes = tm * d * 2 + 3 * d * f * 4 + tm * d * 2
    return pl.pallas_call(
        _expert_kernel,
        out_shape=jax.ShapeDtypeStruct((p, d), BF16),
        grid_spec=grid_spec,
        compiler_params=pltpu.CompilerParams(
            dimension_semantics=("arbitrary",),
            vmem_limit_bytes=_vmem_limit(block_bytes, 3 * d * f * 2 + tm * f * 16)),
        name="grouped_swiglu_experts",
    )(tile_expert, tile_valid, n_used, *xs, w_gate, w_up, w_down)


def _pos_kernel(route_ref, starts_ref, pos_ref):
    rt = jnp.transpose(route_ref[...])
    starts = starts_ref[...]
    ids = lax.broadcasted_iota(jnp.int32, (N_EXPERTS, rt.shape[1]), 0).astype(F32)
    for c in range(2):
        start_of = jnp.sum(jnp.where(ids == rt[c:c + 1, :], starts, 0.0), axis=0, keepdims=True)
        pos_ref[c:c + 1, :] = (start_of + rt[4 + c:5 + c, :]).astype(jnp.int32)


def _row_positions(route, starts):
    n_tok = route.shape[0]
    tm = TOKEN_TILE
    return pl.pallas_call(
        _pos_kernel,
        out_shape=jax.ShapeDtypeStruct((2, n_tok), jnp.int32),
        grid=(n_tok // tm,),
        in_specs=[pl.BlockSpec((tm, LANES), lambda i: (i, 0)),
                  pl.BlockSpec((N_EXPERTS, 1), lambda i: (0, 0))],
        out_specs=pl.BlockSpec((2, tm), lambda i: (0, i)),
        compiler_params=pltpu.CompilerParams(
            dimension_semantics=("arbitrary",),
            vmem_limit_bytes=_vmem_limit(tm * LANES * 4 + N_EXPERTS * LANES * 4 + 8 * tm * 4)),
        name="expert_row_positions",
    )(route, starts.astype(F32).reshape(N_EXPERTS, 1))


def _scatter_rows(rows, pos, n_out):
    n_tok, d = rows.shape
    n_idx = pos.shape[0]
    w = SC_SCATTER_WINDOW
    per_pass = n_tok // w
    mesh = plsc.VectorSubcoreMesh(core_axis_name="core", subcore_axis_name="subcore")

    @pl.kernel(out_type=jax.ShapeDtypeStruct((n_out, d), rows.dtype), mesh=mesh, scratch_types=[])
    def scatter_kernel(x_hbm, i_hbm, o_hbm):
        def body(x_vmem, i_vmem):
            pltpu.sync_copy(x_vmem, o_hbm.at[i_vmem.at[0]])

        pltpu.emit_pipeline(
            body,
            grid=(n_idx // w,),
            in_specs=[pl.BlockSpec((w, d), lambda i: (i % per_pass, 0)),
                      pl.BlockSpec((1, w), lambda i: (0, i))],
            out_specs=[],
            core_axis_name=("core", "subcore"),
            dimension_semantics=(pltpu.PARALLEL,),
        )(x_hbm, i_hbm)

    return scatter_kernel(rows, pos.reshape(1, n_idx))


def _dispatch(route, counts, n_tok):
    tm = EXPERT_TILE
    counts = counts[:N_EXPERTS].astype(jnp.int32)
    padded = ((counts + tm - 1) // tm) * tm
    ends = jnp.cumsum(padded)
    starts = ends - padded
    pos = _row_positions(route, starts).reshape(2 * n_tok)
    p = 2 * n_tok + N_EXPERTS * tm
    tile_start = jnp.arange(p // tm, dtype=jnp.int32) * tm
    tile_expert = jnp.minimum(jnp.sum(ends[None, :] <= tile_start[:, None], axis=1),
                              N_EXPERTS - 1).astype(jnp.int32)
    n_used = (ends[-1] // tm).astype(jnp.int32).reshape(1)
    row_end = starts + counts
    tile_valid = jnp.clip(jnp.sum(jnp.where(tile_expert[:, None] == jnp.arange(N_EXPERTS)[None, :],
                                            row_end[None, :], 0), axis=1) - tile_start, 0, tm)
    return p, tile_expert, tile_valid.astype(jnp.int32), n_used, pos


def _final_kernel(x_ref, y_ref, route_ref, mod_ref, g_ref, o_ref):
    o_ref[0] = _rms(_expert_residual(x_ref[0], y_ref, route_ref, mod_ref), g_ref[...])


def _final_norm(x, y, route, mod, g):
    bsz, s, d = x.shape
    tm = TOKEN_TILE
    tok = lambda b, i: (b, i, 0)
    return pl.pallas_call(
        _final_kernel,
        out_shape=jax.ShapeDtypeStruct((bsz, s, d), F32),
        grid=(bsz, s // tm),
        in_specs=[pl.BlockSpec((1, tm, d), tok),
                  pl.BlockSpec((2, 1, tm, d), lambda b, i: (0, b, i, 0)),
                  pl.BlockSpec((1, tm, LANES), tok),
                  pl.BlockSpec((1, 6, d), lambda b, i: (b, 0, 0)),
                  pl.BlockSpec((1, d), lambda b, i: (0, 0))],
        out_specs=pl.BlockSpec((1, tm, d), tok),
        compiler_params=pltpu.CompilerParams(
            dimension_semantics=("arbitrary", "arbitrary"),
            vmem_limit_bytes=_vmem_limit(4 * tm * d * 4 + tm * LANES * 4)),
        name="residual_final_norm",
    )(x, y, route, mod, g.reshape(1, d))


def kernel(x, c, w_ada, b_ada, norm_mix, norm_ffn, norm_final, rel_bias, even_w_in, even_lambda, even_subln, even_w_gk2, even_b_gk, even_gla_norm, even_w_out, odd_w_in, odd_w_out, router_group_w, router_group_b, router_expert_w, router_expert_b, expert_w_gate, expert_w_up, expert_w_down):
    bsz, s, d = x.shape
    n_tok = bsz * s
    assert s % TOKEN_TILE == 0 and s % GLA_STEP == 0
    assert s % (DIFF_KEY_TILES * ATTN_TILE) == 0 and s % (DIFF_Q_TILES * ATTN_TILE) == 0
    mod_all = _modulation(c, w_ada, b_ada).reshape(DEPTH, bsz, 6, d)
    bias_tiles = _diff_bias_tiles(rel_bias)
    n_even_main = 3072
    y = route = prev_mod = None
    for l in range(DEPTH):
        mod = mod_all[l]
        i = l // 2
        if l % 2 == 0:
            w_in = even_w_in[i]
            w_main = w_in[:, :n_even_main].astype(BF16)
            w_gate = jnp.zeros((d, LANES), BF16).at[:, :GATE_RANK].set(
                w_in[:, n_even_main:].astype(BF16))
            x, proj, gate = _in_projection(x, y, route, prev_mod, norm_mix[l], mod, w_main, w_gate)
            oa = _diff_attention(proj, bias_tiles, even_lambda[i], even_subln[i], l)
            ob = _gla(proj, gate, even_w_gk2[i], even_b_gk[i], even_gla_norm[i])
            w_out = even_w_out[i].astype(BF16)
            o_list = [oa, ob]
            w_list = [w_out[:H_A * DV_A], w_out[H_A * DV_A:]]
        else:
            x, proj, _ = _in_projection(x, y, route, prev_mod, norm_mix[l], mod,
                                        odd_w_in[i].astype(BF16), None)
            o_list = [_stick_breaking(proj)]
            w_list = [odd_w_out[i].astype(BF16)]
        w_router = jnp.zeros((d, LANES), F32)
        w_router = w_router.at[:, :N_EXPERTS].set(router_expert_w[l].astype(F32))
        w_router = w_router.at[:, N_EXPERTS:N_EXPERTS + N_GROUPS].set(router_group_w[l].astype(F32))
        b_router = jnp.zeros((1, LANES), F32)
        b_router = b_router.at[0, :N_EXPERTS].set(router_expert_b[l].astype(F32))
        b_router = b_router.at[0, N_EXPERTS:N_EXPERTS + N_GROUPS].set(router_group_b[l].astype(F32))
        outs = _out_projection(o_list, w_list, x, mod, norm_ffn[l], w_router, b_router)
        x, pieces, route, counts = outs[0], outs[1:1 + MOE_PIECES], outs[-2], outs[-1]
        n_rows, tile_expert, tile_valid, n_used, pos = _dispatch(route.reshape(n_tok, LANES),
                                                                 counts[0], n_tok)
        xs = [_scatter_rows(piece.reshape(n_tok, piece.shape[2]), pos, n_rows) for piece in pieces]
        out = _grouped_experts(xs, tile_expert, tile_valid, n_used, expert_w_gate, expert_w_up,
                               expert_w_down, l)
        y = out.at[pos].get(mode="promise_in_bounds").reshape(2, bsz, s, d)
        prev_mod = mod
    return _final_norm(x, y, route, prev_mod, norm_final)
```

```python
import functools
import math

import jax
import jax.numpy as jnp
from jax import lax
from jax.experimental import pallas as pl
from jax.experimental.pallas import tpu as pltpu
from jax.experimental.pallas import tpu_sc as plsc

F32 = jnp.float32
BF16 = jnp.bfloat16

DEPTH = 4
CHUNK = 64
EPS = 1e-6
H_A, DH_A, DV_A = 4, 64, 128
H_B, DK_B, DV_B = 4, 64, 128
GATE_RANK = 16
GATE_TAU = 16.0
H_C, DH_C = 16, 64
N_BUCKETS = 32
MAX_DISTANCE = 128
N_GROUPS = 4
EXPERTS_PER_GROUP = 8
N_EXPERTS = N_GROUPS * EXPERTS_PER_GROUP
EVEN_MAIN_COLS = 3 * H_A * DV_A + 2 * H_B * DK_B + 2 * H_B * DV_B
GLA_Q_COL = 3 * H_A * DV_A
GLA_K_COL = GLA_Q_COL + H_B * DK_B
GLA_V_COL = GLA_K_COL + H_B * DK_B
GLA_R_COL = GLA_V_COL + H_B * DV_B

LANES = 128
V7X_VMEM_BYTES = 64 * 1024 * 1024
VMEM_TEMPORARIES_BYTES = 16 * 1024 * 1024
VMEM_RESERVED_BYTES = 8 * 1024 * 1024
NEG_BIG = -1e30
EXP_ZERO_BELOW = -104.0

TOKEN_TILE = 512
PROJ_ROW_PARTS = 2
ATTN_TILE = 128
STICK_PAIRS = 8
STICK_AHEAD = 2
DIFF_KEY_TILES = 8
DIFF_Q_TILES = 8
GLA_STEP = 256
EXPERT_TILE = 512
EXPERT_ROW_PARTS = 4
SC_SCATTER_WINDOW = 128
MOE_PIECES = 2


def _vmem_limit(block_bytes, scratch_bytes=0):
    est = 2 * block_bytes + scratch_bytes + VMEM_TEMPORARIES_BYTES
    return int(min(est, V7X_VMEM_BYTES - VMEM_RESERVED_BYTES))


def _nt(a, b):
    return lax.dot_general(a, b, (((1,), (1,)), ((), ())), preferred_element_type=F32)


def _tn(a, b):
    return lax.dot_general(a, b, (((0,), (0,)), ((), ())), preferred_element_type=F32)


def _dot(a, b):
    return jnp.dot(a, b, preferred_element_type=F32)


def _split(x):
    hi = x.astype(BF16)
    lo = (x - hi.astype(F32)).astype(BF16)
    return hi, lo


def _dot3(a, b):
    ah, al = _split(a)
    bh, bl = _split(b)
    return _dot(ah, bh) + _dot(al, bh) + _dot(ah, bl)


def _dot3_narrow(a, b):
    ah, al = _split(a)
    bh, bl = _split(b)
    n = b.shape[1]
    both = _dot(ah, jnp.concatenate([bh, bl], axis=1))
    return both[:, :n] + both[:, n:] + _dot(al, bh)


def _rms(x, g):
    return x * lax.rsqrt(jnp.mean(x * x, axis=-1, keepdims=True) + EPS) * g


def _softplus(z):
    return jnp.maximum(z, 0.0) + jnp.log(1.0 + jnp.exp(-jnp.abs(z)))


def _silu(x):
    return x / (1.0 + jnp.exp(-x))


def _mod_kernel(c_ref, w_ref, b_ref, o_ref):
    c = c_ref[...]
    o_ref[0] = _dot3(_silu(c), w_ref[0]) + b_ref[0]


def _modulation(c, w_ada, b_ada):
    depth, d, n = w_ada.shape
    bsz = c.shape[0]
    tn = n // 4
    return pl.pallas_call(
        _mod_kernel,
        out_shape=jax.ShapeDtypeStruct((depth, bsz, n), F32),
        grid=(depth, n // tn),
        in_specs=[
            pl.BlockSpec((bsz, d), lambda l, j: (0, 0)),
            pl.BlockSpec((1, d, tn), lambda l, j: (l, 0, j)),
            pl.BlockSpec((1, 1, tn), lambda l, j: (l, 0, j)),
        ],
        out_specs=pl.BlockSpec((1, bsz, tn), lambda l, j: (l, 0, j)),
        compiler_params=pltpu.CompilerParams(
            dimension_semantics=("arbitrary", "arbitrary"),
            vmem_limit_bytes=_vmem_limit(d * tn * 4 + bsz * (d + tn) * 4)),
        name="adaln_modulation",
    )(c, w_ada, b_ada.reshape(depth, 1, n))


def _expert_residual(x, y_ref, route_ref, mod_ref, rows=slice(None)):
    route = route_ref[0, rows, :]
    y = (route[:, 2:3] * y_ref[0, 0, rows, :].astype(F32)
         + route[:, 3:4] * y_ref[1, 0, rows, :].astype(F32))
    return x + mod_ref[0, 5:6, :] * y


def _proj_kernel(*refs, has_res, n_main, has_gate):
    it = iter(refs)
    x_ref = next(it)
    if has_res:
        y_ref = next(it)
        route_ref = next(it)
        pmod_ref = next(it)
    g_ref = next(it)
    mod_ref = next(it)
    w_ref = next(it)
    wg_ref = next(it) if has_gate else None
    xo_ref = next(it) if has_res else None
    o_ref = next(it)
    og_ref = next(it) if has_gate else None

    th = x_ref.shape[1] // PROJ_ROW_PARTS
    hbs = []
    for r in range(PROJ_ROW_PARTS):
        rows = slice(r * th, (r + 1) * th)
        x = x_ref[0, rows, :]
        if has_res:
            x = _expert_residual(x, y_ref, route_ref, pmod_ref, rows)
            xo_ref[0, rows, :] = x
        h = _rms(x, g_ref[...]) * (1.0 + mod_ref[0, 1:2, :]) + mod_ref[0, 0:1, :]
        hbs.append(h.astype(BF16))
    step = 4 * LANES
    for r in range(PROJ_ROW_PARTS):
        rows = slice(r * th, (r + 1) * th)
        for n0 in range(0, n_main, step):
            o_ref[0, rows, n0:n0 + step] = _dot(hbs[r], w_ref[:, n0:n0 + step]).astype(BF16)
        if has_gate:
            og_ref[0, rows, :] = _dot(hbs[r], wg_ref[...])


def _in_projection(x, y, route, prev_mod, g, mod, w, w_gate):
    bsz, s, d = x.shape
    tm = TOKEN_TILE
    has_res = y is not None
    has_gate = w_gate is not None
    n_main = w.shape[1]
    tok = lambda b, i: (b, i, 0)
    per_b = lambda b, i: (b, 0, 0)
    const2 = lambda b, i: (0, 0)
    in_specs = [pl.BlockSpec((1, tm, d), tok)]
    args = [x]
    if has_res:
        in_specs += [pl.BlockSpec((2, 1, tm, d), lambda b, i: (0, b, i, 0)),
                     pl.BlockSpec((1, tm, LANES), tok), pl.BlockSpec((1, 6, d), per_b)]
        args += [y, route, prev_mod]
    in_specs += [pl.BlockSpec((1, d), const2), pl.BlockSpec((1, 6, d), per_b),
                 pl.BlockSpec((d, n_main), const2)]
    args += [g.reshape(1, d), mod, w]
    if has_gate:
        in_specs.append(pl.BlockSpec((d, LANES), const2))
        args.append(w_gate)
    out_shape, out_specs = [], []
    if has_res:
        out_shape.append(jax.ShapeDtypeStruct((bsz, s, d), F32))
        out_specs.append(pl.BlockSpec((1, tm, d), tok))
    out_shape.append(jax.ShapeDtypeStruct((bsz, s, n_main), BF16))
    out_specs.append(pl.BlockSpec((1, tm, n_main), tok))
    if has_gate:
        out_shape.append(jax.ShapeDtypeStruct((bsz, s, LANES), F32))
        out_specs.append(pl.BlockSpec((1, tm, LANES), tok))
    block_bytes = (tm * d * 4 * (4 if has_res else 1) + d * n_main * 2 + tm * n_main * 2
                   + d * LANES * 2 + 2 * tm * LANES * 4)
    outs = pl.pallas_call(
        functools.partial(_proj_kernel, has_res=has_res, n_main=n_main, has_gate=has_gate),
        out_shape=out_shape,
        grid=(bsz, s // tm),
        in_specs=in_specs,
        out_specs=out_specs,
        compiler_params=pltpu.CompilerParams(
            dimension_semantics=("arbitrary", "arbitrary"),
            vmem_limit_bytes=_vmem_limit(block_bytes, tm * d * 8)),
        name="norm_mod_in_projection",
    )(*args)
    outs = list(outs)
    x_new = outs.pop(0) if has_res else x
    proj = outs.pop(0)
    gate = outs.pop(0) if has_gate else None
    return x_new, proj, gate


def _t5_bucket(rel):
    nb = N_BUCKETS // 2
    max_exact = nb // 2
    ret = jnp.where(rel > 0, nb, 0)
    n = jnp.abs(rel)
    nf = jnp.maximum(n, 1).astype(F32)
    large = max_exact + (jnp.log(nf / max_exact) / math.log(MAX_DISTANCE / max_exact)
                         * (nb - max_exact)).astype(jnp.int32)
    large = jnp.minimum(large, nb - 1)
    return ret + jnp.where(n < max_exact, n, large)


def _diff_bias_tiles(rel_bias):
    t = ATTN_TILE
    assert t >= MAX_DISTANCE, "tiles two or more away must lie beyond the last bucket edge"
    qp = jnp.arange(t)[:, None]
    kp = jnp.arange(t)[None, :]
    tiles = []
    for off in (0, 1, 2):
        rel = (kp - off * t) - qp
        hit = _t5_bucket(rel)[None, :, :, None] == jnp.arange(N_BUCKETS)[None, None, None, :]
        b = jnp.sum(jnp.where(hit, jnp.transpose(rel_bias.astype(F32))[:, None, None, :], 0.0),
                    axis=-1)
        if off == 0:
            b = jnp.where((kp // CHUNK) <= (qp // CHUNK), b, NEG_BIG)
        tiles.append(b)
    tab = jnp.stack(tiles, axis=1)
    return jnp.concatenate([tab, tab], axis=2)


def _diff_attn_kernel(q_ref, k_ref, v_ref, bias_ref, lam_ref, g_ref, o_ref, *, lambda_init):
    t = ATTN_TILE
    chains = DIFF_Q_TILES
    i = pl.program_id(2)
    lane = lax.broadcasted_iota(jnp.int32, (t, LANES), 1)
    scale = jnp.asarray(DH_A ** -0.5, BF16)
    stacked = []
    for c in range(chains):
        q = q_ref[0, c * t:(c + 1) * t, :]
        zero = jnp.zeros_like(q)
        stacked += [jnp.where(lane < DH_A, q, zero), jnp.where(lane >= DH_A, q, zero)]
    qq = jnp.concatenate(stacked, axis=0) * scale

    group = DIFF_KEY_TILES
    span = group * t
    assert chains == group, "a grid step's query tiles must be exactly one key block"

    def softmax_step(carry_c, s, which_of, vb):
        m, l, acc = carry_c
        width = s.shape[1] // t

        def logits(u):
            return s[:, u * t:(u + 1) * t] + bias_ref[0, which_of(u)]

        top = logits(0)
        for u in range(1, width):
            top = jnp.maximum(top, logits(u))
        m_new = jnp.maximum(m, jnp.max(top, axis=-1, keepdims=True))
        alpha = jnp.exp(m - m_new)
        ps = [jnp.exp(logits(u) - m_new) for u in range(width)]
        tot = ps[0]
        for u in range(1, width):
            tot = tot + ps[u]
        l = alpha * l + jnp.sum(tot, axis=-1, keepdims=True)
        p = jnp.concatenate([pu.astype(BF16) for pu in ps], axis=1)
        return m_new, l, alpha * acc + _dot(p, vb)

    def body(n, carry):
        r0 = pl.multiple_of(n * span, span)
        kb = k_ref[0, pl.ds(r0, span), :]
        vb = v_ref[0, pl.ds(r0, span), :]
        out = []
        s_next = _nt(qq[0:2 * t], kb)
        for c in range(chains):
            s = s_next
            if c + 1 < chains:
                s_next = _nt(qq[(c + 1) * 2 * t:(c + 2) * 2 * t], kb)

            def which_of(u, c=c):
                return jnp.minimum((i * chains + c) - (n * group + u), 2)
            out.append(softmax_step(carry[c], s, which_of, vb))
        return tuple(out)

    init = tuple((jnp.full((2 * t, 1), NEG_BIG, F32), jnp.zeros((2 * t, 1), F32),
                  jnp.zeros((2 * t, DV_A), F32)) for _ in range(chains))
    before = lax.fori_loop(0, i, body, init)
    r0 = pl.multiple_of(i * span, span)
    def diag_logits(c):
        return _nt(qq[c * 2 * t:(c + 1) * 2 * t], k_ref[0, pl.ds(r0, (c + 1) * t), :])

    final = []
    s_next = diag_logits(0)
    for c in range(chains):
        s = s_next
        if c + 1 < chains:
            s_next = diag_logits(c + 1)
        final.append(softmax_step(before[c], s, lambda u, c=c: min(c - u, 2),
                                  v_ref[0, pl.ds(r0, (c + 1) * t), :]))
    lp = lam_ref[...].astype(F32)
    lam = (jnp.exp(jnp.sum(lp[0:1] * lp[1:2], axis=-1, keepdims=True))
           - jnp.exp(jnp.sum(lp[2:3] * lp[3:4], axis=-1, keepdims=True)) + lambda_init)
    for c in range(chains):
        _, l, acc = final[c]
        o = acc / l
        w = o[:t] - lam * o[t:]
        o_ref[0, c * t:(c + 1) * t, :] = (_rms(w, g_ref[...]) * (1.0 - lambda_init)
                                          ).astype(o_ref.dtype)


def _diff_attention(proj, bias_tiles, lam_p, subln_g, layer_idx):
    bsz, s, _ = proj.shape
    t = ATTN_TILE
    tq = DIFF_Q_TILES * t
    lambda_init = 0.8 - 0.6 * math.exp(-0.3 * layer_idx)
    return pl.pallas_call(
        functools.partial(_diff_attn_kernel, lambda_init=lambda_init),
        out_shape=jax.ShapeDtypeStruct((bsz, s, H_A * DV_A), BF16),
        grid=(bsz, H_A, s // tq),
        in_specs=[
            pl.BlockSpec((1, tq, LANES), lambda b, h, i: (b, i, h)),
            pl.BlockSpec((1, s, LANES), lambda b, h, i: (b, 0, H_A + h)),
            pl.BlockSpec((1, s, LANES), lambda b, h, i: (b, 0, 2 * H_A + h)),
            pl.BlockSpec((1, 3, 2 * t, t), lambda b, h, i: (h, 0, 0, 0)),
            pl.BlockSpec((4, DH_A), lambda b, h, i: (0, 0)),
            pl.BlockSpec((1, DV_A), lambda b, h, i: (0, 0)),
        ],
        out_specs=pl.BlockSpec((1, tq, DV_A), lambda b, h, i: (b, i, h)),
        compiler_params=pltpu.CompilerParams(
            dimension_semantics=("arbitrary", "arbitrary", "arbitrary"),
            vmem_limit_bytes=_vmem_limit(2 * s * LANES * 2 + 3 * 2 * t * t * 4 + 4 * tq * LANES,
                                         4 * tq * DIFF_KEY_TILES * t * 4)),
        name="diff_attention",
    )(proj, proj, proj, bias_tiles, lam_p, subln_g.reshape(1, DV_A))


def _gla_kernel(q_ref, k_ref, v_ref, r_ref, bg_ref, wgk_ref, bgk_ref, g_ref, o_ref, st_ref):
    c = CHUNK
    ts = q_ref.shape[1]
    n_chunks = ts // c

    @pl.when(pl.program_id(1) == 0)
    def _():
        st_ref[...] = jnp.zeros_like(st_ref)

    row = lax.broadcasted_iota(jnp.int32, (ts, ts), 0)
    col = lax.broadcasted_iota(jnp.int32, (ts, ts), 1)
    same_chunk_before = ((row // c) == (col // c)) & (col <= row)
    tri = jnp.where(same_chunk_before, 1.0, 0.0).astype(BF16)
    causal = (lax.broadcasted_iota(jnp.int32, (c, c), 1)
              <= lax.broadcasted_iota(jnp.int32, (c, c), 0))
    lane = lax.broadcasted_iota(jnp.int32, (ts, LANES), 1)
    wgk = wgk_ref[...]
    bgk = bgk_ref[...]
    g = g_ref[...]
    bg = bg_ref[0]

    las = [-_softplus(-(_dot3(bg, wgk[:, hp * LANES:(hp + 1) * LANES])
                        + bgk[:, hp * LANES:(hp + 1) * LANES])) / GATE_TAU
           for hp in range(H_B // 2)]
    bs = []
    for la in las:
        la_hi, la_lo = _split(la)
        bs.append(_dot(tri, la_hi) + _dot(tri, la_lo))

    heads = []
    for hp in range(H_B // 2):
        cols = slice(hp * LANES, (hp + 1) * LANES)
        b = bs[hp]
        b_last = jnp.concatenate(
            [jnp.broadcast_to(b[(ci + 1) * c - 1:(ci + 1) * c, :], (c, LANES))
             for ci in range(n_chunks)], axis=0)
        qf = q_ref[0, :, cols].astype(F32) * (DK_B ** -0.5)
        kf = k_ref[0, :, cols].astype(F32)
        q_dec = qf * jnp.exp(b)
        k_inv = (kf * jnp.exp(-b)).astype(BF16)
        k_dec = kf * jnp.exp(b_last - b)
        decays = [jnp.exp(b[(ci + 1) * c - 1:(ci + 1) * c, :]) for ci in range(n_chunks)]
        for hh in range(2):
            own = (lane >= hh * DK_B) & (lane < (hh + 1) * DK_B)
            heads.append((jnp.where(own, q_dec, 0.0).astype(BF16), k_inv,
                          jnp.where(own, k_dec, 0.0).astype(BF16), decays))

    def rows(x, ci):
        return x[ci * c:(ci + 1) * c]

    scores, increments = [], []
    for h, (qd, k_inv, kd, _) in enumerate(heads):
        vh = v_ref[0, :, h * DV_B:(h + 1) * DV_B]
        scores.append([jnp.where(causal, _nt(rows(qd, ci), rows(k_inv, ci)), 0.0).astype(BF16)
                       for ci in range(n_chunks)])
        increments.append([_tn(rows(vh, ci), rows(kd, ci)) for ci in range(n_chunks)])

    for h, (qd, _, _, decays) in enumerate(heads):
        vh = v_ref[0, :, h * DV_B:(h + 1) * DV_B]
        st = st_ref[h]
        states = []
        for ci in range(n_chunks):
            states.append(st.astype(BF16))
            st = st * decays[ci] + increments[h][ci]
        st_ref[h] = st
        o = jnp.concatenate(
            [_dot(scores[h][ci], rows(vh, ci)) + _nt(rows(qd, ci), states[ci])
             for ci in range(n_chunks)], axis=0)
        rr = r_ref[0, :, h * DV_B:(h + 1) * DV_B].astype(F32)
        o_ref[0, :, h * DV_B:(h + 1) * DV_B] = (_rms(o, g) * _silu(rr)).astype(o_ref.dtype)


def _gla(proj, gate, w_gk2, b_gk, gla_g):
    bsz, s, _ = proj.shape
    ts = GLA_STEP
    nk = H_B * DK_B
    nv = H_B * DV_B
    w_pad = jnp.zeros((LANES, nk), F32).at[:GATE_RANK].set(w_gk2.astype(F32))
    return pl.pallas_call(
        _gla_kernel,
        out_shape=jax.ShapeDtypeStruct((bsz, s, nv), BF16),
        grid=(bsz, s // ts),
        in_specs=[
            pl.BlockSpec((1, ts, nk), lambda b, i: (b, i, GLA_Q_COL // nk)),
            pl.BlockSpec((1, ts, nk), lambda b, i: (b, i, GLA_K_COL // nk)),
            pl.BlockSpec((1, ts, nv), lambda b, i: (b, i, GLA_V_COL // nv)),
            pl.BlockSpec((1, ts, nv), lambda b, i: (b, i, GLA_R_COL // nv)),
            pl.BlockSpec((1, ts, LANES), lambda b, i: (b, i, 0)),
            pl.BlockSpec((LANES, nk), lambda b, i: (0, 0)),
            pl.BlockSpec((1, nk), lambda b, i: (0, 0)),
            pl.BlockSpec((1, DV_B), lambda b, i: (0, 0)),
        ],
        out_specs=pl.BlockSpec((1, ts, nv), lambda b, i: (b, i, 0)),
        scratch_shapes=[pltpu.VMEM((H_B, DV_B, LANES), F32)],
        compiler_params=pltpu.CompilerParams(
            dimension_semantics=("arbitrary", "arbitrary"),
            vmem_limit_bytes=_vmem_limit(ts * (2 * nk + 3 * nv) * 2 + ts * LANES * 4,
                                         H_B * DV_B * LANES * 4)),
        name="gated_linear_attention",
    )(proj, proj, proj, proj, gate, w_pad, b_gk.reshape(1, nk), gla_g.reshape(1, DV_B))


def _stick_kernel(q_ref, k_ref, v_ref, o_ref, run_ref, acc_ref):
    t = ATTN_TILE
    chains = STICK_PAIRS
    i = pl.program_id(2)
    lane = lax.broadcasted_iota(jnp.int32, (t, LANES), 1)
    scale = jnp.asarray(DH_C ** -0.5, BF16)
    qqs = []
    for c in range(chains):
        q = q_ref[0, :, c * LANES:(c + 1) * LANES]
        zero = jnp.zeros_like(q)
        qqs.append(jnp.concatenate([jnp.where(lane < DH_C, q, zero),
                                    jnp.where(lane >= DH_C, q, zero)], axis=0) * scale)
    row = lax.broadcasted_iota(jnp.int32, (t, t), 0)
    col = lax.broadcasted_iota(jnp.int32, (t, t), 1)
    later = jnp.where(row > col, 1.0, 0.0).astype(BF16)
    below = (lax.broadcasted_iota(jnp.int32, (2 * t, t), 1)
             < lax.broadcasted_iota(jnp.int32, (2 * t, t), 0) % t)

    def logits(j):
        r0 = pl.multiple_of(jnp.maximum(j, 0) * t, t)
        return [_nt(qqs[c], k_ref[0, pl.ds(r0, t), c * LANES:(c + 1) * LANES])
                for c in range(chains)]

    def tiles(first, zs, runs, accs, diagonal_first):
        runs, accs = list(runs), list(accs)
        mids = []
        for u in range(len(zs)):
            for c in range(chains):
                z = zs[u][c]
                sp = _softplus(z)
                log_1m = -sp
                masked = diagonal_first and u == 0
                if masked:
                    log_1m = jnp.where(below, log_1m, 0.0)
                hi, lo = _split(log_1m)
                within = _dot(hi, later) + _dot(lo, later)
                mids.append((u, c, z - sp, log_1m, within, masked))
        for u, c, log_b, log_1m, within, masked in mids:
            r0 = pl.multiple_of((first - u) * t, t)
            vb = v_ref[0, pl.ds(r0, t), c * LANES:(c + 1) * LANES]
            a = jnp.exp(log_b + (within + runs[c]))
            if masked:
                a = jnp.where(below, a, 0.0)
            ab = a.astype(BF16)
            zero_v = jnp.zeros_like(vb)
            a2 = jnp.concatenate([ab[:t], ab[t:]], axis=1)
            v2 = jnp.concatenate([jnp.where(lane < DH_C, vb, zero_v),
                                  jnp.where(lane >= DH_C, vb, zero_v)], axis=0)
            accs[c] = accs[c] + _dot(a2, v2)
            runs[c] = runs[c] + jnp.sum(log_1m, axis=-1, keepdims=True)
        return runs, accs

    def first_block(n_tiles):
        zs = [logits(i - u) for u in range(n_tiles)]
        runs, accs = tiles(i, zs, [jnp.zeros((2 * t, 1), F32)] * chains,
                           [jnp.zeros((t, LANES), F32)] * chains, True)
        for c in range(chains):
            run_ref[c] = runs[c]
            acc_ref[c] = accs[c]

    ahead = STICK_AHEAD

    @pl.when(i >= ahead)
    def _():
        first_block(1 + ahead)

    @pl.when(i < ahead)
    def _():
        first_block(1)

    n_left = i
    done0 = jnp.where(i >= ahead, ahead, 0)

    def go_on(done, runs):
        top = runs[0]
        for c in range(1, chains):
            top = jnp.maximum(top, runs[c])
        return ((done < n_left) & (jnp.max(top) > EXP_ZERO_BELOW)).astype(jnp.int32)

    def body(carry):
        done, _, runs, accs = carry
        first = n_left - 1 - done
        runs, accs = tiles(first, [logits(first)], runs, accs, False)
        done = done + 1
        return done, go_on(done, runs), tuple(runs), tuple(accs)

    runs = tuple(run_ref[c] for c in range(chains))
    accs = tuple(acc_ref[c] for c in range(chains))
    carry = (done0, go_on(done0, runs), runs, accs)
    accs = lax.while_loop(lambda carry: carry[1] > 0, body, carry)[3]
    for c in range(chains):
        o_ref[0, :, c * LANES:(c + 1) * LANES] = accs[c].astype(o_ref.dtype)


def _stick_breaking(proj):
    bsz, s, _ = proj.shape
    t = ATTN_TILE
    width = STICK_PAIRS * LANES
    n_blocks = H_C * DH_C // width
    return pl.pallas_call(
        _stick_kernel,
        out_shape=jax.ShapeDtypeStruct((bsz, s, H_C * DH_C), BF16),
        grid=(bsz, n_blocks, s // t),
        in_specs=[
            pl.BlockSpec((1, t, width), lambda b, h, i: (b, i, h)),
            pl.BlockSpec((1, s, width), lambda b, h, i: (b, 0, n_blocks + h)),
            pl.BlockSpec((1, s, width), lambda b, h, i: (b, 0, 2 * n_blocks + h)),
        ],
        out_specs=pl.BlockSpec((1, t, width), lambda b, h, i: (b, i, h)),
        scratch_shapes=[pltpu.VMEM((STICK_PAIRS, 2 * t, 1), F32),
                        pltpu.VMEM((STICK_PAIRS, t, LANES), F32)],
        compiler_params=pltpu.CompilerParams(
            dimension_semantics=("arbitrary", "arbitrary", "arbitrary"),
            vmem_limit_bytes=_vmem_limit(2 * s * width * 2 + 4 * t * width,
                                         STICK_PAIRS * 3 * t * LANES * 4)),
        name="stick_breaking_attention",
    )(proj, proj, proj)


def _route(logits):
    lane_i = lax.broadcasted_iota(jnp.int32, logits.shape, 1)
    lane = lane_i.astype(F32)
    group_of_lane = (lane_i // EXPERTS_PER_GROUP).astype(F32)
    big = jnp.float32(1 << 20)
    neg = jnp.float32(-jnp.inf)
    is_group = (lane_i >= N_EXPERTS) & (lane_i < N_EXPERTS + N_GROUPS)
    gl = jnp.where(is_group, logits, neg)
    gmax = jnp.max(gl, axis=-1, keepdims=True)
    g_val = 1.0 / jnp.sum(jnp.exp(gl - gmax), axis=-1, keepdims=True)
    g_sel = jnp.min(jnp.where(gl == gmax, lane - N_EXPERTS, big), axis=-1, keepdims=True)
    in_group = (lane_i < N_EXPERTS) & (group_of_lane == g_sel)
    el = jnp.where(in_group, logits, neg)
    emax = jnp.max(el, axis=-1, keepdims=True)
    esum = jnp.sum(jnp.exp(el - emax), axis=-1, keepdims=True)
    i1 = jnp.min(jnp.where(el == emax, lane, big), axis=-1, keepdims=True)
    el2 = jnp.where(lane == i1, neg, el)
    emax2 = jnp.max(el2, axis=-1, keepdims=True)
    i2 = jnp.min(jnp.where(el2 == emax2, lane, big), axis=-1, keepdims=True)
    v1 = 1.0 / esum
    v2 = jnp.exp(emax2 - emax) / esum
    tot = v1 + v2
    w1 = g_val * (v1 / tot)
    w2 = g_val * (v2 / tot)
    out = jnp.where(lane_i == 0, i1, 0.0)
    out = jnp.where(lane_i == 1, i2, out)
    out = jnp.where(lane_i == 2, w1, out)
    out = jnp.where(lane_i == 3, w2, out)
    return out, (lane == i1), (lane == i2)


def _pack_halves(h):
    half = h.shape[1] // 2
    lo = pltpu.bitcast(h[:, :half].astype(BF16).astype(F32), jnp.uint32)
    hi = pltpu.bitcast(h[:, half:].astype(BF16).astype(F32), jnp.uint32)
    return lax.shift_right_logical(lo, jnp.uint32(16)) | hi


def _unpack_halves(p):
    lo = pltpu.bitcast(lax.shift_left(p, jnp.uint32(16)), F32).astype(BF16)
    hi = pltpu.bitcast(p & jnp.uint32(0xFFFF0000), F32).astype(BF16)
    return lo, hi


def _out_kernel(*refs, n_o):
    o_refs = refs[:n_o]
    w_refs = refs[n_o:2 * n_o]
    x_ref, mod_ref, g_ref, wr_ref, br_ref, xo_ref = refs[2 * n_o:2 * n_o + 6]
    h_refs = refs[2 * n_o + 6:2 * n_o + 6 + MOE_PIECES]
    r_ref, cnt_ref, run_ref = refs[2 * n_o + 6 + MOE_PIECES:]

    @pl.when((pl.program_id(0) == 0) & (pl.program_id(1) == 0))
    def _():
        run_ref[...] = jnp.zeros_like(run_ref)

    th = x_ref.shape[1] // PROJ_ROW_PARTS
    ms = []
    for r in range(PROJ_ROW_PARTS):
        rows = slice(r * th, (r + 1) * th)
        m = _dot(o_refs[0][0, rows, :], w_refs[0][...])
        for a in range(1, n_o):
            m = m + _dot(o_refs[a][0, rows, :], w_refs[a][...])
        ms.append(m)
    routes, hits1, hits2 = [], [], []
    for r in range(PROJ_ROW_PARTS):
        rows = slice(r * th, (r + 1) * th)
        x = x_ref[0, rows, :] + mod_ref[0, 2:3, :] * ms[r]
        xo_ref[0, rows, :] = x
        h = _rms(x, g_ref[...]) * (1.0 + mod_ref[0, 4:5, :]) + mod_ref[0, 3:4, :]
        piece = h.shape[1] // MOE_PIECES
        for k in range(MOE_PIECES):
            h_refs[k][0, rows, :] = _pack_halves(h[:, k * piece:(k + 1) * piece])
        part_route, part_hit1, part_hit2 = _route(_dot3_narrow(h, wr_ref[...]) + br_ref[...])
        routes.append(part_route)
        hits1.append(part_hit1)
        hits2.append(part_hit2)
    route = jnp.concatenate(routes, axis=0)
    hit1 = jnp.concatenate(hits1, axis=0)
    hit2 = jnp.concatenate(hits2, axis=0)
    tm = route.shape[0]
    picks = jnp.where(hit1 | hit2, 1.0, 0.0)
    row = lax.broadcasted_iota(jnp.int32, (tm, tm), 0)
    col = lax.broadcasted_iota(jnp.int32, (tm, tm), 1)
    before = jnp.where(col < row, 1.0, 0.0).astype(BF16)
    earlier = _dot(before, picks.astype(BF16)) + run_ref[...]
    rank1 = jnp.sum(jnp.where(hit1, earlier, 0.0), axis=-1, keepdims=True)
    rank2 = jnp.sum(jnp.where(hit2, earlier, 0.0), axis=-1, keepdims=True)
    lane_i = lax.broadcasted_iota(jnp.int32, route.shape, 1)
    route = jnp.where(lane_i == 4, rank1, route)
    route = jnp.where(lane_i == 5, rank2, route)
    r_ref[0] = route
    run_ref[...] = run_ref[...] + jnp.sum(picks, axis=0, keepdims=True)
    cnt_ref[...] = run_ref[...]


def _out_projection(o_list, w_list, x, mod, g, w_router, b_router):
    bsz, s, d = x.shape
    tm = TOKEN_TILE
    n_o = len(o_list)
    packed = d // (2 * MOE_PIECES)
    tok = lambda b, i: (b, i, 0)
    const2 = lambda b, i: (0, 0)
    in_specs = ([pl.BlockSpec((1, tm, o.shape[2]), tok) for o in o_list]
                + [pl.BlockSpec(w.shape, const2) for w in w_list]
                + [pl.BlockSpec((1, tm, d), tok), pl.BlockSpec((1, 6, d), lambda b, i: (b, 0, 0)),
                   pl.BlockSpec((1, d), const2), pl.BlockSpec((d, LANES), const2),
                   pl.BlockSpec((1, LANES), const2)])
    block_bytes = (sum(tm * o.shape[2] * 2 for o in o_list) + sum(w.size * 2 for w in w_list)
                   + 2 * tm * d * 4 + tm * d * 2 + d * LANES * 4 + tm * LANES * 4)
    return pl.pallas_call(
        functools.partial(_out_kernel, n_o=n_o),
        out_shape=([jax.ShapeDtypeStruct((bsz, s, d), F32)]
                   + [jax.ShapeDtypeStruct((bsz, s, packed), jnp.uint32)] * MOE_PIECES
                   + [jax.ShapeDtypeStruct((bsz, s, LANES), F32),
                      jax.ShapeDtypeStruct((1, LANES), F32)]),
        grid=(bsz, s // tm),
        in_specs=in_specs,
        out_specs=([pl.BlockSpec((1, tm, d), tok)]
                   + [pl.BlockSpec((1, tm, packed), tok)] * MOE_PIECES
                   + [pl.BlockSpec((1, tm, LANES), tok), pl.BlockSpec((1, LANES), const2)]),
        scratch_shapes=[pltpu.VMEM((1, LANES), F32)],
        compiler_params=pltpu.CompilerParams(
            dimension_semantics=("arbitrary", "arbitrary"),
            vmem_limit_bytes=_vmem_limit(block_bytes, tm * d * 12 + tm * tm * 4)),
        name="out_projection_norm_router",
    )(*o_list, *w_list, x, mod, g.reshape(1, d), w_router, b_router)


def _expert_kernel(te_ref, tv_ref, nu_ref, *refs):
    x_refs = refs[:MOE_PIECES]
    wg_ref, wu_ref, wd_ref, o_ref, wgb_ref, wub_ref, wdb_ref = refs[MOE_PIECES:]
    t = pl.program_id(0)
    prev = te_ref[jnp.maximum(t - 1, 0)]

    @pl.when((t == 0) | (te_ref[t] != prev))
    def _():
        wgb_ref[...] = wg_ref[0, 0].astype(BF16)
        wub_ref[...] = wu_ref[0, 0].astype(BF16)
        wdb_ref[...] = wd_ref[0, 0].astype(BF16)

    @pl.when(t < nu_ref[0])
    def _():
        th = o_ref.shape[0] // EXPERT_ROW_PARTS
        hidden = []
        for r in range(EXPERT_ROW_PARTS):
            rows = slice(r * th, (r + 1) * th)
            cols = []
            for x_ref in x_refs:
                row = lax.broadcasted_iota(jnp.int32, (th, x_ref.shape[1]), 0) + r * th
                cols += list(_unpack_halves(jnp.where(row < tv_ref[t], x_ref[rows, :],
                                                      jnp.uint32(0))))
            w = cols[0].shape[1]
            gt = _dot(cols[0], wgb_ref[:w, :])
            up = _dot(cols[0], wub_ref[:w, :])
            for k in range(1, len(cols)):
                gt = gt + _dot(cols[k], wgb_ref[k * w:(k + 1) * w, :])
                up = up + _dot(cols[k], wub_ref[k * w:(k + 1) * w, :])
            hidden.append((gt, up))
        for r in range(EXPERT_ROW_PARTS):
            gt, up = hidden[r]
            o_ref[r * th:(r + 1) * th, :] = _dot((_silu(gt) * up).astype(BF16),
                                                 wdb_ref[...]).astype(o_ref.dtype)

    @pl.when(t >= nu_ref[0])
    def _():
        o_ref[...] = jnp.zeros_like(o_ref)


def _grouped_experts(xs, tile_expert, tile_valid, n_used, w_gate, w_up, w_down, layer):
    p, packed = xs[0].shape
    d = w_gate.shape[2]
    tm = EXPERT_TILE
    f = w_gate.shape[3]
    grid_spec = pltpu.PrefetchScalarGridSpec(
        num_scalar_prefetch=3,
        grid=(p // tm,),
        in_specs=[pl.BlockSpec((tm, packed), lambda t, te, tv, nu: (t, 0))] * MOE_PIECES + [
            pl.BlockSpec((1, 1, d, f), lambda t, te, tv, nu: (layer, te[t], 0, 0)),
            pl.BlockSpec((1, 1, d, f), lambda t, te, tv, nu: (layer, te[t], 0, 0)),
            pl.BlockSpec((1, 1, f, d), lambda t, te, tv, nu: (layer, te[t], 0, 0)),
        ],
        out_specs=pl.BlockSpec((tm, d), lambda t, te, tv, nu: (t, 0)),
        scratch_shapes=[pltpu.VMEM((d, f), BF16), pltpu.VMEM((d, f), BF16),
                        pltpu.VMEM((f, d), BF16)],
    )
    block_bytes = tm * d * 2 + 3 * d * f * 4 + tm * d * 2
    return pl.pallas_call(
        _expert_kernel,
        out_shape=jax.ShapeDtypeStruct((p, d), BF16),
        grid_spec=grid_spec,
        compiler_params=pltpu.CompilerParams(
            dimension_semantics=("arbitrary",),
            vmem_limit_bytes=_vmem_limit(block_bytes, 3 * d * f * 2 + tm * f * 16)),
        name="grouped_swiglu_experts",
    )(tile_expert, tile_valid, n_used, *xs, w_gate, w_up, w_down)


def _pos_kernel(route_ref, starts_ref, pos_ref):
    rt = jnp.transpose(route_ref[...])
    starts = starts_ref[...]
    ids = lax.broadcasted_iota(jnp.int32, (N_EXPERTS, rt.shape[1]), 0).astype(F32)
    for c in range(2):
        start_of = jnp.sum(jnp.where(ids == rt[c:c + 1, :], starts, 0.0), axis=0, keepdims=True)
        pos_ref[c:c + 1, :] = (start_of + rt[4 + c:5 + c, :]).astype(jnp.int32)


def _row_positions(route, starts):
    n_tok = route.shape[0]
    tm = TOKEN_TILE
    return pl.pallas_call(
        _pos_kernel,
        out_shape=jax.ShapeDtypeStruct((2, n_tok), jnp.int32),
        grid=(n_tok // tm,),
        in_specs=[pl.BlockSpec((tm, LANES), lambda i: (i, 0)),
                  pl.BlockSpec((N_EXPERTS, 1), lambda i: (0, 0))],
        out_specs=pl.BlockSpec((2, tm), lambda i: (0, i)),
        compiler_params=pltpu.CompilerParams(
            dimension_semantics=("arbitrary",),
            vmem_limit_bytes=_vmem_limit(tm * LANES * 4 + N_EXPERTS * LANES * 4 + 8 * tm * 4)),
        name="expert_row_positions",
    )(route, starts.astype(F32).reshape(N_EXPERTS, 1))


def _scatter_rows(rows, pos, n_out):
    n_tok, d = rows.shape
    n_idx = pos.shape[0]
    w = SC_SCATTER_WINDOW
    per_pass = n_tok // w
    mesh = plsc.VectorSubcoreMesh(core_axis_name="core", subcore_axis_name="subcore")

    @pl.kernel(out_type=jax.ShapeDtypeStruct((n_out, d), rows.dtype), mesh=mesh, scratch_types=[])
    def scatter_kernel(x_hbm, i_hbm, o_hbm):
        def body(x_vmem, i_vmem):
            pltpu.sync_copy(x_vmem, o_hbm.at[i_vmem.at[0]])

        pltpu.emit_pipeline(
            body,
            grid=(n_idx // w,),
            in_specs=[pl.BlockSpec((w, d), lambda i: (i % per_pass, 0)),
                      pl.BlockSpec((1, w), lambda i: (0, i))],
            out_specs=[],
            core_axis_name=("core", "subcore"),
            dimension_semantics=(pltpu.PARALLEL,),
        )(x_hbm, i_hbm)

    return scatter_kernel(rows, pos.reshape(1, n_idx))


def _dispatch(route, counts, n_tok):
    tm = EXPERT_TILE
    counts = counts[:N_EXPERTS].astype(jnp.int32)
    padded = ((counts + tm - 1) // tm) * tm
    ends = jnp.cumsum(padded)
    starts = ends - padded
    pos = _row_positions(route, starts).reshape(2 * n_tok)
    p = 2 * n_tok + N_EXPERTS * tm
    tile_start = jnp.arange(p // tm, dtype=jnp.int32) * tm
    tile_expert = jnp.minimum(jnp.sum(ends[None, :] <= tile_start[:, None], axis=1),
                              N_EXPERTS - 1).astype(jnp.int32)
    n_used = (ends[-1] // tm).astype(jnp.int32).reshape(1)
    row_end = starts + counts
    tile_valid = jnp.clip(jnp.sum(jnp.where(tile_expert[:, None] == jnp.arange(N_EXPERTS)[None, :],
                                            row_end[None, :], 0), axis=1) - tile_start, 0, tm)
    return p, tile_expert, tile_valid.astype(jnp.int32), n_used, pos


def _final_kernel(x_ref, y_ref, route_ref, mod_ref, g_ref, o_ref):
    o_ref[0] = _rms(_expert_residual(x_ref[0], y_ref, route_ref, mod_ref), g_ref[...])


def _final_norm(x, y, route, mod, g):
    bsz, s, d = x.shape
    tm = TOKEN_TILE
    tok = lambda b, i: (b, i, 0)
    return pl.pallas_call(
        _final_kernel,
        out_shape=jax.ShapeDtypeStruct((bsz, s, d), F32),
        grid=(bsz, s // tm),
        in_specs=[pl.BlockSpec((1, tm, d), tok),
                  pl.BlockSpec((2, 1, tm, d), lambda b, i: (0, b, i, 0)),
                  pl.BlockSpec((1, tm, LANES), tok),
                  pl.BlockSpec((1, 6, d), lambda b, i: (b, 0, 0)),
                  pl.BlockSpec((1, d), lambda b, i: (0, 0))],
        out_specs=pl.BlockSpec((1, tm, d), tok),
        compiler_params=pltpu.CompilerParams(
            dimension_semantics=("arbitrary", "arbitrary"),
            vmem_limit_bytes=_vmem_limit(4 * tm * d * 4 + tm * LANES * 4)),
        name="residual_final_norm",
    )(x, y, route, mod, g.reshape(1, d))


def kernel(x, c, w_ada, b_ada, norm_mix, norm_ffn, norm_final, rel_bias, even_w_in, even_lambda, even_subln, even_w_gk2, even_b_gk, even_gla_norm, even_w_out, odd_w_in, odd_w_out, router_group_w, router_group_b, router_expert_w, router_expert_b, expert_w_gate, expert_w_up, expert_w_down):
    bsz, s, d = x.shape
    n_tok = bsz * s
    assert s % TOKEN_TILE == 0 and s % GLA_STEP == 0
    assert s % (DIFF_KEY_TILES * ATTN_TILE) == 0 and s % (DIFF_Q_TILES * ATTN_TILE) == 0
    mod_all = _modulation(c, w_ada, b_ada).reshape(DEPTH, bsz, 6, d)
    bias_tiles = _diff_bias_tiles(rel_bias)
    y = route = prev_mod = None
    for l in range(DEPTH):
        mod = mod_all[l]
        i = l // 2
        if l % 2 == 0:
            w_in = even_w_in[i]
            w_main = w_in[:, :EVEN_MAIN_COLS].astype(BF16)
            w_gate = jnp.zeros((d, LANES), BF16).at[:, :GATE_RANK].set(
                w_in[:, EVEN_MAIN_COLS:].astype(BF16))
            x, proj, gate = _in_projection(x, y, route, prev_mod, norm_mix[l], mod, w_main, w_gate)
            oa = _diff_attention(proj, bias_tiles, even_lambda[i], even_subln[i], l)
            ob = _gla(proj, gate, even_w_gk2[i], even_b_gk[i], even_gla_norm[i])
            w_out = even_w_out[i].astype(BF16)
            o_list = [oa, ob]
            w_list = [w_out[:H_A * DV_A], w_out[H_A * DV_A:]]
        else:
            x, proj, _ = _in_projection(x, y, route, prev_mod, norm_mix[l], mod,
                                        odd_w_in[i].astype(BF16), None)
            o_list = [_stick_breaking(proj)]
            w_list = [odd_w_out[i].astype(BF16)]
        w_router = jnp.zeros((d, LANES), F32)
        w_router = w_router.at[:, :N_EXPERTS].set(router_expert_w[l].astype(F32))
        w_router = w_router.at[:, N_EXPERTS:N_EXPERTS + N_GROUPS].set(router_group_w[l].astype(F32))
        b_router = jnp.zeros((1, LANES), F32)
        b_router = b_router.at[0, :N_EXPERTS].set(router_expert_b[l].astype(F32))
        b_router = b_router.at[0, N_EXPERTS:N_EXPERTS + N_GROUPS].set(router_group_b[l].astype(F32))
        outs = _out_projection(o_list, w_list, x, mod, norm_ffn[l], w_router, b_router)
        x, pieces, route, counts = outs[0], outs[1:1 + MOE_PIECES], outs[-2], outs[-1]
        n_rows, tile_expert, tile_valid, n_used, pos = _dispatch(route.reshape(n_tok, LANES),
                                                                 counts[0], n_tok)
        xs = [_scatter_rows(piece.reshape(n_tok, piece.shape[2]), pos, n_rows) for piece in pieces]
        out = _grouped_experts(xs, tile_expert, tile_valid, n_used, expert_w_gate, expert_w_up,
                               expert_w_down, l)
        y = out.at[pos].get(mode="promise_in_bounds").reshape(2, bsz, s, d)
        prev_mod = mod
    return _final_norm(x, y, route, prev_mod, norm_final)
```

```python
import functools
import math

import jax
import jax.numpy as jnp
from jax import lax
from jax.experimental import pallas as pl
from jax.experimental.pallas import tpu as pltpu
from jax.experimental.pallas import tpu_sc as plsc

F32 = jnp.float32
BF16 = jnp.bfloat16

DEPTH = 4
CHUNK = 64
EPS = 1e-6
H_A, DH_A, DV_A = 4, 64, 128
H_B, DK_B, DV_B = 4, 64, 128
GATE_RANK = 16
GATE_TAU = 16.0
H_C, DH_C = 16, 64
N_BUCKETS = 32
MAX_DISTANCE = 128
N_GROUPS = 4
EXPERTS_PER_GROUP = 8
N_EXPERTS = N_GROUPS * EXPERTS_PER_GROUP
EVEN_MAIN_COLS = 3 * H_A * DV_A + 2 * H_B * DK_B + 2 * H_B * DV_B
GLA_Q_COL = 3 * H_A * DV_A
GLA_K_COL = GLA_Q_COL + H_B * DK_B
GLA_V_COL = GLA_K_COL + H_B * DK_B
GLA_R_COL = GLA_V_COL + H_B * DV_B

LANES = 128
V7X_VMEM_BYTES = 64 * 1024 * 1024
VMEM_TEMPORARIES_BYTES = 16 * 1024 * 1024
VMEM_RESERVED_BYTES = 8 * 1024 * 1024
NEG_BIG = -1e30
EXP_ZERO_BELOW = -104.0

TOKEN_TILE = 512
PROJ_ROW_PARTS = 2
ATTN_TILE = 128
STICK_PAIRS = 8
STICK_AHEAD = 2
STICK_LOGITS_AHEAD = 8
DIFF_KEY_TILES = 8
DIFF_Q_TILES = 8
GLA_STEP = 256
EXPERT_TILE = 512
EXPERT_ROW_PARTS = 4
SC_SCATTER_WINDOW = 128
MOE_PIECES = 2


def _vmem_limit(block_bytes, scratch_bytes=0):
    est = 2 * block_bytes + scratch_bytes + VMEM_TEMPORARIES_BYTES
    return int(min(est, V7X_VMEM_BYTES - VMEM_RESERVED_BYTES))


def _nt(a, b):
    return lax.dot_general(a, b, (((1,), (1,)), ((), ())), preferred_element_type=F32)


def _tn(a, b):
    return lax.dot_general(a, b, (((0,), (0,)), ((), ())), preferred_element_type=F32)


def _dot(a, b):
    return jnp.dot(a, b, preferred_element_type=F32)


def _split(x):
    hi = x.astype(BF16)
    lo = (x - hi.astype(F32)).astype(BF16)
    return hi, lo


def _dot3(a, b):
    ah, al = _split(a)
    bh, bl = _split(b)
    return _dot(ah, bh) + _dot(al, bh) + _dot(ah, bl)


def _dot3_narrow(a, b):
    ah, al = _split(a)
    bh, bl = _split(b)
    n = b.shape[1]
    both = _dot(ah, jnp.concatenate([bh, bl], axis=1))
    return both[:, :n] + both[:, n:] + _dot(al, bh)


def _rms(x, g):
    return x * lax.rsqrt(jnp.mean(x * x, axis=-1, keepdims=True) + EPS) * g


def _softplus(z):
    return jnp.maximum(z, 0.0) + jnp.log(1.0 + jnp.exp(-jnp.abs(z)))


def _silu(x):
    return x / (1.0 + jnp.exp(-x))


def _mod_kernel(c_ref, w_ref, b_ref, o_ref):
    c = c_ref[...]
    o_ref[0] = _dot3(_silu(c), w_ref[0]) + b_ref[0]


def _modulation(c, w_ada, b_ada):
    depth, d, n = w_ada.shape
    bsz = c.shape[0]
    tn = n // 4
    return pl.pallas_call(
        _mod_kernel,
        out_shape=jax.ShapeDtypeStruct((depth, bsz, n), F32),
        grid=(depth, n // tn),
        in_specs=[
            pl.BlockSpec((bsz, d), lambda l, j: (0, 0)),
            pl.BlockSpec((1, d, tn), lambda l, j: (l, 0, j)),
            pl.BlockSpec((1, 1, tn), lambda l, j: (l, 0, j)),
        ],
        out_specs=pl.BlockSpec((1, bsz, tn), lambda l, j: (l, 0, j)),
        compiler_params=pltpu.CompilerParams(
            dimension_semantics=("arbitrary", "arbitrary"),
            vmem_limit_bytes=_vmem_limit(d * tn * 4 + bsz * (d + tn) * 4)),
        name="adaln_modulation",
    )(c, w_ada, b_ada.reshape(depth, 1, n))


def _expert_residual(x, y_ref, route_ref, mod_ref, rows=slice(None)):
    route = route_ref[0, rows, :]
    y = (route[:, 2:3] * y_ref[0, 0, rows, :].astype(F32)
         + route[:, 3:4] * y_ref[1, 0, rows, :].astype(F32))
    return x + mod_ref[0, 5:6, :] * y


def _proj_kernel(*refs, has_res, n_main, has_gate):
    it = iter(refs)
    x_ref = next(it)
    if has_res:
        y_ref = next(it)
        route_ref = next(it)
        pmod_ref = next(it)
    g_ref = next(it)
    mod_ref = next(it)
    w_ref = next(it)
    wg_ref = next(it) if has_gate else None
    xo_ref = next(it) if has_res else None
    o_ref = next(it)
    og_ref = next(it) if has_gate else None

    th = x_ref.shape[1] // PROJ_ROW_PARTS
    hbs = []
    for r in range(PROJ_ROW_PARTS):
        rows = slice(r * th, (r + 1) * th)
        x = x_ref[0, rows, :]
        if has_res:
            x = _expert_residual(x, y_ref, route_ref, pmod_ref, rows)
            xo_ref[0, rows, :] = x
        h = _rms(x, g_ref[...]) * (1.0 + mod_ref[0, 1:2, :]) + mod_ref[0, 0:1, :]
        hbs.append(h.astype(BF16))
    step = 4 * LANES
    for r in range(PROJ_ROW_PARTS):
        rows = slice(r * th, (r + 1) * th)
        for n0 in range(0, n_main, step):
            o_ref[0, rows, n0:n0 + step] = _dot(hbs[r], w_ref[:, n0:n0 + step]).astype(BF16)
        if has_gate:
            og_ref[0, rows, :] = _dot(hbs[r], wg_ref[...])


def _in_projection(x, y, route, prev_mod, g, mod, w, w_gate):
    bsz, s, d = x.shape
    tm = TOKEN_TILE
    has_res = y is not None
    has_gate = w_gate is not None
    n_main = w.shape[1]
    tok = lambda b, i: (b, i, 0)
    per_b = lambda b, i: (b, 0, 0)
    const2 = lambda b, i: (0, 0)
    in_specs = [pl.BlockSpec((1, tm, d), tok)]
    args = [x]
    if has_res:
        in_specs += [pl.BlockSpec((2, 1, tm, d), lambda b, i: (0, b, i, 0)),
                     pl.BlockSpec((1, tm, LANES), tok), pl.BlockSpec((1, 6, d), per_b)]
        args += [y, route, prev_mod]
    in_specs += [pl.BlockSpec((1, d), const2), pl.BlockSpec((1, 6, d), per_b),
                 pl.BlockSpec((d, n_main), const2)]
    args += [g.reshape(1, d), mod, w]
    if has_gate:
        in_specs.append(pl.BlockSpec((d, LANES), const2))
        args.append(w_gate)
    out_shape, out_specs = [], []
    if has_res:
        out_shape.append(jax.ShapeDtypeStruct((bsz, s, d), F32))
        out_specs.append(pl.BlockSpec((1, tm, d), tok))
    out_shape.append(jax.ShapeDtypeStruct((bsz, s, n_main), BF16))
    out_specs.append(pl.BlockSpec((1, tm, n_main), tok))
    if has_gate:
        out_shape.append(jax.ShapeDtypeStruct((bsz, s, LANES), F32))
        out_specs.append(pl.BlockSpec((1, tm, LANES), tok))
    block_bytes = (tm * d * 4 * (4 if has_res else 1) + d * n_main * 2 + tm * n_main * 2
                   + d * LANES * 2 + 2 * tm * LANES * 4)
    outs = pl.pallas_call(
        functools.partial(_proj_kernel, has_res=has_res, n_main=n_main, has_gate=has_gate),
        out_shape=out_shape,
        grid=(bsz, s // tm),
        in_specs=in_specs,
        out_specs=out_specs,
        compiler_params=pltpu.CompilerParams(
            dimension_semantics=("arbitrary", "arbitrary"),
            vmem_limit_bytes=_vmem_limit(block_bytes, tm * d * 8)),
        name="norm_mod_in_projection",
    )(*args)
    outs = list(outs)
    x_new = outs.pop(0) if has_res else x
    proj = outs.pop(0)
    gate = outs.pop(0) if has_gate else None
    return x_new, proj, gate


def _t5_bucket(rel):
    nb = N_BUCKETS // 2
    max_exact = nb // 2
    ret = jnp.where(rel > 0, nb, 0)
    n = jnp.abs(rel)
    nf = jnp.maximum(n, 1).astype(F32)
    large = max_exact + (jnp.log(nf / max_exact) / math.log(MAX_DISTANCE / max_exact)
                         * (nb - max_exact)).astype(jnp.int32)
    large = jnp.minimum(large, nb - 1)
    return ret + jnp.where(n < max_exact, n, large)


def _diff_bias_tiles(rel_bias):
    t = ATTN_TILE
    assert t >= MAX_DISTANCE, "tiles two or more away must lie beyond the last bucket edge"
    qp = jnp.arange(t)[:, None]
    kp = jnp.arange(t)[None, :]
    tiles = []
    for off in (0, 1, 2):
        rel = (kp - off * t) - qp
        hit = _t5_bucket(rel)[None, :, :, None] == jnp.arange(N_BUCKETS)[None, None, None, :]
        b = jnp.sum(jnp.where(hit, jnp.transpose(rel_bias.astype(F32))[:, None, None, :], 0.0),
                    axis=-1)
        if off == 0:
            b = jnp.where((kp // CHUNK) <= (qp // CHUNK), b, NEG_BIG)
        tiles.append(b)
    tab = jnp.stack(tiles, axis=1)
    return jnp.concatenate([tab, tab], axis=2)


def _diff_attn_kernel(q_ref, k_ref, v_ref, bias_ref, lam_ref, g_ref, o_ref, *, lambda_init):
    t = ATTN_TILE
    chains = DIFF_Q_TILES
    i = pl.program_id(2)
    lane = lax.broadcasted_iota(jnp.int32, (t, LANES), 1)
    scale = jnp.asarray(DH_A ** -0.5, BF16)
    stacked = []
    for c in range(chains):
        q = q_ref[0, c * t:(c + 1) * t, :]
        zero = jnp.zeros_like(q)
        stacked += [jnp.where(lane < DH_A, q, zero), jnp.where(lane >= DH_A, q, zero)]
    qq = jnp.concatenate(stacked, axis=0) * scale

    group = DIFF_KEY_TILES
    span = group * t
    assert chains == group, "a grid step's query tiles must be exactly one key block"

    def softmax_step(carry_c, s, which_of, vb):
        m, l, acc = carry_c
        width = s.shape[1] // t

        def logits(u):
            return s[:, u * t:(u + 1) * t] + bias_ref[0, which_of(u)]

        top = logits(0)
        for u in range(1, width):
            top = jnp.maximum(top, logits(u))
        m_new = jnp.maximum(m, jnp.max(top, axis=-1, keepdims=True))
        alpha = jnp.exp(m - m_new)
        ps = [jnp.exp(logits(u) - m_new) for u in range(width)]
        tot = ps[0]
        for u in range(1, width):
            tot = tot + ps[u]
        l = alpha * l + jnp.sum(tot, axis=-1, keepdims=True)
        p = jnp.concatenate([pu.astype(BF16) for pu in ps], axis=1)
        return m_new, l, alpha * acc + _dot(p, vb)

    def body(n, carry):
        r0 = pl.multiple_of(n * span, span)
        kb = k_ref[0, pl.ds(r0, span), :]
        vb = v_ref[0, pl.ds(r0, span), :]
        out = []
        s_next = _nt(qq[0:2 * t], kb)
        for c in range(chains):
            s = s_next
            if c + 1 < chains:
                s_next = _nt(qq[(c + 1) * 2 * t:(c + 2) * 2 * t], kb)

            def which_of(u, c=c):
                return jnp.minimum((i * chains + c) - (n * group + u), 2)
            out.append(softmax_step(carry[c], s, which_of, vb))
        return tuple(out)

    init = tuple((jnp.full((2 * t, 1), NEG_BIG, F32), jnp.zeros((2 * t, 1), F32),
                  jnp.zeros((2 * t, DV_A), F32)) for _ in range(chains))
    before = lax.fori_loop(0, i, body, init)
    r0 = pl.multiple_of(i * span, span)
    def diag_logits(c):
        return _nt(qq[c * 2 * t:(c + 1) * 2 * t], k_ref[0, pl.ds(r0, (c + 1) * t), :])

    final = []
    s_next = diag_logits(0)
    for c in range(chains):
        s = s_next
        if c + 1 < chains:
            s_next = diag_logits(c + 1)
        final.append(softmax_step(before[c], s, lambda u, c=c: min(c - u, 2),
                                  v_ref[0, pl.ds(r0, (c + 1) * t), :]))
    lp = lam_ref[...].astype(F32)
    lam = (jnp.exp(jnp.sum(lp[0:1] * lp[1:2], axis=-1, keepdims=True))
           - jnp.exp(jnp.sum(lp[2:3] * lp[3:4], axis=-1, keepdims=True)) + lambda_init)
    for c in range(chains):
        _, l, acc = final[c]
        o = acc / l
        w = o[:t] - lam * o[t:]
        o_ref[0, c * t:(c + 1) * t, :] = (_rms(w, g_ref[...]) * (1.0 - lambda_init)
                                          ).astype(o_ref.dtype)


def _diff_attention(proj, bias_tiles, lam_p, subln_g, layer_idx):
    bsz, s, _ = proj.shape
    t = ATTN_TILE
    tq = DIFF_Q_TILES * t
    lambda_init = 0.8 - 0.6 * math.exp(-0.3 * layer_idx)
    return pl.pallas_call(
        functools.partial(_diff_attn_kernel, lambda_init=lambda_init),
        out_shape=jax.ShapeDtypeStruct((bsz, s, H_A * DV_A), BF16),
        grid=(bsz, H_A, s // tq),
        in_specs=[
            pl.BlockSpec((1, tq, LANES), lambda b, h, i: (b, i, h)),
            pl.BlockSpec((1, s, LANES), lambda b, h, i: (b, 0, H_A + h)),
            pl.BlockSpec((1, s, LANES), lambda b, h, i: (b, 0, 2 * H_A + h)),
            pl.BlockSpec((1, 3, 2 * t, t), lambda b, h, i: (h, 0, 0, 0)),
            pl.BlockSpec((4, DH_A), lambda b, h, i: (0, 0)),
            pl.BlockSpec((1, DV_A), lambda b, h, i: (0, 0)),
        ],
        out_specs=pl.BlockSpec((1, tq, DV_A), lambda b, h, i: (b, i, h)),
        compiler_params=pltpu.CompilerParams(
            dimension_semantics=("arbitrary", "arbitrary", "arbitrary"),
            vmem_limit_bytes=_vmem_limit(2 * s * LANES * 2 + 3 * 2 * t * t * 4 + 4 * tq * LANES,
                                         4 * tq * DIFF_KEY_TILES * t * 4)),
        name="diff_attention",
    )(proj, proj, proj, bias_tiles, lam_p, subln_g.reshape(1, DV_A))


def _gla_kernel(q_ref, k_ref, v_ref, r_ref, bg_ref, wgk_ref, bgk_ref, g_ref, o_ref, st_ref):
    c = CHUNK
    ts = q_ref.shape[1]
    n_chunks = ts // c

    @pl.when(pl.program_id(1) == 0)
    def _():
        st_ref[...] = jnp.zeros_like(st_ref)

    row = lax.broadcasted_iota(jnp.int32, (ts, ts), 0)
    col = lax.broadcasted_iota(jnp.int32, (ts, ts), 1)
    same_chunk_before = ((row // c) == (col // c)) & (col <= row)
    tri = jnp.where(same_chunk_before, 1.0, 0.0).astype(BF16)
    causal = (lax.broadcasted_iota(jnp.int32, (c, c), 1)
              <= lax.broadcasted_iota(jnp.int32, (c, c), 0))
    lane = lax.broadcasted_iota(jnp.int32, (ts, LANES), 1)
    wgk = wgk_ref[...]
    bgk = bgk_ref[...]
    g = g_ref[...]
    bg = bg_ref[0]

    las = [-_softplus(-(_dot3(bg, wgk[:, hp * LANES:(hp + 1) * LANES])
                        + bgk[:, hp * LANES:(hp + 1) * LANES])) / GATE_TAU
           for hp in range(H_B // 2)]
    bs = []
    for la in las:
        la_hi, la_lo = _split(la)
        bs.append(_dot(tri, la_hi) + _dot(tri, la_lo))

    heads = []
    for hp in range(H_B // 2):
        cols = slice(hp * LANES, (hp + 1) * LANES)
        b = bs[hp]
        b_last = jnp.concatenate(
            [jnp.broadcast_to(b[(ci + 1) * c - 1:(ci + 1) * c, :], (c, LANES))
             for ci in range(n_chunks)], axis=0)
        qf = q_ref[0, :, cols].astype(F32) * (DK_B ** -0.5)
        kf = k_ref[0, :, cols].astype(F32)
        q_dec = qf * jnp.exp(b)
        k_inv = (kf * jnp.exp(-b)).astype(BF16)
        k_dec = kf * jnp.exp(b_last - b)
        decays = [jnp.exp(b[(ci + 1) * c - 1:(ci + 1) * c, :]) for ci in range(n_chunks)]
        for hh in range(2):
            own = (lane >= hh * DK_B) & (lane < (hh + 1) * DK_B)
            heads.append((jnp.where(own, q_dec, 0.0).astype(BF16), k_inv,
                          jnp.where(own, k_dec, 0.0).astype(BF16), decays))

    def rows(x, ci):
        return x[ci * c:(ci + 1) * c]

    scores, increments = [], []
    for h, (qd, k_inv, kd, _) in enumerate(heads):
        vh = v_ref[0, :, h * DV_B:(h + 1) * DV_B]
        scores.append([jnp.where(causal, _nt(rows(qd, ci), rows(k_inv, ci)), 0.0).astype(BF16)
                       for ci in range(n_chunks)])
        increments.append([_tn(rows(vh, ci), rows(kd, ci)) for ci in range(n_chunks)])

    for h, (qd, _, _, decays) in enumerate(heads):
        vh = v_ref[0, :, h * DV_B:(h + 1) * DV_B]
        st = st_ref[h]
        states = []
        for ci in range(n_chunks):
            states.append(st.astype(BF16))
            st = st * decays[ci] + increments[h][ci]
        st_ref[h] = st
        o = jnp.concatenate(
            [_dot(scores[h][ci], rows(vh, ci)) + _nt(rows(qd, ci), states[ci])
             for ci in range(n_chunks)], axis=0)
        rr = r_ref[0, :, h * DV_B:(h + 1) * DV_B].astype(F32)
        o_ref[0, :, h * DV_B:(h + 1) * DV_B] = (_rms(o, g) * _silu(rr)).astype(o_ref.dtype)


def _gla(proj, gate, w_gk2, b_gk, gla_g):
    bsz, s, _ = proj.shape
    ts = GLA_STEP
    nk = H_B * DK_B
    nv = H_B * DV_B
    w_pad = jnp.zeros((LANES, nk), F32).at[:GATE_RANK].set(w_gk2.astype(F32))
    return pl.pallas_call(
        _gla_kernel,
        out_shape=jax.ShapeDtypeStruct((bsz, s, nv), BF16),
        grid=(bsz, s // ts),
        in_specs=[
            pl.BlockSpec((1, ts, nk), lambda b, i: (b, i, GLA_Q_COL // nk)),
            pl.BlockSpec((1, ts, nk), lambda b, i: (b, i, GLA_K_COL // nk)),
            pl.BlockSpec((1, ts, nv), lambda b, i: (b, i, GLA_V_COL // nv)),
            pl.BlockSpec((1, ts, nv), lambda b, i: (b, i, GLA_R_COL // nv)),
            pl.BlockSpec((1, ts, LANES), lambda b, i: (b, i, 0)),
            pl.BlockSpec((LANES, nk), lambda b, i: (0, 0)),
            pl.BlockSpec((1, nk), lambda b, i: (0, 0)),
            pl.BlockSpec((1, DV_B), lambda b, i: (0, 0)),
        ],
        out_specs=pl.BlockSpec((1, ts, nv), lambda b, i: (b, i, 0)),
        scratch_shapes=[pltpu.VMEM((H_B, DV_B, LANES), F32)],
        compiler_params=pltpu.CompilerParams(
            dimension_semantics=("arbitrary", "arbitrary"),
            vmem_limit_bytes=_vmem_limit(ts * (2 * nk + 3 * nv) * 2 + ts * LANES * 4,
                                         H_B * DV_B * LANES * 4)),
        name="gated_linear_attention",
    )(proj, proj, proj, proj, gate, w_pad, b_gk.reshape(1, nk), gla_g.reshape(1, DV_B))


def _stick_kernel(q_ref, k_ref, v_ref, o_ref, run_ref, acc_ref):
    t = ATTN_TILE
    chains = STICK_PAIRS
    i = pl.program_id(2)
    lane = lax.broadcasted_iota(jnp.int32, (t, LANES), 1)
    scale = jnp.asarray(DH_C ** -0.5, BF16)
    qqs = []
    for c in range(chains):
        q = q_ref[0, :, c * LANES:(c + 1) * LANES]
        zero = jnp.zeros_like(q)
        qqs.append(jnp.concatenate([jnp.where(lane < DH_C, q, zero),
                                    jnp.where(lane >= DH_C, q, zero)], axis=0) * scale)
    row = lax.broadcasted_iota(jnp.int32, (t, t), 0)
    col = lax.broadcasted_iota(jnp.int32, (t, t), 1)
    later = jnp.where(row > col, 1.0, 0.0).astype(BF16)
    below = (lax.broadcasted_iota(jnp.int32, (2 * t, t), 1)
             < lax.broadcasted_iota(jnp.int32, (2 * t, t), 0) % t)

    def logit(j, c):
        r0 = pl.multiple_of(jnp.maximum(j, 0) * t, t)
        return _nt(qqs[c], k_ref[0, pl.ds(r0, t), c * LANES:(c + 1) * LANES])

    def logits(j):
        return [functools.partial(logit, j, c) for c in range(chains)]

    def tiles(first, zs, runs, accs, diagonal_first):
        runs, accs = list(runs), list(accs)
        mids = []
        order = [(u, c) for u in range(len(zs)) for c in range(chains)]
        issued = [zs[u][c]() for u, c in order[:STICK_LOGITS_AHEAD]]
        for k, (u, c) in enumerate(order):
            if k + STICK_LOGITS_AHEAD < len(order):
                nu, nc = order[k + STICK_LOGITS_AHEAD]
                issued.append(zs[nu][nc]())
            z = issued[k]
            sp = _softplus(z)
            log_1m = -sp
            masked = diagonal_first and u == 0
            if masked:
                log_1m = jnp.where(below, log_1m, 0.0)
            hi, lo = _split(log_1m)
            within = _dot(hi, later) + _dot(lo, later)
            mids.append((u, c, z - sp, log_1m, within, masked))
        for u, c, log_b, log_1m, within, masked in mids:
            r0 = pl.multiple_of((first - u) * t, t)
            vb = v_ref[0, pl.ds(r0, t), c * LANES:(c + 1) * LANES]
            a = jnp.exp(log_b + (within + runs[c]))
            if masked:
                a = jnp.where(below, a, 0.0)
            ab = a.astype(BF16)
            zero_v = jnp.zeros_like(vb)
            a2 = jnp.concatenate([ab[:t], ab[t:]], axis=1)
            v2 = jnp.concatenate([jnp.where(lane < DH_C, vb, zero_v),
                                  jnp.where(lane >= DH_C, vb, zero_v)], axis=0)
            accs[c] = accs[c] + _dot(a2, v2)
            runs[c] = runs[c] + jnp.sum(log_1m, axis=-1, keepdims=True)
        return runs, accs

    def first_block(n_tiles):
        zs = [logits(i - u) for u in range(n_tiles)]
        runs, accs = tiles(i, zs, [jnp.zeros((2 * t, 1), F32)] * chains,
                           [jnp.zeros((t, LANES), F32)] * chains, True)
        for c in range(chains):
            run_ref[c] = runs[c]
            acc_ref[c] = accs[c]

    ahead = STICK_AHEAD

    @pl.when(i >= ahead)
    def _():
        first_block(1 + ahead)

    @pl.when(i < ahead)
    def _():
        first_block(1)

    n_left = i
    done0 = jnp.where(i >= ahead, ahead, 0)

    def go_on(done, runs):
        top = runs[0]
        for c in range(1, chains):
            top = jnp.maximum(top, runs[c])
        return ((done < n_left) & (jnp.max(top) > EXP_ZERO_BELOW)).astype(jnp.int32)

    def body(carry):
        done, _, runs, accs = carry
        first = n_left - 1 - done
        runs, accs = tiles(first, [logits(first)], runs, accs, False)
        done = done + 1
        return done, go_on(done, runs), tuple(runs), tuple(accs)

    runs = tuple(run_ref[c] for c in range(chains))
    accs = tuple(acc_ref[c] for c in range(chains))
    carry = (done0, go_on(done0, runs), runs, accs)
    accs = lax.while_loop(lambda carry: carry[1] > 0, body, carry)[3]
    for c in range(chains):
        o_ref[0, :, c * LANES:(c + 1) * LANES] = accs[c].astype(o_ref.dtype)


def _stick_breaking(proj):
    bsz, s, _ = proj.shape
    t = ATTN_TILE
    width = STICK_PAIRS * LANES
    n_blocks = H_C * DH_C // width
    return pl.pallas_call(
        _stick_kernel,
        out_shape=jax.ShapeDtypeStruct((bsz, s, H_C * DH_C), BF16),
        grid=(bsz, n_blocks, s // t),
        in_specs=[
            pl.BlockSpec((1, t, width), lambda b, h, i: (b, i, h)),
            pl.BlockSpec((1, s, width), lambda b, h, i: (b, 0, n_blocks + h)),
            pl.BlockSpec((1, s, width), lambda b, h, i: (b, 0, 2 * n_blocks + h)),
        ],
        out_specs=pl.BlockSpec((1, t, width), lambda b, h, i: (b, i, h)),
        scratch_shapes=[pltpu.VMEM((STICK_PAIRS, 2 * t, 1), F32),
                        pltpu.VMEM((STICK_PAIRS, t, LANES), F32)],
        compiler_params=pltpu.CompilerParams(
            dimension_semantics=("arbitrary", "arbitrary", "arbitrary"),
            vmem_limit_bytes=_vmem_limit(2 * s * width * 2 + 4 * t * width,
                                         STICK_PAIRS * 3 * t * LANES * 4)),
        name="stick_breaking_attention",
    )(proj, proj, proj)


def _route(logits):
    lane_i = lax.broadcasted_iota(jnp.int32, logits.shape, 1)
    lane = lane_i.astype(F32)
    group_of_lane = (lane_i // EXPERTS_PER_GROUP).astype(F32)
    big = jnp.float32(1 << 20)
    neg = jnp.float32(-jnp.inf)
    is_group = (lane_i >= N_EXPERTS) & (lane_i < N_EXPERTS + N_GROUPS)
    gl = jnp.where(is_group, logits, neg)
    gmax = jnp.max(gl, axis=-1, keepdims=True)
    g_val = 1.0 / jnp.sum(jnp.exp(gl - gmax), axis=-1, keepdims=True)
    g_sel = jnp.min(jnp.where(gl == gmax, lane - N_EXPERTS, big), axis=-1, keepdims=True)
    in_group = (lane_i < N_EXPERTS) & (group_of_lane == g_sel)
    el = jnp.where(in_group, logits, neg)
    emax = jnp.max(el, axis=-1, keepdims=True)
    esum = jnp.sum(jnp.exp(el - emax), axis=-1, keepdims=True)
    i1 = jnp.min(jnp.where(el == emax, lane, big), axis=-1, keepdims=True)
    el2 = jnp.where(lane == i1, neg, el)
    emax2 = jnp.max(el2, axis=-1, keepdims=True)
    i2 = jnp.min(jnp.where(el2 == emax2, lane, big), axis=-1, keepdims=True)
    v1 = 1.0 / esum
    v2 = jnp.exp(emax2 - emax) / esum
    tot = v1 + v2
    w1 = g_val * (v1 / tot)
    w2 = g_val * (v2 / tot)
    out = jnp.where(lane_i == 0, i1, 0.0)
    out = jnp.where(lane_i == 1, i2, out)
    out = jnp.where(lane_i == 2, w1, out)
    out = jnp.where(lane_i == 3, w2, out)
    return out, (lane == i1), (lane == i2)


def _pack_halves(h):
    half = h.shape[1] // 2
    lo = pltpu.bitcast(h[:, :half].astype(BF16).astype(F32), jnp.uint32)
    hi = pltpu.bitcast(h[:, half:].astype(BF16).astype(F32), jnp.uint32)
    return lax.shift_right_logical(lo, jnp.uint32(16)) | hi


def _unpack_halves(p):
    lo = pltpu.bitcast(lax.shift_left(p, jnp.uint32(16)), F32).astype(BF16)
    hi = pltpu.bitcast(p & jnp.uint32(0xFFFF0000), F32).astype(BF16)
    return lo, hi


def _out_kernel(*refs, n_o):
    o_refs = refs[:n_o]
    w_refs = refs[n_o:2 * n_o]
    x_ref, mod_ref, g_ref, wr_ref, br_ref, xo_ref = refs[2 * n_o:2 * n_o + 6]
    h_refs = refs[2 * n_o + 6:2 * n_o + 6 + MOE_PIECES]
    r_ref, cnt_ref, run_ref = refs[2 * n_o + 6 + MOE_PIECES:]

    @pl.when((pl.program_id(0) == 0) & (pl.program_id(1) == 0))
    def _():
        run_ref[...] = jnp.zeros_like(run_ref)

    th = x_ref.shape[1] // PROJ_ROW_PARTS
    ms = []
    for r in range(PROJ_ROW_PARTS):
        rows = slice(r * th, (r + 1) * th)
        m = _dot(o_refs[0][0, rows, :], w_refs[0][...])
        for a in range(1, n_o):
            m = m + _dot(o_refs[a][0, rows, :], w_refs[a][...])
        ms.append(m)
    routes, hits1, hits2 = [], [], []
    for r in range(PROJ_ROW_PARTS):
        rows = slice(r * th, (r + 1) * th)
        x = x_ref[0, rows, :] + mod_ref[0, 2:3, :] * ms[r]
        xo_ref[0, rows, :] = x
        h = _rms(x, g_ref[...]) * (1.0 + mod_ref[0, 4:5, :]) + mod_ref[0, 3:4, :]
        piece = h.shape[1] // MOE_PIECES
        for k in range(MOE_PIECES):
            h_refs[k][0, rows, :] = _pack_halves(h[:, k * piece:(k + 1) * piece])
        part_route, part_hit1, part_hit2 = _route(_dot3_narrow(h, wr_ref[...]) + br_ref[...])
        routes.append(part_route)
        hits1.append(part_hit1)
        hits2.append(part_hit2)
    route = jnp.concatenate(routes, axis=0)
    hit1 = jnp.concatenate(hits1, axis=0)
    hit2 = jnp.concatenate(hits2, axis=0)
    tm = route.shape[0]
    picks = jnp.where(hit1 | hit2, 1.0, 0.0)
    row = lax.broadcasted_iota(jnp.int32, (tm, tm), 0)
    col = lax.broadcasted_iota(jnp.int32, (tm, tm), 1)
    before = jnp.where(col < row, 1.0, 0.0).astype(BF16)
    earlier = _dot(before, picks.astype(BF16)) + run_ref[...]
    rank1 = jnp.sum(jnp.where(hit1, earlier, 0.0), axis=-1, keepdims=True)
    rank2 = jnp.sum(jnp.where(hit2, earlier, 0.0), axis=-1, keepdims=True)
    lane_i = lax.broadcasted_iota(jnp.int32, route.shape, 1)
    route = jnp.where(lane_i == 4, rank1, route)
    route = jnp.where(lane_i == 5, rank2, route)
    r_ref[0] = route
    run_ref[...] = run_ref[...] + jnp.sum(picks, axis=0, keepdims=True)
    cnt_ref[...] = run_ref[...]


def _out_projection(o_list, w_list, x, mod, g, w_router, b_router):
    bsz, s, d = x.shape
    tm = TOKEN_TILE
    n_o = len(o_list)
    packed = d // (2 * MOE_PIECES)
    tok = lambda b, i: (b, i, 0)
    const2 = lambda b, i: (0, 0)
    in_specs = ([pl.BlockSpec((1, tm, o.shape[2]), tok) for o in o_list]
                + [pl.BlockSpec(w.shape, const2) for w in w_list]
                + [pl.BlockSpec((1, tm, d), tok), pl.BlockSpec((1, 6, d), lambda b, i: (b, 0, 0)),
                   pl.BlockSpec((1, d), const2), pl.BlockSpec((d, LANES), const2),
                   pl.BlockSpec((1, LANES), const2)])
    block_bytes = (sum(tm * o.shape[2] * 2 for o in o_list) + sum(w.size * 2 for w in w_list)
                   + 2 * tm * d * 4 + tm * d * 2 + d * LANES * 4 + tm * LANES * 4)
    return pl.pallas_call(
        functools.partial(_out_kernel, n_o=n_o),
        out_shape=([jax.ShapeDtypeStruct((bsz, s, d), F32)]
                   + [jax.ShapeDtypeStruct((bsz, s, packed), jnp.uint32)] * MOE_PIECES
                   + [jax.ShapeDtypeStruct((bsz, s, LANES), F32),
                      jax.ShapeDtypeStruct((1, LANES), F32)]),
        grid=(bsz, s // tm),
        in_specs=in_specs,
        out_specs=([pl.BlockSpec((1, tm, d), tok)]
                   + [pl.BlockSpec((1, tm, packed), tok)] * MOE_PIECES
                   + [pl.BlockSpec((1, tm, LANES), tok), pl.BlockSpec((1, LANES), const2)]),
        scratch_shapes=[pltpu.VMEM((1, LANES), F32)],
        compiler_params=pltpu.CompilerParams(
            dimension_semantics=("arbitrary", "arbitrary"),
            vmem_limit_bytes=_vmem_limit(block_bytes, tm * d * 12 + tm * tm * 4)),
        name="out_projection_norm_router",
    )(*o_list, *w_list, x, mod, g.reshape(1, d), w_router, b_router)


def _expert_kernel(te_ref, tv_ref, nu_ref, *refs):
    x_refs = refs[:MOE_PIECES]
    wg_ref, wu_ref, wd_ref, o_ref, wgb_ref, wub_ref, wdb_ref = refs[MOE_PIECES:]
    t = pl.program_id(0)
    prev = te_ref[jnp.maximum(t - 1, 0)]

    @pl.when((t == 0) | (te_ref[t] != prev))
    def _():
        wgb_ref[...] = wg_ref[0, 0].astype(BF16)
        wub_ref[...] = wu_ref[0, 0].astype(BF16)
        wdb_ref[...] = wd_ref[0, 0].astype(BF16)

    @pl.when(t < nu_ref[0])
    def _():
        th = o_ref.shape[0] // EXPERT_ROW_PARTS
        hidden = []
        for r in range(EXPERT_ROW_PARTS):
            rows = slice(r * th, (r + 1) * th)
            cols = []
            for x_ref in x_refs:
                row = lax.broadcasted_iota(jnp.int32, (th, x_ref.shape[1]), 0) + r * th
                cols += list(_unpack_halves(jnp.where(row < tv_ref[t], x_ref[rows, :],
                                                      jnp.uint32(0))))
            w = cols[0].shape[1]
            gt = _dot(cols[0], wgb_ref[:w, :])
            up = _dot(cols[0], wub_ref[:w, :])
            for k in range(1, len(cols)):
                gt = gt + _dot(cols[k], wgb_ref[k * w:(k + 1) * w, :])
                up = up + _dot(cols[k], wub_ref[k * w:(k + 1) * w, :])
            hidden.append((gt, up))
        for r in range(EXPERT_ROW_PARTS):
            gt, up = hidden[r]
            o_ref[r * th:(r + 1) * th, :] = _dot((_silu(gt) * up).astype(BF16),
                                                 wdb_ref[...]).astype(o_ref.dtype)

    @pl.when(t >= nu_ref[0])
    def _():
        o_ref[...] = jnp.zeros_like(o_ref)


def _grouped_experts(xs, tile_expert, tile_valid, n_used, w_gate, w_up, w_down, layer):
    p, packed = xs[0].shape
    d = w_gate.shape[2]
    tm = EXPERT_TILE
    f = w_gate.shape[3]
    grid_spec = pltpu.PrefetchScalarGridSpec(
        num_scalar_prefetch=3,
        grid=(p // tm,),
        in_specs=[pl.BlockSpec((tm, packed), lambda t, te, tv, nu: (t, 0))] * MOE_PIECES + [
            pl.BlockSpec((1, 1, d, f), lambda t, te, tv, nu: (layer, te[t], 0, 0)),
            pl.BlockSpec((1, 1, d, f), lambda t, te, tv, nu: (layer, te[t], 0, 0)),
            pl.BlockSpec((1, 1, f, d), lambda t, te, tv, nu: (layer, te[t], 0, 0)),
        ],
        out_specs=pl.BlockSpec((tm, d), lambda t, te, tv, nu: (t, 0)),
        scratch_shapes=[pltpu.VMEM((d, f), BF16), pltpu.VMEM((d, f), BF16),
                        pltpu.VMEM((f, d), BF16)],
    )
    block_bytes = tm * d * 2 + 3 * d * f * 4 + tm * d * 2
    return pl.pallas_call(
        _expert_kernel,
        out_shape=jax.ShapeDtypeStruct((p, d), BF16),
        grid_spec=grid_spec,
        compiler_params=pltpu.CompilerParams(
            dimension_semantics=("arbitrary",),
            vmem_limit_bytes=_vmem_limit(block_bytes, 3 * d * f * 2 + tm * f * 16)),
        name="grouped_swiglu_experts",
    )(tile_expert, tile_valid, n_used, *xs, w_gate, w_up, w_down)


def _pos_kernel(route_ref, starts_ref, pos_ref):
    rt = jnp.transpose(route_ref[...])
    starts = starts_ref[...]
    ids = lax.broadcasted_iota(jnp.int32, (N_EXPERTS, rt.shape[1]), 0).astype(F32)
    for c in range(2):
        start_of = jnp.sum(jnp.where(ids == rt[c:c + 1, :], starts, 0.0), axis=0, keepdims=True)
        pos_ref[c:c + 1, :] = (start_of + rt[4 + c:5 + c, :]).astype(jnp.int32)


def _row_positions(route, starts):
    n_tok = route.shape[0]
    tm = TOKEN_TILE
    return pl.pallas_call(
        _pos_kernel,
        out_shape=jax.ShapeDtypeStruct((2, n_tok), jnp.int32),
        grid=(n_tok // tm,),
        in_specs=[pl.BlockSpec((tm, LANES), lambda i: (i, 0)),
                  pl.BlockSpec((N_EXPERTS, 1), lambda i: (0, 0))],
        out_specs=pl.BlockSpec((2, tm), lambda i: (0, i)),
        compiler_params=pltpu.CompilerParams(
            dimension_semantics=("arbitrary",),
            vmem_limit_bytes=_vmem_limit(tm * LANES * 4 + N_EXPERTS * LANES * 4 + 8 * tm * 4)),
        name="expert_row_positions",
    )(route, starts.astype(F32).reshape(N_EXPERTS, 1))


def _scatter_rows(rows, pos, n_out):
    n_tok, d = rows.shape
    n_idx = pos.shape[0]
    w = SC_SCATTER_WINDOW
    per_pass = n_tok // w
    mesh = plsc.VectorSubcoreMesh(core_axis_name="core", subcore_axis_name="subcore")

    @pl.kernel(out_type=jax.ShapeDtypeStruct((n_out, d), rows.dtype), mesh=mesh, scratch_types=[])
    def scatter_kernel(x_hbm, i_hbm, o_hbm):
        def body(x_vmem, i_vmem):
            pltpu.sync_copy(x_vmem, o_hbm.at[i_vmem.at[0]])

        pltpu.emit_pipeline(
            body,
            grid=(n_idx // w,),
            in_specs=[pl.BlockSpec((w, d), lambda i: (i % per_pass, 0)),
                      pl.BlockSpec((1, w), lambda i: (0, i))],
            out_specs=[],
            core_axis_name=("core", "subcore"),
            dimension_semantics=(pltpu.PARALLEL,),
        )(x_hbm, i_hbm)

    return scatter_kernel(rows, pos.reshape(1, n_idx))


def _dispatch(route, counts, n_tok):
    tm = EXPERT_TILE
    counts = counts[:N_EXPERTS].astype(jnp.int32)
    padded = ((counts + tm - 1) // tm) * tm
    ends = jnp.cumsum(padded)
    starts = ends - padded
    pos = _row_positions(route, starts).reshape(2 * n_tok)
    p = 2 * n_tok + N_EXPERTS * tm
    tile_start = jnp.arange(p // tm, dtype=jnp.int32) * tm
    tile_expert = jnp.minimum(jnp.sum(ends[None, :] <= tile_start[:, None], axis=1),
                              N_EXPERTS - 1).astype(jnp.int32)
    n_used = (ends[-1] // tm).astype(jnp.int32).reshape(1)
    row_end = starts + counts
    tile_valid = jnp.clip(jnp.sum(jnp.where(tile_expert[:, None] == jnp.arange(N_EXPERTS)[None, :],
                                            row_end[None, :], 0), axis=1) - tile_start, 0, tm)
    return p, tile_expert, tile_valid.astype(jnp.int32), n_used, pos


def _final_kernel(x_ref, y_ref, route_ref, mod_ref, g_ref, o_ref):
    o_ref[0] = _rms(_expert_residual(x_ref[0], y_ref, route_ref, mod_ref), g_ref[...])


def _final_norm(x, y, route, mod, g):
    bsz, s, d = x.shape
    tm = TOKEN_TILE
    tok = lambda b, i: (b, i, 0)
    return pl.pallas_call(
        _final_kernel,
        out_shape=jax.ShapeDtypeStruct((bsz, s, d), F32),
        grid=(bsz, s // tm),
        in_specs=[pl.BlockSpec((1, tm, d), tok),
                  pl.BlockSpec((2, 1, tm, d), lambda b, i: (0, b, i, 0)),
                  pl.BlockSpec((1, tm, LANES), tok),
                  pl.BlockSpec((1, 6, d), lambda b, i: (b, 0, 0)),
                  pl.BlockSpec((1, d), lambda b, i: (0, 0))],
        out_specs=pl.BlockSpec((1, tm, d), tok),
        compiler_params=pltpu.CompilerParams(
            dimension_semantics=("arbitrary", "arbitrary"),
            vmem_limit_bytes=_vmem_limit(4 * tm * d * 4 + tm * LANES * 4)),
        name="residual_final_norm",
    )(x, y, route, mod, g.reshape(1, d))


def kernel(x, c, w_ada, b_ada, norm_mix, norm_ffn, norm_final, rel_bias, even_w_in, even_lambda, even_subln, even_w_gk2, even_b_gk, even_gla_norm, even_w_out, odd_w_in, odd_w_out, router_group_w, router_group_b, router_expert_w, router_expert_b, expert_w_gate, expert_w_up, expert_w_down):
    bsz, s, d = x.shape
    n_tok = bsz * s
    assert s % TOKEN_TILE == 0 and s % GLA_STEP == 0
    assert s % (DIFF_KEY_TILES * ATTN_TILE) == 0 and s % (DIFF_Q_TILES * ATTN_TILE) == 0
    mod_all = _modulation(c, w_ada, b_ada).reshape(DEPTH, bsz, 6, d)
    bias_tiles = _diff_bias_tiles(rel_bias)
    y = route = prev_mod = None
    for l in range(DEPTH):
        mod = mod_all[l]
        i = l // 2
        if l % 2 == 0:
            w_in = even_w_in[i]
            w_main = w_in[:, :EVEN_MAIN_COLS].astype(BF16)
            w_gate = jnp.zeros((d, LANES), BF16).at[:, :GATE_RANK].set(
                w_in[:, EVEN_MAIN_COLS:].astype(BF16))
            x, proj, gate = _in_projection(x, y, route, prev_mod, norm_mix[l], mod, w_main, w_gate)
            oa = _diff_attention(proj, bias_tiles, even_lambda[i], even_subln[i], l)
            ob = _gla(proj, gate, even_w_gk2[i], even_b_gk[i], even_gla_norm[i])
            w_out = even_w_out[i].astype(BF16)
            o_list = [oa, ob]
            w_list = [w_out[:H_A * DV_A], w_out[H_A * DV_A:]]
        else:
            x, proj, _ = _in_projection(x, y, route, prev_mod, norm_mix[l], mod,
                                        odd_w_in[i].astype(BF16), None)
            o_list = [_stick_breaking(proj)]
            w_list = [odd_w_out[i].astype(BF16)]
        w_router = jnp.zeros((d, LANES), F32)
        w_router = w_router.at[:, :N_EXPERTS].set(router_expert_w[l].astype(F32))
        w_router = w_router.at[:, N_EXPERTS:N_EXPERTS + N_GROUPS].set(router_group_w[l].astype(F32))
        b_router = jnp.zeros((1, LANES), F32)
        b_router = b_router.at[0, :N_EXPERTS].set(router_expert_b[l].astype(F32))
        b_router = b_router.at[0, N_EXPERTS:N_EXPERTS + N_GROUPS].set(router_group_b[l].astype(F32))
        outs = _out_projection(o_list, w_list, x, mod, norm_ffn[l], w_router, b_router)
        x, pieces, route, counts = outs[0], outs[1:1 + MOE_PIECES], outs[-2], outs[-1]
        n_rows, tile_expert, tile_valid, n_used, pos = _dispatch(route.reshape(n_tok, LANES),
                                                                 counts[0], n_tok)
        xs = [_scatter_rows(piece.reshape(n_tok, piece.shape[2]), pos, n_rows) for piece in pieces]
        out = _grouped_experts(xs, tile_expert, tile_valid, n_used, expert_w_gate, expert_w_up,
                               expert_w_down, l)
        y = out.at[pos].get(mode="promise_in_bounds").reshape(2, bsz, s, d)
        prev_mod = mod
    return _final_norm(x, y, route, prev_mod, norm_final)
```

```python
import functools
import math

import jax
import jax.numpy as jnp
from jax import lax
from jax.experimental import pallas as pl
from jax.experimental.pallas import tpu as pltpu
from jax.experimental.pallas import tpu_sc as plsc

F32 = jnp.float32
BF16 = jnp.bfloat16

DEPTH = 4
CHUNK = 64
EPS = 1e-6
H_A, DH_A, DV_A = 4, 64, 128
H_B, DK_B, DV_B = 4, 64, 128
GATE_RANK = 16
GATE_TAU = 16.0
H_C, DH_C = 16, 64
N_BUCKETS = 32
MAX_DISTANCE = 128
N_GROUPS = 4
EXPERTS_PER_GROUP = 8
N_EXPERTS = N_GROUPS * EXPERTS_PER_GROUP
EVEN_MAIN_COLS = 3 * H_A * DV_A + 2 * H_B * DK_B + 2 * H_B * DV_B
GLA_Q_COL = 3 * H_A * DV_A
GLA_K_COL = GLA_Q_COL + H_B * DK_B
GLA_V_COL = GLA_K_COL + H_B * DK_B
GLA_R_COL = GLA_V_COL + H_B * DV_B

LANES = 128
V7X_VMEM_BYTES = 64 * 1024 * 1024
VMEM_TEMPORARIES_BYTES = 16 * 1024 * 1024
VMEM_RESERVED_BYTES = 8 * 1024 * 1024
NEG_BIG = -1e30
EXP_ZERO_BELOW = -104.0

TOKEN_TILE = 512
PROJ_ROW_PARTS = 2
ATTN_TILE = 128
STICK_PAIRS = 8
STICK_AHEAD = 2
STICK_LOGITS_AHEAD = 8
DIFF_KEY_TILES = 8
DIFF_Q_TILES = 8
GLA_STEP = 256
EXPERT_TILE = 512
EXPERT_ROW_PARTS = 4
BATCH_PARTS = 2
SC_SCATTER_WINDOW = 128
MOE_PIECES = 2


def _vmem_limit(block_bytes, scratch_bytes=0):
    est = 2 * block_bytes + scratch_bytes + VMEM_TEMPORARIES_BYTES
    return int(min(est, V7X_VMEM_BYTES - VMEM_RESERVED_BYTES))


def _nt(a, b):
    return lax.dot_general(a, b, (((1,), (1,)), ((), ())), preferred_element_type=F32)


def _tn(a, b):
    return lax.dot_general(a, b, (((0,), (0,)), ((), ())), preferred_element_type=F32)


def _dot(a, b):
    return jnp.dot(a, b, preferred_element_type=F32)


def _split(x):
    hi = x.astype(BF16)
    lo = (x - hi.astype(F32)).astype(BF16)
    return hi, lo


def _dot3(a, b):
    ah, al = _split(a)
    bh, bl = _split(b)
    return _dot(ah, bh) + _dot(al, bh) + _dot(ah, bl)


def _dot3_narrow(a, b):
    ah, al = _split(a)
    bh, bl = _split(b)
    n = b.shape[1]
    both = _dot(ah, jnp.concatenate([bh, bl], axis=1))
    return both[:, :n] + both[:, n:] + _dot(al, bh)


def _rms(x, g):
    return x * lax.rsqrt(jnp.mean(x * x, axis=-1, keepdims=True) + EPS) * g


def _softplus(z):
    return jnp.maximum(z, 0.0) + jnp.log(1.0 + jnp.exp(-jnp.abs(z)))


def _silu(x):
    return x / (1.0 + jnp.exp(-x))


def _mod_kernel(c_ref, w_ref, b_ref, o_ref):
    c = c_ref[...]
    o_ref[0] = _dot3(_silu(c), w_ref[0]) + b_ref[0]


def _modulation(c, w_ada, b_ada):
    depth, d, n = w_ada.shape
    bsz = c.shape[0]
    tn = n // 4
    return pl.pallas_call(
        _mod_kernel,
        out_shape=jax.ShapeDtypeStruct((depth, bsz, n), F32),
        grid=(depth, n // tn),
        in_specs=[
            pl.BlockSpec((bsz, d), lambda l, j: (0, 0)),
            pl.BlockSpec((1, d, tn), lambda l, j: (l, 0, j)),
            pl.BlockSpec((1, 1, tn), lambda l, j: (l, 0, j)),
        ],
        out_specs=pl.BlockSpec((1, bsz, tn), lambda l, j: (l, 0, j)),
        compiler_params=pltpu.CompilerParams(
            dimension_semantics=("arbitrary", "arbitrary"),
            vmem_limit_bytes=_vmem_limit(d * tn * 4 + bsz * (d + tn) * 4)),
        name="adaln_modulation",
    )(c, w_ada, b_ada.reshape(depth, 1, n))


def _expert_residual(x, y_ref, route_ref, mod_ref, rows=slice(None)):
    route = route_ref[0, rows, :]
    y = (route[:, 2:3] * y_ref[0, 0, rows, :].astype(F32)
         + route[:, 3:4] * y_ref[1, 0, rows, :].astype(F32))
    return x + mod_ref[0, 5:6, :] * y


def _proj_kernel(*refs, has_res, n_main, has_gate):
    it = iter(refs)
    x_ref = next(it)
    if has_res:
        y_ref = next(it)
        route_ref = next(it)
        pmod_ref = next(it)
    g_ref = next(it)
    mod_ref = next(it)
    w_ref = next(it)
    wg_ref = next(it) if has_gate else None
    xo_ref = next(it) if has_res else None
    o_ref = next(it)
    og_ref = next(it) if has_gate else None

    th = x_ref.shape[1] // PROJ_ROW_PARTS
    hbs = []
    for r in range(PROJ_ROW_PARTS):
        rows = slice(r * th, (r + 1) * th)
        x = x_ref[0, rows, :]
        if has_res:
            x = _expert_residual(x, y_ref, route_ref, pmod_ref, rows)
            xo_ref[0, rows, :] = x
        h = _rms(x, g_ref[...]) * (1.0 + mod_ref[0, 1:2, :]) + mod_ref[0, 0:1, :]
        hbs.append(h.astype(BF16))
    step = 4 * LANES
    for r in range(PROJ_ROW_PARTS):
        rows = slice(r * th, (r + 1) * th)
        for n0 in range(0, n_main, step):
            o_ref[0, rows, n0:n0 + step] = _dot(hbs[r], w_ref[:, n0:n0 + step]).astype(BF16)
        if has_gate:
            og_ref[0, rows, :] = _dot(hbs[r], wg_ref[...])


def _in_projection(x, y, route, prev_mod, g, mod, w, w_gate):
    bsz, s, d = x.shape
    tm = TOKEN_TILE
    has_res = y is not None
    has_gate = w_gate is not None
    n_main = w.shape[1]
    tok = lambda b, i: (b, i, 0)
    per_b = lambda b, i: (b, 0, 0)
    const2 = lambda b, i: (0, 0)
    in_specs = [pl.BlockSpec((1, tm, d), tok)]
    args = [x]
    if has_res:
        in_specs += [pl.BlockSpec((2, 1, tm, d), lambda b, i: (0, b, i, 0)),
                     pl.BlockSpec((1, tm, LANES), tok), pl.BlockSpec((1, 6, d), per_b)]
        args += [y, route, prev_mod]
    in_specs += [pl.BlockSpec((1, d), const2), pl.BlockSpec((1, 6, d), per_b),
                 pl.BlockSpec((d, n_main), const2)]
    args += [g.reshape(1, d), mod, w]
    if has_gate:
        in_specs.append(pl.BlockSpec((d, LANES), const2))
        args.append(w_gate)
    out_shape, out_specs = [], []
    if has_res:
        out_shape.append(jax.ShapeDtypeStruct((bsz, s, d), F32))
        out_specs.append(pl.BlockSpec((1, tm, d), tok))
    out_shape.append(jax.ShapeDtypeStruct((bsz, s, n_main), BF16))
    out_specs.append(pl.BlockSpec((1, tm, n_main), tok))
    if has_gate:
        out_shape.append(jax.ShapeDtypeStruct((bsz, s, LANES), F32))
        out_specs.append(pl.BlockSpec((1, tm, LANES), tok))
    block_bytes = (tm * d * 4 * (4 if has_res else 1) + d * n_main * 2 + tm * n_main * 2
                   + d * LANES * 2 + 2 * tm * LANES * 4)
    outs = pl.pallas_call(
        functools.partial(_proj_kernel, has_res=has_res, n_main=n_main, has_gate=has_gate),
        out_shape=out_shape,
        grid=(bsz, s // tm),
        in_specs=in_specs,
        out_specs=out_specs,
        compiler_params=pltpu.CompilerParams(
            dimension_semantics=("arbitrary", "arbitrary"),
            vmem_limit_bytes=_vmem_limit(block_bytes, tm * d * 8)),
        name="norm_mod_in_projection",
    )(*args)
    outs = list(outs)
    x_new = outs.pop(0) if has_res else x
    proj = outs.pop(0)
    gate = outs.pop(0) if has_gate else None
    return x_new, proj, gate


def _t5_bucket(rel):
    nb = N_BUCKETS // 2
    max_exact = nb // 2
    ret = jnp.where(rel > 0, nb, 0)
    n = jnp.abs(rel)
    nf = jnp.maximum(n, 1).astype(F32)
    large = max_exact + (jnp.log(nf / max_exact) / math.log(MAX_DISTANCE / max_exact)
                         * (nb - max_exact)).astype(jnp.int32)
    large = jnp.minimum(large, nb - 1)
    return ret + jnp.where(n < max_exact, n, large)


def _diff_bias_tiles(rel_bias):
    t = ATTN_TILE
    assert t >= MAX_DISTANCE, "tiles two or more away must lie beyond the last bucket edge"
    qp = jnp.arange(t)[:, None]
    kp = jnp.arange(t)[None, :]
    tiles = []
    for off in (0, 1, 2):
        rel = (kp - off * t) - qp
        hit = _t5_bucket(rel)[None, :, :, None] == jnp.arange(N_BUCKETS)[None, None, None, :]
        b = jnp.sum(jnp.where(hit, jnp.transpose(rel_bias.astype(F32))[:, None, None, :], 0.0),
                    axis=-1)
        if off == 0:
            b = jnp.where((kp // CHUNK) <= (qp // CHUNK), b, NEG_BIG)
        tiles.append(b)
    tab = jnp.stack(tiles, axis=1)
    return jnp.concatenate([tab, tab], axis=2)


def _diff_attn_kernel(q_ref, k_ref, v_ref, bias_ref, lam_ref, g_ref, o_ref, *, lambda_init):
    t = ATTN_TILE
    chains = DIFF_Q_TILES
    i = pl.program_id(2)
    lane = lax.broadcasted_iota(jnp.int32, (t, LANES), 1)
    scale = jnp.asarray(DH_A ** -0.5, BF16)
    stacked = []
    for c in range(chains):
        q = q_ref[0, c * t:(c + 1) * t, :]
        zero = jnp.zeros_like(q)
        stacked += [jnp.where(lane < DH_A, q, zero), jnp.where(lane >= DH_A, q, zero)]
    qq = jnp.concatenate(stacked, axis=0) * scale

    group = DIFF_KEY_TILES
    span = group * t
    assert chains == group, "a grid step's query tiles must be exactly one key block"

    def softmax_step(carry_c, s, which_of, vb):
        m, l, acc = carry_c
        width = s.shape[1] // t

        def logits(u):
            return s[:, u * t:(u + 1) * t] + bias_ref[0, which_of(u)]

        top = logits(0)
        for u in range(1, width):
            top = jnp.maximum(top, logits(u))
        m_new = jnp.maximum(m, jnp.max(top, axis=-1, keepdims=True))
        alpha = jnp.exp(m - m_new)
        ps = [jnp.exp(logits(u) - m_new) for u in range(width)]
        tot = ps[0]
        for u in range(1, width):
            tot = tot + ps[u]
        l = alpha * l + jnp.sum(tot, axis=-1, keepdims=True)
        p = jnp.concatenate([pu.astype(BF16) for pu in ps], axis=1)
        return m_new, l, alpha * acc + _dot(p, vb)

    def body(n, carry):
        r0 = pl.multiple_of(n * span, span)
        kb = k_ref[0, pl.ds(r0, span), :]
        vb = v_ref[0, pl.ds(r0, span), :]
        out = []
        s_next = _nt(qq[0:2 * t], kb)
        for c in range(chains):
            s = s_next
            if c + 1 < chains:
                s_next = _nt(qq[(c + 1) * 2 * t:(c + 2) * 2 * t], kb)

            def which_of(u, c=c):
                return jnp.minimum((i * chains + c) - (n * group + u), 2)
            out.append(softmax_step(carry[c], s, which_of, vb))
        return tuple(out)

    init = tuple((jnp.full((2 * t, 1), NEG_BIG, F32), jnp.zeros((2 * t, 1), F32),
                  jnp.zeros((2 * t, DV_A), F32)) for _ in range(chains))
    before = lax.fori_loop(0, i, body, init)
    r0 = pl.multiple_of(i * span, span)
    def diag_logits(c):
        return _nt(qq[c * 2 * t:(c + 1) * 2 * t], k_ref[0, pl.ds(r0, (c + 1) * t), :])

    final = []
    s_next = diag_logits(0)
    for c in range(chains):
        s = s_next
        if c + 1 < chains:
            s_next = diag_logits(c + 1)
        final.append(softmax_step(before[c], s, lambda u, c=c: min(c - u, 2),
                                  v_ref[0, pl.ds(r0, (c + 1) * t), :]))
    lp = lam_ref[...].astype(F32)
    lam = (jnp.exp(jnp.sum(lp[0:1] * lp[1:2], axis=-1, keepdims=True))
           - jnp.exp(jnp.sum(lp[2:3] * lp[3:4], axis=-1, keepdims=True)) + lambda_init)
    for c in range(chains):
        _, l, acc = final[c]
        o = acc / l
        w = o[:t] - lam * o[t:]
        o_ref[0, c * t:(c + 1) * t, :] = (_rms(w, g_ref[...]) * (1.0 - lambda_init)
                                          ).astype(o_ref.dtype)


def _diff_attention(proj, bias_tiles, lam_p, subln_g, layer_idx):
    bsz, s, _ = proj.shape
    t = ATTN_TILE
    tq = DIFF_Q_TILES * t
    lambda_init = 0.8 - 0.6 * math.exp(-0.3 * layer_idx)
    return pl.pallas_call(
        functools.partial(_diff_attn_kernel, lambda_init=lambda_init),
        out_shape=jax.ShapeDtypeStruct((bsz, s, H_A * DV_A), BF16),
        grid=(bsz, H_A, s // tq),
        in_specs=[
            pl.BlockSpec((1, tq, LANES), lambda b, h, i: (b, i, h)),
            pl.BlockSpec((1, s, LANES), lambda b, h, i: (b, 0, H_A + h)),
            pl.BlockSpec((1, s, LANES), lambda b, h, i: (b, 0, 2 * H_A + h)),
            pl.BlockSpec((1, 3, 2 * t, t), lambda b, h, i: (h, 0, 0, 0)),
            pl.BlockSpec((4, DH_A), lambda b, h, i: (0, 0)),
            pl.BlockSpec((1, DV_A), lambda b, h, i: (0, 0)),
        ],
        out_specs=pl.BlockSpec((1, tq, DV_A), lambda b, h, i: (b, i, h)),
        compiler_params=pltpu.CompilerParams(
            dimension_semantics=("arbitrary", "arbitrary", "arbitrary"),
            vmem_limit_bytes=_vmem_limit(2 * s * LANES * 2 + 3 * 2 * t * t * 4 + 4 * tq * LANES,
                                         4 * tq * DIFF_KEY_TILES * t * 4)),
        name="diff_attention",
    )(proj, proj, proj, bias_tiles, lam_p, subln_g.reshape(1, DV_A))


def _gla_kernel(q_ref, k_ref, v_ref, r_ref, bg_ref, wgk_ref, bgk_ref, g_ref, o_ref, st_ref):
    c = CHUNK
    ts = q_ref.shape[1]
    n_chunks = ts // c

    @pl.when(pl.program_id(1) == 0)
    def _():
        st_ref[...] = jnp.zeros_like(st_ref)

    row = lax.broadcasted_iota(jnp.int32, (ts, ts), 0)
    col = lax.broadcasted_iota(jnp.int32, (ts, ts), 1)
    same_chunk_before = ((row // c) == (col // c)) & (col <= row)
    tri = jnp.where(same_chunk_before, 1.0, 0.0).astype(BF16)
    causal = (lax.broadcasted_iota(jnp.int32, (c, c), 1)
              <= lax.broadcasted_iota(jnp.int32, (c, c), 0))
    lane = lax.broadcasted_iota(jnp.int32, (ts, LANES), 1)
    wgk = wgk_ref[...]
    bgk = bgk_ref[...]
    g = g_ref[...]
    bg = bg_ref[0]

    las = [-_softplus(-(_dot3(bg, wgk[:, hp * LANES:(hp + 1) * LANES])
                        + bgk[:, hp * LANES:(hp + 1) * LANES])) / GATE_TAU
           for hp in range(H_B // 2)]
    bs = []
    for la in las:
        la_hi, la_lo = _split(la)
        bs.append(_dot(tri, la_hi) + _dot(tri, la_lo))

    heads = []
    for hp in range(H_B // 2):
        cols = slice(hp * LANES, (hp + 1) * LANES)
        b = bs[hp]
        b_last = jnp.concatenate(
            [jnp.broadcast_to(b[(ci + 1) * c - 1:(ci + 1) * c, :], (c, LANES))
             for ci in range(n_chunks)], axis=0)
        qf = q_ref[0, :, cols].astype(F32) * (DK_B ** -0.5)
        kf = k_ref[0, :, cols].astype(F32)
        q_dec = qf * jnp.exp(b)
        k_inv = (kf * jnp.exp(-b)).astype(BF16)
        k_dec = kf * jnp.exp(b_last - b)
        decays = [jnp.exp(b[(ci + 1) * c - 1:(ci + 1) * c, :]) for ci in range(n_chunks)]
        for hh in range(2):
            own = (lane >= hh * DK_B) & (lane < (hh + 1) * DK_B)
            heads.append((jnp.where(own, q_dec, 0.0).astype(BF16), k_inv,
                          jnp.where(own, k_dec, 0.0).astype(BF16), decays))

    def rows(x, ci):
        return x[ci * c:(ci + 1) * c]

    scores, increments = [], []
    for h, (qd, k_inv, kd, _) in enumerate(heads):
        vh = v_ref[0, :, h * DV_B:(h + 1) * DV_B]
        scores.append([jnp.where(causal, _nt(rows(qd, ci), rows(k_inv, ci)), 0.0).astype(BF16)
                       for ci in range(n_chunks)])
        increments.append([_tn(rows(vh, ci), rows(kd, ci)) for ci in range(n_chunks)])

    for h, (qd, _, _, decays) in enumerate(heads):
        vh = v_ref[0, :, h * DV_B:(h + 1) * DV_B]
        st = st_ref[h]
        states = []
        for ci in range(n_chunks):
            states.append(st.astype(BF16))
            st = st * decays[ci] + increments[h][ci]
        st_ref[h] = st
        o = jnp.concatenate(
            [_dot(scores[h][ci], rows(vh, ci)) + _nt(rows(qd, ci), states[ci])
             for ci in range(n_chunks)], axis=0)
        rr = r_ref[0, :, h * DV_B:(h + 1) * DV_B].astype(F32)
        o_ref[0, :, h * DV_B:(h + 1) * DV_B] = (_rms(o, g) * _silu(rr)).astype(o_ref.dtype)


def _gla(proj, gate, w_gk2, b_gk, gla_g):
    bsz, s, _ = proj.shape
    ts = GLA_STEP
    nk = H_B * DK_B
    nv = H_B * DV_B
    w_pad = jnp.zeros((LANES, nk), F32).at[:GATE_RANK].set(w_gk2.astype(F32))
    return pl.pallas_call(
        _gla_kernel,
        out_shape=jax.ShapeDtypeStruct((bsz, s, nv), BF16),
        grid=(bsz, s // ts),
        in_specs=[
            pl.BlockSpec((1, ts, nk), lambda b, i: (b, i, GLA_Q_COL // nk)),
            pl.BlockSpec((1, ts, nk), lambda b, i: (b, i, GLA_K_COL // nk)),
            pl.BlockSpec((1, ts, nv), lambda b, i: (b, i, GLA_V_COL // nv)),
            pl.BlockSpec((1, ts, nv), lambda b, i: (b, i, GLA_R_COL // nv)),
            pl.BlockSpec((1, ts, LANES), lambda b, i: (b, i, 0)),
            pl.BlockSpec((LANES, nk), lambda b, i: (0, 0)),
            pl.BlockSpec((1, nk), lambda b, i: (0, 0)),
            pl.BlockSpec((1, DV_B), lambda b, i: (0, 0)),
        ],
        out_specs=pl.BlockSpec((1, ts, nv), lambda b, i: (b, i, 0)),
        scratch_shapes=[pltpu.VMEM((H_B, DV_B, LANES), F32)],
        compiler_params=pltpu.CompilerParams(
            dimension_semantics=("arbitrary", "arbitrary"),
            vmem_limit_bytes=_vmem_limit(ts * (2 * nk + 3 * nv) * 2 + ts * LANES * 4,
                                         H_B * DV_B * LANES * 4)),
        name="gated_linear_attention",
    )(proj, proj, proj, proj, gate, w_pad, b_gk.reshape(1, nk), gla_g.reshape(1, DV_B))


def _stick_kernel(q_ref, k_ref, v_ref, o_ref, run_ref, acc_ref):
    t = ATTN_TILE
    chains = STICK_PAIRS
    i = pl.program_id(2)
    lane = lax.broadcasted_iota(jnp.int32, (t, LANES), 1)
    scale = jnp.asarray(DH_C ** -0.5, BF16)
    qqs = []
    for c in range(chains):
        q = q_ref[0, :, c * LANES:(c + 1) * LANES]
        zero = jnp.zeros_like(q)
        qqs.append(jnp.concatenate([jnp.where(lane < DH_C, q, zero),
                                    jnp.where(lane >= DH_C, q, zero)], axis=0) * scale)
    row = lax.broadcasted_iota(jnp.int32, (t, t), 0)
    col = lax.broadcasted_iota(jnp.int32, (t, t), 1)
    later = jnp.where(row > col, 1.0, 0.0).astype(BF16)
    below = (lax.broadcasted_iota(jnp.int32, (2 * t, t), 1)
             < lax.broadcasted_iota(jnp.int32, (2 * t, t), 0) % t)

    def logit(j, c):
        r0 = pl.multiple_of(jnp.maximum(j, 0) * t, t)
        return _nt(qqs[c], k_ref[0, pl.ds(r0, t), c * LANES:(c + 1) * LANES])

    def logits(j):
        return [functools.partial(logit, j, c) for c in range(chains)]

    def tiles(first, zs, runs, accs, diagonal_first):
        runs, accs = list(runs), list(accs)
        mids = []
        order = [(u, c) for u in range(len(zs)) for c in range(chains)]
        issued = [zs[u][c]() for u, c in order[:STICK_LOGITS_AHEAD]]
        for k, (u, c) in enumerate(order):
            if k + STICK_LOGITS_AHEAD < len(order):
                nu, nc = order[k + STICK_LOGITS_AHEAD]
                issued.append(zs[nu][nc]())
            z = issued[k]
            sp = _softplus(z)
            log_1m = -sp
            masked = diagonal_first and u == 0
            if masked:
                log_1m = jnp.where(below, log_1m, 0.0)
            hi, lo = _split(log_1m)
            within = _dot(hi, later) + _dot(lo, later)
            mids.append((u, c, z - sp, log_1m, within, masked))
        for u, c, log_b, log_1m, within, masked in mids:
            r0 = pl.multiple_of((first - u) * t, t)
            vb = v_ref[0, pl.ds(r0, t), c * LANES:(c + 1) * LANES]
            a = jnp.exp(log_b + (within + runs[c]))
            if masked:
                a = jnp.where(below, a, 0.0)
            ab = a.astype(BF16)
            zero_v = jnp.zeros_like(vb)
            a2 = jnp.concatenate([ab[:t], ab[t:]], axis=1)
            v2 = jnp.concatenate([jnp.where(lane < DH_C, vb, zero_v),
                                  jnp.where(lane >= DH_C, vb, zero_v)], axis=0)
            accs[c] = accs[c] + _dot(a2, v2)
            runs[c] = runs[c] + jnp.sum(log_1m, axis=-1, keepdims=True)
        return runs, accs

    def first_block(n_tiles):
        zs = [logits(i - u) for u in range(n_tiles)]
        runs, accs = tiles(i, zs, [jnp.zeros((2 * t, 1), F32)] * chains,
                           [jnp.zeros((t, LANES), F32)] * chains, True)
        for c in range(chains):
            run_ref[c] = runs[c]
            acc_ref[c] = accs[c]

    ahead = STICK_AHEAD

    @pl.when(i >= ahead)
    def _():
        first_block(1 + ahead)

    @pl.when(i < ahead)
    def _():
        first_block(1)

    n_left = i
    done0 = jnp.where(i >= ahead, ahead, 0)

    def go_on(done, runs):
        top = runs[0]
        for c in range(1, chains):
            top = jnp.maximum(top, runs[c])
        return ((done < n_left) & (jnp.max(top) > EXP_ZERO_BELOW)).astype(jnp.int32)

    def body(carry):
        done, _, runs, accs = carry
        first = n_left - 1 - done
        runs, accs = tiles(first, [logits(first)], runs, accs, False)
        done = done + 1
        return done, go_on(done, runs), tuple(runs), tuple(accs)

    runs = tuple(run_ref[c] for c in range(chains))
    accs = tuple(acc_ref[c] for c in range(chains))
    carry = (done0, go_on(done0, runs), runs, accs)
    accs = lax.while_loop(lambda carry: carry[1] > 0, body, carry)[3]
    for c in range(chains):
        o_ref[0, :, c * LANES:(c + 1) * LANES] = accs[c].astype(o_ref.dtype)


def _stick_breaking(proj):
    bsz, s, _ = proj.shape
    t = ATTN_TILE
    width = STICK_PAIRS * LANES
    n_blocks = H_C * DH_C // width
    return pl.pallas_call(
        _stick_kernel,
        out_shape=jax.ShapeDtypeStruct((bsz, s, H_C * DH_C), BF16),
        grid=(bsz, n_blocks, s // t),
        in_specs=[
            pl.BlockSpec((1, t, width), lambda b, h, i: (b, i, h)),
            pl.BlockSpec((1, s, width), lambda b, h, i: (b, 0, n_blocks + h)),
            pl.BlockSpec((1, s, width), lambda b, h, i: (b, 0, 2 * n_blocks + h)),
        ],
        out_specs=pl.BlockSpec((1, t, width), lambda b, h, i: (b, i, h)),
        scratch_shapes=[pltpu.VMEM((STICK_PAIRS, 2 * t, 1), F32),
                        pltpu.VMEM((STICK_PAIRS, t, LANES), F32)],
        compiler_params=pltpu.CompilerParams(
            dimension_semantics=("arbitrary", "arbitrary", "arbitrary"),
            vmem_limit_bytes=_vmem_limit(2 * s * width * 2 + 4 * t * width,
                                         STICK_PAIRS * 3 * t * LANES * 4)),
        name="stick_breaking_attention",
    )(proj, proj, proj)


def _route(logits):
    lane_i = lax.broadcasted_iota(jnp.int32, logits.shape, 1)
    lane = lane_i.astype(F32)
    group_of_lane = (lane_i // EXPERTS_PER_GROUP).astype(F32)
    big = jnp.float32(1 << 20)
    neg = jnp.float32(-jnp.inf)
    is_group = (lane_i >= N_EXPERTS) & (lane_i < N_EXPERTS + N_GROUPS)
    gl = jnp.where(is_group, logits, neg)
    gmax = jnp.max(gl, axis=-1, keepdims=True)
    g_val = 1.0 / jnp.sum(jnp.exp(gl - gmax), axis=-1, keepdims=True)
    g_sel = jnp.min(jnp.where(gl == gmax, lane - N_EXPERTS, big), axis=-1, keepdims=True)
    in_group = (lane_i < N_EXPERTS) & (group_of_lane == g_sel)
    el = jnp.where(in_group, logits, neg)
    emax = jnp.max(el, axis=-1, keepdims=True)
    esum = jnp.sum(jnp.exp(el - emax), axis=-1, keepdims=True)
    i1 = jnp.min(jnp.where(el == emax, lane, big), axis=-1, keepdims=True)
    el2 = jnp.where(lane == i1, neg, el)
    emax2 = jnp.max(el2, axis=-1, keepdims=True)
    i2 = jnp.min(jnp.where(el2 == emax2, lane, big), axis=-1, keepdims=True)
    v1 = 1.0 / esum
    v2 = jnp.exp(emax2 - emax) / esum
    tot = v1 + v2
    w1 = g_val * (v1 / tot)
    w2 = g_val * (v2 / tot)
    out = jnp.where(lane_i == 0, i1, 0.0)
    out = jnp.where(lane_i == 1, i2, out)
    out = jnp.where(lane_i == 2, w1, out)
    out = jnp.where(lane_i == 3, w2, out)
    return out, (lane == i1), (lane == i2)


def _pack_halves(h):
    half = h.shape[1] // 2
    lo = pltpu.bitcast(h[:, :half].astype(BF16).astype(F32), jnp.uint32)
    hi = pltpu.bitcast(h[:, half:].astype(BF16).astype(F32), jnp.uint32)
    return lax.shift_right_logical(lo, jnp.uint32(16)) | hi


def _unpack_halves(p):
    lo = pltpu.bitcast(lax.shift_left(p, jnp.uint32(16)), F32).astype(BF16)
    hi = pltpu.bitcast(p & jnp.uint32(0xFFFF0000), F32).astype(BF16)
    return lo, hi


def _out_kernel(*refs, n_o):
    o_refs = refs[:n_o]
    w_refs = refs[n_o:2 * n_o]
    x_ref, mod_ref, g_ref, wr_ref, br_ref, xo_ref = refs[2 * n_o:2 * n_o + 6]
    h_refs = refs[2 * n_o + 6:2 * n_o + 6 + MOE_PIECES]
    r_ref, cnt_ref, run_ref = refs[2 * n_o + 6 + MOE_PIECES:]

    @pl.when((pl.program_id(0) == 0) & (pl.program_id(1) == 0))
    def _():
        run_ref[...] = jnp.zeros_like(run_ref)

    th = x_ref.shape[1] // PROJ_ROW_PARTS
    ms = []
    for r in range(PROJ_ROW_PARTS):
        rows = slice(r * th, (r + 1) * th)
        m = _dot(o_refs[0][0, rows, :], w_refs[0][...])
        for a in range(1, n_o):
            m = m + _dot(o_refs[a][0, rows, :], w_refs[a][...])
        ms.append(m)
    routes, hits1, hits2 = [], [], []
    for r in range(PROJ_ROW_PARTS):
        rows = slice(r * th, (r + 1) * th)
        x = x_ref[0, rows, :] + mod_ref[0, 2:3, :] * ms[r]
        xo_ref[0, rows, :] = x
        h = _rms(x, g_ref[...]) * (1.0 + mod_ref[0, 4:5, :]) + mod_ref[0, 3:4, :]
        piece = h.shape[1] // MOE_PIECES
        for k in range(MOE_PIECES):
            h_refs[k][0, rows, :] = _pack_halves(h[:, k * piece:(k + 1) * piece])
        part_route, part_hit1, part_hit2 = _route(_dot3_narrow(h, wr_ref[...]) + br_ref[...])
        routes.append(part_route)
        hits1.append(part_hit1)
        hits2.append(part_hit2)
    route = jnp.concatenate(routes, axis=0)
    hit1 = jnp.concatenate(hits1, axis=0)
    hit2 = jnp.concatenate(hits2, axis=0)
    tm = route.shape[0]
    picks = jnp.where(hit1 | hit2, 1.0, 0.0)
    row = lax.broadcasted_iota(jnp.int32, (tm, tm), 0)
    col = lax.broadcasted_iota(jnp.int32, (tm, tm), 1)
    before = jnp.where(col < row, 1.0, 0.0).astype(BF16)
    earlier = _dot(before, picks.astype(BF16)) + run_ref[...]
    rank1 = jnp.sum(jnp.where(hit1, earlier, 0.0), axis=-1, keepdims=True)
    rank2 = jnp.sum(jnp.where(hit2, earlier, 0.0), axis=-1, keepdims=True)
    lane_i = lax.broadcasted_iota(jnp.int32, route.shape, 1)
    route = jnp.where(lane_i == 4, rank1, route)
    route = jnp.where(lane_i == 5, rank2, route)
    r_ref[0] = route
    run_ref[...] = run_ref[...] + jnp.sum(picks, axis=0, keepdims=True)
    cnt_ref[...] = run_ref[...]


def _out_projection(o_list, w_list, x, mod, g, w_router, b_router):
    bsz, s, d = x.shape
    tm = TOKEN_TILE
    n_o = len(o_list)
    packed = d // (2 * MOE_PIECES)
    tok = lambda b, i: (b, i, 0)
    const2 = lambda b, i: (0, 0)
    in_specs = ([pl.BlockSpec((1, tm, o.shape[2]), tok) for o in o_list]
                + [pl.BlockSpec(w.shape, const2) for w in w_list]
                + [pl.BlockSpec((1, tm, d), tok), pl.BlockSpec((1, 6, d), lambda b, i: (b, 0, 0)),
                   pl.BlockSpec((1, d), const2), pl.BlockSpec((d, LANES), const2),
                   pl.BlockSpec((1, LANES), const2)])
    block_bytes = (sum(tm * o.shape[2] * 2 for o in o_list) + sum(w.size * 2 for w in w_list)
                   + 2 * tm * d * 4 + tm * d * 2 + d * LANES * 4 + tm * LANES * 4)
    return pl.pallas_call(
        functools.partial(_out_kernel, n_o=n_o),
        out_shape=([jax.ShapeDtypeStruct((bsz, s, d), F32)]
                   + [jax.ShapeDtypeStruct((bsz, s, packed), jnp.uint32)] * MOE_PIECES
                   + [jax.ShapeDtypeStruct((bsz, s, LANES), F32),
                      jax.ShapeDtypeStruct((1, LANES), F32)]),
        grid=(bsz, s // tm),
        in_specs=in_specs,
        out_specs=([pl.BlockSpec((1, tm, d), tok)]
                   + [pl.BlockSpec((1, tm, packed), tok)] * MOE_PIECES
                   + [pl.BlockSpec((1, tm, LANES), tok), pl.BlockSpec((1, LANES), const2)]),
        scratch_shapes=[pltpu.VMEM((1, LANES), F32)],
        compiler_params=pltpu.CompilerParams(
            dimension_semantics=("arbitrary", "arbitrary"),
            vmem_limit_bytes=_vmem_limit(block_bytes, tm * d * 12 + tm * tm * 4)),
        name="out_projection_norm_router",
    )(*o_list, *w_list, x, mod, g.reshape(1, d), w_router, b_router)


def _expert_kernel(te_ref, tv_ref, nu_ref, *refs):
    x_refs = refs[:MOE_PIECES]
    wg_ref, wu_ref, wd_ref, o_ref, wgb_ref, wub_ref, wdb_ref = refs[MOE_PIECES:]
    t = pl.program_id(0)
    prev = te_ref[jnp.maximum(t - 1, 0)]

    @pl.when((t == 0) | (te_ref[t] != prev))
    def _():
        wgb_ref[...] = wg_ref[0, 0].astype(BF16)
        wub_ref[...] = wu_ref[0, 0].astype(BF16)
        wdb_ref[...] = wd_ref[0, 0].astype(BF16)

    @pl.when(t < nu_ref[0])
    def _():
        th = o_ref.shape[0] // EXPERT_ROW_PARTS
        hidden = []
        for r in range(EXPERT_ROW_PARTS):
            rows = slice(r * th, (r + 1) * th)
            cols = []
            for x_ref in x_refs:
                row = lax.broadcasted_iota(jnp.int32, (th, x_ref.shape[1]), 0) + r * th
                cols += list(_unpack_halves(jnp.where(row < tv_ref[t], x_ref[rows, :],
                                                      jnp.uint32(0))))
            w = cols[0].shape[1]
            gt = _dot(cols[0], wgb_ref[:w, :])
            up = _dot(cols[0], wub_ref[:w, :])
            for k in range(1, len(cols)):
                gt = gt + _dot(cols[k], wgb_ref[k * w:(k + 1) * w, :])
                up = up + _dot(cols[k], wub_ref[k * w:(k + 1) * w, :])
            hidden.append((gt, up))
        for r in range(EXPERT_ROW_PARTS):
            gt, up = hidden[r]
            o_ref[r * th:(r + 1) * th, :] = _dot((_silu(gt) * up).astype(BF16),
                                                 wdb_ref[...]).astype(o_ref.dtype)

    @pl.when(t >= nu_ref[0])
    def _():
        o_ref[...] = jnp.zeros_like(o_ref)


def _grouped_experts(xs, tile_expert, tile_valid, n_used, w_gate, w_up, w_down, layer):
    p, packed = xs[0].shape
    d = w_gate.shape[2]
    tm = EXPERT_TILE
    f = w_gate.shape[3]
    grid_spec = pltpu.PrefetchScalarGridSpec(
        num_scalar_prefetch=3,
        grid=(p // tm,),
        in_specs=[pl.BlockSpec((tm, packed), lambda t, te, tv, nu: (t, 0))] * MOE_PIECES + [
            pl.BlockSpec((1, 1, d, f), lambda t, te, tv, nu: (layer, te[t], 0, 0)),
            pl.BlockSpec((1, 1, d, f), lambda t, te, tv, nu: (layer, te[t], 0, 0)),
            pl.BlockSpec((1, 1, f, d), lambda t, te, tv, nu: (layer, te[t], 0, 0)),
        ],
        out_specs=pl.BlockSpec((tm, d), lambda t, te, tv, nu: (t, 0)),
        scratch_shapes=[pltpu.VMEM((d, f), BF16), pltpu.VMEM((d, f), BF16),
                        pltpu.VMEM((f, d), BF16)],
    )
    block_bytes = tm * d * 2 + 3 * d * f * 4 + tm * d * 2
    return pl.pallas_call(
        _expert_kernel,
        out_shape=jax.ShapeDtypeStruct((p, d), BF16),
        grid_spec=grid_spec,
        compiler_params=pltpu.CompilerParams(
            dimension_semantics=("arbitrary",),
            vmem_limit_bytes=_vmem_limit(block_bytes, 3 * d * f * 2 + tm * f * 16)),
        name="grouped_swiglu_experts",
    )(tile_expert, tile_valid, n_used, *xs, w_gate, w_up, w_down)


def _pos_kernel(route_ref, starts_ref, pos_ref):
    rt = jnp.transpose(route_ref[...])
    starts = starts_ref[...]
    ids = lax.broadcasted_iota(jnp.int32, (N_EXPERTS, rt.shape[1]), 0).astype(F32)
    for c in range(2):
        start_of = jnp.sum(jnp.where(ids == rt[c:c + 1, :], starts, 0.0), axis=0, keepdims=True)
        pos_ref[c:c + 1, :] = (start_of + rt[4 + c:5 + c, :]).astype(jnp.int32)


def _row_positions(route, starts):
    n_tok = route.shape[0]
    tm = TOKEN_TILE
    return pl.pallas_call(
        _pos_kernel,
        out_shape=jax.ShapeDtypeStruct((2, n_tok), jnp.int32),
        grid=(n_tok // tm,),
        in_specs=[pl.BlockSpec((tm, LANES), lambda i: (i, 0)),
                  pl.BlockSpec((N_EXPERTS, 1), lambda i: (0, 0))],
        out_specs=pl.BlockSpec((2, tm), lambda i: (0, i)),
        compiler_params=pltpu.CompilerParams(
            dimension_semantics=("arbitrary",),
            vmem_limit_bytes=_vmem_limit(tm * LANES * 4 + N_EXPERTS * LANES * 4 + 8 * tm * 4)),
        name="expert_row_positions",
    )(route, starts.astype(F32).reshape(N_EXPERTS, 1))


def _scatter_rows(rows, pos, n_out):
    n_tok, d = rows.shape
    n_idx = pos.shape[0]
    w = SC_SCATTER_WINDOW
    per_pass = n_tok // w
    mesh = plsc.VectorSubcoreMesh(core_axis_name="core", subcore_axis_name="subcore")

    @pl.kernel(out_type=jax.ShapeDtypeStruct((n_out, d), rows.dtype), mesh=mesh, scratch_types=[])
    def scatter_kernel(x_hbm, i_hbm, o_hbm):
        def body(x_vmem, i_vmem):
            pltpu.sync_copy(x_vmem, o_hbm.at[i_vmem.at[0]])

        pltpu.emit_pipeline(
            body,
            grid=(n_idx // w,),
            in_specs=[pl.BlockSpec((w, d), lambda i: (i % per_pass, 0)),
                      pl.BlockSpec((1, w), lambda i: (0, i))],
            out_specs=[],
            core_axis_name=("core", "subcore"),
            dimension_semantics=(pltpu.PARALLEL,),
        )(x_hbm, i_hbm)

    return scatter_kernel(rows, pos.reshape(1, n_idx))


def _dispatch(route, counts, n_tok):
    tm = EXPERT_TILE
    counts = counts[:N_EXPERTS].astype(jnp.int32)
    padded = ((counts + tm - 1) // tm) * tm
    ends = jnp.cumsum(padded)
    starts = ends - padded
    pos = _row_positions(route, starts).reshape(2 * n_tok)
    p = 2 * n_tok + N_EXPERTS * tm
    tile_start = jnp.arange(p // tm, dtype=jnp.int32) * tm
    tile_expert = jnp.minimum(jnp.sum(ends[None, :] <= tile_start[:, None], axis=1),
                              N_EXPERTS - 1).astype(jnp.int32)
    n_used = (ends[-1] // tm).astype(jnp.int32).reshape(1)
    row_end = starts + counts
    tile_valid = jnp.clip(jnp.sum(jnp.where(tile_expert[:, None] == jnp.arange(N_EXPERTS)[None, :],
                                            row_end[None, :], 0), axis=1) - tile_start, 0, tm)
    return p, tile_expert, tile_valid.astype(jnp.int32), n_used, pos


def _final_kernel(x_ref, y_ref, route_ref, mod_ref, g_ref, o_ref):
    o_ref[0] = _rms(_expert_residual(x_ref[0], y_ref, route_ref, mod_ref), g_ref[...])


def _final_norm(x, y, route, mod, g):
    bsz, s, d = x.shape
    tm = TOKEN_TILE
    tok = lambda b, i: (b, i, 0)
    return pl.pallas_call(
        _final_kernel,
        out_shape=jax.ShapeDtypeStruct((bsz, s, d), F32),
        grid=(bsz, s // tm),
        in_specs=[pl.BlockSpec((1, tm, d), tok),
                  pl.BlockSpec((2, 1, tm, d), lambda b, i: (0, b, i, 0)),
                  pl.BlockSpec((1, tm, LANES), tok),
                  pl.BlockSpec((1, 6, d), lambda b, i: (b, 0, 0)),
                  pl.BlockSpec((1, d), lambda b, i: (0, 0))],
        out_specs=pl.BlockSpec((1, tm, d), tok),
        compiler_params=pltpu.CompilerParams(
            dimension_semantics=("arbitrary", "arbitrary"),
            vmem_limit_bytes=_vmem_limit(4 * tm * d * 4 + tm * LANES * 4)),
        name="residual_final_norm",
    )(x, y, route, mod, g.reshape(1, d))


def kernel(x, c, w_ada, b_ada, norm_mix, norm_ffn, norm_final, rel_bias, even_w_in, even_lambda, even_subln, even_w_gk2, even_b_gk, even_gla_norm, even_w_out, odd_w_in, odd_w_out, router_group_w, router_group_b, router_expert_w, router_expert_b, expert_w_gate, expert_w_up, expert_w_down):
    bsz, s, d = x.shape
    assert s % TOKEN_TILE == 0 and s % GLA_STEP == 0
    assert s % (DIFF_KEY_TILES * ATTN_TILE) == 0 and s % (DIFF_Q_TILES * ATTN_TILE) == 0
    mod_all = _modulation(c, w_ada, b_ada).reshape(DEPTH, bsz, 6, d)
    bias_tiles = _diff_bias_tiles(rel_bias)
    parts = BATCH_PARTS if bsz % BATCH_PARTS == 0 else 1
    pb = bsz // parts
    n_tok = pb * s
    streams = [dict(x=x[p * pb:(p + 1) * pb], y=None, route=None, prev_mod=None,
                    rows=slice(p * pb, (p + 1) * pb)) for p in range(parts)]
    for l in range(DEPTH):
        i = l // 2
        if l % 2 == 0:
            w_in = even_w_in[i]
            w_main = w_in[:, :EVEN_MAIN_COLS].astype(BF16)
            w_gate = jnp.zeros((d, LANES), BF16).at[:, :GATE_RANK].set(
                w_in[:, EVEN_MAIN_COLS:].astype(BF16))
            w_out = even_w_out[i].astype(BF16)
            w_list = [w_out[:H_A * DV_A], w_out[H_A * DV_A:]]
        else:
            w_main, w_gate = odd_w_in[i].astype(BF16), None
            w_list = [odd_w_out[i].astype(BF16)]
        w_router = jnp.zeros((d, LANES), F32)
        w_router = w_router.at[:, :N_EXPERTS].set(router_expert_w[l].astype(F32))
        w_router = w_router.at[:, N_EXPERTS:N_EXPERTS + N_GROUPS].set(router_group_w[l].astype(F32))
        b_router = jnp.zeros((1, LANES), F32)
        b_router = b_router.at[0, :N_EXPERTS].set(router_expert_b[l].astype(F32))
        b_router = b_router.at[0, N_EXPERTS:N_EXPERTS + N_GROUPS].set(router_group_b[l].astype(F32))
        for st in streams:
            mod = mod_all[l, st["rows"]]
            xp, proj, gate = _in_projection(st["x"], st["y"], st["route"], st["prev_mod"],
                                            norm_mix[l], mod, w_main, w_gate)
            if l % 2 == 0:
                o_list = [_diff_attention(proj, bias_tiles, even_lambda[i], even_subln[i], l),
                          _gla(proj, gate, even_w_gk2[i], even_b_gk[i], even_gla_norm[i])]
            else:
                o_list = [_stick_breaking(proj)]
            outs = _out_projection(o_list, w_list, xp, mod, norm_ffn[l], w_router, b_router)
            xp, pieces, route, counts = outs[0], outs[1:1 + MOE_PIECES], outs[-2], outs[-1]
            n_rows, tile_expert, tile_valid, n_used, pos = _dispatch(
                route.reshape(n_tok, LANES), counts[0], n_tok)
            xs = [_scatter_rows(piece.reshape(n_tok, piece.shape[2]), pos, n_rows)
                  for piece in pieces]
            out = _grouped_experts(xs, tile_expert, tile_valid, n_used, expert_w_gate,
                                   expert_w_up, expert_w_down, l)
            st.update(x=xp, route=route, prev_mod=mod,
                      y=out.at[pos].get(mode="promise_in_bounds").reshape(2, pb, s, d))
    return jnp.concatenate([_final_norm(st["x"], st["y"], st["route"], st["prev_mod"], norm_final)
                            for st in streams], axis=0)
```
